```python
import math, functools
import jax, jax.numpy as jnp
from jax import lax
import numpy as np

D_MODEL = 2048
BATCH = 4
SEQ = 2048
DEPTH = 4
DEC_BATCH = 8
DEC_SEQ = 1
PAST_LEN = 16384
PAGE_SIZE = 128

MIX_W = D_MODEL // 2
NSA_HD = 64
NSA_HEADS = MIX_W // NSA_HD
NSA_KV = 4
NSA_REP = NSA_HEADS // NSA_KV
CMP_LEN = 32
CMP_STRIDE = 16
CMP_HID = NSA_HD
SLC_BLK = 64
SLC_TOPK = 16
WINDOW = 512
Q_BLK = 64
GLA_HEADS = 4
GLA_DK = MIX_W // 2
GLA_DV = MIX_W
GLA_HK = GLA_DK // GLA_HEADS
GLA_HV = GLA_DV // GLA_HEADS
GLA_RANK = 16
GLA_TAU = 16.0
GLA_CHUNK = 16
M_DINNER = MIX_W
M_HD = 64
M_HEADS = M_DINNER // M_HD
M_GROUPS = 4
M_STATE = 128
M_CONV = 4
M_CONVDIM = M_DINNER + 2 * M_GROUPS * M_STATE
M_CHUNK = 64
N_BRANCH = 3
EPS = 1e-6
NEG = -1e30
FORCE = 1e4

IN_SIZES = (NSA_HEADS * NSA_HD, 6 * NSA_KV * NSA_HD, 3 * NSA_HEADS, MIX_W,
            GLA_DK, GLA_DK, GLA_DV, GLA_RANK, GLA_DV,
            M_CONVDIM, M_HEADS, M_DINNER,
            N_BRANCH * D_MODEL)
IN_COLS = sum(IN_SIZES)

kernel_name = 'nsa_gla_ssd_gated_hybrid_step'


def rmsnorm(x, w):
    xf = x.astype(jnp.float32)
    y = xf * lax.rsqrt(jnp.mean(xf * xf, axis=-1, keepdims=True) + EPS)
    return (y * w.astype(jnp.float32)).astype(x.dtype)


def masked_softmax(s, mask, axis):
    p = jax.nn.softmax(jnp.where(mask, s.astype(jnp.float32), NEG), axis=axis)
    return jnp.where(mask, p, 0.0)


def pad_time(a, mult):
    t = a.shape[1]
    tp = -(-t // mult) * mult
    return jnp.pad(a, [(0, 0), (0, tp - t)] + [(0, 0)] * (a.ndim - 2))


def split_cols(u):
    offs = np.cumsum((0,) + IN_SIZES)
    return [u[..., int(offs[i]):int(offs[i + 1])] for i in range(len(IN_SIZES))]


def gather_past(pool, page_table):
    g = pool[page_table]
    return g.reshape(page_table.shape[0], page_table.shape[1] * pool.shape[1], NSA_KV, NSA_HD)


def compress(k, pe, w1, w2):
    b = k.shape[0]
    kp = pad_time(k, CMP_STRIDE)
    nseg = kp.shape[1] // CMP_STRIDE
    seg = kp.reshape(b, nseg, CMP_STRIDE, NSA_KV, NSA_HD)
    blk = jnp.concatenate([seg[:, :-1], seg[:, 1:]], axis=2) + pe[None, None, :, None, :]
    flat = blk.transpose(0, 1, 3, 2, 4).reshape(b, nseg - 1, NSA_KV, CMP_LEN * NSA_HD)
    kc = jax.nn.silu(flat @ w1) @ w2
    c_end = jnp.arange(nseg - 1) * CMP_STRIDE + (CMP_LEN - 1)
    return kc, c_end


def to_blocks(k):
    b = k.shape[0]
    kp = pad_time(k, SLC_BLK)
    return kp.reshape(b, kp.shape[1] // SLC_BLK, SLC_BLK, NSA_KV, NSA_HD).transpose(0, 3, 1, 2, 4)


def nsa_attend(q, gates, t_pos, kc, vc, c_end, ks_blk, vs_blk, kw, vw, w_pos):
    b, tq = q.shape[:2]
    ns = ks_blk.shape[2]
    nc = kc.shape[1]
    qg = q.reshape(b, tq, NSA_KV, NSA_REP, NSA_HD) * (NSA_HD ** -0.5)
    s_c = jnp.einsum('bqgrd,bcgd->bqgrc', qg, kc)
    m_c = (c_end[None, :] <= t_pos[:, None])[None, :, None, None, :]
    p_c = masked_softmax(s_c, m_c, -1)
    o_c = jnp.einsum('bqgrc,bcgd->bqgrd', p_c, vc)
    c_start = jnp.arange(nc) * CMP_STRIDE
    s_idx = jnp.arange(ns)
    cover = ((c_start[:, None] + CMP_LEN > s_idx[None, :] * SLC_BLK)
             & (c_start[:, None] < (s_idx[None, :] + 1) * SLC_BLK)).astype(jnp.float32)
    imp = jnp.einsum('bqgc,cs->bqgs', p_c.sum(axis=3), cover)
    cur = t_pos // SLC_BLK
    valid = s_idx[None, :] * SLC_BLK <= t_pos[:, None]
    forced = (s_idx[None, :] == 0) | (s_idx[None, :] == cur[:, None]) | (s_idx[None, :] == cur[:, None] - 1)
    score = jnp.where(valid[None, :, None, :], jnp.where(forced[None, :, None, :], FORCE, imp), NEG)
    _, sel = lax.top_k(score, min(SLC_TOPK, ns))
    b_ix = jnp.arange(b)[:, None, None, None]
    g_ix = jnp.arange(NSA_KV)[None, None, :, None]
    ks = ks_blk[b_ix, g_ix, sel]
    vs = vs_blk[b_ix, g_ix, sel]
    tok = sel[..., None] * SLC_BLK + jnp.arange(SLC_BLK)
    m_s = (tok <= t_pos[None, :, None, None, None])[:, :, :, None]
    s_s = jnp.einsum('bqgrd,bqgknd->bqgrkn', qg, ks)
    p_s = masked_softmax(s_s, m_s, (-2, -1))
    o_s = jnp.einsum('bqgrkn,bqgknd->bqgrd', p_s, vs)
    m_w = ((w_pos[None, :] <= t_pos[:, None]) & (w_pos[None, :] > t_pos[:, None] - WINDOW)
           & (w_pos[None, :] >= 0))[None, :, None, None, :]
    s_w = jnp.einsum('bqgrd,blgd->bqgrl', qg, kw)
    p_w = masked_softmax(s_w, m_w, -1)
    o_w = jnp.einsum('bqgrl,blgd->bqgrd', p_w, vw)
    g = gates.reshape(b, tq, NSA_KV, NSA_REP, 3)
    o = g[..., 0:1] * o_c + g[..., 1:2] * o_s + g[..., 2:3] * o_w
    return o.reshape(b, tq, NSA_HEADS * NSA_HD)


def nsa_prompt(q, kv, gates, cmp_l):
    pe_k, w1_k, w2_k, pe_v, w1_v, w2_v = cmp_l
    b, t = q.shape[:2]
    k_c, v_c, k_s, v_s, k_w, v_w = (kv[:, :, i] for i in range(6))
    kc, c_end = compress(k_c, pe_k, w1_k, w2_k)
    vc, _ = compress(v_c, pe_v, w1_v, w2_v)
    ks_blk, vs_blk = to_blocks(k_s), to_blocks(v_s)
    pad_w = ((0, 0), (WINDOW, 0), (0, 0), (0, 0))
    kw_pad, vw_pad = jnp.pad(k_w, pad_w), jnp.pad(v_w, pad_w)
    nq = t // Q_BLK
    q_ch = q.reshape(b, nq, Q_BLK, NSA_HEADS, NSA_HD).swapaxes(0, 1)
    g_ch = gates.reshape(b, nq, Q_BLK, NSA_HEADS, 3).swapaxes(0, 1)

    def query_block(args):
        qc, gc, s0 = args
        t_pos = s0 + jnp.arange(Q_BLK)
        w_pos = s0 - WINDOW + jnp.arange(WINDOW + Q_BLK)
        kw = lax.dynamic_slice_in_dim(kw_pad, s0, WINDOW + Q_BLK, axis=1)
        vw = lax.dynamic_slice_in_dim(vw_pad, s0, WINDOW + Q_BLK, axis=1)
        return nsa_attend(qc, gc, t_pos, kc, vc, c_end, ks_blk, vs_blk, kw, vw, w_pos)

    o = lax.map(query_block, (q_ch, g_ch, jnp.arange(nq) * Q_BLK))
    o = o.swapaxes(0, 1).reshape(b, t, NSA_HEADS * NSA_HD)
    n_w = min(WINDOW, t)
    return o, (k_c, v_c, k_s, v_s, k_w[:, t - n_w:], v_w[:, t - n_w:])


def nsa_sample(q, kv, gates, cmp_l, past):
    pe_k, w1_k, w2_k, pe_v, w1_v, w2_v = cmp_l
    past_kc, past_vc, past_ks, past_vs, buf_kw, buf_vw = past
    b, t = q.shape[:2]
    p_len = past_kc.shape[1]
    nb = buf_kw.shape[1]
    k_c, v_c, k_s, v_s, k_w, v_w = (kv[:, :, i] for i in range(6))
    kc, c_end = compress(jnp.concatenate([past_kc, k_c], axis=1), pe_k, w1_k, w2_k)
    vc, _ = compress(jnp.concatenate([past_vc, v_c], axis=1), pe_v, w1_v, w2_v)
    ks_blk = to_blocks(jnp.concatenate([past_ks, k_s], axis=1))
    vs_blk = to_blocks(jnp.concatenate([past_vs, v_s], axis=1))
    kw_all = jnp.concatenate([buf_kw.astype(k_w.dtype), k_w], axis=1)
    vw_all = jnp.concatenate([buf_vw.astype(v_w.dtype), v_w], axis=1)
    t_pos = p_len + jnp.arange(t)
    w_pos = p_len - nb + jnp.arange(nb + t)
    o = nsa_attend(q, gates, t_pos, kc, vc, c_end, ks_blk, vs_blk, kw_all, vw_all, w_pos)
    return o, (k_c, v_c, k_s, v_s, kw_all[:, t:], vw_all[:, t:])


def gla_chunked(q, k, v, log_a, s0):
    b, t, h, _ = q.shape
    dv = v.shape[-1]
    n = -(-t // GLA_CHUNK)

    def chunks(a):
        return pad_time(a, GLA_CHUNK).reshape(b, n, GLA_CHUNK, h, a.shape[-1]).swapaxes(0, 1)

    causal = jnp.tril(jnp.ones((GLA_CHUNK, GLA_CHUNK), dtype=bool))[None, :, :, None, None]

    def step(S, inp):
        qc, kc, vc, lac = inp
        cb = jnp.cumsum(lac, axis=1)
        decay = jnp.exp(jnp.where(causal, cb[:, :, None] - cb[:, None, :], -jnp.inf))
        att = jnp.einsum('bthd,bshd,btshd->bhts', qc, kc, decay)
        o = jnp.einsum('bhts,bshv->bthv', att, vc) + jnp.einsum('bthd,bhdv->bthv', qc * jnp.exp(cb), S)
        c_end = cb[:, -1]
        S = jnp.exp(c_end)[..., None] * S + jnp.einsum('bshd,bshv->bhdv', kc * jnp.exp(c_end[:, None] - cb), vc)
        return S, o

    S, o = lax.scan(step, s0.astype(jnp.float32), (chunks(q), chunks(k), chunks(v), chunks(log_a)))
    o = o.swapaxes(0, 1).reshape(b, n * GLA_CHUNK, h, dv)[:, :t]
    return o, S.astype(s0.dtype)


def ssd_chunked(x, dt, A, bm, cm, s0):
    b, t, h, p = x.shape
    r = h // M_GROUPS
    n = -(-t // M_CHUNK)

    def chunks(a, tail):
        return pad_time(a, M_CHUNK).reshape((b, n, M_CHUNK) + tail).swapaxes(0, 1)

    causal = jnp.tril(jnp.ones((M_CHUNK, M_CHUNK), dtype=bool))[None, :, :, None, None]

    def step(S, inp):
        xc, dtc, ac, bc, cc = inp
        cum = jnp.cumsum(ac, axis=1)
        L = jnp.exp(jnp.where(causal, cum[:, :, None] - cum[:, None, :], -jnp.inf))
        cbm = jnp.einsum('btgn,bsgn->btsg', cc, bc)
        y = jnp.einsum('btsg,btsgr,bsgr,bsgrp->btgrp', cbm, L, dtc, xc)
        y = y + jnp.einsum('btgn,bgrpn,btgr->btgrp', cc, S, jnp.exp(cum))
        c_end = cum[:, -1]
        w = jnp.exp(c_end[:, None] - cum) * dtc
        S = jnp.exp(c_end)[..., None, None] * S + jnp.einsum('bsgn,bsgr,bsgrp->bgrpn', bc, w, xc)
        return S, y

    S0 = s0.astype(jnp.float32).reshape(b, M_GROUPS, r, p, M_STATE)
    xs = (chunks(x, (M_GROUPS, r, p)), chunks(dt, (M_GROUPS, r)), chunks(dt * A, (M_GROUPS, r)),
          chunks(bm, (M_GROUPS, M_STATE)), chunks(cm, (M_GROUPS, M_STATE)))
    S, y = lax.scan(step, S0, xs)
    y = y.swapaxes(0, 1).reshape(b, n * M_CHUNK, h, p)[:, :t]
    return y, S.reshape(b, h, p, M_STATE).astype(s0.dtype)


def layer_forward(x, attend, gla_s0, ssm_s0, conv_buf, ln_w, w_in, gla_wa2, gla_ba, gla_norm,
                  conv_w, conv_b, dt_bias, a_log, d_skip, m_norm, w_br_a, w_br_b, w_br_c, w_out):
    b, t, _ = x.shape
    h = rmsnorm(x, ln_w)
    (a_q, a_kv, a_g, a_z, b_q, b_k, b_v, b_a, b_z, c_xbc, c_dt, c_z, m_g) = split_cols(h @ w_in)
    q = a_q.reshape(b, t, NSA_HEADS, NSA_HD)
    kv = a_kv.reshape(b, t, 6, NSA_KV, NSA_HD)
    o_a, nsa_new = attend(q, kv, jax.nn.sigmoid(a_g).reshape(b, t, NSA_HEADS, 3))
    o_a = o_a * jax.nn.silu(a_z)
    log_a = jax.nn.log_sigmoid((b_a @ gla_wa2 + gla_ba).astype(jnp.float32)) / GLA_TAU
    o_b, gla_s = gla_chunked(b_q.reshape(b, t, GLA_HEADS, GLA_HK) * (GLA_HK ** -0.5),
                             b_k.reshape(b, t, GLA_HEADS, GLA_HK),
                             b_v.reshape(b, t, GLA_HEADS, GLA_HV),
                             log_a.reshape(b, t, GLA_HEADS, GLA_HK), gla_s0)
    o_b = rmsnorm(o_b, gla_norm).reshape(b, t, GLA_DV) * jax.nn.silu(b_z)
    full = jnp.concatenate([conv_buf.astype(c_xbc.dtype), c_xbc], axis=1)
    conv = conv_b + sum(full[:, i:i + t] * conv_w[i] for i in range(M_CONV))
    xbc = jax.nn.silu(conv)
    nbc = M_GROUPS * M_STATE
    xm = xbc[..., :M_DINNER].reshape(b, t, M_HEADS, M_HD)
    bm = xbc[..., M_DINNER:M_DINNER + nbc].reshape(b, t, M_GROUPS, M_STATE)
    cm = xbc[..., M_DINNER + nbc:].reshape(b, t, M_GROUPS, M_STATE)
    dt = jax.nn.softplus((c_dt + dt_bias).astype(jnp.float32))
    y, ssm_s = ssd_chunked(xm, dt, -jnp.exp(a_log.astype(jnp.float32)), bm, cm, ssm_s0)
    y = y + d_skip[:, None] * xm
    yg = (y.reshape(b, t, M_DINNER) * jax.nn.silu(c_z)).reshape(b, t, M_GROUPS, M_DINNER // M_GROUPS)
    o_c = rmsnorm(yg, m_norm.reshape(M_GROUPS, M_DINNER // M_GROUPS)).reshape(b, t, M_DINNER)
    g = jax.nn.sigmoid(m_g).reshape(b, t, N_BRANCH, D_MODEL)
    mix = g[:, :, 0] * (o_a @ w_br_a) + g[:, :, 1] * (o_b @ w_br_b) + g[:, :, 2] * (o_c @ w_br_c)
    x_out = x + (mix @ w_out).astype(x.dtype)
    return x_out, nsa_new, gla_s, ssm_s, full[:, t:]


def setup_inputs(seed: int = 0) -> dict:
    key = jax.random.key(seed)
    keys = jax.random.split(key, 40)

    def nrm(i, shape, scale):
        return jax.random.normal(keys[i], shape, jnp.float32) * scale

    n_pages = PAST_LEN // PAGE_SIZE
    n_used = DEC_BATCH * n_pages
    n_pool = n_used + -(-n_used // 4)
    win_buf = min(WINDOW, PAST_LEN)
    page_shape = (DEPTH, n_pool, PAGE_SIZE, NSA_KV, NSA_HD)
    win_shape = (DEPTH, DEC_BATCH, win_buf, NSA_KV, NSA_HD)
    page_table = jax.random.permutation(keys[11], n_pool)[:n_used].reshape(DEC_BATCH, n_pages).astype(jnp.int32)
    dt0 = jnp.exp(jax.random.uniform(keys[24], (DEPTH, M_HEADS), jnp.float32, math.log(1e-3), math.log(1e-1)))
    return {
        'x_prompt': nrm(0, (BATCH, SEQ, D_MODEL), 1.0),
        'x_sample': nrm(1, (DEC_BATCH, DEC_SEQ, D_MODEL), 1.0),
        'cache_cmp_k': nrm(2, page_shape, 1.0),
        'cache_cmp_v': nrm(3, page_shape, 1.0),
        'cache_slc_k': nrm(4, page_shape, 1.0),
        'cache_slc_v': nrm(5, page_shape, 1.0),
        'cache_win_k': nrm(6, win_shape, 1.0),
        'cache_win_v': nrm(7, win_shape, 1.0),
        'state_gla': nrm(8, (DEPTH, DEC_BATCH, GLA_HEADS, GLA_HK, GLA_HV), 0.3),
        'state_ssm': nrm(9, (DEPTH, DEC_BATCH, M_HEADS, M_HD, M_STATE), 0.3),
        'state_conv': nrm(10, (DEPTH, DEC_BATCH, M_CONV - 1, M_CONVDIM), 1.0),
        'page_table': page_table,
        'ln_w': 1.0 + nrm(12, (DEPTH, D_MODEL), 0.02),
        'w_in': nrm(13, (DEPTH, D_MODEL, IN_COLS), D_MODEL ** -0.5),
        'cmp_pe_k': nrm(14, (DEPTH, CMP_LEN, NSA_HD), 0.02),
        'cmp_w1_k': nrm(15, (DEPTH, CMP_LEN * NSA_HD, CMP_HID), (CMP_LEN * NSA_HD) ** -0.5),
        'cmp_w2_k': nrm(16, (DEPTH, CMP_HID, NSA_HD), CMP_HID ** -0.5),
        'cmp_pe_v': nrm(17, (DEPTH, CMP_LEN, NSA_HD), 0.02),
        'cmp_w1_v': nrm(18, (DEPTH, CMP_LEN * NSA_HD, CMP_HID), (CMP_LEN * NSA_HD) ** -0.5),
        'cmp_w2_v': nrm(19, (DEPTH, CMP_HID, NSA_HD), CMP_HID ** -0.5),
        'gla_wa2': nrm(20, (DEPTH, GLA_RANK, GLA_DK), GLA_RANK ** -0.5),
        'gla_ba': nrm(21, (DEPTH, GLA_DK), 0.1),
        'gla_norm': 1.0 + nrm(22, (DEPTH, GLA_HV), 0.02),
        'conv_w': nrm(23, (DEPTH, M_CONV, M_CONVDIM), M_CONV ** -0.5),
        'conv_b': nrm(25, (DEPTH, M_CONVDIM), 0.02),
        'dt_bias': dt0 + jnp.log(-jnp.expm1(-dt0)),
        'a_log': jnp.log(jax.random.uniform(keys[26], (DEPTH, M_HEADS), jnp.float32, 1.0, 16.0)),
        'd_skip': 1.0 + nrm(27, (DEPTH, M_HEADS), 0.02),
        'm_norm': 1.0 + nrm(28, (DEPTH, M_DINNER), 0.02),
        'w_br_a': nrm(29, (DEPTH, NSA_HEADS * NSA_HD, D_MODEL), (NSA_HEADS * NSA_HD) ** -0.5),
        'w_br_b': nrm(30, (DEPTH, GLA_DV, D_MODEL), GLA_DV ** -0.5),
        'w_br_c': nrm(31, (DEPTH, M_DINNER, D_MODEL), M_DINNER ** -0.5),
        'w_out': nrm(32, (DEPTH, D_MODEL, D_MODEL), D_MODEL ** -0.5),
        'final_norm': 1.0 + nrm(33, (D_MODEL,), 0.02),
    }


def reference(x_prompt, x_sample, cache_cmp_k, cache_cmp_v, cache_slc_k, cache_slc_v,
              cache_win_k, cache_win_v, state_gla, state_ssm, state_conv, page_table,
              ln_w, w_in, cmp_pe_k, cmp_w1_k, cmp_w2_k, cmp_pe_v, cmp_w1_v, cmp_w2_v,
              gla_wa2, gla_ba, gla_norm, conv_w, conv_b, dt_bias, a_log, d_skip, m_norm,
              w_br_a, w_br_b, w_br_c, w_out, final_norm):
    xp, xs = x_prompt, x_sample
    bp = x_prompt.shape[0]
    p_lists = [[] for _ in range(9)]
    s_lists = [[] for _ in range(9)]
    for l in range(DEPTH):
        cmp_l = (cmp_pe_k[l], cmp_w1_k[l], cmp_w2_k[l], cmp_pe_v[l], cmp_w1_v[l], cmp_w2_v[l])
        shared = (ln_w[l], w_in[l], gla_wa2[l], gla_ba[l], gla_norm[l], conv_w[l], conv_b[l],
                  dt_bias[l], a_log[l], d_skip[l], m_norm[l], w_br_a[l], w_br_b[l], w_br_c[l], w_out[l])
        attend_p = functools.partial(nsa_prompt, cmp_l=cmp_l)
        xp, nsa_p, gla_p, ssm_p, conv_p = layer_forward(
            xp, attend_p,
            jnp.zeros((bp, GLA_HEADS, GLA_HK, GLA_HV), jnp.float32),
            jnp.zeros((bp, M_HEADS, M_HD, M_STATE), jnp.float32),
            jnp.zeros((bp, M_CONV - 1, M_CONVDIM), xp.dtype), *shared)
        past = (gather_past(cache_cmp_k[l], page_table), gather_past(cache_cmp_v[l], page_table),
                gather_past(cache_slc_k[l], page_table), gather_past(cache_slc_v[l], page_table),
                cache_win_k[l], cache_win_v[l])
        attend_s = functools.partial(nsa_sample, cmp_l=cmp_l, past=past)
        xs, nsa_s, gla_s, ssm_s, conv_s = layer_forward(
            xs, attend_s, state_gla[l], state_ssm[l], state_conv[l], *shared)
        for i, a in enumerate(nsa_p + (gla_p, ssm_p, conv_p)):
            p_lists[i].append(a)
        for i, a in enumerate(nsa_s + (gla_s, ssm_s, conv_s)):
            s_lists[i].append(a)
    y_prompt = rmsnorm(xp, final_norm)
    y_sample = rmsnorm(xs, final_norm)
    (p_cmp_k, p_cmp_v, p_slc_k, p_slc_v, p_win_k, p_win_v, p_gla, p_ssm, p_conv) = [jnp.stack(a) for a in p_lists]
    (s_cmp_k, s_cmp_v, s_slc_k, s_slc_v, s_win_k, s_win_v, s_gla, s_ssm, s_conv) = [jnp.stack(a) for a in s_lists]
    return (y_prompt, y_sample,
            p_cmp_k, p_cmp_v, p_slc_k, p_slc_v, p_win_k, p_win_v, p_gla, p_ssm, p_conv,
            s_cmp_k, s_cmp_v, s_slc_k, s_slc_v, s_win_k, s_win_v, s_gla, s_ssm, s_conv)
```

```python
import functools
import math

import jax
import jax.numpy as jnp
import numpy as np
from jax import lax
from jax.experimental import pallas as pl
from jax.experimental.pallas import tpu as pltpu

F32 = jnp.float32
BF16 = jnp.bfloat16
HI = lax.Precision.HIGHEST

D_MODEL = 2048
DEPTH = 4
PAGE = 128
MIX_W = D_MODEL // 2
HD = 64
NSA_HEADS = MIX_W // HD
NSA_KV = 4
NSA_REP = NSA_HEADS // NSA_KV
CMP_LEN = 32
CMP_STRIDE = 16
SLC_BLK = 64
SLC_TOPK = 16
WINDOW = 512
GLA_HEADS = 4
GLA_DK = MIX_W // 2
GLA_DV = MIX_W
GLA_HK = GLA_DK // GLA_HEADS
GLA_HV = GLA_DV // GLA_HEADS
GLA_RANK = 16
GLA_TAU = 16.0
M_DINNER = MIX_W
M_HD = 64
M_HEADS = M_DINNER // M_HD
M_GROUPS = 4
M_STATE = 128
M_CONV = 4
M_CONVDIM = M_DINNER + 2 * M_GROUPS * M_STATE
EPS = 1e-6
NEG = -1e30
FORCE = 1e4

_IN_NAMES = ("a_q", "a_kv", "a_g", "a_z", "b_q", "b_k", "b_v", "b_a", "b_z", "c_xbc", "c_dt", "c_z", "m_g")
_IN_SIZES = (NSA_HEADS * HD, 6 * NSA_KV * HD, 3 * NSA_HEADS, MIX_W, GLA_DK, GLA_DK, GLA_DV, GLA_RANK, GLA_DV,
             M_CONVDIM, M_HEADS, M_DINNER, 3 * D_MODEL)
_IN_OFF = dict(zip(_IN_NAMES, np.cumsum((0,) + _IN_SIZES)[:-1].tolist()))
_IN_SZ = dict(zip(_IN_NAMES, _IN_SIZES))

_MAIN_ORDER = ("m_g", "c_xbc", "a_q", "a_z", "b_v", "b_z", "c_z", "b_q", "b_k", "a_g", "b_a", "c_dt")
_MAIN_OFF = {}
_o = 0
for _n in _MAIN_ORDER:
    _MAIN_OFF[_n] = _o
    _o += _IN_SZ[_n]
OFF_SMALL = _MAIN_OFF["a_g"]
SM_AG, SM_BA, SM_DT = 0, 3 * NSA_HEADS, 3 * NSA_HEADS + GLA_RANK
LANE = 128
MAIN_TN = 512
NP_MAIN = -(-(OFF_SMALL + LANE) // MAIN_TN) * MAIN_TN
VMEM_LIMIT = 56 * 1024 * 1024


def _cparams(sem):
    return pltpu.CompilerParams(dimension_semantics=sem, vmem_limit_bytes=VMEM_LIMIT)


def _dot(a, b, prec=None, nt=False, tn=False):
    if prec is None:
        a, b = a.astype(BF16), b.astype(BF16)
    dn = (((0,) if tn else (1,),), ((1,) if nt else (0,),))
    dn = ((dn[0][0], dn[1][0]), ((), ()))
    return lax.dot_general(a, b, dn, precision=prec, preferred_element_type=F32)


def _sigmoid(x):
    return 1.0 / (1.0 + jnp.exp(-x))


def _silu(x):
    return x * _sigmoid(x)


def _log_sigmoid(x):
    return jnp.minimum(x, 0.0) - jnp.log(1.0 + jnp.exp(-jnp.abs(x)))


def _softplus(x):
    return jnp.maximum(x, 0.0) + jnp.log(1.0 + jnp.exp(-jnp.abs(x)))


def _rms_kernel(x_ref, w_ref, o_ref):
    x = x_ref[...]
    y = x * lax.rsqrt(jnp.mean(x * x, axis=-1, keepdims=True) + EPS)
    o_ref[...] = (y * w_ref[...]).astype(o_ref.dtype)


def rmsnorm_rows(x, w, out_dtype, tm):
    m, d = x.shape
    return pl.pallas_call(
        _rms_kernel,
        grid=(m // tm,),
        in_specs=[pl.BlockSpec((tm, d), lambda i: (i, 0)), pl.BlockSpec((1, d), lambda i: (0, 0))],
        out_specs=pl.BlockSpec((tm, d), lambda i: (i, 0)),
        out_shape=jax.ShapeDtypeStruct((m, d), out_dtype),
        compiler_params=_cparams(("parallel",)),
        name="rmsnorm",
    )(x, w.reshape(1, d))


def _mm_kernel(*refs, nt, has_res):
    a_ref, b_ref, o_ref = refs[0], refs[1], refs[-1]
    acc = _dot(a_ref[...], b_ref[...], nt=nt)
    if has_res:
        acc = acc + refs[2][...]
    o_ref[...] = acc.astype(o_ref.dtype)


def matmul(a, b, *, nt, tm, tn, res=None, out_dtype=F32, name="matmul"):
    m, k = a.shape
    n = b.shape[0] if nt else b.shape[1]
    tm, tn = min(tm, m), min(tn, n)
    b_spec = pl.BlockSpec((tn, k), lambda i, j: (j, 0)) if nt else pl.BlockSpec((k, tn), lambda i, j: (0, j))
    in_specs = [pl.BlockSpec((tm, k), lambda i, j: (i, 0)), b_spec]
    args = [a, b]
    if res is not None:
        in_specs.append(pl.BlockSpec((tm, tn), lambda i, j: (i, j)))
        args.append(res)
    return pl.pallas_call(
        functools.partial(_mm_kernel, nt=nt, has_res=res is not None),
        grid=(m // tm, n // tn),
        in_specs=in_specs,
        out_specs=pl.BlockSpec((tm, tn), lambda i, j: (i, j)),
        out_shape=jax.ShapeDtypeStruct((m, n), out_dtype),
        compiler_params=_cparams(("parallel", "parallel")),
        name=name,
    )(*args)


def matmul_kvt(wt_kv, h, nb, t, *, tm, tn):
    r, k = wt_kv.shape
    tn = min(tn, t)
    nj = t // tn
    return pl.pallas_call(
        functools.partial(_mm_kernel, nt=True, has_res=False),
        grid=(nb, r // tm, nj),
        in_specs=[pl.BlockSpec((tm, k), lambda b, i, j: (i, 0)),
                  pl.BlockSpec((tn, k), lambda b, i, j: (b * nj + j, 0))],
        out_specs=pl.BlockSpec((None, tm, tn), lambda b, i, j: (b, i, j)),
        out_shape=jax.ShapeDtypeStruct((nb, r, t), F32),
        compiler_params=_cparams(("parallel", "parallel", "parallel")),
        name="kv_proj_t",
    )(wt_kv, h)


def _merge_kernel(oa_ref, ob_ref, oc_ref, wa_ref, wb_ref, wc_ref, g0_ref, g1_ref, g2_ref, o_ref):
    acc = _sigmoid(g0_ref[...]) * _dot(oa_ref[...], wa_ref[...])
    acc = acc + _sigmoid(g1_ref[...]) * _dot(ob_ref[...], wb_ref[...])
    acc = acc + _sigmoid(g2_ref[...]) * _dot(oc_ref[...], wc_ref[...])
    o_ref[...] = acc.astype(o_ref.dtype)


def merge_branches(o_a, o_b, o_c, wa, wb, wc, u, *, tm, tn):
    m, k = o_a.shape
    n = wa.shape[1]
    tm = min(tm, m)
    gb = _MAIN_OFF["m_g"] // tn
    nj = n // tn
    o_spec = pl.BlockSpec((tm, k), lambda i, j: (i, 0))
    w_spec = pl.BlockSpec((k, tn), lambda i, j: (0, j))
    g_specs = [pl.BlockSpec((tm, tn), functools.partial(lambda i, j, br: (i, gb + br * nj + j), br=br))
               for br in range(3)]
    return pl.pallas_call(
        _merge_kernel,
        grid=(m // tm, nj),
        in_specs=[o_spec, o_spec, o_spec, w_spec, w_spec, w_spec] + g_specs,
        out_specs=pl.BlockSpec((tm, tn), lambda i, j: (i, j)),
        out_shape=jax.ShapeDtypeStruct((m, n), BF16),
        compiler_params=_cparams(("parallel", "parallel")),
        name="merge",
    )(o_a, o_b, o_c, wa, wb, wc, u, u, u)


def _cmp_kernel(seg_ref, w1_ref, pe_ref, w2_ref, o_ref, *, nseg):
    w1 = w1_ref[...]
    ab = _dot(seg_ref[...], w1, HI)
    pe = _dot(pe_ref[...], w1, HI)
    cst = pe[0:1, :HD] + pe[1:2, HD:]
    nxt = pltpu.roll(ab[:, HD:], shift=nseg - 1, axis=0)
    pre = ab[:, :HD] + nxt + cst
    o_ref[...] = _dot(_silu(pre), w2_ref[...], HI)


def compress_blocks(seg, w1cat, pe2, w2, nseg):
    _, r, kdim = seg.shape
    return pl.pallas_call(
        functools.partial(_cmp_kernel, nseg=nseg),
        grid=(2, r // nseg),
        in_specs=[pl.BlockSpec((None, nseg, kdim), lambda s, i: (s, i, 0)),
                  pl.BlockSpec((None, kdim, 2 * HD), lambda s, i: (s, 0, 0)),
                  pl.BlockSpec((None, 8, kdim), lambda s, i: (s, 0, 0)),
                  pl.BlockSpec((None, HD, HD), lambda s, i: (s, 0, 0))],
        out_specs=pl.BlockSpec((None, nseg, HD), lambda s, i: (s, i, 0)),
        out_shape=jax.ShapeDtypeStruct((2, r, HD), F32),
        compiler_params=_cparams(("parallel", "parallel")),
        name="compress",
    )(seg, w1cat, pe2, w2)


def _masked_softmax_rows(s, mask):
    s = jnp.where(mask, s, NEG)
    m = jnp.max(s, axis=-1, keepdims=True)
    p = jnp.where(mask, jnp.exp(s - m), 0.0)
    l = jnp.sum(p, axis=-1, keepdims=True)
    return p / jnp.where(l > 0.0, l, 1.0)


def _topk_mask(score, sidx, ns, topk):
    rank = jnp.zeros(score.shape, F32)
    for j in range(ns):
        sj = score[:, j:j + 1]
        rank = rank + jnp.where((sj > score) | ((sj == score) & (j < sidx)), 1.0, 0.0)
    return (rank < float(topk)) & (sidx < ns)


def _nsa_prompt_kernel(q_ref, az_ref, sm_ref, kst_ref, vst_ref, kwt_ref, vwt_ref, kc_ref, vc_ref, o_ref,
                       *, t, tq, ck, nseg):
    g = pl.program_id(1)
    s0 = pl.program_id(2) * tq
    ns = t // SLC_BLK
    rows = NSA_REP * tq
    q4 = q_ref[...] * (HD ** -0.5)
    qs = jnp.concatenate([q4[:, r * HD:(r + 1) * HD] for r in range(NSA_REP)], axis=0)
    qb = qs.astype(BF16)
    tpos1 = s0 + lax.broadcasted_iota(jnp.int32, (tq, 1), 0)
    tpos = jnp.concatenate([tpos1] * NSA_REP, axis=0)

    kc = kc_ref[...]
    cidx = lax.broadcasted_iota(jnp.int32, (1, nseg), 1)
    m_c = (cidx * CMP_STRIDE + (CMP_LEN - 1) <= tpos) & (cidx < nseg - 1)
    p_c = _masked_softmax_rows(_dot(qs, kc, HI, nt=True), m_c)
    o_c = _dot(p_c, vc_ref[...])
    psum = p_c[0:tq]
    for r in range(1, NSA_REP):
        psum = psum + p_c[r * tq:(r + 1) * tq]

    crow = lax.broadcasted_iota(jnp.int32, (nseg, LANE), 0) * CMP_STRIDE
    scol = lax.broadcasted_iota(jnp.int32, (nseg, LANE), 1) * SLC_BLK
    cover = jnp.where((crow + CMP_LEN > scol) & (crow < scol + SLC_BLK), 1.0, 0.0)
    imp = _dot(psum, cover, HI)
    sidx = lax.broadcasted_iota(jnp.int32, (1, LANE), 1)
    cur = tpos1 // SLC_BLK
    valid = (sidx * SLC_BLK <= tpos1) & (sidx < ns)
    forced = (sidx == 0) | (sidx == cur) | (sidx == cur - 1)
    score = jnp.where(valid, jnp.where(forced, FORCE, imp), NEG)
    sel = _topk_mask(score, sidx, ns, min(SLC_TOPK, ns))
    sel_b = jnp.where(sel, 1.0, 0.0).astype(BF16)
    srow = lax.broadcasted_iota(jnp.int32, (LANE, ck), 0)

    def sel_chunk(j, carry):
        m, l, acc = carry
        k0 = pl.multiple_of(j * ck, ck)
        pos = k0 + lax.broadcasted_iota(jnp.int32, (1, ck), 1)
        expand = jnp.where(srow == pos // SLC_BLK, 1.0, 0.0).astype(BF16)
        member = _dot(sel_b, expand) > 0.5
        msk1 = member & (pos <= tpos1)
        msk = jnp.concatenate([msk1] * NSA_REP, axis=0)
        s = jnp.where(msk, _dot(qb, kst_ref[:, pl.ds(k0, ck)]), NEG)
        m_new = jnp.maximum(m, jnp.max(s, axis=-1, keepdims=True))
        p = jnp.where(msk, jnp.exp(s - m_new), 0.0)
        alpha = jnp.exp(m - m_new)
        l = alpha * l + jnp.sum(p, axis=-1, keepdims=True)
        acc = alpha * acc + _dot(p, vst_ref[:, pl.ds(k0, ck)], nt=True)
        return m_new, l, acc

    n_chunks = (s0 + tq + ck - 1) // ck
    _, l_s, acc_s = lax.fori_loop(
        0, n_chunks, sel_chunk,
        (jnp.full((rows, 1), NEG, F32), jnp.zeros((rows, 1), F32), jnp.zeros((rows, HD), F32)))
    o_s = acc_s / jnp.where(l_s > 0.0, l_s, 1.0)

    ww = min(WINDOW + tq, t)
    w0 = pl.multiple_of(jnp.clip(s0 - WINDOW, 0, t - ww), LANE)
    wpos = w0 + lax.broadcasted_iota(jnp.int32, (1, ww), 1)
    m_w = (wpos <= tpos) & (wpos > tpos - WINDOW)
    p_w = _masked_softmax_rows(_dot(qb, kwt_ref[:, pl.ds(w0, ww)]), m_w)
    o_w = _dot(p_w, vwt_ref[:, pl.ds(w0, ww)], nt=True)

    gates = _sigmoid(pltpu.roll(sm_ref[...], shift=(LANE - 3 * NSA_REP * g) % LANE, axis=1))
    outs = []
    for r in range(NSA_REP):
        sl = slice(r * tq, (r + 1) * tq)
        outs.append(gates[:, 3 * r:3 * r + 1] * o_c[sl] + gates[:, 3 * r + 1:3 * r + 2] * o_s[sl]
                    + gates[:, 3 * r + 2:3 * r + 3] * o_w[sl])
    o = jnp.concatenate(outs, axis=1) * _silu(az_ref[...])
    o_ref[...] = o.astype(o_ref.dtype)


def nsa_prompt(u, kvt, kc, nb, t, *, tq, ck):
    nq = t // tq
    nseg = t // CMP_STRIDE
    ck = min(ck, t)
    gw = NSA_REP * HD

    def kv_spec(branch):
        return pl.BlockSpec((None, HD, t), lambda b, g, i: (b, branch * NSA_KV + g, 0))

    return pl.pallas_call(
        functools.partial(_nsa_prompt_kernel, t=t, tq=tq, ck=ck, nseg=nseg),
        grid=(nb, NSA_KV, nq),
        in_specs=[pl.BlockSpec((tq, gw), lambda b, g, i: (b * nq + i, _MAIN_OFF["a_q"] // gw + g)),
                  pl.BlockSpec((tq, gw), lambda b, g, i: (b * nq + i, _MAIN_OFF["a_z"] // gw + g)),
                  pl.BlockSpec((tq, LANE), lambda b, g, i: (b * nq + i, OFF_SMALL // LANE)),
                  kv_spec(2), kv_spec(3), kv_spec(4), kv_spec(5),
                  pl.BlockSpec((None, nseg, HD), lambda b, g, i: (0, b * NSA_KV + g, 0)),
                  pl.BlockSpec((None, nseg, HD), lambda b, g, i: (1, b * NSA_KV + g, 0))],
        out_specs=pl.BlockSpec((tq, gw), lambda b, g, i: (b * nq + i, g)),
        out_shape=jax.ShapeDtypeStruct((nb * t, NSA_HEADS * HD), BF16),
        compiler_params=_cparams(("parallel", "parallel", "arbitrary")),
        name="nsa_prompt",
    )(u, u, u, kvt, kvt, kvt, kvt, kc, kc)


def _gla_kernel(q_ref, k_ref, v_ref, z_ref, sm_ref, s0_ref, wa2_ref, ba_ref, gn_ref, o_ref, sout_ref, s_scr,
                *, c, t_valid):
    ci = pl.program_id(1)

    @pl.when(ci == 0)
    def _():
        s_scr[...] = s0_ref[...]

    row = ci * c + lax.broadcasted_iota(jnp.int32, (c, 1), 0)
    gate_in = sm_ref[...][:, SM_BA:SM_BA + GLA_RANK]
    log_a = _log_sigmoid(_dot(gate_in, wa2_ref[...], HI) + ba_ref[...]) / GLA_TAU
    log_a = jnp.where(row < t_valid, log_a, 0.0)
    tri = jnp.where(lax.broadcasted_iota(jnp.int32, (c, c), 0) >= lax.broadcasted_iota(jnp.int32, (c, c), 1), 1.0, 0.0)
    causal = tri > 0.5
    for h in range(GLA_HEADS):
        ks = slice(h * GLA_HK, (h + 1) * GLA_HK)
        vs = slice(h * GLA_HV, (h + 1) * GLA_HV)
        cb = _dot(tri, log_a[:, ks], HI)
        ecb = jnp.exp(cb)
        qd = q_ref[...][:, ks] * (GLA_HK ** -0.5) * ecb
        kh = k_ref[...][:, ks]
        vh = v_ref[...][:, vs]
        att = jnp.where(causal, _dot(qd, kh * jnp.exp(-cb), nt=True), 0.0)
        s_prev = s_scr[h]
        o = _dot(att, vh) + _dot(qd, s_prev)
        c_end = cb[c - 1:c, :]
        k_dec = kh * jnp.exp(c_end - cb)
        e_col = jnp.transpose(ecb)[:, c - 1:c]
        s_scr[h] = e_col * s_prev + _dot(jnp.transpose(k_dec), vh)
        y = o * lax.rsqrt(jnp.mean(o * o, axis=-1, keepdims=True) + EPS) * gn_ref[...]
        o_ref[:, vs] = (y * _silu(z_ref[...][:, vs])).astype(o_ref.dtype)

    @pl.when(ci == pl.num_programs(1) - 1)
    def _():
        sout_ref[...] = s_scr[...]


def gla_mixer(u, s0, wa2, ba, gnorm, nb, t, *, c, t_valid):
    nc = t // c

    def u_spec(name, width):
        return pl.BlockSpec((c, width), lambda b, i: (b * nc + i, _MAIN_OFF[name] // width))

    s_spec = pl.BlockSpec((None, GLA_HEADS, GLA_HK, GLA_HV), lambda b, i: (b, 0, 0, 0))
    return pl.pallas_call(
        functools.partial(_gla_kernel, c=c, t_valid=t_valid),
        grid=(nb, nc),
        in_specs=[u_spec("b_q", GLA_DK), u_spec("b_k", GLA_DK), u_spec("b_v", GLA_DV), u_spec("b_z", GLA_DV),
                  pl.BlockSpec((c, LANE), lambda b, i: (b * nc + i, OFF_SMALL // LANE)),
                  s_spec,
                  pl.BlockSpec((GLA_RANK, GLA_DK), lambda b, i: (0, 0)),
                  pl.BlockSpec((1, GLA_DK), lambda b, i: (0, 0)),
                  pl.BlockSpec((1, GLA_HV), lambda b, i: (0, 0))],
        out_specs=[pl.BlockSpec((c, GLA_DV), lambda b, i: (b * nc + i, 0)), s_spec],
        out_shape=[jax.ShapeDtypeStruct((nb * t, GLA_DV), BF16),
                   jax.ShapeDtypeStruct((nb, GLA_HEADS, GLA_HK, GLA_HV), F32)],
        scratch_shapes=[pltpu.VMEM((GLA_HEADS, GLA_HK, GLA_HV), F32)],
        compiler_params=_cparams(("parallel", "arbitrary")),
        name="gla",
    )(u, u, u, u, u, s0, wa2, ba.reshape(1, GLA_DK), gnorm.reshape(1, GLA_HV))


def _ssd_kernel(xbc_ref, z_ref, sm_ref, s0_ref, cbuf_ref, cw_ref, cbias_ref, dtb_ref, alog_ref, dsk_ref, mn_ref,
                o_ref, sout_ref, cout_ref, s_scr, f_scr, y_scr, *, q, t_valid, n_chunks):
    ci = pl.program_id(1)
    npad = 8
    hist = M_CONV - 1

    @pl.when(ci == 0)
    def _():
        s_scr[...] = s0_ref[...]
        f_scr[npad - hist:npad, :] = cbuf_ref[...]

    f_scr[npad:npad + q, :] = xbc_ref[...]
    conv = cbias_ref[...]
    for i in range(M_CONV):
        conv = conv + f_scr[npad - hist + i:npad - hist + i + q, :] * cw_ref[i:i + 1, :]
    xbc = _silu(conv)

    last_valid = t_valid - (n_chunks - 1) * q

    @pl.when(ci == n_chunks - 1)
    def _():
        cout_ref[...] = f_scr[npad - hist + last_valid:npad + last_valid, :]

    f_scr[npad - hist:npad, :] = f_scr[npad - hist + q:npad + q, :]

    lane = lax.broadcasted_iota(jnp.int32, (1, LANE), 1)
    row = ci * q + lax.broadcasted_iota(jnp.int32, (q, 1), 0)
    head_lane = (lane >= SM_DT) & (lane < SM_DT + M_HEADS)
    dt = jnp.where(head_lane & (row < t_valid), _softplus(sm_ref[...] + dtb_ref[...]), 0.0)
    a = dt * (-jnp.exp(alog_ref[...]))
    tri = jnp.where(lax.broadcasted_iota(jnp.int32, (q, q), 0) >= lax.broadcasted_iota(jnp.int32, (q, q), 1), 1.0, 0.0)
    causal = tri > 0.5
    cum = _dot(tri, a, HI)
    cum_t = jnp.transpose(cum)
    dt_t = jnp.transpose(dt)
    nbc = M_GROUPS * M_STATE
    rep = M_HEADS // M_GROUPS
    for g in range(M_GROUPS):
        bg = xbc[:, M_DINNER + g * M_STATE:M_DINNER + (g + 1) * M_STATE]
        cg = xbc[:, M_DINNER + nbc + g * M_STATE:M_DINNER + nbc + (g + 1) * M_STATE]
        cbm = _dot(cg, bg, nt=True)
        for r in range(rep):
            h = g * rep + r
            hl = SM_DT + h
            xh = xbc[:, h * M_HD:(h + 1) * M_HD]
            cum_c = cum[:, hl:hl + 1]
            decay = jnp.exp(jnp.where(causal, cum_c - cum_t[hl:hl + 1, :], -jnp.inf))
            s_prev = s_scr[h]
            y = _dot(cbm * decay * dt_t[hl:hl + 1, :], xh)
            y = y + jnp.exp(cum_c) * _dot(cg, s_prev, nt=True)
            c_end = cum[q - 1:q, hl:hl + 1]
            w = jnp.exp(c_end - cum_c) * dt[:, hl:hl + 1]
            s_scr[h] = jnp.exp(c_end) * s_prev + _dot(xh * w, bg, tn=True)
            y_scr[:, h * M_HD:(h + 1) * M_HD] = y + dsk_ref[:, hl:hl + 1] * xh
    gw = M_DINNER // M_GROUPS
    for g in range(M_GROUPS):
        sl = slice(g * gw, (g + 1) * gw)
        yg = y_scr[:, sl] * _silu(z_ref[...][:, sl])
        yn = yg * lax.rsqrt(jnp.mean(yg * yg, axis=-1, keepdims=True) + EPS) * mn_ref[:, sl]
        o_ref[:, sl] = yn.astype(o_ref.dtype)

    @pl.when(ci == n_chunks - 1)
    def _():
        sout_ref[...] = s_scr[...]


def _pad_heads(v):
    return jnp.zeros((1, LANE), F32).at[0, SM_DT:SM_DT + M_HEADS].set(v)


def ssd_mixer(u, s0, cbuf, conv_w, conv_b, dt_bias, a_log, d_skip, m_norm, nb, t, *, q, t_valid):
    nc = t // q
    n_chunks = -(-t_valid // q)
    assert n_chunks == nc
    hist = M_CONV - 1

    def u_spec(name, width):
        return pl.BlockSpec((q, width), lambda b, i: (b * nc + i, _MAIN_OFF[name] // width))

    def full(shape):
        return pl.BlockSpec(shape, lambda b, i: (0,) * len(shape))

    s_spec = pl.BlockSpec((None, M_HEADS, M_HD, M_STATE), lambda b, i: (b, 0, 0, 0))
    c_spec = pl.BlockSpec((None, hist, M_CONVDIM), lambda b, i: (b, 0, 0))
    return pl.pallas_call(
        functools.partial(_ssd_kernel, q=q, t_valid=t_valid, n_chunks=n_chunks),
        grid=(nb, nc),
        in_specs=[u_spec("c_xbc", M_CONVDIM), u_spec("c_z", M_DINNER),
                  pl.BlockSpec((q, LANE), lambda b, i: (b * nc + i, OFF_SMALL // LANE)),
                  s_spec, c_spec,
                  full((M_CONV, M_CONVDIM)), full((1, M_CONVDIM)), full((1, LANE)), full((1, LANE)),
                  full((1, LANE)), full((1, M_DINNER))],
        out_specs=[pl.BlockSpec((q, M_DINNER), lambda b, i: (b * nc + i, 0)), s_spec, c_spec],
        out_shape=[jax.ShapeDtypeStruct((nb * t, M_DINNER), BF16),
                   jax.ShapeDtypeStruct((nb, M_HEADS, M_HD, M_STATE), F32),
                   jax.ShapeDtypeStruct((nb, hist, M_CONVDIM), F32)],
        scratch_shapes=[pltpu.VMEM((M_HEADS, M_HD, M_STATE), F32),
                        pltpu.VMEM((q + 8, M_CONVDIM), F32),
                        pltpu.VMEM((q, M_DINNER), F32)],
        compiler_params=_cparams(("parallel", "arbitrary")),
        name="ssd",
    )(u, u, u, s0, cbuf, conv_w, conv_b.reshape(1, M_CONVDIM), _pad_heads(dt_bias), _pad_heads(a_log),
      _pad_heads(d_skip), m_norm.reshape(1, M_DINNER))


def _prep_weights(w_in, cmp_pe_k, cmp_w1_k, cmp_w2_k, cmp_pe_v, cmp_w1_v, cmp_w2_v, w_br_a, w_br_b, w_br_c, w_out):
    wt = jnp.transpose(w_in, (0, 2, 1))
    parts = [wt[:, _IN_OFF[n]:_IN_OFF[n] + _IN_SZ[n]] for n in _MAIN_ORDER]
    used = sum(_IN_SZ[n] for n in _MAIN_ORDER)
    parts.append(jnp.zeros((wt.shape[0], NP_MAIN - used, wt.shape[2]), wt.dtype))
    wt_main = jnp.concatenate(parts, axis=1).astype(BF16)
    wt_kv = wt[:, _IN_OFF["a_kv"]:_IN_OFF["a_kv"] + _IN_SZ["a_kv"]].astype(BF16)
    half = CMP_STRIDE * HD

    def cat(w1):
        return jnp.concatenate([w1[:, :half], w1[:, half:]], axis=2)

    def pe_rows(pe):
        d = pe.shape[0]
        rows = pe.reshape(d, 2, half)
        return jnp.concatenate([rows, jnp.zeros((d, 6, half), pe.dtype)], axis=1)

    w1cat = jnp.stack([cat(cmp_w1_k), cat(cmp_w1_v)], axis=1)
    pe2 = jnp.stack([pe_rows(cmp_pe_k), pe_rows(cmp_pe_v)], axis=1)
    w2 = jnp.stack([cmp_w2_k, cmp_w2_v], axis=1)
    return dict(wt_main=wt_main, wt_kv=wt_kv, w1cat=w1cat, pe2=pe2, w2=w2, wbd=_block_diag_w1(w1cat),
                wa=w_br_a.astype(BF16), wb=w_br_b.astype(BF16), wc=w_br_c.astype(BF16), wo=w_out.astype(BF16))


def _prompt_layer(x, nb, t, w, p, tiles):
    h = rmsnorm_rows(x, p["ln_w"], BF16, tiles["rms_tm"])
    u = matmul(h, w["wt_main"], nt=True, tm=tiles["mm_tm"], tn=MAIN_TN, name="in_proj")
    kvt = matmul_kvt(w["wt_kv"], h, nb, t, tm=tiles["kv_tm"], tn=tiles["kv_tn"])
    nseg = t // CMP_STRIDE
    kv5 = kvt.reshape(nb, 6, NSA_KV, HD, t)
    seg = kv5[:, 0:2].reshape(nb, 2, NSA_KV, HD, nseg, CMP_STRIDE)
    seg = jnp.transpose(seg, (1, 0, 2, 4, 5, 3)).reshape(2, nb * NSA_KV * nseg, CMP_STRIDE * HD)
    kc = compress_blocks(seg, w["w1cat"], w["pe2"], w["w2"], nseg)
    o_a = nsa_prompt(u, kvt, kc, nb, t, tq=tiles["tq"], ck=tiles["ck"])
    zeros_gla = jnp.zeros((nb, GLA_HEADS, GLA_HK, GLA_HV), F32)
    zeros_ssm = jnp.zeros((nb, M_HEADS, M_HD, M_STATE), F32)
    zeros_conv = jnp.zeros((nb, M_CONV - 1, M_CONVDIM), F32)
    o_b, gla_s = gla_mixer(u, zeros_gla, p["gla_wa2"], p["gla_ba"], p["gla_norm"], nb, t,
                           c=tiles["gla_c"], t_valid=t)
    o_c, ssm_s, conv_s = ssd_mixer(u, zeros_ssm, zeros_conv, p["conv_w"], p["conv_b"], p["dt_bias"], p["a_log"],
                                   p["d_skip"], p["m_norm"], nb, t, q=tiles["ssd_q"], t_valid=t)
    mix = merge_branches(o_a, o_b, o_c, w["wa"], w["wb"], w["wc"], u, tm=tiles["mg_tm"], tn=tiles["mg_tn"])
    x_out = matmul(mix, w["wo"], nt=False, tm=tiles["mm_tm"], tn=tiles["out_tn"], res=x, name="out_proj")
    n_w = min(WINDOW, t)
    return x_out, (kv5[:, 0], kv5[:, 1], kv5[:, 2], kv5[:, 3], kv5[:, 4, :, :, t - n_w:], kv5[:, 5, :, :, t - n_w:],
                   gla_s, ssm_s, conv_s)


SEG_PER_PAGE = PAGE // CMP_STRIDE


def _page_cmp_kernel(pt_ref, *refs, npg):
    wk_ref, wv_ref, ok_ref, ov_ref, scr = refs[2 * npg:]
    n_lane_blk = NSA_KV * HD // LANE
    for pages, w_ref, o_ref in ((refs[:npg], wk_ref, ok_ref), (refs[npg:2 * npg], wv_ref, ov_ref)):
        for k in range(npg):
            page_t = pages[k][...].reshape(NSA_KV * HD, PAGE)
            pos_major = jnp.transpose(page_t)
            for c in range(n_lane_blk):
                scr[c, k * PAGE:(k + 1) * PAGE, :] = pos_major[:, c * LANE:(c + 1) * LANE]
        acc = jnp.zeros((npg * SEG_PER_PAGE, 2 * HD * NSA_KV), F32)
        for j in range(CMP_STRIDE):
            rows = pl.ds(j, npg * SEG_PER_PAGE, stride=CMP_STRIDE)
            xj = jnp.concatenate([scr[c, rows, :] for c in range(n_lane_blk)], axis=1)
            acc = acc + _dot(xj, w_ref[j])
        o_ref[...] = acc


def page_compress(cache_k_t, cache_v_t, page_table, layer, wbd_k, wbd_v, *, npg):
    nb, n_pages = page_table.shape
    steps = n_pages // npg

    def page_spec(k):
        return pl.BlockSpec((None, None, NSA_KV, HD, PAGE),
                            lambda b, i, pt: (layer, pt[b, i * npg + k], 0, 0, 0))

    w_spec = pl.BlockSpec((CMP_STRIDE, NSA_KV * HD, NSA_KV * 2 * HD), lambda b, i, pt: (0, 0, 0))
    o_spec = pl.BlockSpec((None, npg * SEG_PER_PAGE, NSA_KV * 2 * HD), lambda b, i, pt: (b, i, 0))
    o_shape = jax.ShapeDtypeStruct((nb, n_pages * SEG_PER_PAGE, NSA_KV * 2 * HD), F32)
    return pl.pallas_call(
        functools.partial(_page_cmp_kernel, npg=npg),
        grid_spec=pltpu.PrefetchScalarGridSpec(
            num_scalar_prefetch=1,
            grid=(nb, steps),
            in_specs=[page_spec(k) for k in range(npg)] * 2 + [w_spec, w_spec],
            out_specs=[o_spec, o_spec],
            scratch_shapes=[pltpu.VMEM((NSA_KV * HD // LANE, npg * PAGE, LANE), F32)]),
        out_shape=[o_shape, o_shape],
        compiler_params=_cparams(("parallel", "arbitrary")),
        name="page_compress",
    )(page_table, *([cache_k_t] * npg), *([cache_v_t] * npg), wbd_k, wbd_v)


def _nsa_sample_cmp_kernel(abk_ref, abv_ref, kvs_ref, q_ref, w1_ref, pe_ref, w2_ref, oc_ref, sel_ref,
                           *, p_len, nc, sl):
    g = pl.program_id(1)
    ns = p_len // SLC_BLK + 1
    rowi = lax.broadcasted_iota(jnp.int32, (nc, 1), 0)

    def compressed(ab_ref, idx):
        ab = ab_ref[...]
        w1 = w1_ref[idx]
        pe = _dot(pe_ref[idx], w1, HI)
        cst = pe[0:1, :HD] + pe[1:2, HD:]
        new = kvs_ref[pl.ds(idx * NSA_KV + g, 1), :]
        b_new = _dot(new, w1[0:HD, HD:])
        nxt = jnp.where(rowi == nc - 1, b_new, pltpu.roll(ab[:, HD:], shift=nc - 1, axis=0))
        return _dot(_silu(ab[:, :HD] + nxt + cst), w2_ref[idx], HI)

    kc = compressed(abk_ref, 0)
    vc = compressed(abv_ref, 1)
    q4 = q_ref[...] * (HD ** -0.5)
    cidx = lax.broadcasted_iota(jnp.int32, (1, nc), 1)
    m_c = cidx * CMP_STRIDE + (CMP_LEN - 1) <= p_len
    p_c = _masked_softmax_rows(_dot(q4, kc, HI, nt=True), m_c)
    oc_ref[...] = _dot(p_c, vc)
    psum = jnp.sum(p_c, axis=0, keepdims=True)
    crow = lax.broadcasted_iota(jnp.int32, (nc, sl), 0) * CMP_STRIDE
    scol = lax.broadcasted_iota(jnp.int32, (nc, sl), 1) * SLC_BLK
    cover = jnp.where((crow + CMP_LEN > scol) & (crow < scol + SLC_BLK), 1.0, 0.0)
    imp = _dot(psum, cover, HI)
    sidx = lax.broadcasted_iota(jnp.int32, (1, sl), 1)
    cur = p_len // SLC_BLK
    valid = (sidx * SLC_BLK <= p_len) & (sidx < ns)
    forced = (sidx == 0) | (sidx == cur) | (sidx == cur - 1)
    score = jnp.where(valid, jnp.where(forced, FORCE, imp), NEG)
    ri = lax.broadcasted_iota(jnp.int32, (sl, sl), 0)
    li = lax.broadcasted_iota(jnp.int32, (sl, sl), 1)
    score_b = jnp.broadcast_to(score, (sl, sl))
    score_col = jnp.sum(jnp.where(ri == li, score_b, 0.0), axis=1, keepdims=True)
    ahead = (score_b > score_col) | ((score_b == score_col) & (li < ri))
    rank_col = jnp.sum(jnp.where(ahead, 1.0, 0.0), axis=1, keepdims=True)
    kl = lax.broadcasted_iota(jnp.int32, (sl, LANE), 1).astype(F32)
    sv = lax.broadcasted_iota(jnp.int32, (sl, LANE), 0).astype(F32)
    picked = jnp.sum(jnp.where(rank_col == kl, sv, 0.0), axis=0, keepdims=True)
    sel_ref[...] = picked.astype(jnp.int32)


def nsa_sample_cmp(abk, abv, kvs, q3, w1cat, pe2, w2, p_len):
    nb, nc, _ = abk.shape
    ns = p_len // SLC_BLK + 1
    sl = -(-ns // LANE) * LANE

    def ab_spec():
        return pl.BlockSpec((None, nc, 2 * HD), lambda b, g: (b, 0, g))

    def full(a):
        return pl.BlockSpec(a.shape, lambda b, g: (0,) * a.ndim)

    row_spec = pl.BlockSpec((None, None, NSA_REP, HD), lambda b, g: (b, g, 0, 0))
    return pl.pallas_call(
        functools.partial(_nsa_sample_cmp_kernel, p_len=p_len, nc=nc, sl=sl),
        grid=(nb, NSA_KV),
        in_specs=[ab_spec(), ab_spec(), pl.BlockSpec((None, 6 * NSA_KV, HD), lambda b, g: (b, 0, 0)), row_spec,
                  full(w1cat), full(pe2), full(w2)],
        out_specs=[row_spec, pl.BlockSpec((None, None, 1, LANE), lambda b, g: (b, g, 0, 0))],
        out_shape=[jax.ShapeDtypeStruct((nb, NSA_KV, NSA_REP, HD), F32),
                   jax.ShapeDtypeStruct((nb, NSA_KV, 1, LANE), jnp.int32)],
        compiler_params=_cparams(("parallel", "arbitrary")),
        name="nsa_sample_cmp",
    )(abk, abv, kvs, q3, w1cat, pe2, w2)


def _nsa_sample_sel_kernel(sel_ref, pt_ref, *refs, p_len, nb, n_buf):
    nk = SLC_TOPK
    kpages, vpages = refs[:nk], refs[nk:2 * nk]
    (kwb_ref, vwb_ref, kvt_ref, kvs_ref, q_ref, az_ref, sm_ref, oc_ref,
     o_ref, kwo_ref, vwo_ref) = refs[2 * nk:]
    b = pl.program_id(0)
    g = pl.program_id(1)
    n_past = p_len // SLC_BLK
    q4 = q_ref[...] * (HD ** -0.5)
    lane_b = lax.broadcasted_iota(jnp.int32, (1, nb), 1) == b

    def new_row(i):
        return kvs_ref[pl.ds(i * NSA_KV + g, 1), :]

    def new_col(i):
        blk = kvt_ref[pl.ds(pl.multiple_of((i * NSA_KV + g) * HD, HD), HD), :]
        return jnp.sum(jnp.where(lane_b, blk, 0.0), axis=1, keepdims=True)

    half_of_lane = lax.broadcasted_iota(jnp.int32, (1, PAGE), 1) // SLC_BLK
    s_parts, m_parts = [], []
    has_new = jnp.zeros((1, 1), jnp.int32)
    for k in range(nk):
        blk = sel_ref[b, g, k]
        m_parts.append((half_of_lane == blk % 2) & (blk < n_past))
        s_parts.append(_dot(q4, kpages[k][...]))
        has_new = has_new + jnp.where(blk == n_past, 1, 0)
    s_sel = jnp.concatenate(s_parts, axis=1)
    m_sel = jnp.concatenate(m_parts, axis=1)
    new_ok = has_new > 0
    s_new = jnp.where(new_ok, jnp.sum(q4 * new_row(2), axis=1, keepdims=True), NEG)
    s_sel = jnp.where(m_sel, s_sel, NEG)
    mx = jnp.maximum(jnp.max(s_sel, axis=1, keepdims=True), s_new)
    p_sel = jnp.where(m_sel, jnp.exp(s_sel - mx), 0.0)
    p_new = jnp.where(new_ok, jnp.exp(s_new - mx), 0.0)
    den = jnp.sum(p_sel, axis=1, keepdims=True) + p_new
    acc = p_new * new_row(3)
    for k in range(nk):
        acc = acc + _dot(p_sel[:, k * PAGE:(k + 1) * PAGE], vpages[k][...], nt=True)
    o_s = acc / jnp.where(den > 0.0, den, 1.0)

    kwb = kwb_ref[...]
    vwb = vwb_ref[...]
    wlane = lax.broadcasted_iota(jnp.int32, (1, n_buf), 1)
    m_w = wlane > n_buf - WINDOW
    s_w = jnp.where(m_w, _dot(q4, kwb), NEG)
    s_wn = jnp.sum(q4 * new_row(4), axis=1, keepdims=True)
    mw = jnp.maximum(jnp.max(s_w, axis=1, keepdims=True), s_wn)
    p_w = jnp.where(m_w, jnp.exp(s_w - mw), 0.0)
    p_wn = jnp.exp(s_wn - mw)
    o_w = (_dot(p_w, vwb, nt=True) + p_wn * new_row(5)) / (jnp.sum(p_w, axis=1, keepdims=True) + p_wn)
    kwo_ref[...] = jnp.where(wlane == n_buf - 1, new_col(4), pltpu.roll(kwb, shift=n_buf - 1, axis=1))
    vwo_ref[...] = jnp.where(wlane == n_buf - 1, new_col(5), pltpu.roll(vwb, shift=n_buf - 1, axis=1))

    gate_row = _sigmoid(pltpu.roll(sm_ref[pl.ds(b, 1), :], shift=(LANE - 3 * NSA_REP * g) % LANE, axis=1))
    gl = lax.broadcasted_iota(jnp.int32, (NSA_REP, LANE), 1)
    gr = lax.broadcasted_iota(jnp.int32, (NSA_REP, LANE), 0)

    def gate(j):
        return jnp.sum(jnp.where(gl == 3 * gr + j, gate_row, 0.0), axis=1, keepdims=True)

    o = gate(0) * oc_ref[...] + gate(1) * o_s + gate(2) * o_w
    o_ref[...] = o * _silu(az_ref[...])


def nsa_sample_sel(sel, page_table, slc_k_t, slc_v_t, win_k_t, win_v_t, layer, kvt, kvs, q3, az3, u, o_cmp, p_len):
    nb = page_table.shape[0]
    n_buf = win_k_t.shape[-1]
    last_blk = p_len // SLC_BLK - 1

    def page_spec(k):
        def imap(b, g, sel_r, pt_r):
            blk = jnp.minimum(sel_r[b, g, k], last_blk)
            return (layer, pt_r[b, blk // (PAGE // SLC_BLK)], g, 0, 0)
        return pl.BlockSpec((None, None, None, HD, PAGE), imap)

    win_spec = pl.BlockSpec((None, None, None, HD, n_buf), lambda b, g, s, p: (layer, b, g, 0, 0))
    wout_spec = pl.BlockSpec((None, None, HD, n_buf), lambda b, g, s, p: (b, g, 0, 0))
    row_spec = pl.BlockSpec((None, None, NSA_REP, HD), lambda b, g, s, p: (b, g, 0, 0))

    def full(a):
        return pl.BlockSpec(a.shape, lambda b, g, s, p: (0,) * a.ndim)

    small = pl.BlockSpec((nb, LANE), lambda b, g, s, p: (0, OFF_SMALL // LANE))
    return pl.pallas_call(
        functools.partial(_nsa_sample_sel_kernel, p_len=p_len, nb=nb, n_buf=n_buf),
        grid_spec=pltpu.PrefetchScalarGridSpec(
            num_scalar_prefetch=2,
            grid=(nb, NSA_KV),
            in_specs=[page_spec(k) for k in range(SLC_TOPK)] * 2
                     + [win_spec, win_spec, full(kvt), pl.BlockSpec((None, 6 * NSA_KV, HD), lambda b, g, s, p: (b, 0, 0)),
                        row_spec, row_spec, small, row_spec],
            out_specs=[row_spec, wout_spec, wout_spec]),
        out_shape=[jax.ShapeDtypeStruct((nb, NSA_KV, NSA_REP, HD), F32),
                   jax.ShapeDtypeStruct((nb, NSA_KV, HD, n_buf), F32),
                   jax.ShapeDtypeStruct((nb, NSA_KV, HD, n_buf), F32)],
        compiler_params=_cparams(("parallel", "arbitrary")),
        name="nsa_sample_sel",
    )(sel, page_table, *([slc_k_t] * SLC_TOPK), *([slc_v_t] * SLC_TOPK), win_k_t, win_v_t, kvt, kvs, q3, az3, u, o_cmp)


def _block_diag_w1(w1cat):
    d = w1cat.shape[0]
    wj = w1cat.reshape(d, 2, CMP_STRIDE, HD, 2 * HD)
    eye = jnp.eye(NSA_KV, dtype=w1cat.dtype)
    bd = jnp.einsum("lsjdo,gh->lsjgdho", wj, eye)
    return bd.reshape(d, 2, CMP_STRIDE, NSA_KV * HD, NSA_KV * 2 * HD).astype(BF16)


ROW_PAD = 8


def _sample_layer(x, layer, w, p, caches, states, page_table, tiles):
    nb = x.shape[0]
    p_len = page_table.shape[1] * PAGE
    ck_t, cv_t, sk_t, sv_t, wk_t, wv_t = caches
    h = rmsnorm_rows(x, p["ln_w"], BF16, nb)
    u = matmul(h, w["wt_main"], nt=True, tm=nb, tn=MAIN_TN, name="in_proj_s")
    kv = matmul(h, w["wt_kv"], nt=True, tm=nb, tn=512, name="kv_proj_s")
    kvt = matmul(w["wt_kv"], h, nt=True, tm=512, tn=nb, name="kv_proj_st")
    kvs = kv.reshape(nb, 6 * NSA_KV, HD)
    u_pad = jnp.pad(u.reshape(nb, 1, NP_MAIN), ((0, 0), (0, ROW_PAD - 1), (0, 0))).reshape(nb * ROW_PAD, NP_MAIN)
    o_b, gla_s = gla_mixer(u_pad, states[0], p["gla_wa2"], p["gla_ba"], p["gla_norm"], nb, ROW_PAD,
                           c=ROW_PAD, t_valid=1)
    o_c, ssm_s, conv_s = ssd_mixer(u_pad, states[1], states[2], p["conv_w"], p["conv_b"], p["dt_bias"], p["a_log"],
                                   p["d_skip"], p["m_norm"], nb, ROW_PAD, q=ROW_PAD, t_valid=1)
    o_b = o_b.reshape(nb, ROW_PAD, GLA_DV)[:, 0]
    o_c = o_c.reshape(nb, ROW_PAD, M_DINNER)[:, 0]
    abk, abv = page_compress(ck_t, cv_t, page_table, layer, w["wbd"][0], w["wbd"][1], npg=tiles["npg"])
    q3 = u[:, _MAIN_OFF["a_q"]:_MAIN_OFF["a_q"] + NSA_HEADS * HD].reshape(nb, NSA_KV, NSA_REP, HD)
    az3 = u[:, _MAIN_OFF["a_z"]:_MAIN_OFF["a_z"] + MIX_W].reshape(nb, NSA_KV, NSA_REP, HD)
    o_cmp, sel = nsa_sample_cmp(abk, abv, kvs, q3, w["w1cat"], w["pe2"], w["w2"], p_len)
    o_a, win_k, win_v = nsa_sample_sel(sel.reshape(nb, NSA_KV, LANE), page_table, sk_t, sv_t, wk_t, wv_t, layer,
                                       kvt, kvs, q3, az3, u, o_cmp, p_len)
    mix = merge_branches(o_a.reshape(nb, MIX_W).astype(BF16), o_b, o_c, w["wa"], w["wb"], w["wc"], u, tm=nb, tn=tiles["mg_tn"])
    x_out = matmul(mix, w["wo"], nt=False, tm=nb, tn=tiles["out_tn"], res=x, name="out_proj_s")
    kv4 = kv.reshape(nb, 6, 1, NSA_KV, HD)
    return x_out, (kv4[:, 0], kv4[:, 1], kv4[:, 2], kv4[:, 3], win_k, win_v, gla_s, ssm_s, conv_s)


_TILES = dict(rms_tm=256, mm_tm=1024, kv_tm=512, kv_tn=512, tq=128, ck=512, gla_c=64, ssd_q=64,
              mg_tm=512, mg_tn=512, out_tn=512, npg=8)
_PARAM_NAMES = ("ln_w", "gla_wa2", "gla_ba", "gla_norm", "conv_w", "conv_b", "dt_bias", "a_log", "d_skip", "m_norm")


def kernel(x_prompt, x_sample, cache_cmp_k, cache_cmp_v, cache_slc_k, cache_slc_v, cache_win_k, cache_win_v,
           state_gla, state_ssm, state_conv, page_table, ln_w, w_in, cmp_pe_k, cmp_w1_k, cmp_w2_k, cmp_pe_v,
           cmp_w1_v, cmp_w2_v, gla_wa2, gla_ba, gla_norm, conv_w, conv_b, dt_bias, a_log, d_skip, m_norm,
           w_br_a, w_br_b, w_br_c, w_out, final_norm):
    nbp, t, d = x_prompt.shape
    nbs = x_sample.shape[0]
    depth = w_in.shape[0]
    weights = _prep_weights(w_in, cmp_pe_k, cmp_w1_k, cmp_w2_k, cmp_pe_v, cmp_w1_v, cmp_w2_v,
                            w_br_a, w_br_b, w_br_c, w_out)
    params = dict(zip(_PARAM_NAMES, (ln_w, gla_wa2, gla_ba, gla_norm, conv_w, conv_b, dt_bias, a_log, d_skip, m_norm)))
    caches = tuple(jnp.transpose(c, (0, 1, 3, 4, 2))
                   for c in (cache_cmp_k, cache_cmp_v, cache_slc_k, cache_slc_v, cache_win_k, cache_win_v))
    xp = x_prompt.reshape(nbp * t, d)
    xs = x_sample.reshape(nbs, d)
    p_out, s_out = [], []
    for l in range(depth):
        w = {k: v[l] for k, v in weights.items()}
        p = {k: v[l] for k, v in params.items()}
        xp, st_p = _prompt_layer(xp, nbp, t, w, p, _TILES)
        xs, st_s = _sample_layer(xs, l, w, p, caches, (state_gla[l], state_ssm[l], state_conv[l]), page_table, _TILES)
        p_out.append(st_p)
        s_out.append(st_s)
    y_prompt = rmsnorm_rows(xp, final_norm, F32, _TILES["rms_tm"]).reshape(nbp, t, d)
    y_sample = rmsnorm_rows(xs, final_norm, F32, nbs).reshape(nbs, 1, d)

    def stacked(outs, i, kv_layout):
        a = jnp.stack([o[i] for o in outs])
        return jnp.transpose(a, (0, 1, 4, 2, 3)) if kv_layout else a

    p_leaves = [stacked(p_out, i, i < 6) for i in range(9)]
    s_leaves = [stacked(s_out, i, i in (4, 5)) for i in range(9)]
    return (y_prompt, y_sample, *p_leaves, *s_leaves)
```

```python
import functools
import math

import jax
import jax.numpy as jnp
import numpy as np
from jax import lax
from jax.experimental import pallas as pl
from jax.experimental.pallas import tpu as pltpu

F32 = jnp.float32
BF16 = jnp.bfloat16
HI = lax.Precision.HIGHEST

D_MODEL = 2048
DEPTH = 4
PAGE = 128
MIX_W = D_MODEL // 2
HD = 64
NSA_HEADS = MIX_W // HD
NSA_KV = 4
NSA_REP = NSA_HEADS // NSA_KV
CMP_LEN = 32
CMP_STRIDE = 16
SLC_BLK = 64
SLC_TOPK = 16
WINDOW = 512
GLA_HEADS = 4
GLA_DK = MIX_W // 2
GLA_DV = MIX_W
GLA_HK = GLA_DK // GLA_HEADS
GLA_HV = GLA_DV // GLA_HEADS
GLA_RANK = 16
GLA_TAU = 16.0
M_DINNER = MIX_W
M_HD = 64
M_HEADS = M_DINNER // M_HD
M_GROUPS = 4
M_STATE = 128
M_CONV = 4
M_CONVDIM = M_DINNER + 2 * M_GROUPS * M_STATE
EPS = 1e-6
NEG = -1e30
FORCE = 1e4

_IN_NAMES = ("a_q", "a_kv", "a_g", "a_z", "b_q", "b_k", "b_v", "b_a", "b_z", "c_xbc", "c_dt", "c_z", "m_g")
_IN_SIZES = (NSA_HEADS * HD, 6 * NSA_KV * HD, 3 * NSA_HEADS, MIX_W, GLA_DK, GLA_DK, GLA_DV, GLA_RANK, GLA_DV,
             M_CONVDIM, M_HEADS, M_DINNER, 3 * D_MODEL)
_IN_OFF = dict(zip(_IN_NAMES, np.cumsum((0,) + _IN_SIZES)[:-1].tolist()))
_IN_SZ = dict(zip(_IN_NAMES, _IN_SIZES))

_MAIN_ORDER = ("m_g", "c_xbc", "a_z", "b_v", "b_z", "c_z", "b_q", "b_k", "a_g", "b_a", "c_dt")
_TP_ORDER = ("a_kv", "a_q", "a_g")
TP_Q = _IN_SZ["a_kv"]
TP_AG = TP_Q + _IN_SZ["a_q"]
TP_TM = 384
TP_ROWS = -(-(TP_AG + _IN_SZ["a_g"]) // TP_TM) * TP_TM
_MAIN_OFF = {}
_o = 0
for _n in _MAIN_ORDER:
    _MAIN_OFF[_n] = _o
    _o += _IN_SZ[_n]
OFF_SMALL = _MAIN_OFF["a_g"]
SM_AG, SM_BA, SM_DT = 0, 3 * NSA_HEADS, 3 * NSA_HEADS + GLA_RANK
LANE = 128
MAIN_TN = 512
NP_MAIN = -(-(OFF_SMALL + LANE) // MAIN_TN) * MAIN_TN
VMEM_LIMIT = 56 * 1024 * 1024


def _cparams(sem):
    return pltpu.CompilerParams(dimension_semantics=sem, vmem_limit_bytes=VMEM_LIMIT)


def _dot(a, b, prec=None, nt=False, tn=False):
    if prec is None:
        a, b = a.astype(BF16), b.astype(BF16)
    dn = (((0,) if tn else (1,),), ((1,) if nt else (0,),))
    dn = ((dn[0][0], dn[1][0]), ((), ()))
    return lax.dot_general(a, b, dn, precision=prec, preferred_element_type=F32)


def _sigmoid(x):
    return 1.0 / (1.0 + jnp.exp(-x))


def _silu(x):
    return x * _sigmoid(x)


def _log_sigmoid(x):
    return jnp.minimum(x, 0.0) - jnp.log(1.0 + jnp.exp(-jnp.abs(x)))


def _softplus(x):
    return jnp.maximum(x, 0.0) + jnp.log(1.0 + jnp.exp(-jnp.abs(x)))


def _rms_kernel(x_ref, w_ref, o_ref):
    x = x_ref[...]
    y = x * lax.rsqrt(jnp.mean(x * x, axis=-1, keepdims=True) + EPS)
    o_ref[...] = (y * w_ref[...]).astype(o_ref.dtype)


def rmsnorm_rows(x, w, out_dtype, tm):
    m, d = x.shape
    return pl.pallas_call(
        _rms_kernel,
        grid=(m // tm,),
        in_specs=[pl.BlockSpec((tm, d), lambda i: (i, 0)), pl.BlockSpec((1, d), lambda i: (0, 0))],
        out_specs=pl.BlockSpec((tm, d), lambda i: (i, 0)),
        out_shape=jax.ShapeDtypeStruct((m, d), out_dtype),
        compiler_params=_cparams(("parallel",)),
        name="rmsnorm",
    )(x, w.reshape(1, d))


def _mm_kernel(*refs, nt, has_res):
    a_ref, b_ref, o_ref = refs[0], refs[1], refs[-1]
    acc = _dot(a_ref[...], b_ref[...], nt=nt)
    if has_res:
        acc = acc + refs[2][...]
    o_ref[...] = acc.astype(o_ref.dtype)


def matmul(a, b, *, nt, tm, tn, res=None, out_dtype=F32, name="matmul"):
    m, k = a.shape
    n = b.shape[0] if nt else b.shape[1]
    tm, tn = min(tm, m), min(tn, n)
    b_spec = pl.BlockSpec((tn, k), lambda i, j: (j, 0)) if nt else pl.BlockSpec((k, tn), lambda i, j: (0, j))
    in_specs = [pl.BlockSpec((tm, k), lambda i, j: (i, 0)), b_spec]
    args = [a, b]
    if res is not None:
        in_specs.append(pl.BlockSpec((tm, tn), lambda i, j: (i, j)))
        args.append(res)
    return pl.pallas_call(
        functools.partial(_mm_kernel, nt=nt, has_res=res is not None),
        grid=(m // tm, n // tn),
        in_specs=in_specs,
        out_specs=pl.BlockSpec((tm, tn), lambda i, j: (i, j)),
        out_shape=jax.ShapeDtypeStruct((m, n), out_dtype),
        compiler_params=_cparams(("parallel", "parallel")),
        name=name,
    )(*args)


def matmul_kvt(wt_kv, h, nb, t, *, tm, tn):
    r, k = wt_kv.shape
    tn = min(tn, t)
    nj = t // tn
    return pl.pallas_call(
        functools.partial(_mm_kernel, nt=True, has_res=False),
        grid=(nb, r // tm, nj),
        in_specs=[pl.BlockSpec((tm, k), lambda b, i, j: (i, 0)),
                  pl.BlockSpec((tn, k), lambda b, i, j: (b * nj + j, 0))],
        out_specs=pl.BlockSpec((None, tm, tn), lambda b, i, j: (b, i, j)),
        out_shape=jax.ShapeDtypeStruct((nb, r, t), F32),
        compiler_params=_cparams(("parallel", "parallel", "parallel")),
        name="kv_proj_t",
    )(wt_kv, h)


def _merge_kernel(oa_ref, ob_ref, oc_ref, wa_ref, wb_ref, wc_ref, g0_ref, g1_ref, g2_ref, o_ref):
    acc = _sigmoid(g0_ref[...]) * _dot(oa_ref[...], wa_ref[...])
    acc = acc + _sigmoid(g1_ref[...]) * _dot(ob_ref[...], wb_ref[...])
    acc = acc + _sigmoid(g2_ref[...]) * _dot(oc_ref[...], wc_ref[...])
    o_ref[...] = acc.astype(o_ref.dtype)


def merge_branches(o_a, o_b, o_c, wa, wb, wc, u, *, tm, tn):
    m, k = o_a.shape
    n = wa.shape[1]
    tm = min(tm, m)
    gb = _MAIN_OFF["m_g"] // tn
    nj = n // tn
    o_spec = pl.BlockSpec((tm, k), lambda i, j: (i, 0))
    w_spec = pl.BlockSpec((k, tn), lambda i, j: (0, j))
    g_specs = [pl.BlockSpec((tm, tn), functools.partial(lambda i, j, br: (i, gb + br * nj + j), br=br))
               for br in range(3)]
    return pl.pallas_call(
        _merge_kernel,
        grid=(m // tm, nj),
        in_specs=[o_spec, o_spec, o_spec, w_spec, w_spec, w_spec] + g_specs,
        out_specs=pl.BlockSpec((tm, tn), lambda i, j: (i, j)),
        out_shape=jax.ShapeDtypeStruct((m, n), BF16),
        compiler_params=_cparams(("parallel", "parallel")),
        name="merge",
    )(o_a, o_b, o_c, wa, wb, wc, u, u, u)


def _cmp_kernel(seg_ref, w1_ref, pe_ref, w2_ref, o_ref, *, nseg):
    w1 = w1_ref[...]
    ab = _dot(seg_ref[...], w1, HI)
    pe = _dot(pe_ref[...], w1, HI)
    cst = pe[0:1, :HD] + pe[1:2, HD:]
    nxt = pltpu.roll(ab[:, HD:], shift=nseg - 1, axis=0)
    pre = ab[:, :HD] + nxt + cst
    o_ref[...] = _dot(_silu(pre), w2_ref[...], HI)


def compress_blocks(seg, w1cat, pe2, w2, nseg):
    _, r, kdim = seg.shape
    return pl.pallas_call(
        functools.partial(_cmp_kernel, nseg=nseg),
        grid=(2, r // nseg),
        in_specs=[pl.BlockSpec((None, nseg, kdim), lambda s, i: (s, i, 0)),
                  pl.BlockSpec((None, kdim, 2 * HD), lambda s, i: (s, 0, 0)),
                  pl.BlockSpec((None, 8, kdim), lambda s, i: (s, 0, 0)),
                  pl.BlockSpec((None, HD, HD), lambda s, i: (s, 0, 0))],
        out_specs=pl.BlockSpec((None, nseg, HD), lambda s, i: (s, i, 0)),
        out_shape=jax.ShapeDtypeStruct((2, r, HD), F32),
        compiler_params=_cparams(("parallel", "parallel")),
        name="compress",
    )(seg, w1cat, pe2, w2)


def _masked_softmax_rows(s, mask):
    s = jnp.where(mask, s, NEG)
    m = jnp.max(s, axis=-1, keepdims=True)
    p = jnp.where(mask, jnp.exp(s - m), 0.0)
    l = jnp.sum(p, axis=-1, keepdims=True)
    return p / jnp.where(l > 0.0, l, 1.0)


def _topk_mask(score, sidx, ns, topk):
    rank = jnp.zeros(score.shape, F32)
    for j in range(ns):
        sj = score[:, j:j + 1]
        rank = rank + jnp.where((sj > score) | ((sj == score) & (j < sidx)), 1.0, 0.0)
    return (rank < float(topk)) & (sidx < ns)


def _nsa_prompt_kernel(q_ref, az_ref, sm_ref, kst_ref, vst_ref, kwt_ref, vwt_ref, kc_ref, vc_ref, o_ref,
                       *, t, tq, ck, nseg):
    g = pl.program_id(1)
    s0 = pl.program_id(2) * tq
    ns = t // SLC_BLK
    rows = NSA_REP * tq
    q4 = q_ref[...] * (HD ** -0.5)
    qs = jnp.concatenate([q4[:, r * HD:(r + 1) * HD] for r in range(NSA_REP)], axis=0)
    qb = qs.astype(BF16)
    tpos1 = s0 + lax.broadcasted_iota(jnp.int32, (tq, 1), 0)
    tpos = jnp.concatenate([tpos1] * NSA_REP, axis=0)

    kc = kc_ref[...]
    cidx = lax.broadcasted_iota(jnp.int32, (1, nseg), 1)
    m_c = (cidx * CMP_STRIDE + (CMP_LEN - 1) <= tpos) & (cidx < nseg - 1)
    p_c = _masked_softmax_rows(_dot(qs, kc, HI, nt=True), m_c)
    o_c = _dot(p_c, vc_ref[...])
    psum = p_c[0:tq]
    for r in range(1, NSA_REP):
        psum = psum + p_c[r * tq:(r + 1) * tq]

    crow = lax.broadcasted_iota(jnp.int32, (nseg, LANE), 0) * CMP_STRIDE
    scol = lax.broadcasted_iota(jnp.int32, (nseg, LANE), 1) * SLC_BLK
    cover = jnp.where((crow + CMP_LEN > scol) & (crow < scol + SLC_BLK), 1.0, 0.0)
    imp = _dot(psum, cover, HI)
    sidx = lax.broadcasted_iota(jnp.int32, (1, LANE), 1)
    cur = tpos1 // SLC_BLK
    valid = (sidx * SLC_BLK <= tpos1) & (sidx < ns)
    forced = (sidx == 0) | (sidx == cur) | (sidx == cur - 1)
    score = jnp.where(valid, jnp.where(forced, FORCE, imp), NEG)
    sel = _topk_mask(score, sidx, ns, min(SLC_TOPK, ns))
    sel_b = jnp.where(sel, 1.0, 0.0).astype(BF16)
    srow = lax.broadcasted_iota(jnp.int32, (LANE, ck), 0)

    def sel_chunk(j, carry):
        m, l, acc = carry
        k0 = pl.multiple_of(j * ck, ck)
        pos = k0 + lax.broadcasted_iota(jnp.int32, (1, ck), 1)
        expand = jnp.where(srow == pos // SLC_BLK, 1.0, 0.0).astype(BF16)
        member = _dot(sel_b, expand) > 0.5
        msk1 = member & (pos <= tpos1)
        msk = jnp.concatenate([msk1] * NSA_REP, axis=0)
        s = jnp.where(msk, _dot(qb, kst_ref[:, pl.ds(k0, ck)]), NEG)
        m_new = jnp.maximum(m, jnp.max(s, axis=-1, keepdims=True))
        p = jnp.where(msk, jnp.exp(s - m_new), 0.0)
        alpha = jnp.exp(m - m_new)
        l = alpha * l + jnp.sum(p, axis=-1, keepdims=True)
        acc = alpha * acc + _dot(p, vst_ref[:, pl.ds(k0, ck)], nt=True)
        return m_new, l, acc

    n_chunks = (s0 + tq + ck - 1) // ck
    _, l_s, acc_s = lax.fori_loop(
        0, n_chunks, sel_chunk,
        (jnp.full((rows, 1), NEG, F32), jnp.zeros((rows, 1), F32), jnp.zeros((rows, HD), F32)))
    o_s = acc_s / jnp.where(l_s > 0.0, l_s, 1.0)

    ww = min(WINDOW + tq, t)
    w0 = pl.multiple_of(jnp.clip(s0 - WINDOW, 0, t - ww), LANE)
    wpos = w0 + lax.broadcasted_iota(jnp.int32, (1, ww), 1)
    m_w = (wpos <= tpos) & (wpos > tpos - WINDOW)
    p_w = _masked_softmax_rows(_dot(qb, kwt_ref[:, pl.ds(w0, ww)]), m_w)
    o_w = _dot(p_w, vwt_ref[:, pl.ds(w0, ww)], nt=True)

    gates = _sigmoid(pltpu.roll(sm_ref[...], shift=(LANE - 3 * NSA_REP * g) % LANE, axis=1))
    outs = []
    for r in range(NSA_REP):
        sl = slice(r * tq, (r + 1) * tq)
        outs.append(gates[:, 3 * r:3 * r + 1] * o_c[sl] + gates[:, 3 * r + 1:3 * r + 2] * o_s[sl]
                    + gates[:, 3 * r + 2:3 * r + 3] * o_w[sl])
    o = jnp.concatenate(outs, axis=1) * _silu(az_ref[...])
    o_ref[...] = o.astype(o_ref.dtype)


def nsa_prompt(u, kvt, kc, nb, t, *, tq, ck):
    nq = t // tq
    nseg = t // CMP_STRIDE
    ck = min(ck, t)
    gw = NSA_REP * HD

    def kv_spec(branch):
        return pl.BlockSpec((None, HD, t), lambda b, g, i: (b, branch * NSA_KV + g, 0))

    return pl.pallas_call(
        functools.partial(_nsa_prompt_kernel, t=t, tq=tq, ck=ck, nseg=nseg),
        grid=(nb, NSA_KV, nq),
        in_specs=[pl.BlockSpec((tq, gw), lambda b, g, i: (b * nq + i, _MAIN_OFF["a_q"] // gw + g)),
                  pl.BlockSpec((tq, gw), lambda b, g, i: (b * nq + i, _MAIN_OFF["a_z"] // gw + g)),
                  pl.BlockSpec((tq, LANE), lambda b, g, i: (b * nq + i, OFF_SMALL // LANE)),
                  kv_spec(2), kv_spec(3), kv_spec(4), kv_spec(5),
                  pl.BlockSpec((None, nseg, HD), lambda b, g, i: (0, b * NSA_KV + g, 0)),
                  pl.BlockSpec((None, nseg, HD), lambda b, g, i: (1, b * NSA_KV + g, 0))],
        out_specs=pl.BlockSpec((tq, gw), lambda b, g, i: (b * nq + i, g)),
        out_shape=jax.ShapeDtypeStruct((nb * t, NSA_HEADS * HD), BF16),
        compiler_params=_cparams(("parallel", "parallel", "arbitrary")),
        name="nsa_prompt",
    )(u, u, u, kvt, kvt, kvt, kvt, kc, kc)


LOG2E = math.log2(math.e)


def _nsa_prompt_t_kernel(qt_ref, ag_ref, az_ref, kst_ref, vst_ref, kwt_ref, vwt_ref, kc_ref, vc_ref, o_ref,
                         ks_scr, vs_scr, kw_scr, vw_scr, bias_scr, *, t, tq, ck, nseg):
    g = pl.program_id(1)
    i = pl.program_id(2)
    s0 = i * tq
    ns = t // SLC_BLK
    nsr = bias_scr.shape[0]
    cols = NSA_REP * tq

    @pl.when(i == 0)
    def _():
        ks_scr[...] = jnp.transpose(kst_ref[...]).astype(BF16)
        kw_scr[...] = jnp.transpose(kwt_ref[...]).astype(BF16)
        ones = jnp.ones((HD, t), BF16)
        vs_scr[...] = jnp.concatenate([vst_ref[...].astype(BF16), ones], axis=0)
        vw_scr[...] = jnp.concatenate([vwt_ref[...].astype(BF16), ones], axis=0)

    qt = qt_ref[...]
    q2 = jnp.concatenate([qt[r * HD:(r + 1) * HD, :] for r in range(NSA_REP)], axis=1) * (HD ** -0.5 * LOG2E)
    q2b = q2.astype(BF16)
    tpos1 = s0 + lax.broadcasted_iota(jnp.int32, (1, tq), 1)
    tpos = jnp.concatenate([tpos1] * NSA_REP, axis=1)

    crow = lax.broadcasted_iota(jnp.int32, (nseg, 1), 0)
    m_c = (crow * CMP_STRIDE + (CMP_LEN - 1) <= tpos) & (crow < nseg - 1)
    s_c = jnp.where(m_c, _dot(kc_ref[...], q2, HI), NEG)
    p_c = jnp.where(m_c, jnp.exp2(s_c - jnp.max(s_c, axis=0, keepdims=True)), 0.0)
    l_c = jnp.sum(p_c, axis=0, keepdims=True)
    p_c = p_c / jnp.where(l_c > 0.0, l_c, 1.0)
    o_c = _dot(jnp.transpose(vc_ref[...]), p_c)
    psum = p_c[:, 0:tq]
    for r in range(1, NSA_REP):
        psum = psum + p_c[:, r * tq:(r + 1) * tq]

    srow = lax.broadcasted_iota(jnp.int32, (nsr, 1), 0)
    ccol = lax.broadcasted_iota(jnp.int32, (1, nseg), 1) * CMP_STRIDE
    cover = jnp.where((ccol + CMP_LEN > srow * SLC_BLK) & (ccol < (srow + 1) * SLC_BLK), 1.0, 0.0)
    imp = _dot(cover, psum, HI)
    cur = tpos1 // SLC_BLK
    valid = (srow * SLC_BLK <= tpos1) & (srow < ns)
    forced = (srow == 0) | (srow == cur) | (srow == cur - 1)
    score = jnp.where(valid, jnp.where(forced, FORCE, imp), NEG)
    rank = jnp.zeros((nsr, tq), F32)
    for j in range(ns):
        sj = score[j:j + 1, :]
        rank = rank + jnp.where((sj > score) | ((sj == score) & (j < srow)), 1.0, 0.0)
    sel = (rank < float(min(SLC_TOPK, ns))) & valid
    bias_scr[...] = jnp.where(sel, 0.0, NEG)

    def attend(k_rows, v_cols, bias1, m, acc):
        s = _dot(k_rows, q2b)
        s = jnp.concatenate([s[:, r * tq:(r + 1) * tq] + bias1 for r in range(NSA_REP)], axis=1)
        m_new = jnp.maximum(m, jnp.max(s, axis=0, keepdims=True))
        p = jnp.exp2(s - m_new)
        return m_new, jnp.exp2(m - m_new) * acc + _dot(v_cols, p)

    def finish(acc):
        l = acc[HD:HD + 1, :]
        return acc[:HD, :] / jnp.where(l > 0.0, l, 1.0)

    blk_per_chunk = ck // SLC_BLK
    m0 = jnp.full((1, cols), NEG, F32)
    acc0 = jnp.zeros((2 * HD, cols), F32)

    def sel_chunk(j, carry):
        k0 = pl.multiple_of(j * ck, ck)
        npos = k0 + lax.broadcasted_iota(jnp.int32, (ck, 1), 0)
        rows = [jnp.broadcast_to(bias_scr[pl.ds(j * blk_per_chunk + k, 1), :], (SLC_BLK, tq))
                for k in range(blk_per_chunk)]
        bias1 = jnp.where(npos <= tpos1, jnp.concatenate(rows, axis=0), NEG)
        return attend(ks_scr[pl.ds(k0, ck), :], vs_scr[:, pl.ds(k0, ck)], bias1, *carry)

    n_chunks = (s0 + tq + ck - 1) // ck
    o_s = finish(lax.fori_loop(0, n_chunks, sel_chunk, (m0, acc0))[1])

    ww = min(WINDOW + tq, t)
    w0 = pl.multiple_of(jnp.clip(s0 - WINDOW, 0, t - ww), LANE)
    wpos = w0 + lax.broadcasted_iota(jnp.int32, (ww, 1), 0)
    bias_w = jnp.where((wpos <= tpos1) & (wpos > tpos1 - WINDOW), 0.0, NEG)
    o_w = finish(attend(kw_scr[pl.ds(w0, ww), :], vw_scr[:, pl.ds(w0, ww)], bias_w, m0, acc0)[1])

    outs = []
    for r in range(NSA_REP):
        sl = slice(r * tq, (r + 1) * tq)
        gt = [_sigmoid(ag_ref[pl.ds(3 * (NSA_REP * g + r) + j, 1), :]) for j in range(3)]
        outs.append(jnp.transpose(gt[0] * o_c[:, sl] + gt[1] * o_s[:, sl] + gt[2] * o_w[:, sl]))
    o = jnp.concatenate(outs, axis=1) * _silu(az_ref[...])
    o_ref[...] = o.astype(o_ref.dtype)


def nsa_prompt_t(u, tp, kc, nb, t, *, tq, ck):
    nq = t // tq
    nseg = t // CMP_STRIDE
    ck = min(ck, t)
    gw = NSA_REP * HD
    nsr = -(-(t // SLC_BLK) // 8) * 8

    def kv_spec(branch):
        return pl.BlockSpec((None, HD, t), lambda b, g, i: (b, branch * NSA_KV + g, 0))

    return pl.pallas_call(
        functools.partial(_nsa_prompt_t_kernel, t=t, tq=tq, ck=ck, nseg=nseg),
        grid=(nb, NSA_KV, nq),
        in_specs=[pl.BlockSpec((None, gw, tq), lambda b, g, i: (b, TP_Q // gw + g, i)),
                  pl.BlockSpec((None, LANE, tq), lambda b, g, i: (b, TP_AG // LANE, i)),
                  pl.BlockSpec((tq, gw), lambda b, g, i: (b * nq + i, _MAIN_OFF["a_z"] // gw + g)),
                  kv_spec(2), kv_spec(3), kv_spec(4), kv_spec(5),
                  pl.BlockSpec((None, nseg, HD), lambda b, g, i: (0, b * NSA_KV + g, 0)),
                  pl.BlockSpec((None, nseg, HD), lambda b, g, i: (1, b * NSA_KV + g, 0))],
        out_specs=pl.BlockSpec((tq, gw), lambda b, g, i: (b * nq + i, g)),
        out_shape=jax.ShapeDtypeStruct((nb * t, NSA_HEADS * HD), BF16),
        scratch_shapes=[pltpu.VMEM((t, HD), BF16), pltpu.VMEM((2 * HD, t), BF16),
                        pltpu.VMEM((t, HD), BF16), pltpu.VMEM((2 * HD, t), BF16),
                        pltpu.VMEM((nsr, tq), F32)],
        compiler_params=_cparams(("parallel", "parallel", "arbitrary")),
        name="nsa_prompt",
    )(tp, tp, u, tp, tp, tp, tp, kc, kc)


def _gla_kernel(q_ref, k_ref, v_ref, z_ref, sm_ref, s0_ref, wa2_ref, ba_ref, gn_ref, o_ref, sout_ref, s_scr,
                *, c, sub, t_valid):
    ci = pl.program_id(1)
    rows = c * sub

    @pl.when(ci == 0)
    def _():
        s_scr[...] = s0_ref[...]

    row = ci * rows + lax.broadcasted_iota(jnp.int32, (rows, 1), 0)
    gate_in = sm_ref[...][:, SM_BA:SM_BA + GLA_RANK]
    log_a = _log_sigmoid(_dot(gate_in, wa2_ref[...], HI) + ba_ref[...]) / GLA_TAU
    log_a = jnp.where(row < t_valid, log_a, 0.0)
    ri = lax.broadcasted_iota(jnp.int32, (rows, rows), 0)
    li = lax.broadcasted_iota(jnp.int32, (rows, rows), 1)
    tri = jnp.where((ri >= li) & (ri // c == li // c), 1.0, 0.0)
    cb_all = _dot(tri, log_a, HI)
    causal = lax.broadcasted_iota(jnp.int32, (c, c), 0) >= lax.broadcasted_iota(jnp.int32, (c, c), 1)
    for h in range(GLA_HEADS):
        ks = slice(h * GLA_HK, (h + 1) * GLA_HK)
        vs = slice(h * GLA_HV, (h + 1) * GLA_HV)
        state = s_scr[h]
        for j in range(sub):
            rs = slice(j * c, (j + 1) * c)
            cb = cb_all[rs, ks]
            ecb = jnp.exp(cb)
            qd = q_ref[rs, ks] * (GLA_HK ** -0.5) * ecb
            kh = k_ref[rs, ks]
            vh = v_ref[rs, vs]
            att = jnp.where(causal, _dot(qd, kh * jnp.exp(-cb), nt=True), 0.0)
            o = _dot(att, vh) + _dot(qd, state)
            c_end = cb[c - 1:c, :]
            k_dec = kh * jnp.exp(c_end - cb)
            e_col = jnp.transpose(ecb)[:, c - 1:c]
            state = e_col * state + _dot(jnp.transpose(k_dec), vh)
            y = o * lax.rsqrt(jnp.mean(o * o, axis=-1, keepdims=True) + EPS) * gn_ref[...]
            o_ref[rs, vs] = (y * _silu(z_ref[rs, vs])).astype(o_ref.dtype)
        s_scr[h] = state

    @pl.when(ci == pl.num_programs(1) - 1)
    def _():
        sout_ref[...] = s_scr[...]


def gla_mixer(u, s0, wa2, ba, gnorm, nb, t, *, c, sub, t_valid):
    rows = c * sub
    nc = t // rows

    def u_spec(name, width):
        return pl.BlockSpec((rows, width), lambda b, i: (b * nc + i, _MAIN_OFF[name] // width))

    s_spec = pl.BlockSpec((None, GLA_HEADS, GLA_HK, GLA_HV), lambda b, i: (b, 0, 0, 0))
    return pl.pallas_call(
        functools.partial(_gla_kernel, c=c, sub=sub, t_valid=t_valid),
        grid=(nb, nc),
        in_specs=[u_spec("b_q", GLA_DK), u_spec("b_k", GLA_DK), u_spec("b_v", GLA_DV), u_spec("b_z", GLA_DV),
                  pl.BlockSpec((rows, LANE), lambda b, i: (b * nc + i, OFF_SMALL // LANE)),
                  s_spec,
                  pl.BlockSpec((GLA_RANK, GLA_DK), lambda b, i: (0, 0)),
                  pl.BlockSpec((1, GLA_DK), lambda b, i: (0, 0)),
                  pl.BlockSpec((1, GLA_HV), lambda b, i: (0, 0))],
        out_specs=[pl.BlockSpec((rows, GLA_DV), lambda b, i: (b * nc + i, 0)), s_spec],
        out_shape=[jax.ShapeDtypeStruct((nb * t, GLA_DV), BF16),
                   jax.ShapeDtypeStruct((nb, GLA_HEADS, GLA_HK, GLA_HV), F32)],
        scratch_shapes=[pltpu.VMEM((GLA_HEADS, GLA_HK, GLA_HV), F32)],
        compiler_params=_cparams(("parallel", "arbitrary")),
        name="gla",
    )(u, u, u, u, u, s0, wa2, ba.reshape(1, GLA_DK), gnorm.reshape(1, GLA_HV))


def _ssd_kernel(xbc_ref, z_ref, sm_ref, s0_ref, cbuf_ref, cw_ref, cbias_ref, dtb_ref, alog_ref, dsk_ref, mn_ref,
                o_ref, sout_ref, cout_ref, s_scr, f_scr, y_scr, *, q, t_valid, n_chunks):
    ci = pl.program_id(1)
    npad = 8
    hist = M_CONV - 1

    @pl.when(ci == 0)
    def _():
        s_scr[...] = s0_ref[...]
        f_scr[npad - hist:npad, :] = cbuf_ref[...]

    f_scr[npad:npad + q, :] = xbc_ref[...]
    conv = cbias_ref[...]
    for i in range(M_CONV):
        conv = conv + f_scr[npad - hist + i:npad - hist + i + q, :] * cw_ref[i:i + 1, :]
    xbc = _silu(conv)

    last_valid = t_valid - (n_chunks - 1) * q

    @pl.when(ci == n_chunks - 1)
    def _():
        cout_ref[...] = f_scr[npad - hist + last_valid:npad + last_valid, :]

    f_scr[npad - hist:npad, :] = f_scr[npad - hist + q:npad + q, :]

    lane = lax.broadcasted_iota(jnp.int32, (1, LANE), 1)
    row = ci * q + lax.broadcasted_iota(jnp.int32, (q, 1), 0)
    head_lane = (lane >= SM_DT) & (lane < SM_DT + M_HEADS)
    dt = jnp.where(head_lane & (row < t_valid), _softplus(sm_ref[...] + dtb_ref[...]), 0.0)
    a = dt * (-jnp.exp(alog_ref[...]))
    tri = jnp.where(lax.broadcasted_iota(jnp.int32, (q, q), 0) >= lax.broadcasted_iota(jnp.int32, (q, q), 1), 1.0, 0.0)
    causal = tri > 0.5
    cum = _dot(tri, a, HI)
    cum_t = jnp.transpose(cum)
    dt_t = jnp.transpose(dt)
    nbc = M_GROUPS * M_STATE
    rep = M_HEADS // M_GROUPS
    for g in range(M_GROUPS):
        bg = xbc[:, M_DINNER + g * M_STATE:M_DINNER + (g + 1) * M_STATE]
        cg = xbc[:, M_DINNER + nbc + g * M_STATE:M_DINNER + nbc + (g + 1) * M_STATE]
        cbm = _dot(cg, bg, nt=True)
        for r in range(rep):
            h = g * rep + r
            hl = SM_DT + h
            xh = xbc[:, h * M_HD:(h + 1) * M_HD]
            cum_c = cum[:, hl:hl + 1]
            decay = jnp.exp(jnp.where(causal, cum_c - cum_t[hl:hl + 1, :], -jnp.inf))
            s_prev = s_scr[h]
            y = _dot(cbm * decay * dt_t[hl:hl + 1, :], xh)
            y = y + jnp.exp(cum_c) * _dot(cg, s_prev, nt=True)
            c_end = cum[q - 1:q, hl:hl + 1]
            w = jnp.exp(c_end - cum_c) * dt[:, hl:hl + 1]
            s_scr[h] = jnp.exp(c_end) * s_prev + _dot(xh * w, bg, tn=True)
            y_scr[:, h * M_HD:(h + 1) * M_HD] = y + dsk_ref[:, hl:hl + 1] * xh
    gw = M_DINNER // M_GROUPS
    for g in range(M_GROUPS):
        sl = slice(g * gw, (g + 1) * gw)
        yg = y_scr[:, sl] * _silu(z_ref[...][:, sl])
        yn = yg * lax.rsqrt(jnp.mean(yg * yg, axis=-1, keepdims=True) + EPS) * mn_ref[:, sl]
        o_ref[:, sl] = yn.astype(o_ref.dtype)

    @pl.when(ci == n_chunks - 1)
    def _():
        sout_ref[...] = s_scr[...]


def _pad_heads(v):
    return jnp.zeros((1, LANE), F32).at[0, SM_DT:SM_DT + M_HEADS].set(v)


def ssd_mixer(u, s0, cbuf, conv_w, conv_b, dt_bias, a_log, d_skip, m_norm, nb, t, *, q, t_valid):
    nc = t // q
    n_chunks = -(-t_valid // q)
    assert n_chunks == nc
    hist = M_CONV - 1

    def u_spec(name, width):
        return pl.BlockSpec((q, width), lambda b, i: (b * nc + i, _MAIN_OFF[name] // width))

    def full(shape):
        return pl.BlockSpec(shape, lambda b, i: (0,) * len(shape))

    s_spec = pl.BlockSpec((None, M_HEADS, M_HD, M_STATE), lambda b, i: (b, 0, 0, 0))
    c_spec = pl.BlockSpec((None, hist, M_CONVDIM), lambda b, i: (b, 0, 0))
    return pl.pallas_call(
        functools.partial(_ssd_kernel, q=q, t_valid=t_valid, n_chunks=n_chunks),
        grid=(nb, nc),
        in_specs=[u_spec("c_xbc", M_CONVDIM), u_spec("c_z", M_DINNER),
                  pl.BlockSpec((q, LANE), lambda b, i: (b * nc + i, OFF_SMALL // LANE)),
                  s_spec, c_spec,
                  full((M_CONV, M_CONVDIM)), full((1, M_CONVDIM)), full((1, LANE)), full((1, LANE)),
                  full((1, LANE)), full((1, M_DINNER))],
        out_specs=[pl.BlockSpec((q, M_DINNER), lambda b, i: (b * nc + i, 0)), s_spec, c_spec],
        out_shape=[jax.ShapeDtypeStruct((nb * t, M_DINNER), BF16),
                   jax.ShapeDtypeStruct((nb, M_HEADS, M_HD, M_STATE), F32),
                   jax.ShapeDtypeStruct((nb, hist, M_CONVDIM), F32)],
        scratch_shapes=[pltpu.VMEM((M_HEADS, M_HD, M_STATE), F32),
                        pltpu.VMEM((q + 8, M_CONVDIM), F32),
                        pltpu.VMEM((q, M_DINNER), F32)],
        compiler_params=_cparams(("parallel", "arbitrary")),
        name="ssd",
    )(u, u, u, s0, cbuf, conv_w, conv_b.reshape(1, M_CONVDIM), _pad_heads(dt_bias), _pad_heads(a_log),
      _pad_heads(d_skip), m_norm.reshape(1, M_DINNER))


def _prep_weights(w_in, cmp_pe_k, cmp_w1_k, cmp_w2_k, cmp_pe_v, cmp_w1_v, cmp_w2_v, w_br_a, w_br_b, w_br_c, w_out):
    wt = jnp.transpose(w_in, (0, 2, 1))
    parts = [wt[:, _IN_OFF[n]:_IN_OFF[n] + _IN_SZ[n]] for n in _MAIN_ORDER]
    used = sum(_IN_SZ[n] for n in _MAIN_ORDER)
    parts.append(jnp.zeros((wt.shape[0], NP_MAIN - used, wt.shape[2]), wt.dtype))
    wt_main = jnp.concatenate(parts, axis=1).astype(BF16)
    tparts = [wt[:, _IN_OFF[n]:_IN_OFF[n] + _IN_SZ[n]] for n in _TP_ORDER]
    tparts.append(jnp.zeros((wt.shape[0], TP_ROWS - TP_AG - _IN_SZ["a_g"], wt.shape[2]), wt.dtype))
    wt_kv = jnp.concatenate(tparts, axis=1).astype(BF16)
    half = CMP_STRIDE * HD

    def cat(w1):
        return jnp.concatenate([w1[:, :half], w1[:, half:]], axis=2)

    def pe_rows(pe):
        d = pe.shape[0]
        rows = pe.reshape(d, 2, half)
        return jnp.concatenate([rows, jnp.zeros((d, 6, half), pe.dtype)], axis=1)

    w1cat = jnp.stack([cat(cmp_w1_k), cat(cmp_w1_v)], axis=1)
    pe2 = jnp.stack([pe_rows(cmp_pe_k), pe_rows(cmp_pe_v)], axis=1)
    w2 = jnp.stack([cmp_w2_k, cmp_w2_v], axis=1)
    return dict(wt_main=wt_main, wt_kv=wt_kv, w1cat=w1cat, pe2=pe2, w2=w2, wbd=_block_diag_w1(w1cat),
                wa=w_br_a.astype(BF16), wb=w_br_b.astype(BF16), wc=w_br_c.astype(BF16), wo=w_out.astype(BF16))


def _prompt_layer(x, nb, t, w, p, tiles):
    h = rmsnorm_rows(x, p["ln_w"], BF16, tiles["rms_tm"])
    u = matmul(h, w["wt_main"], nt=True, tm=tiles["mm_tm"], tn=MAIN_TN, name="in_proj")
    tp = matmul_kvt(w["wt_kv"], h, nb, t, tm=TP_TM, tn=tiles["kv_tn"])
    nseg = t // CMP_STRIDE
    kv5 = tp[:, :TP_Q].reshape(nb, 6, NSA_KV, HD, t)
    seg = kv5[:, 0:2].reshape(nb, 2, NSA_KV, HD, nseg, CMP_STRIDE)
    seg = jnp.transpose(seg, (1, 0, 2, 4, 5, 3)).reshape(2, nb * NSA_KV * nseg, CMP_STRIDE * HD)
    kc = compress_blocks(seg, w["w1cat"], w["pe2"], w["w2"], nseg)
    o_a = nsa_prompt_t(u, tp, kc, nb, t, tq=tiles["tq"], ck=tiles["ck"])
    zeros_gla = jnp.zeros((nb, GLA_HEADS, GLA_HK, GLA_HV), F32)
    zeros_ssm = jnp.zeros((nb, M_HEADS, M_HD, M_STATE), F32)
    zeros_conv = jnp.zeros((nb, M_CONV - 1, M_CONVDIM), F32)
    o_b, gla_s = gla_mixer(u, zeros_gla, p["gla_wa2"], p["gla_ba"], p["gla_norm"], nb, t,
                           c=tiles["gla_c"], sub=min(tiles["gla_sub"], t // tiles["gla_c"]), t_valid=t)
    o_c, ssm_s, conv_s = ssd_mixer(u, zeros_ssm, zeros_conv, p["conv_w"], p["conv_b"], p["dt_bias"], p["a_log"],
                                   p["d_skip"], p["m_norm"], nb, t, q=tiles["ssd_q"], t_valid=t)
    mix = merge_branches(o_a, o_b, o_c, w["wa"], w["wb"], w["wc"], u, tm=tiles["mg_tm"], tn=tiles["mg_tn"])
    x_out = matmul(mix, w["wo"], nt=False, tm=tiles["mm_tm"], tn=tiles["out_tn"], res=x, name="out_proj")
    n_w = min(WINDOW, t)
    return x_out, (kv5[:, 0], kv5[:, 1], kv5[:, 2], kv5[:, 3], kv5[:, 4, :, :, t - n_w:], kv5[:, 5, :, :, t - n_w:],
                   gla_s, ssm_s, conv_s)


SEG_PER_PAGE = PAGE // CMP_STRIDE


def _page_cmp_kernel(pt_ref, *refs, npg):
    wk_ref, wv_ref, ok_ref, ov_ref, scr = refs[2 * npg:]
    n_lane_blk = NSA_KV * HD // LANE
    for pages, w_ref, o_ref in ((refs[:npg], wk_ref, ok_ref), (refs[npg:2 * npg], wv_ref, ov_ref)):
        for k in range(npg):
            page_t = pages[k][...].reshape(NSA_KV * HD, PAGE)
            pos_major = jnp.transpose(page_t)
            for c in range(n_lane_blk):
                scr[c, k * PAGE:(k + 1) * PAGE, :] = pos_major[:, c * LANE:(c + 1) * LANE]
        acc = jnp.zeros((npg * SEG_PER_PAGE, 2 * HD * NSA_KV), F32)
        for j in range(CMP_STRIDE):
            rows = pl.ds(j, npg * SEG_PER_PAGE, stride=CMP_STRIDE)
            xj = jnp.concatenate([scr[c, rows, :] for c in range(n_lane_blk)], axis=1)
            acc = acc + _dot(xj, w_ref[j])
        o_ref[...] = acc


def page_compress(cache_k_t, cache_v_t, page_table, layer, wbd_k, wbd_v, *, npg):
    nb, n_pages = page_table.shape
    steps = n_pages // npg

    def page_spec(k):
        return pl.BlockSpec((None, None, NSA_KV, HD, PAGE),
                            lambda b, i, pt: (layer, pt[b, i * npg + k], 0, 0, 0))

    w_spec = pl.BlockSpec((CMP_STRIDE, NSA_KV * HD, NSA_KV * 2 * HD), lambda b, i, pt: (0, 0, 0))
    o_spec = pl.BlockSpec((None, npg * SEG_PER_PAGE, NSA_KV * 2 * HD), lambda b, i, pt: (b, i, 0))
    o_shape = jax.ShapeDtypeStruct((nb, n_pages * SEG_PER_PAGE, NSA_KV * 2 * HD), F32)
    return pl.pallas_call(
        functools.partial(_page_cmp_kernel, npg=npg),
        grid_spec=pltpu.PrefetchScalarGridSpec(
            num_scalar_prefetch=1,
            grid=(nb, steps),
            in_specs=[page_spec(k) for k in range(npg)] * 2 + [w_spec, w_spec],
            out_specs=[o_spec, o_spec],
            scratch_shapes=[pltpu.VMEM((NSA_KV * HD // LANE, npg * PAGE, LANE), F32)]),
        out_shape=[o_shape, o_shape],
        compiler_params=_cparams(("parallel", "arbitrary")),
        name="page_compress",
    )(page_table, *([cache_k_t] * npg), *([cache_v_t] * npg), wbd_k, wbd_v)


def _nsa_sample_cmp_kernel(abk_ref, abv_ref, kvs_ref, q_ref, w1_ref, pe_ref, w2_ref, oc_ref, sel_ref,
                           *, p_len, nc, sl):
    g = pl.program_id(1)
    ns = p_len // SLC_BLK + 1
    rowi = lax.broadcasted_iota(jnp.int32, (nc, 1), 0)

    def compressed(ab_ref, idx):
        ab = ab_ref[...]
        w1 = w1_ref[idx]
        pe = _dot(pe_ref[idx], w1, HI)
        cst = pe[0:1, :HD] + pe[1:2, HD:]
        new = kvs_ref[pl.ds(idx * NSA_KV + g, 1), :]
        b_new = _dot(new, w1[0:HD, HD:])
        nxt = jnp.where(rowi == nc - 1, b_new, pltpu.roll(ab[:, HD:], shift=nc - 1, axis=0))
        return _dot(_silu(ab[:, :HD] + nxt + cst), w2_ref[idx], HI)

    kc = compressed(abk_ref, 0)
    vc = compressed(abv_ref, 1)
    q4 = q_ref[...] * (HD ** -0.5)
    cidx = lax.broadcasted_iota(jnp.int32, (1, nc), 1)
    m_c = cidx * CMP_STRIDE + (CMP_LEN - 1) <= p_len
    p_c = _masked_softmax_rows(_dot(q4, kc, HI, nt=True), m_c)
    oc_ref[...] = _dot(p_c, vc)
    psum = jnp.sum(p_c, axis=0, keepdims=True)
    crow = lax.broadcasted_iota(jnp.int32, (nc, sl), 0) * CMP_STRIDE
    scol = lax.broadcasted_iota(jnp.int32, (nc, sl), 1) * SLC_BLK
    cover = jnp.where((crow + CMP_LEN > scol) & (crow < scol + SLC_BLK), 1.0, 0.0)
    imp = _dot(psum, cover, HI)
    sidx = lax.broadcasted_iota(jnp.int32, (1, sl), 1)
    cur = p_len // SLC_BLK
    valid = (sidx * SLC_BLK <= p_len) & (sidx < ns)
    forced = (sidx == 0) | (sidx == cur) | (sidx == cur - 1)
    score = jnp.where(valid, jnp.where(forced, FORCE, imp), NEG)
    ri = lax.broadcasted_iota(jnp.int32, (sl, sl), 0)
    li = lax.broadcasted_iota(jnp.int32, (sl, sl), 1)
    score_b = jnp.broadcast_to(score, (sl, sl))
    score_col = jnp.sum(jnp.where(ri == li, score_b, 0.0), axis=1, keepdims=True)
    ahead = (score_b > score_col) | ((score_b == score_col) & (li < ri))
    rank_col = jnp.sum(jnp.where(ahead, 1.0, 0.0), axis=1, keepdims=True)
    kl = lax.broadcasted_iota(jnp.int32, (sl, LANE), 1).astype(F32)
    sv = lax.broadcasted_iota(jnp.int32, (sl, LANE), 0).astype(F32)
    picked = jnp.sum(jnp.where(rank_col == kl, sv, 0.0), axis=0, keepdims=True)
    sel_ref[...] = picked.astype(jnp.int32)


def nsa_sample_cmp(abk, abv, kvs, q3, w1cat, pe2, w2, p_len):
    nb, nc, _ = abk.shape
    ns = p_len // SLC_BLK + 1
    sl = -(-ns // LANE) * LANE

    def ab_spec():
        return pl.BlockSpec((None, nc, 2 * HD), lambda b, g: (b, 0, g))

    def full(a):
        return pl.BlockSpec(a.shape, lambda b, g: (0,) * a.ndim)

    row_spec = pl.BlockSpec((None, None, NSA_REP, HD), lambda b, g: (b, g, 0, 0))
    return pl.pallas_call(
        functools.partial(_nsa_sample_cmp_kernel, p_len=p_len, nc=nc, sl=sl),
        grid=(nb, NSA_KV),
        in_specs=[ab_spec(), ab_spec(), pl.BlockSpec((None, 6 * NSA_KV, HD), lambda b, g: (b, 0, 0)), row_spec,
                  full(w1cat), full(pe2), full(w2)],
        out_specs=[row_spec, pl.BlockSpec((None, None, 1, LANE), lambda b, g: (b, g, 0, 0))],
        out_shape=[jax.ShapeDtypeStruct((nb, NSA_KV, NSA_REP, HD), F32),
                   jax.ShapeDtypeStruct((nb, NSA_KV, 1, LANE), jnp.int32)],
        compiler_params=_cparams(("parallel", "arbitrary")),
        name="nsa_sample_cmp",
    )(abk, abv, kvs, q3, w1cat, pe2, w2)


def _nsa_sample_sel_kernel(sel_ref, pt_ref, *refs, p_len, nb, n_buf):
    nk = SLC_TOPK
    kpages, vpages = refs[:nk], refs[nk:2 * nk]
    (kwb_ref, vwb_ref, kvt_ref, kvs_ref, q_ref, az_ref, sm_ref, oc_ref,
     o_ref, kwo_ref, vwo_ref) = refs[2 * nk:]
    b = pl.program_id(0)
    g = pl.program_id(1)
    n_past = p_len // SLC_BLK
    q4 = q_ref[...] * (HD ** -0.5)
    lane_b = lax.broadcasted_iota(jnp.int32, (1, nb), 1) == b

    def new_row(i):
        return kvs_ref[pl.ds(i * NSA_KV + g, 1), :]

    def new_col(i):
        blk = kvt_ref[pl.ds(pl.multiple_of((i * NSA_KV + g) * HD, HD), HD), :]
        return jnp.sum(jnp.where(lane_b, blk, 0.0), axis=1, keepdims=True)

    half_of_lane = lax.broadcasted_iota(jnp.int32, (1, PAGE), 1) // SLC_BLK
    s_parts, m_parts = [], []
    has_new = jnp.zeros((1, 1), jnp.int32)
    for k in range(nk):
        blk = sel_ref[b, g, k]
        m_parts.append((half_of_lane == blk % 2) & (blk < n_past))
        s_parts.append(_dot(q4, kpages[k][...]))
        has_new = has_new + jnp.where(blk == n_past, 1, 0)
    s_sel = jnp.concatenate(s_parts, axis=1)
    m_sel = jnp.concatenate(m_parts, axis=1)
    new_ok = has_new > 0
    s_new = jnp.where(new_ok, jnp.sum(q4 * new_row(2), axis=1, keepdims=True), NEG)
    s_sel = jnp.where(m_sel, s_sel, NEG)
    mx = jnp.maximum(jnp.max(s_sel, axis=1, keepdims=True), s_new)
    p_sel = jnp.where(m_sel, jnp.exp(s_sel - mx), 0.0)
    p_new = jnp.where(new_ok, jnp.exp(s_new - mx), 0.0)
    den = jnp.sum(p_sel, axis=1, keepdims=True) + p_new
    acc = p_new * new_row(3)
    for k in range(nk):
        acc = acc + _dot(p_sel[:, k * PAGE:(k + 1) * PAGE], vpages[k][...], nt=True)
    o_s = acc / jnp.where(den > 0.0, den, 1.0)

    kwb = kwb_ref[...]
    vwb = vwb_ref[...]
    wlane = lax.broadcasted_iota(jnp.int32, (1, n_buf), 1)
    m_w = wlane > n_buf - WINDOW
    s_w = jnp.where(m_w, _dot(q4, kwb), NEG)
    s_wn = jnp.sum(q4 * new_row(4), axis=1, keepdims=True)
    mw = jnp.maximum(jnp.max(s_w, axis=1, keepdims=True), s_wn)
    p_w = jnp.where(m_w, jnp.exp(s_w - mw), 0.0)
    p_wn = jnp.exp(s_wn - mw)
    o_w = (_dot(p_w, vwb, nt=True) + p_wn * new_row(5)) / (jnp.sum(p_w, axis=1, keepdims=True) + p_wn)
    kwo_ref[...] = jnp.where(wlane == n_buf - 1, new_col(4), pltpu.roll(kwb, shift=n_buf - 1, axis=1))
    vwo_ref[...] = jnp.where(wlane == n_buf - 1, new_col(5), pltpu.roll(vwb, shift=n_buf - 1, axis=1))

    gate_row = _sigmoid(pltpu.roll(sm_ref[pl.ds(b, 1), :], shift=(LANE - 3 * NSA_REP * g) % LANE, axis=1))
    gl = lax.broadcasted_iota(jnp.int32, (NSA_REP, LANE), 1)
    gr = lax.broadcasted_iota(jnp.int32, (NSA_REP, LANE), 0)

    def gate(j):
        return jnp.sum(jnp.where(gl == 3 * gr + j, gate_row, 0.0), axis=1, keepdims=True)

    o = gate(0) * oc_ref[...] + gate(1) * o_s + gate(2) * o_w
    o_ref[...] = o * _silu(az_ref[...])


def nsa_sample_sel(sel, page_table, slc_k_t, slc_v_t, win_k_t, win_v_t, layer, kvt, kvs, q3, az3, u, o_cmp, p_len):
    nb = page_table.shape[0]
    n_buf = win_k_t.shape[-1]
    last_blk = p_len // SLC_BLK - 1

    def page_spec(k):
        def imap(b, g, sel_r, pt_r):
            blk = jnp.minimum(sel_r[b, g, k], last_blk)
            return (layer, pt_r[b, blk // (PAGE // SLC_BLK)], g, 0, 0)
        return pl.BlockSpec((None, None, None, HD, PAGE), imap)

    win_spec = pl.BlockSpec((None, None, None, HD, n_buf), lambda b, g, s, p: (layer, b, g, 0, 0))
    wout_spec = pl.BlockSpec((None, None, HD, n_buf), lambda b, g, s, p: (b, g, 0, 0))
    row_spec = pl.BlockSpec((None, None, NSA_REP, HD), lambda b, g, s, p: (b, g, 0, 0))

    def full(a):
        return pl.BlockSpec(a.shape, lambda b, g, s, p: (0,) * a.ndim)

    small = pl.BlockSpec((nb, LANE), lambda b, g, s, p: (0, OFF_SMALL // LANE))
    return pl.pallas_call(
        functools.partial(_nsa_sample_sel_kernel, p_len=p_len, nb=nb, n_buf=n_buf),
        grid_spec=pltpu.PrefetchScalarGridSpec(
            num_scalar_prefetch=2,
            grid=(nb, NSA_KV),
            in_specs=[page_spec(k) for k in range(SLC_TOPK)] * 2
                     + [win_spec, win_spec, full(kvt), pl.BlockSpec((None, 6 * NSA_KV, HD), lambda b, g, s, p: (b, 0, 0)),
                        row_spec, row_spec, small, row_spec],
            out_specs=[row_spec, wout_spec, wout_spec]),
        out_shape=[jax.ShapeDtypeStruct((nb, NSA_KV, NSA_REP, HD), F32),
                   jax.ShapeDtypeStruct((nb, NSA_KV, HD, n_buf), F32),
                   jax.ShapeDtypeStruct((nb, NSA_KV, HD, n_buf), F32)],
        compiler_params=_cparams(("parallel", "arbitrary")),
        name="nsa_sample_sel",
    )(sel, page_table, *([slc_k_t] * SLC_TOPK), *([slc_v_t] * SLC_TOPK), win_k_t, win_v_t, kvt, kvs, q3, az3, u, o_cmp)


def _block_diag_w1(w1cat):
    d = w1cat.shape[0]
    wj = w1cat.reshape(d, 2, CMP_STRIDE, HD, 2 * HD)
    eye = jnp.eye(NSA_KV, dtype=w1cat.dtype)
    bd = jnp.einsum("lsjdo,gh->lsjgdho", wj, eye)
    return bd.reshape(d, 2, CMP_STRIDE, NSA_KV * HD, NSA_KV * 2 * HD).astype(BF16)


ROW_PAD = 8


def _sample_layer(x, layer, w, p, caches, states, page_table, tiles):
    nb = x.shape[0]
    p_len = page_table.shape[1] * PAGE
    ck_t, cv_t, sk_t, sv_t, wk_t, wv_t = caches
    h = rmsnorm_rows(x, p["ln_w"], BF16, nb)
    u = matmul(h, w["wt_main"], nt=True, tm=nb, tn=MAIN_TN, name="in_proj_s")
    kvq = matmul(h, w["wt_kv"], nt=True, tm=nb, tn=TP_TM, name="kv_proj_s")
    kvt = matmul(w["wt_kv"], h, nt=True, tm=TP_TM, tn=nb, name="kv_proj_st")
    kv = kvq[:, :TP_Q]
    kvs = kv.reshape(nb, 6 * NSA_KV, HD)
    u_pad = jnp.pad(u.reshape(nb, 1, NP_MAIN), ((0, 0), (0, ROW_PAD - 1), (0, 0))).reshape(nb * ROW_PAD, NP_MAIN)
    o_b, gla_s = gla_mixer(u_pad, states[0], p["gla_wa2"], p["gla_ba"], p["gla_norm"], nb, ROW_PAD,
                           c=ROW_PAD, sub=1, t_valid=1)
    o_c, ssm_s, conv_s = ssd_mixer(u_pad, states[1], states[2], p["conv_w"], p["conv_b"], p["dt_bias"], p["a_log"],
                                   p["d_skip"], p["m_norm"], nb, ROW_PAD, q=ROW_PAD, t_valid=1)
    o_b = o_b.reshape(nb, ROW_PAD, GLA_DV)[:, 0]
    o_c = o_c.reshape(nb, ROW_PAD, M_DINNER)[:, 0]
    abk, abv = page_compress(ck_t, cv_t, page_table, layer, w["wbd"][0], w["wbd"][1], npg=tiles["npg"])
    q3 = kvq[:, TP_Q:TP_AG].reshape(nb, NSA_KV, NSA_REP, HD)
    az3 = u[:, _MAIN_OFF["a_z"]:_MAIN_OFF["a_z"] + MIX_W].reshape(nb, NSA_KV, NSA_REP, HD)
    o_cmp, sel = nsa_sample_cmp(abk, abv, kvs, q3, w["w1cat"], w["pe2"], w["w2"], p_len)
    o_a, win_k, win_v = nsa_sample_sel(sel.reshape(nb, NSA_KV, LANE), page_table, sk_t, sv_t, wk_t, wv_t, layer,
                                       kvt, kvs, q3, az3, u, o_cmp, p_len)
    mix = merge_branches(o_a.reshape(nb, MIX_W).astype(BF16), o_b, o_c, w["wa"], w["wb"], w["wc"], u, tm=nb, tn=tiles["mg_tn"])
    x_out = matmul(mix, w["wo"], nt=False, tm=nb, tn=tiles["out_tn"], res=x, name="out_proj_s")
    kv4 = kv.reshape(nb, 6, 1, NSA_KV, HD)
    return x_out, (kv4[:, 0], kv4[:, 1], kv4[:, 2], kv4[:, 3], win_k, win_v, gla_s, ssm_s, conv_s)


_TILES = dict(rms_tm=256, mm_tm=1024, kv_tm=512, kv_tn=512, tq=128, ck=512, gla_c=64, gla_sub=4, ssd_q=256,
              mg_tm=512, mg_tn=512, out_tn=512, npg=8)
_PARAM_NAMES = ("ln_w", "gla_wa2", "gla_ba", "gla_norm", "conv_w", "conv_b", "dt_bias", "a_log", "d_skip", "m_norm")


def kernel(x_prompt, x_sample, cache_cmp_k, cache_cmp_v, cache_slc_k, cache_slc_v, cache_win_k, cache_win_v,
           state_gla, state_ssm, state_conv, page_table, ln_w, w_in, cmp_pe_k, cmp_w1_k, cmp_w2_k, cmp_pe_v,
           cmp_w1_v, cmp_w2_v, gla_wa2, gla_ba, gla_norm, conv_w, conv_b, dt_bias, a_log, d_skip, m_norm,
           w_br_a, w_br_b, w_br_c, w_out, final_norm):
    nbp, t, d = x_prompt.shape
    nbs = x_sample.shape[0]
    depth = w_in.shape[0]
    weights = _prep_weights(w_in, cmp_pe_k, cmp_w1_k, cmp_w2_k, cmp_pe_v, cmp_w1_v, cmp_w2_v,
                            w_br_a, w_br_b, w_br_c, w_out)
    params = dict(zip(_PARAM_NAMES, (ln_w, gla_wa2, gla_ba, gla_norm, conv_w, conv_b, dt_bias, a_log, d_skip, m_norm)))
    caches = tuple(jnp.transpose(c, (0, 1, 3, 4, 2))
                   for c in (cache_cmp_k, cache_cmp_v, cache_slc_k, cache_slc_v, cache_win_k, cache_win_v))
    xp = x_prompt.reshape(nbp * t, d)
    xs = x_sample.reshape(nbs, d)
    p_out, s_out = [], []
    for l in range(depth):
        w = {k: v[l] for k, v in weights.items()}
        p = {k: v[l] for k, v in params.items()}
        xp, st_p = _prompt_layer(xp, nbp, t, w, p, _TILES)
        xs, st_s = _sample_layer(xs, l, w, p, caches, (state_gla[l], state_ssm[l], state_conv[l]), page_table, _TILES)
        p_out.append(st_p)
        s_out.append(st_s)
    y_prompt = rmsnorm_rows(xp, final_norm, F32, _TILES["rms_tm"]).reshape(nbp, t, d)
    y_sample = rmsnorm_rows(xs, final_norm, F32, nbs).reshape(nbs, 1, d)

    def stacked(outs, i, kv_layout):
        a = jnp.stack([o[i] for o in outs])
        return jnp.transpose(a, (0, 1, 4, 2, 3)) if kv_layout else a

    p_leaves = [stacked(p_out, i, i < 6) for i in range(9)]
    s_leaves = [stacked(s_out, i, i in (4, 5)) for i in range(9)]
    return (y_prompt, y_sample, *p_leaves, *s_leaves)
```

```python
import functools
import math

import jax
import jax.numpy as jnp
import numpy as np
from jax import lax
from jax.experimental import pallas as pl
from jax.experimental.pallas import tpu as pltpu

F32 = jnp.float32
BF16 = jnp.bfloat16
HI = lax.Precision.HIGHEST

D_MODEL = 2048
PAGE = 128
MIX_W = D_MODEL // 2
HD = 64
NSA_HEADS = MIX_W // HD
NSA_KV = 4
NSA_REP = NSA_HEADS // NSA_KV
CMP_LEN = 32
CMP_STRIDE = 16
SLC_BLK = 64
SLC_TOPK = 16
WINDOW = 512
GLA_HEADS = 4
GLA_DK = MIX_W // 2
GLA_DV = MIX_W
GLA_HK = GLA_DK // GLA_HEADS
GLA_HV = GLA_DV // GLA_HEADS
GLA_RANK = 16
GLA_TAU = 16.0
M_DINNER = MIX_W
M_HD = 64
M_HEADS = M_DINNER // M_HD
M_GROUPS = 4
M_STATE = 128
M_CONV = 4
M_CONVDIM = M_DINNER + 2 * M_GROUPS * M_STATE
EPS = 1e-6
NEG = -1e30
FORCE = 1e4
LOG2E = math.log2(math.e)

_IN_NAMES = ("a_q", "a_kv", "a_g", "a_z", "b_q", "b_k", "b_v", "b_a", "b_z", "c_xbc", "c_dt", "c_z", "m_g")
_IN_SIZES = (NSA_HEADS * HD, 6 * NSA_KV * HD, 3 * NSA_HEADS, MIX_W, GLA_DK, GLA_DK, GLA_DV, GLA_RANK, GLA_DV,
             M_CONVDIM, M_HEADS, M_DINNER, 3 * D_MODEL)
_IN_OFF = dict(zip(_IN_NAMES, np.cumsum((0,) + _IN_SIZES)[:-1].tolist()))
_IN_SZ = dict(zip(_IN_NAMES, _IN_SIZES))

_MAIN_ORDER = ("m_g", "c_xbc", "a_z", "b_v", "b_z", "c_z", "b_q", "b_k", "a_g", "b_a", "c_dt")
_MAIN_OFF = {}
_o = 0
for _n in _MAIN_ORDER:
    _MAIN_OFF[_n] = _o
    _o += _IN_SZ[_n]
OFF_SMALL = _MAIN_OFF["a_g"]
SM_AG, SM_BA, SM_DT = 0, 3 * NSA_HEADS, 3 * NSA_HEADS + GLA_RANK
LANE = 128
MAIN_TN = 512
NP_MAIN = -(-(OFF_SMALL + LANE) // MAIN_TN) * MAIN_TN
KV_ROWS = _IN_SZ["a_kv"]
GRP_ROWS = NSA_KV * HD
_TQ_ORDER = ("a_q", "a_g")
TQ_AG = _IN_SZ["a_q"]
TQ_TM = 384
TQ_ROWS = -(-(TQ_AG + _IN_SZ["a_g"]) // TQ_TM) * TQ_TM
VMEM_LIMIT = 56 * 1024 * 1024


def _cparams(sem):
    return pltpu.CompilerParams(dimension_semantics=sem, vmem_limit_bytes=VMEM_LIMIT)


def _dot(a, b, prec=None, nt=False, tn=False):
    if prec is None:
        a, b = a.astype(BF16), b.astype(BF16)
    dn = (((0,) if tn else (1,), (1,) if nt else (0,)), ((), ()))
    return lax.dot_general(a, b, dn, precision=prec, preferred_element_type=F32)


def _dot_exact_rhs(a, b_bf16):
    hi = a.astype(BF16)
    lo = (a - hi.astype(F32)).astype(BF16)
    return _dot(hi, b_bf16) + _dot(lo, b_bf16)


def _sigmoid(x):
    return 1.0 / (1.0 + jnp.exp(-x))


def _silu(x):
    return x * _sigmoid(x)


def _log_sigmoid(x):
    return jnp.minimum(x, 0.0) - jnp.log(1.0 + jnp.exp(-jnp.abs(x)))


def _softplus(x):
    return jnp.maximum(x, 0.0) + jnp.log(1.0 + jnp.exp(-jnp.abs(x)))


def _stacked_spec(block, imap, layer):
    if layer is None:
        return pl.BlockSpec(block, imap)
    return pl.BlockSpec((None,) + block, lambda *idx: (layer,) + imap(*idx))


def _rms_kernel(x_ref, w_ref, o_ref):
    x = x_ref[...]
    y = x * lax.rsqrt(jnp.mean(x * x, axis=-1, keepdims=True) + EPS)
    o_ref[...] = (y * w_ref[...]).astype(o_ref.dtype)


def rmsnorm_rows(x, w, out_dtype, tm):
    m, d = x.shape
    return pl.pallas_call(
        _rms_kernel,
        grid=(m // tm,),
        in_specs=[pl.BlockSpec((tm, d), lambda i: (i, 0)), pl.BlockSpec((1, d), lambda i: (0, 0))],
        out_specs=pl.BlockSpec((tm, d), lambda i: (i, 0)),
        out_shape=jax.ShapeDtypeStruct((m, d), out_dtype),
        compiler_params=_cparams(("parallel",)),
        name="rmsnorm",
    )(x, w.reshape(1, d))


def _mm_kernel(*refs, nt, has_res):
    a_ref, b_ref, o_ref = refs[0], refs[1], refs[-1]
    acc = _dot(a_ref[...], b_ref[...], nt=nt)
    if has_res:
        acc = acc + refs[2][...]
    o_ref[...] = acc.astype(o_ref.dtype)


def matmul(a, b, *, nt, tm, tn, res=None, out_dtype=F32, name="matmul", a_layer=None, b_layer=None):
    m, k = a.shape[-2:]
    n = b.shape[-2] if nt else b.shape[-1]
    tm, tn = min(tm, m), min(tn, n)
    if nt:
        b_spec = _stacked_spec((tn, k), lambda i, j: (j, 0), b_layer)
    else:
        b_spec = _stacked_spec((k, tn), lambda i, j: (0, j), b_layer)
    in_specs = [_stacked_spec((tm, k), lambda i, j: (i, 0), a_layer), b_spec]
    args = [a, b]
    if res is not None:
        in_specs.append(pl.BlockSpec((tm, tn), lambda i, j: (i, j)))
        args.append(res)
    return pl.pallas_call(
        functools.partial(_mm_kernel, nt=nt, has_res=res is not None),
        grid=(m // tm, n // tn),
        in_specs=in_specs,
        out_specs=pl.BlockSpec((tm, tn), lambda i, j: (i, j)),
        out_shape=jax.ShapeDtypeStruct((m, n), out_dtype),
        compiler_params=_cparams(("parallel", "parallel")),
        name=name,
    )(*args)


def matmul_t(wt, layer, h, nb, t, *, tm, tn, name):
    _, r, k = wt.shape
    tn = min(tn, t)
    nj = t // tn
    return pl.pallas_call(
        functools.partial(_mm_kernel, nt=True, has_res=False),
        grid=(nb, nj, r // tm),
        in_specs=[pl.BlockSpec((None, tm, k), lambda b, j, i: (layer, i, 0)),
                  pl.BlockSpec((tn, k), lambda b, j, i: (b * nj + j, 0))],
        out_specs=pl.BlockSpec((None, tm, tn), lambda b, j, i: (b, i, j)),
        out_shape=jax.ShapeDtypeStruct((nb, r, t), F32),
        compiler_params=_cparams(("parallel", "parallel", "parallel")),
        name=name,
    )(wt, h)


def _kv_stack_kernel(w_ref, h_ref, *refs):
    outs = refs[-6:]
    hb = h_ref[...]
    for i in range(6):
        outs[i][...] = _dot(w_ref[i * GRP_ROWS:(i + 1) * GRP_ROWS, :], hb, nt=True)


def kv_proj_stacked(wt_kv, layer, h, nb, t, stacks, *, tn):
    depth, _, k = wt_kv.shape
    tn = min(tn, t)
    nj = t // tn
    in_specs = [pl.BlockSpec((None, KV_ROWS, k), lambda b, j: (layer, 0, 0)),
                pl.BlockSpec((tn, k), lambda b, j: (b * nj + j, 0))]
    args = [wt_kv, h]
    aliases = {}
    if stacks is not None:
        in_specs += [pl.BlockSpec(memory_space=pl.ANY)] * 6
        args += list(stacks)
        aliases = {2 + i: i for i in range(6)}
    o_spec = pl.BlockSpec((None, None, GRP_ROWS, tn), lambda b, j: (layer, b, 0, j))
    o_shape = jax.ShapeDtypeStruct((depth, nb, GRP_ROWS, t), F32)
    return pl.pallas_call(
        _kv_stack_kernel,
        grid=(nb, nj),
        in_specs=in_specs,
        out_specs=[o_spec] * 6,
        out_shape=[o_shape] * 6,
        input_output_aliases=aliases,
        compiler_params=_cparams(("parallel", "parallel")),
        name="kv_proj_t",
    )(*args)


def _merge_kernel(oa_ref, ob_ref, oc_ref, wa_ref, wb_ref, wc_ref, g0_ref, g1_ref, g2_ref, o_ref):
    acc = _sigmoid(g0_ref[...]) * _dot(oa_ref[...], wa_ref[...])
    acc = acc + _sigmoid(g1_ref[...]) * _dot(ob_ref[...], wb_ref[...])
    acc = acc + _sigmoid(g2_ref[...]) * _dot(oc_ref[...], wc_ref[...])
    o_ref[...] = acc.astype(o_ref.dtype)


def merge_branches(o_a, o_b, o_c, wa, wb, wc, layer, u, *, tm, tn):
    m, k = o_a.shape
    n = wa.shape[-1]
    tm = min(tm, m)
    gb = _MAIN_OFF["m_g"] // tn
    nj = n // tn
    o_spec = pl.BlockSpec((tm, k), lambda i, j: (i, 0))
    w_spec = pl.BlockSpec((None, k, tn), lambda i, j: (layer, 0, j))
    g_specs = [pl.BlockSpec((tm, tn), functools.partial(lambda i, j, br: (i, gb + br * nj + j), br=br))
               for br in range(3)]
    return pl.pallas_call(
        _merge_kernel,
        grid=(m // tm, nj),
        in_specs=[o_spec, o_spec, o_spec, w_spec, w_spec, w_spec] + g_specs,
        out_specs=pl.BlockSpec((tm, tn), lambda i, j: (i, j)),
        out_shape=jax.ShapeDtypeStruct((m, n), BF16),
        compiler_params=_cparams(("parallel", "parallel")),
        name="merge",
    )(o_a, o_b, o_c, wa, wb, wc, u, u, u)


def _cmp_kernel(seg_ref, w1_ref, pe_ref, w2_ref, o_ref, *, nseg):
    w1 = w1_ref[...]
    ab = _dot(seg_ref[...], w1, HI)
    pe = _dot(pe_ref[...], w1, HI)
    cst = pe[0:1, :HD] + pe[1:2, HD:]
    nxt = pltpu.roll(ab[:, HD:], shift=nseg - 1, axis=0)
    pre = ab[:, :HD] + nxt + cst
    o_ref[...] = _dot(_silu(pre), w2_ref[...], HI)


def compress_blocks(seg, w1cat, pe2, w2, nseg):
    _, r, kdim = seg.shape
    return pl.pallas_call(
        functools.partial(_cmp_kernel, nseg=nseg),
        grid=(2, r // nseg),
        in_specs=[pl.BlockSpec((None, nseg, kdim), lambda s, i: (s, i, 0)),
                  pl.BlockSpec((None, kdim, 2 * HD), lambda s, i: (s, 0, 0)),
                  pl.BlockSpec((None, 8, kdim), lambda s, i: (s, 0, 0)),
                  pl.BlockSpec((None, HD, HD), lambda s, i: (s, 0, 0))],
        out_specs=pl.BlockSpec((None, nseg, HD), lambda s, i: (s, i, 0)),
        out_shape=jax.ShapeDtypeStruct((2, r, HD), F32),
        compiler_params=_cparams(("parallel", "parallel")),
        name="compress",
    )(seg, w1cat, pe2, w2)


def _nsa_prompt_kernel(qt_ref, ag_ref, az_ref, kst_ref, vst_ref, kwt_ref, vwt_ref, kc_ref, vc_ref, o_ref,
                       ks_scr, vs_scr, kw_scr, vw_scr, bias_scr, *, t, tq, ck, nseg):
    g = pl.program_id(1)
    i = pl.program_id(2)
    s0 = i * tq
    ns = t // SLC_BLK
    nsr = bias_scr.shape[0]
    cols = NSA_REP * tq

    @pl.when(i == 0)
    def _():
        ks_scr[...] = jnp.transpose(kst_ref[...]).astype(BF16)
        kw_scr[...] = jnp.transpose(kwt_ref[...]).astype(BF16)
        ones = jnp.ones((HD, t), BF16)
        vs_scr[...] = jnp.concatenate([vst_ref[...].astype(BF16), ones], axis=0)
        vw_scr[...] = jnp.concatenate([vwt_ref[...].astype(BF16), ones], axis=0)

    qt = qt_ref[...]
    q2 = jnp.concatenate([qt[r * HD:(r + 1) * HD, :] for r in range(NSA_REP)], axis=1) * (HD ** -0.5 * LOG2E)
    q2b = q2.astype(BF16)
    tpos1 = s0 + lax.broadcasted_iota(jnp.int32, (1, tq), 1)
    tpos = jnp.concatenate([tpos1] * NSA_REP, axis=1)

    crow = lax.broadcasted_iota(jnp.int32, (nseg, 1), 0)
    m_c = (crow * CMP_STRIDE + (CMP_LEN - 1) <= tpos) & (crow < nseg - 1)
    s_c = jnp.where(m_c, _dot(kc_ref[...], q2, HI), NEG)
    p_c = jnp.where(m_c, jnp.exp2(s_c - jnp.max(s_c, axis=0, keepdims=True)), 0.0)
    l_c = jnp.sum(p_c, axis=0, keepdims=True)
    p_c = p_c / jnp.where(l_c > 0.0, l_c, 1.0)
    o_c = _dot(jnp.transpose(vc_ref[...]), p_c)
    psum = p_c[:, 0:tq]
    for r in range(1, NSA_REP):
        psum = psum + p_c[:, r * tq:(r + 1) * tq]

    srow = lax.broadcasted_iota(jnp.int32, (nsr, 1), 0)
    ccol = lax.broadcasted_iota(jnp.int32, (1, nseg), 1) * CMP_STRIDE
    cover = jnp.where((ccol + CMP_LEN > srow * SLC_BLK) & (ccol < (srow + 1) * SLC_BLK), 1.0, 0.0)
    imp = _dot(cover, psum, HI)
    cur = tpos1 // SLC_BLK
    valid = (srow * SLC_BLK <= tpos1) & (srow < ns)
    forced = (srow == 0) | (srow == cur) | (srow == cur - 1)
    score = jnp.where(valid, jnp.where(forced, FORCE, imp), NEG)
    rank = jnp.zeros((nsr, tq), F32)
    for j in range(ns):
        sj = score[j:j + 1, :]
        rank = rank + jnp.where((sj > score) | ((sj == score) & (j < srow)), 1.0, 0.0)
    sel = (rank < float(min(SLC_TOPK, ns))) & valid
    bias_scr[...] = jnp.where(sel, 0.0, NEG)

    def attend(k_rows, v_cols, bias1, m, acc):
        s = _dot(k_rows, q2b)
        s = jnp.concatenate([s[:, r * tq:(r + 1) * tq] + bias1 for r in range(NSA_REP)], axis=1)
        m_new = jnp.maximum(m, jnp.max(s, axis=0, keepdims=True))
        p = jnp.exp2(s - m_new)
        return m_new, jnp.exp2(m - m_new) * acc + _dot(v_cols, p)

    def finish(acc):
        l = acc[HD:HD + 1, :]
        return acc[:HD, :] / jnp.where(l > 0.0, l, 1.0)

    blk_per_chunk = ck // SLC_BLK
    m0 = jnp.full((1, cols), NEG, F32)
    acc0 = jnp.zeros((2 * HD, cols), F32)

    def sel_chunk(j, carry):
        k0 = pl.multiple_of(j * ck, ck)
        npos = k0 + lax.broadcasted_iota(jnp.int32, (ck, 1), 0)
        rows = [jnp.broadcast_to(bias_scr[pl.ds(j * blk_per_chunk + k, 1), :], (SLC_BLK, tq))
                for k in range(blk_per_chunk)]
        bias1 = jnp.where(npos <= tpos1, jnp.concatenate(rows, axis=0), NEG)
        return attend(ks_scr[pl.ds(k0, ck), :], vs_scr[:, pl.ds(k0, ck)], bias1, *carry)

    n_chunks = (s0 + tq + ck - 1) // ck
    o_s = finish(lax.fori_loop(0, n_chunks, sel_chunk, (m0, acc0))[1])

    ww = min(WINDOW + tq, t)
    w0 = pl.multiple_of(jnp.clip(s0 - WINDOW, 0, t - ww), LANE)
    wpos = w0 + lax.broadcasted_iota(jnp.int32, (ww, 1), 0)
    bias_w = jnp.where((wpos <= tpos1) & (wpos > tpos1 - WINDOW), 0.0, NEG)
    o_w = finish(attend(kw_scr[pl.ds(w0, ww), :], vw_scr[:, pl.ds(w0, ww)], bias_w, m0, acc0)[1])

    outs = []
    for r in range(NSA_REP):
        sl = slice(r * tq, (r + 1) * tq)
        gt = [_sigmoid(ag_ref[pl.ds(3 * (NSA_REP * g + r) + j, 1), :]) for j in range(3)]
        outs.append(jnp.transpose(gt[0] * o_c[:, sl] + gt[1] * o_s[:, sl] + gt[2] * o_w[:, sl]))
    o = jnp.concatenate(outs, axis=1) * _silu(az_ref[...])
    o_ref[...] = o.astype(o_ref.dtype)


def nsa_prompt(u, qgt, stacks, layer, kc, nb, t, *, tq, ck):
    tq = min(tq, t)
    nq = t // tq
    nseg = t // CMP_STRIDE
    ck = min(ck, t)
    gw = NSA_REP * HD
    nsr = -(-(t // SLC_BLK) // 8) * 8
    kv_spec = pl.BlockSpec((None, None, HD, t), lambda b, g, i: (layer, b, g, 0))
    return pl.pallas_call(
        functools.partial(_nsa_prompt_kernel, t=t, tq=tq, ck=ck, nseg=nseg),
        grid=(nb, NSA_KV, nq),
        in_specs=[pl.BlockSpec((None, gw, tq), lambda b, g, i: (b, g, i)),
                  pl.BlockSpec((None, LANE, tq), lambda b, g, i: (b, TQ_AG // LANE, i)),
                  pl.BlockSpec((tq, gw), lambda b, g, i: (b * nq + i, _MAIN_OFF["a_z"] // gw + g)),
                  kv_spec, kv_spec, kv_spec, kv_spec,
                  pl.BlockSpec((None, nseg, HD), lambda b, g, i: (0, b * NSA_KV + g, 0)),
                  pl.BlockSpec((None, nseg, HD), lambda b, g, i: (1, b * NSA_KV + g, 0))],
        out_specs=pl.BlockSpec((tq, gw), lambda b, g, i: (b * nq + i, g)),
        out_shape=jax.ShapeDtypeStruct((nb * t, NSA_HEADS * HD), BF16),
        scratch_shapes=[pltpu.VMEM((t, HD), BF16), pltpu.VMEM((2 * HD, t), BF16),
                        pltpu.VMEM((t, HD), BF16), pltpu.VMEM((2 * HD, t), BF16),
                        pltpu.VMEM((nsr, tq), F32)],
        compiler_params=_cparams(("parallel", "parallel", "arbitrary")),
        name="nsa_prompt",
    )(qgt, qgt, u, stacks[2], stacks[3], stacks[4], stacks[5], kc, kc)


def _gla_kernel(q_ref, k_ref, v_ref, z_ref, sm_ref, s0_ref, wa2_ref, ba_ref, gn_ref, o_ref, sout_ref, s_scr,
                *, c, sub, t_valid):
    ci = pl.program_id(1)
    rows = c * sub

    @pl.when(ci == 0)
    def _():
        s_scr[...] = s0_ref[...]

    row = ci * rows + lax.broadcasted_iota(jnp.int32, (rows, 1), 0)
    gate_in = sm_ref[...][:, SM_BA:SM_BA + GLA_RANK]
    log_a = _log_sigmoid(_dot(gate_in, wa2_ref[...], HI) + ba_ref[...]) / GLA_TAU
    log_a = jnp.where(row < t_valid, log_a, 0.0)
    ri = lax.broadcasted_iota(jnp.int32, (rows, rows), 0)
    li = lax.broadcasted_iota(jnp.int32, (rows, rows), 1)
    tri = jnp.where((ri >= li) & (ri // c == li // c), 1.0, 0.0)
    cb_all = _dot(tri, log_a, HI)
    causal = lax.broadcasted_iota(jnp.int32, (c, c), 0) >= lax.broadcasted_iota(jnp.int32, (c, c), 1)
    for h in range(GLA_HEADS):
        ks = slice(h * GLA_HK, (h + 1) * GLA_HK)
        vs = slice(h * GLA_HV, (h + 1) * GLA_HV)
        state = s_scr[h]
        for j in range(sub):
            rs = slice(j * c, (j + 1) * c)
            cb = cb_all[rs, ks]
            ecb = jnp.exp(cb)
            qd = q_ref[rs, ks] * (GLA_HK ** -0.5) * ecb
            kh = k_ref[rs, ks]
            vh = v_ref[rs, vs]
            att = jnp.where(causal, _dot(qd, kh * jnp.exp(-cb), nt=True), 0.0)
            o = _dot(att, vh) + _dot(qd, state)
            c_end = cb[c - 1:c, :]
            k_dec = kh * jnp.exp(c_end - cb)
            e_col = jnp.transpose(ecb)[:, c - 1:c]
            state = e_col * state + _dot(jnp.transpose(k_dec), vh)
            y = o * lax.rsqrt(jnp.mean(o * o, axis=-1, keepdims=True) + EPS) * gn_ref[...]
            o_ref[rs, vs] = (y * _silu(z_ref[rs, vs])).astype(o_ref.dtype)
        s_scr[h] = state

    @pl.when(ci == pl.num_programs(1) - 1)
    def _():
        sout_ref[...] = s_scr[...]


def gla_mixer(u, s0, wa2, ba, gnorm, nb, t, *, c, sub, t_valid):
    rows = c * sub
    nc = t // rows

    def u_spec(name, width):
        return pl.BlockSpec((rows, width), lambda b, i: (b * nc + i, _MAIN_OFF[name] // width))

    s_spec = pl.BlockSpec((None, GLA_HEADS, GLA_HK, GLA_HV), lambda b, i: (b, 0, 0, 0))
    return pl.pallas_call(
        functools.partial(_gla_kernel, c=c, sub=sub, t_valid=t_valid),
        grid=(nb, nc),
        in_specs=[u_spec("b_q", GLA_DK), u_spec("b_k", GLA_DK), u_spec("b_v", GLA_DV), u_spec("b_z", GLA_DV),
                  pl.BlockSpec((rows, LANE), lambda b, i: (b * nc + i, OFF_SMALL // LANE)),
                  s_spec,
                  pl.BlockSpec((GLA_RANK, GLA_DK), lambda b, i: (0, 0)),
                  pl.BlockSpec((1, GLA_DK), lambda b, i: (0, 0)),
                  pl.BlockSpec((1, GLA_HV), lambda b, i: (0, 0))],
        out_specs=[pl.BlockSpec((rows, GLA_DV), lambda b, i: (b * nc + i, 0)), s_spec],
        out_shape=[jax.ShapeDtypeStruct((nb * t, GLA_DV), BF16),
                   jax.ShapeDtypeStruct((nb, GLA_HEADS, GLA_HK, GLA_HV), F32)],
        scratch_shapes=[pltpu.VMEM((GLA_HEADS, GLA_HK, GLA_HV), F32)],
        compiler_params=_cparams(("parallel", "arbitrary")),
        name="gla",
    )(u, u, u, u, u, s0, wa2, ba.reshape(1, GLA_DK), gnorm.reshape(1, GLA_HV))


def _ssd_kernel(xbc_ref, z_ref, sm_ref, s0_ref, cbuf_ref, cw_ref, cbias_ref, dtb_ref, alog_ref, dsk_ref, mn_ref,
                o_ref, sout_ref, cout_ref, s_scr, f_scr, y_scr, *, q, t_valid, n_chunks):
    ci = pl.program_id(1)
    npad = 8
    hist = M_CONV - 1

    @pl.when(ci == 0)
    def _():
        s_scr[...] = s0_ref[...]
        f_scr[npad - hist:npad, :] = cbuf_ref[...]

    f_scr[npad:npad + q, :] = xbc_ref[...]
    conv = cbias_ref[...]
    for i in range(M_CONV):
        conv = conv + f_scr[npad - hist + i:npad - hist + i + q, :] * cw_ref[i:i + 1, :]
    xbc = _silu(conv)

    last_valid = t_valid - (n_chunks - 1) * q

    @pl.when(ci == n_chunks - 1)
    def _():
        cout_ref[...] = f_scr[npad - hist + last_valid:npad + last_valid, :]

    f_scr[npad - hist:npad, :] = f_scr[npad - hist + q:npad + q, :]

    lane = lax.broadcasted_iota(jnp.int32, (1, LANE), 1)
    row = ci * q + lax.broadcasted_iota(jnp.int32, (q, 1), 0)
    head_lane = (lane >= SM_DT) & (lane < SM_DT + M_HEADS)
    dt = jnp.where(head_lane & (row < t_valid), _softplus(sm_ref[...] + dtb_ref[...]), 0.0)
    a = dt * (-jnp.exp(alog_ref[...]))
    tri = jnp.where(lax.broadcasted_iota(jnp.int32, (q, q), 0) >= lax.broadcasted_iota(jnp.int32, (q, q), 1), 1.0, 0.0)
    causal = tri > 0.5
    cum = _dot(tri, a, HI)
    cum_t = jnp.transpose(cum)
    dt_t = jnp.transpose(dt)
    nbc = M_GROUPS * M_STATE
    rep = M_HEADS // M_GROUPS
    for g in range(M_GROUPS):
        bg = xbc[:, M_DINNER + g * M_STATE:M_DINNER + (g + 1) * M_STATE]
        cg = xbc[:, M_DINNER + nbc + g * M_STATE:M_DINNER + nbc + (g + 1) * M_STATE]
        cbm = _dot(cg, bg, nt=True)
        for r in range(rep):
            h = g * rep + r
            hl = SM_DT + h
            xh = xbc[:, h * M_HD:(h + 1) * M_HD]
            cum_c = cum[:, hl:hl + 1]
            decay = jnp.exp(jnp.where(causal, cum_c - cum_t[hl:hl + 1, :], -jnp.inf))
            s_prev = s_scr[h]
            y = _dot(cbm * decay * dt_t[hl:hl + 1, :], xh)
            y = y + jnp.exp(cum_c) * _dot(cg, s_prev, nt=True)
            c_end = cum[q - 1:q, hl:hl + 1]
            w = jnp.exp(c_end - cum_c) * dt[:, hl:hl + 1]
            s_scr[h] = jnp.exp(c_end) * s_prev + _dot(xh * w, bg, tn=True)
            y_scr[:, h * M_HD:(h + 1) * M_HD] = y + dsk_ref[:, hl:hl + 1] * xh
    gw = M_DINNER // M_GROUPS
    for g in range(M_GROUPS):
        sl = slice(g * gw, (g + 1) * gw)
        yg = y_scr[:, sl] * _silu(z_ref[...][:, sl])
        yn = yg * lax.rsqrt(jnp.mean(yg * yg, axis=-1, keepdims=True) + EPS) * mn_ref[:, sl]
        o_ref[:, sl] = yn.astype(o_ref.dtype)

    @pl.when(ci == n_chunks - 1)
    def _():
        sout_ref[...] = s_scr[...]


def _pad_heads(v):
    return jnp.zeros((1, LANE), F32).at[0, SM_DT:SM_DT + M_HEADS].set(v)


def ssd_mixer(u, s0, cbuf, conv_w, conv_b, dt_bias, a_log, d_skip, m_norm, nb, t, *, q, t_valid):
    nc = t // q
    n_chunks = -(-t_valid // q)
    assert n_chunks == nc
    hist = M_CONV - 1

    def u_spec(name, width):
        return pl.BlockSpec((q, width), lambda b, i: (b * nc + i, _MAIN_OFF[name] // width))

    def full(shape):
        return pl.BlockSpec(shape, lambda b, i: (0,) * len(shape))

    s_spec = pl.BlockSpec((None, M_HEADS, M_HD, M_STATE), lambda b, i: (b, 0, 0, 0))
    c_spec = pl.BlockSpec((None, hist, M_CONVDIM), lambda b, i: (b, 0, 0))
    return pl.pallas_call(
        functools.partial(_ssd_kernel, q=q, t_valid=t_valid, n_chunks=n_chunks),
        grid=(nb, nc),
        in_specs=[u_spec("c_xbc", M_CONVDIM), u_spec("c_z", M_DINNER),
                  pl.BlockSpec((q, LANE), lambda b, i: (b * nc + i, OFF_SMALL // LANE)),
                  s_spec, c_spec,
                  full((M_CONV, M_CONVDIM)), full((1, M_CONVDIM)), full((1, LANE)), full((1, LANE)),
                  full((1, LANE)), full((1, M_DINNER))],
        out_specs=[pl.BlockSpec((q, M_DINNER), lambda b, i: (b * nc + i, 0)), s_spec, c_spec],
        out_shape=[jax.ShapeDtypeStruct((nb * t, M_DINNER), BF16),
                   jax.ShapeDtypeStruct((nb, M_HEADS, M_HD, M_STATE), F32),
                   jax.ShapeDtypeStruct((nb, hist, M_CONVDIM), F32)],
        scratch_shapes=[pltpu.VMEM((M_HEADS, M_HD, M_STATE), F32),
                        pltpu.VMEM((q + 8, M_CONVDIM), F32),
                        pltpu.VMEM((q, M_DINNER), F32)],
        compiler_params=_cparams(("parallel", "arbitrary")),
        name="ssd",
    )(u, u, u, s0, cbuf, conv_w, conv_b.reshape(1, M_CONVDIM), _pad_heads(dt_bias), _pad_heads(a_log),
      _pad_heads(d_skip), m_norm.reshape(1, M_DINNER))


SEG_PER_PAGE = PAGE // CMP_STRIDE


def _page_cmp_kernel(pt_ref, *refs, npg):
    wk_ref, wv_ref, ok_ref, ov_ref, scr = refs[2 * npg:]
    n_lane_blk = NSA_KV * HD // LANE
    for pages, w_ref, o_ref in ((refs[:npg], wk_ref, ok_ref), (refs[npg:2 * npg], wv_ref, ov_ref)):
        for k in range(npg):
            page_t = pages[k][...].reshape(NSA_KV * HD, PAGE)
            pos_major = jnp.transpose(page_t)
            for c in range(n_lane_blk):
                scr[c, k * PAGE:(k + 1) * PAGE, :] = pos_major[:, c * LANE:(c + 1) * LANE]
        acc = jnp.zeros((npg * SEG_PER_PAGE, 2 * HD * NSA_KV), F32)
        for j in range(CMP_STRIDE):
            rows = pl.ds(j, npg * SEG_PER_PAGE, stride=CMP_STRIDE)
            xj = jnp.concatenate([scr[c, rows, :] for c in range(n_lane_blk)], axis=1)
            acc = acc + _dot(xj, w_ref[j])
        o_ref[...] = acc


def page_compress(cache_k_t, cache_v_t, page_table, layer, wbd, *, npg):
    nb, n_pages = page_table.shape
    steps = n_pages // npg

    def page_spec(k):
        return pl.BlockSpec((None, None, NSA_KV, HD, PAGE),
                            lambda b, i, pt: (layer, pt[b, i * npg + k], 0, 0, 0))

    def w_spec(s):
        return pl.BlockSpec((None, None, CMP_STRIDE, NSA_KV * HD, NSA_KV * 2 * HD),
                            lambda b, i, pt: (layer, s, 0, 0, 0))

    o_spec = pl.BlockSpec((None, npg * SEG_PER_PAGE, NSA_KV * 2 * HD), lambda b, i, pt: (b, i, 0))
    o_shape = jax.ShapeDtypeStruct((nb, n_pages * SEG_PER_PAGE, NSA_KV * 2 * HD), F32)
    return pl.pallas_call(
        functools.partial(_page_cmp_kernel, npg=npg),
        grid_spec=pltpu.PrefetchScalarGridSpec(
            num_scalar_prefetch=1,
            grid=(nb, steps),
            in_specs=[page_spec(k) for k in range(npg)] * 2 + [w_spec(0), w_spec(1)],
            out_specs=[o_spec, o_spec],
            scratch_shapes=[pltpu.VMEM((NSA_KV * HD // LANE, npg * PAGE, LANE), F32)]),
        out_shape=[o_shape, o_shape],
        compiler_params=_cparams(("parallel", "arbitrary")),
        name="page_compress",
    )(page_table, *([cache_k_t] * npg), *([cache_v_t] * npg), wbd, wbd)


def _masked_softmax_rows(s, mask):
    s = jnp.where(mask, s, NEG)
    m = jnp.max(s, axis=-1, keepdims=True)
    p = jnp.where(mask, jnp.exp(s - m), 0.0)
    l = jnp.sum(p, axis=-1, keepdims=True)
    return p / jnp.where(l > 0.0, l, 1.0)


def _nsa_sample_cmp_kernel(abk_ref, abv_ref, kvs_ref, q_ref, w1_ref, pe_ref, w2_ref, cover_ref, oc_ref, sel_ref,
                           cst_scr, *, p_len, nc, sl):
    g = pl.program_id(1)
    ns = p_len // SLC_BLK + 1
    rowi = lax.broadcasted_iota(jnp.int32, (nc, 1), 0)

    @pl.when((pl.program_id(0) == 0) & (g == 0))
    def _():
        for idx in range(2):
            pe = _dot(pe_ref[idx], w1_ref[idx], HI)
            cst_scr[idx:idx + 1, :] = pe[0:1, :HD] + pe[1:2, HD:]

    def compressed(ab_ref, idx):
        ab = ab_ref[...]
        new = kvs_ref[pl.ds(idx * NSA_KV + g, 1), :]
        b_new = _dot(new, w1_ref[idx][0:HD, HD:])
        nxt = jnp.where(rowi == nc - 1, b_new, pltpu.roll(ab[:, HD:], shift=nc - 1, axis=0))
        return _dot(_silu(ab[:, :HD] + nxt + cst_scr[idx:idx + 1, :]), w2_ref[idx])

    kc = compressed(abk_ref, 0)
    vc = compressed(abv_ref, 1)
    q4 = q_ref[...] * (HD ** -0.5)
    cidx = lax.broadcasted_iota(jnp.int32, (1, nc), 1)
    m_c = cidx * CMP_STRIDE + (CMP_LEN - 1) <= p_len
    p_c = _masked_softmax_rows(_dot(q4, kc, HI, nt=True), m_c)
    oc_ref[...] = _dot(p_c, vc)
    imp = _dot_exact_rhs(jnp.sum(p_c, axis=0, keepdims=True), cover_ref[...])
    sidx = lax.broadcasted_iota(jnp.int32, (1, sl), 1)
    cur = p_len // SLC_BLK
    valid = (sidx * SLC_BLK <= p_len) & (sidx < ns)
    forced = (sidx == 0) | (sidx == cur) | (sidx == cur - 1)
    score = jnp.where(valid, jnp.where(forced, FORCE, imp), NEG)
    ri = lax.broadcasted_iota(jnp.int32, (sl, sl), 0)
    li = lax.broadcasted_iota(jnp.int32, (sl, sl), 1)
    score_b = jnp.broadcast_to(score, (sl, sl))
    score_col = jnp.sum(jnp.where(ri == li, score_b, 0.0), axis=1, keepdims=True)
    ahead = (score_b > score_col) | ((score_b == score_col) & (li < ri))
    rank_col = jnp.sum(jnp.where(ahead, 1.0, 0.0), axis=1, keepdims=True)
    kl = lax.broadcasted_iota(jnp.int32, (sl, LANE), 1).astype(F32)
    sv = lax.broadcasted_iota(jnp.int32, (sl, LANE), 0).astype(F32)
    picked = jnp.sum(jnp.where(rank_col == kl, sv, 0.0), axis=0, keepdims=True)
    sel_ref[...] = picked.astype(jnp.int32)


def _cover_matrix(nc, sl):
    c0 = np.arange(nc)[:, None] * CMP_STRIDE
    s0 = np.arange(sl)[None, :] * SLC_BLK
    return jnp.asarray(((c0 + CMP_LEN > s0) & (c0 < s0 + SLC_BLK)).astype(np.float32), dtype=BF16)


def nsa_sample_cmp(abk, abv, kvs, q3, w1cat, pe2, w2, p_len):
    nb, nc, _ = abk.shape
    ns = p_len // SLC_BLK + 1
    sl = -(-ns // LANE) * LANE
    cover = _cover_matrix(nc, sl)

    def ab_spec():
        return pl.BlockSpec((None, nc, 2 * HD), lambda b, g: (b, 0, g))

    def full(a):
        return pl.BlockSpec(a.shape, lambda b, g: (0,) * a.ndim)

    row_spec = pl.BlockSpec((None, None, NSA_REP, HD), lambda b, g: (b, g, 0, 0))
    return pl.pallas_call(
        functools.partial(_nsa_sample_cmp_kernel, p_len=p_len, nc=nc, sl=sl),
        grid=(nb, NSA_KV),
        in_specs=[ab_spec(), ab_spec(), pl.BlockSpec((None, 6 * NSA_KV, HD), lambda b, g: (b, 0, 0)), row_spec,
                  full(w1cat), full(pe2), full(w2), full(cover)],
        out_specs=[row_spec, pl.BlockSpec((None, None, 1, LANE), lambda b, g: (b, g, 0, 0))],
        out_shape=[jax.ShapeDtypeStruct((nb, NSA_KV, NSA_REP, HD), F32),
                   jax.ShapeDtypeStruct((nb, NSA_KV, 1, LANE), jnp.int32)],
        scratch_shapes=[pltpu.VMEM((8, HD), F32)],
        compiler_params=_cparams(("arbitrary", "arbitrary")),
        name="nsa_sample_cmp",
    )(abk, abv, kvs, q3, w1cat, pe2, w2, cover)


def _nsa_sample_sel_kernel(sel_ref, pt_ref, *refs, p_len, nb, n_buf):
    nk = SLC_TOPK
    kpages, vpages = refs[:nk], refs[nk:2 * nk]
    (kwb_ref, vwb_ref, kvt_ref, kvs_ref, q_ref, az_ref, sm_ref, oc_ref,
     o_ref, kwo_ref, vwo_ref) = refs[2 * nk:]
    b = pl.program_id(0)
    g = pl.program_id(1)
    n_past = p_len // SLC_BLK
    q4 = q_ref[...] * (HD ** -0.5)
    lane_b = lax.broadcasted_iota(jnp.int32, (1, nb), 1) == b

    def new_row(i):
        return kvs_ref[pl.ds(i * NSA_KV + g, 1), :]

    def new_col(i):
        blk = kvt_ref[pl.ds(pl.multiple_of((i * NSA_KV + g) * HD, HD), HD), :]
        return jnp.sum(jnp.where(lane_b, blk, 0.0), axis=1, keepdims=True)

    half_of_lane = lax.broadcasted_iota(jnp.int32, (1, PAGE), 1) // SLC_BLK
    s_parts, m_parts = [], []
    has_new = jnp.zeros((1, 1), jnp.int32)
    for k in range(nk):
        blk = sel_ref[b, g, k]
        m_parts.append((half_of_lane == blk % 2) & (blk < n_past))
        s_parts.append(_dot(q4, kpages[k][...]))
        has_new = has_new + jnp.where(blk == n_past, 1, 0)
    s_sel = jnp.concatenate(s_parts, axis=1)
    m_sel = jnp.concatenate(m_parts, axis=1)
    new_ok = has_new > 0
    s_new = jnp.where(new_ok, jnp.sum(q4 * new_row(2), axis=1, keepdims=True), NEG)
    s_sel = jnp.where(m_sel, s_sel, NEG)
    mx = jnp.maximum(jnp.max(s_sel, axis=1, keepdims=True), s_new)
    p_sel = jnp.where(m_sel, jnp.exp(s_sel - mx), 0.0)
    p_new = jnp.where(new_ok, jnp.exp(s_new - mx), 0.0)
    den = jnp.sum(p_sel, axis=1, keepdims=True) + p_new
    acc = p_new * new_row(3)
    for k in range(nk):
        acc = acc + _dot(p_sel[:, k * PAGE:(k + 1) * PAGE], vpages[k][...], nt=True)
    o_s = acc / jnp.where(den > 0.0, den, 1.0)

    kwb = kwb_ref[...]
    vwb = vwb_ref[...]
    wlane = lax.broadcasted_iota(jnp.int32, (1, n_buf), 1)
    m_w = wlane > n_buf - WINDOW
    s_w = jnp.where(m_w, _dot(q4, kwb), NEG)
    s_wn = jnp.sum(q4 * new_row(4), axis=1, keepdims=True)
    mw = jnp.maximum(jnp.max(s_w, axis=1, keepdims=True), s_wn)
    p_w = jnp.where(m_w, jnp.exp(s_w - mw), 0.0)
    p_wn = jnp.exp(s_wn - mw)
    o_w = (_dot(p_w, vwb, nt=True) + p_wn * new_row(5)) / (jnp.sum(p_w, axis=1, keepdims=True) + p_wn)
    kwo_ref[...] = jnp.where(wlane == n_buf - 1, new_col(4), pltpu.roll(kwb, shift=n_buf - 1, axis=1))
    vwo_ref[...] = jnp.where(wlane == n_buf - 1, new_col(5), pltpu.roll(vwb, shift=n_buf - 1, axis=1))

    gate_row = _sigmoid(pltpu.roll(sm_ref[pl.ds(b, 1), :], shift=(LANE - 3 * NSA_REP * g) % LANE, axis=1))
    gl = lax.broadcasted_iota(jnp.int32, (NSA_REP, LANE), 1)
    gr = lax.broadcasted_iota(jnp.int32, (NSA_REP, LANE), 0)

    def gate(j):
        return jnp.sum(jnp.where(gl == 3 * gr + j, gate_row, 0.0), axis=1, keepdims=True)

    o = gate(0) * oc_ref[...] + gate(1) * o_s + gate(2) * o_w
    o_ref[...] = o * _silu(az_ref[...])


def nsa_sample_sel(sel, page_table, slc_k_t, slc_v_t, win_k_t, win_v_t, layer, kvt, kvs, q3, az3, u, o_cmp, p_len):
    nb = page_table.shape[0]
    n_buf = win_k_t.shape[-1]
    last_blk = p_len // SLC_BLK - 1

    def page_spec(k):
        def imap(b, g, sel_r, pt_r):
            blk = jnp.minimum(sel_r[b, g, k], last_blk)
            return (layer, pt_r[b, blk // (PAGE // SLC_BLK)], g, 0, 0)
        return pl.BlockSpec((None, None, None, HD, PAGE), imap)

    win_spec = pl.BlockSpec((None, None, None, HD, n_buf), lambda b, g, s, p: (layer, b, g, 0, 0))
    wout_spec = pl.BlockSpec((None, None, HD, n_buf), lambda b, g, s, p: (b, g, 0, 0))
    row_spec = pl.BlockSpec((None, None, NSA_REP, HD), lambda b, g, s, p: (b, g, 0, 0))

    def full(a):
        return pl.BlockSpec(a.shape, lambda b, g, s, p: (0,) * a.ndim)

    small = pl.BlockSpec((nb, LANE), lambda b, g, s, p: (0, OFF_SMALL // LANE))
    return pl.pallas_call(
        functools.partial(_nsa_sample_sel_kernel, p_len=p_len, nb=nb, n_buf=n_buf),
        grid_spec=pltpu.PrefetchScalarGridSpec(
            num_scalar_prefetch=2,
            grid=(nb, NSA_KV),
            in_specs=[page_spec(k) for k in range(SLC_TOPK)] * 2
                     + [win_spec, win_spec, full(kvt), pl.BlockSpec((None, 6 * NSA_KV, HD), lambda b, g, s, p: (b, 0, 0)),
                        row_spec, row_spec, small, row_spec],
            out_specs=[row_spec, wout_spec, wout_spec]),
        out_shape=[jax.ShapeDtypeStruct((nb, NSA_KV, NSA_REP, HD), F32),
                   jax.ShapeDtypeStruct((nb, NSA_KV, HD, n_buf), F32),
                   jax.ShapeDtypeStruct((nb, NSA_KV, HD, n_buf), F32)],
        compiler_params=_cparams(("parallel", "arbitrary")),
        name="nsa_sample_sel",
    )(sel, page_table, *([slc_k_t] * SLC_TOPK), *([slc_v_t] * SLC_TOPK), win_k_t, win_v_t, kvt, kvs, q3, az3, u, o_cmp)


def _block_diag_w1(w1cat):
    d = w1cat.shape[0]
    wj = w1cat.reshape(d, 2, CMP_STRIDE, HD, 2 * HD)
    eye = jnp.eye(NSA_KV, dtype=w1cat.dtype)
    bd = jnp.einsum("lsjdo,gh->lsjgdho", wj, eye)
    return bd.reshape(d, 2, CMP_STRIDE, NSA_KV * HD, NSA_KV * 2 * HD).astype(BF16)


def _prep_weights(w_in, cmp_pe_k, cmp_w1_k, cmp_w2_k, cmp_pe_v, cmp_w1_v, cmp_w2_v, w_br_a, w_br_b, w_br_c, w_out):
    wt = jnp.transpose(w_in, (0, 2, 1))

    def gather_rows(order, total):
        parts = [wt[:, _IN_OFF[n]:_IN_OFF[n] + _IN_SZ[n]] for n in order]
        used = sum(_IN_SZ[n] for n in order)
        parts.append(jnp.zeros((wt.shape[0], total - used, wt.shape[2]), wt.dtype))
        return jnp.concatenate(parts, axis=1).astype(BF16)

    half = CMP_STRIDE * HD

    def cat(w1):
        return jnp.concatenate([w1[:, :half], w1[:, half:]], axis=2)

    def pe_rows(pe):
        d = pe.shape[0]
        rows = pe.reshape(d, 2, half)
        return jnp.concatenate([rows, jnp.zeros((d, 6, half), pe.dtype)], axis=1)

    w1cat = jnp.stack([cat(cmp_w1_k), cat(cmp_w1_v)], axis=1)
    pe2 = jnp.stack([pe_rows(cmp_pe_k), pe_rows(cmp_pe_v)], axis=1)
    w2 = jnp.stack([cmp_w2_k, cmp_w2_v], axis=1)
    return dict(wt_main=gather_rows(_MAIN_ORDER, NP_MAIN), wt_kv=gather_rows(("a_kv",), KV_ROWS),
                wt_q=gather_rows(_TQ_ORDER, TQ_ROWS),
                w1cat=w1cat, pe2=pe2, w2=w2, wbd=_block_diag_w1(w1cat),
                wa=w_br_a.astype(BF16), wb=w_br_b.astype(BF16), wc=w_br_c.astype(BF16), wo=w_out.astype(BF16))


def _prompt_layer(x, nb, t, layer, w, p, tiles, stacks):
    h = rmsnorm_rows(x, p["ln_w"], BF16, tiles["rms_tm"])
    u = matmul(h, w["wt_main"], nt=True, tm=tiles["mm_tm"], tn=MAIN_TN, name="in_proj", b_layer=layer)
    stacks = kv_proj_stacked(w["wt_kv"], layer, h, nb, t, stacks, tn=tiles["kv_tn"])
    qgt = matmul_t(w["wt_q"], layer, h, nb, t, tm=TQ_TM, tn=tiles["q_tn"], name="q_proj_t")
    nseg = t // CMP_STRIDE
    seg = jnp.stack([stacks[0][layer], stacks[1][layer]]).reshape(2, nb, NSA_KV, HD, nseg, CMP_STRIDE)
    seg = jnp.transpose(seg, (0, 1, 2, 4, 5, 3)).reshape(2, nb * NSA_KV * nseg, CMP_STRIDE * HD)
    kc = compress_blocks(seg, w["w1cat"][layer], w["pe2"][layer], w["w2"][layer], nseg)
    o_a = nsa_prompt(u, qgt, stacks, layer, kc, nb, t, tq=tiles["tq"], ck=tiles["ck"])
    zeros_gla = jnp.zeros((nb, GLA_HEADS, GLA_HK, GLA_HV), F32)
    zeros_ssm = jnp.zeros((nb, M_HEADS, M_HD, M_STATE), F32)
    zeros_conv = jnp.zeros((nb, M_CONV - 1, M_CONVDIM), F32)
    o_b, gla_s = gla_mixer(u, zeros_gla, p["gla_wa2"], p["gla_ba"], p["gla_norm"], nb, t,
                           c=tiles["gla_c"], sub=min(tiles["gla_sub"], t // tiles["gla_c"]), t_valid=t)
    o_c, ssm_s, conv_s = ssd_mixer(u, zeros_ssm, zeros_conv, p["conv_w"], p["conv_b"], p["dt_bias"], p["a_log"],
                                   p["d_skip"], p["m_norm"], nb, t, q=min(tiles["ssd_q"], t), t_valid=t)
    mix = merge_branches(o_a, o_b, o_c, w["wa"], w["wb"], w["wc"], layer, u, tm=tiles["mg_tm"], tn=tiles["mg_tn"])
    x_out = matmul(mix, w["wo"], nt=False, tm=tiles["mm_tm"], tn=tiles["out_tn"], res=x, name="out_proj",
                   b_layer=layer)
    return x_out, stacks, (gla_s, ssm_s, conv_s)


ROW_PAD = 8


def _sample_layer(x, layer, w, p, caches, states, page_table, tiles):
    nb = x.shape[0]
    p_len = page_table.shape[1] * PAGE
    ck_t, cv_t, sk_t, sv_t, wk_t, wv_t = caches
    h = rmsnorm_rows(x, p["ln_w"], BF16, nb)
    u = matmul(h, w["wt_main"], nt=True, tm=nb, tn=MAIN_TN, name="in_proj_s", b_layer=layer)
    kv = matmul(h, w["wt_kv"], nt=True, tm=nb, tn=512, name="kv_proj_s", b_layer=layer)
    kvt = matmul(w["wt_kv"], h, nt=True, tm=512, tn=nb, name="kv_proj_st", a_layer=layer)
    qg = matmul(h, w["wt_q"], nt=True, tm=nb, tn=TQ_TM, name="q_proj_s", b_layer=layer)
    kvs = kv.reshape(nb, 6 * NSA_KV, HD)
    u_pad = jnp.pad(u.reshape(nb, 1, NP_MAIN), ((0, 0), (0, ROW_PAD - 1), (0, 0))).reshape(nb * ROW_PAD, NP_MAIN)
    o_b, gla_s = gla_mixer(u_pad, states[0], p["gla_wa2"], p["gla_ba"], p["gla_norm"], nb, ROW_PAD,
                           c=ROW_PAD, sub=1, t_valid=1)
    o_c, ssm_s, conv_s = ssd_mixer(u_pad, states[1], states[2], p["conv_w"], p["conv_b"], p["dt_bias"], p["a_log"],
                                   p["d_skip"], p["m_norm"], nb, ROW_PAD, q=ROW_PAD, t_valid=1)
    o_b = o_b.reshape(nb, ROW_PAD, GLA_DV)[:, 0]
    o_c = o_c.reshape(nb, ROW_PAD, M_DINNER)[:, 0]
    abk, abv = page_compress(ck_t, cv_t, page_table, layer, w["wbd"], npg=min(tiles["npg"], page_table.shape[1]))
    q3 = qg[:, :TQ_AG].reshape(nb, NSA_KV, NSA_REP, HD)
    az3 = u[:, _MAIN_OFF["a_z"]:_MAIN_OFF["a_z"] + MIX_W].reshape(nb, NSA_KV, NSA_REP, HD)
    o_cmp, sel = nsa_sample_cmp(abk, abv, kvs, q3, w["w1cat"][layer], w["pe2"][layer], w["w2"][layer], p_len)
    o_a, win_k, win_v = nsa_sample_sel(sel.reshape(nb, NSA_KV, LANE), page_table, sk_t, sv_t, wk_t, wv_t, layer,
                                       kvt, kvs, q3, az3, u, o_cmp, p_len)
    mix = merge_branches(o_a.reshape(nb, MIX_W).astype(BF16), o_b, o_c, w["wa"], w["wb"], w["wc"], layer, u,
                         tm=nb, tn=tiles["mg_tn"])
    x_out = matmul(mix, w["wo"], nt=False, tm=nb, tn=tiles["out_tn"], res=x, name="out_proj_s", b_layer=layer)
    kv4 = kv.reshape(nb, 6, 1, NSA_KV, HD)
    return x_out, (kv4[:, 0], kv4[:, 1], kv4[:, 2], kv4[:, 3], win_k, win_v, gla_s, ssm_s, conv_s)


_TILES = dict(rms_tm=256, mm_tm=1024, kv_tn=512, q_tn=2048, tq=256, ck=512, gla_c=64, gla_sub=4, ssd_q=256,
              mg_tm=1024, mg_tn=512, out_tn=512, npg=16)
_PARAM_NAMES = ("ln_w", "gla_wa2", "gla_ba", "gla_norm", "conv_w", "conv_b", "dt_bias", "a_log", "d_skip", "m_norm")


def kernel(x_prompt, x_sample, cache_cmp_k, cache_cmp_v, cache_slc_k, cache_slc_v, cache_win_k, cache_win_v,
           state_gla, state_ssm, state_conv, page_table, ln_w, w_in, cmp_pe_k, cmp_w1_k, cmp_w2_k, cmp_pe_v,
           cmp_w1_v, cmp_w2_v, gla_wa2, gla_ba, gla_norm, conv_w, conv_b, dt_bias, a_log, d_skip, m_norm,
           w_br_a, w_br_b, w_br_c, w_out, final_norm):
    nbp, t, d = x_prompt.shape
    nbs = x_sample.shape[0]
    depth = w_in.shape[0]
    weights = _prep_weights(w_in, cmp_pe_k, cmp_w1_k, cmp_w2_k, cmp_pe_v, cmp_w1_v, cmp_w2_v,
                            w_br_a, w_br_b, w_br_c, w_out)
    params = dict(zip(_PARAM_NAMES, (ln_w, gla_wa2, gla_ba, gla_norm, conv_w, conv_b, dt_bias, a_log, d_skip, m_norm)))
    caches = tuple(jnp.transpose(c, (0, 1, 3, 4, 2))
                   for c in (cache_cmp_k, cache_cmp_v, cache_slc_k, cache_slc_v, cache_win_k, cache_win_v))
    xp = x_prompt.reshape(nbp * t, d)
    xs = x_sample.reshape(nbs, d)
    p_states, s_out, stacks = [], [], None
    for l in range(depth):
        p = {k: v[l] for k, v in params.items()}
        xp, stacks, st_p = _prompt_layer(xp, nbp, t, l, weights, p, _TILES, stacks)
        xs, st_s = _sample_layer(xs, l, weights, p, caches, (state_gla[l], state_ssm[l], state_conv[l]),
                                 page_table, _TILES)
        p_states.append(st_p)
        s_out.append(st_s)
    y_prompt = rmsnorm_rows(xp, final_norm, F32, _TILES["rms_tm"]).reshape(nbp, t, d)
    y_sample = rmsnorm_rows(xs, final_norm, F32, nbs).reshape(nbs, 1, d)

    def from_kv_layout(a):
        return jnp.transpose(a, (0, 1, 4, 2, 3))

    n_w = min(WINDOW, t)
    kv5 = [a.reshape(depth, nbp, NSA_KV, HD, t) for a in stacks]
    p_leaves = [from_kv_layout(a) for a in kv5[:4]] + [from_kv_layout(a[..., t - n_w:]) for a in kv5[4:]]
    p_leaves += [jnp.stack([st[i] for st in p_states]) for i in range(3)]
    s_leaves = [jnp.stack([o[i] for o in s_out]) for i in range(9)]
    s_leaves[4], s_leaves[5] = from_kv_layout(s_leaves[4]), from_kv_layout(s_leaves[5])
    return (y_prompt, y_sample, *p_leaves, *s_leaves)
```

```python
import functools
import math

import jax
import jax.numpy as jnp
import numpy as np
from jax import lax
from jax.experimental import pallas as pl
from jax.experimental.pallas import tpu as pltpu

F32 = jnp.float32
BF16 = jnp.bfloat16
HI = lax.Precision.HIGHEST

D_MODEL = 2048
PAGE = 128
MIX_W = D_MODEL // 2
HD = 64
NSA_HEADS = MIX_W // HD
NSA_KV = 4
NSA_REP = NSA_HEADS // NSA_KV
CMP_LEN = 32
CMP_STRIDE = 16
SLC_BLK = 64
SLC_TOPK = 16
WINDOW = 512
GLA_HEADS = 4
GLA_DK = MIX_W // 2
GLA_DV = MIX_W
GLA_HK = GLA_DK // GLA_HEADS
GLA_HV = GLA_DV // GLA_HEADS
GLA_RANK = 16
GLA_TAU = 16.0
M_DINNER = MIX_W
M_HD = 64
M_HEADS = M_DINNER // M_HD
M_GROUPS = 4
M_STATE = 128
M_CONV = 4
M_CONVDIM = M_DINNER + 2 * M_GROUPS * M_STATE
EPS = 1e-6
NEG = -1e30
FORCE = 1e4
LOG2E = math.log2(math.e)

_IN_NAMES = ("a_q", "a_kv", "a_g", "a_z", "b_q", "b_k", "b_v", "b_a", "b_z", "c_xbc", "c_dt", "c_z", "m_g")
_IN_SIZES = (NSA_HEADS * HD, 6 * NSA_KV * HD, 3 * NSA_HEADS, MIX_W, GLA_DK, GLA_DK, GLA_DV, GLA_RANK, GLA_DV,
             M_CONVDIM, M_HEADS, M_DINNER, 3 * D_MODEL)
_IN_OFF = dict(zip(_IN_NAMES, np.cumsum((0,) + _IN_SIZES)[:-1].tolist()))
_IN_SZ = dict(zip(_IN_NAMES, _IN_SIZES))

_MAIN_ORDER = ("m_g", "c_xbc", "a_z", "b_v", "b_z", "c_z", "b_q", "b_k", "a_g", "b_a", "c_dt")
_MAIN_OFF = {}
_o = 0
for _n in _MAIN_ORDER:
    _MAIN_OFF[_n] = _o
    _o += _IN_SZ[_n]
OFF_SMALL = _MAIN_OFF["a_g"]
SM_AG, SM_BA, SM_DT = 0, 3 * NSA_HEADS, 3 * NSA_HEADS + GLA_RANK
LANE = 128
MAIN_TN = 512
NP_MAIN = -(-(OFF_SMALL + LANE) // MAIN_TN) * MAIN_TN
KV_ROWS = _IN_SZ["a_kv"]
GRP_ROWS = NSA_KV * HD
_TQ_ORDER = ("a_q", "a_g")
TQ_AG = _IN_SZ["a_q"]
TQ_TM = 384
TQ_ROWS = -(-(TQ_AG + _IN_SZ["a_g"]) // TQ_TM) * TQ_TM
VMEM_LIMIT = 56 * 1024 * 1024


def _cparams(sem):
    return pltpu.CompilerParams(dimension_semantics=sem, vmem_limit_bytes=VMEM_LIMIT)


def _dot(a, b, prec=None, nt=False, tn=False):
    if prec is None:
        a, b = a.astype(BF16), b.astype(BF16)
    dn = (((0,) if tn else (1,), (1,) if nt else (0,)), ((), ()))
    return lax.dot_general(a, b, dn, precision=prec, preferred_element_type=F32)


def _dot_exact_rhs(a, b_bf16):
    hi = a.astype(BF16)
    lo = (a - hi.astype(F32)).astype(BF16)
    return _dot(hi, b_bf16) + _dot(lo, b_bf16)


def _sigmoid(x):
    return 0.5 * jnp.tanh(0.5 * x) + 0.5


def _silu(x):
    return x * _sigmoid(x)


def _log_sigmoid(x):
    return jnp.minimum(x, 0.0) - jnp.log(1.0 + jnp.exp(-jnp.abs(x)))


def _softplus(x):
    return jnp.maximum(x, 0.0) + jnp.log(1.0 + jnp.exp(-jnp.abs(x)))


def _stacked_spec(block, imap, layer):
    if layer is None:
        return pl.BlockSpec(block, imap)
    return pl.BlockSpec((None,) + block, lambda *idx: (layer,) + imap(*idx))


def _rms_kernel(x_ref, w_ref, o_ref):
    x = x_ref[...]
    y = x * lax.rsqrt(jnp.mean(x * x, axis=-1, keepdims=True) + EPS)
    o_ref[...] = (y * w_ref[...]).astype(o_ref.dtype)


def rmsnorm_rows(x, w, out_dtype, tm):
    m, d = x.shape
    return pl.pallas_call(
        _rms_kernel,
        grid=(m // tm,),
        in_specs=[pl.BlockSpec((tm, d), lambda i: (i, 0)), pl.BlockSpec((1, d), lambda i: (0, 0))],
        out_specs=pl.BlockSpec((tm, d), lambda i: (i, 0)),
        out_shape=jax.ShapeDtypeStruct((m, d), out_dtype),
        compiler_params=_cparams(("parallel",)),
        name="rmsnorm",
    )(x, w.reshape(1, d))


def _mm_kernel(*refs, nt, has_res):
    a_ref, b_ref, o_ref = refs[0], refs[1], refs[-1]
    acc = _dot(a_ref[...], b_ref[...], nt=nt)
    if has_res:
        acc = acc + refs[2][...]
    o_ref[...] = acc.astype(o_ref.dtype)


def matmul(a, b, *, nt, tm, tn, res=None, out_dtype=F32, name="matmul", a_layer=None, b_layer=None):
    m, k = a.shape[-2:]
    n = b.shape[-2] if nt else b.shape[-1]
    tm, tn = min(tm, m), min(tn, n)
    if nt:
        b_spec = _stacked_spec((tn, k), lambda i, j: (j, 0), b_layer)
    else:
        b_spec = _stacked_spec((k, tn), lambda i, j: (0, j), b_layer)
    in_specs = [_stacked_spec((tm, k), lambda i, j: (i, 0), a_layer), b_spec]
    args = [a, b]
    if res is not None:
        in_specs.append(pl.BlockSpec((tm, tn), lambda i, j: (i, j)))
        args.append(res)
    return pl.pallas_call(
        functools.partial(_mm_kernel, nt=nt, has_res=res is not None),
        grid=(m // tm, n // tn),
        in_specs=in_specs,
        out_specs=pl.BlockSpec((tm, tn), lambda i, j: (i, j)),
        out_shape=jax.ShapeDtypeStruct((m, n), out_dtype),
        compiler_params=_cparams(("parallel", "parallel")),
        name=name,
    )(*args)


def matmul_t(wt, layer, h, nb, t, *, tm, tn, name):
    _, r, k = wt.shape
    tn = min(tn, t)
    nj = t // tn
    return pl.pallas_call(
        functools.partial(_mm_kernel, nt=True, has_res=False),
        grid=(nb, nj, r // tm),
        in_specs=[pl.BlockSpec((None, tm, k), lambda b, j, i: (layer, i, 0)),
                  pl.BlockSpec((tn, k), lambda b, j, i: (b * nj + j, 0))],
        out_specs=pl.BlockSpec((None, tm, tn), lambda b, j, i: (b, i, j)),
        out_shape=jax.ShapeDtypeStruct((nb, r, t), F32),
        compiler_params=_cparams(("parallel", "parallel", "parallel")),
        name=name,
    )(wt, h)


def _kv_stack_kernel(w_ref, h_ref, *refs):
    outs = refs[-6:]
    hb = h_ref[...]
    for i in range(6):
        outs[i][...] = _dot(w_ref[i * GRP_ROWS:(i + 1) * GRP_ROWS, :], hb, nt=True)


def kv_proj_stacked(wt_kv, layer, h, nb, t, stacks, *, tn):
    depth, _, k = wt_kv.shape
    tn = min(tn, t)
    nj = t // tn
    in_specs = [pl.BlockSpec((None, KV_ROWS, k), lambda b, j: (layer, 0, 0)),
                pl.BlockSpec((tn, k), lambda b, j: (b * nj + j, 0))]
    args = [wt_kv, h]
    aliases = {}
    if stacks is not None:
        in_specs += [pl.BlockSpec(memory_space=pl.ANY)] * 6
        args += list(stacks)
        aliases = {2 + i: i for i in range(6)}
    o_spec = pl.BlockSpec((None, None, GRP_ROWS, tn), lambda b, j: (layer, b, 0, j))
    o_shape = jax.ShapeDtypeStruct((depth, nb, GRP_ROWS, t), F32)
    return pl.pallas_call(
        _kv_stack_kernel,
        grid=(nb, nj),
        in_specs=in_specs,
        out_specs=[o_spec] * 6,
        out_shape=[o_shape] * 6,
        input_output_aliases=aliases,
        compiler_params=_cparams(("parallel", "parallel")),
        name="kv_proj_t",
    )(*args)


def _out_norm_kernel(a_ref, w_ref, res_ref, nw_ref, x_ref, h_ref):
    x = res_ref[...] + _dot(a_ref[...], w_ref[...])
    x_ref[...] = x
    y = x * lax.rsqrt(jnp.mean(x * x, axis=-1, keepdims=True) + EPS)
    h_ref[...] = (y * nw_ref[...]).astype(h_ref.dtype)


def out_proj_norm(mix, wo, layer, res, norm_w, h_dtype, *, tm):
    m, k = mix.shape
    n = wo.shape[-1]
    tm = min(tm, m)
    row = lambda i: (i, 0)
    return pl.pallas_call(
        _out_norm_kernel,
        grid=(m // tm,),
        in_specs=[pl.BlockSpec((tm, k), row), pl.BlockSpec((None, k, n), lambda i: (layer, 0, 0)),
                  pl.BlockSpec((tm, n), row), pl.BlockSpec((1, n), lambda i: (0, 0))],
        out_specs=[pl.BlockSpec((tm, n), row), pl.BlockSpec((tm, n), row)],
        out_shape=[jax.ShapeDtypeStruct((m, n), F32), jax.ShapeDtypeStruct((m, n), h_dtype)],
        compiler_params=_cparams(("parallel",)),
        name="out_proj",
    )(mix, wo, res, norm_w.reshape(1, n))


def _merge_kernel(oa_ref, ob_ref, oc_ref, wa_ref, wb_ref, wc_ref, g0_ref, g1_ref, g2_ref, o_ref):
    acc = _sigmoid(g0_ref[...]) * _dot(oa_ref[...], wa_ref[...])
    acc = acc + _sigmoid(g1_ref[...]) * _dot(ob_ref[...], wb_ref[...])
    acc = acc + _sigmoid(g2_ref[...]) * _dot(oc_ref[...], wc_ref[...])
    o_ref[...] = acc.astype(o_ref.dtype)


def merge_branches(o_a, o_b, o_c, wa, wb, wc, layer, u, *, tm, tn):
    m, k = o_a.shape
    n = wa.shape[-1]
    tm = min(tm, m)
    gb = _MAIN_OFF["m_g"] // tn
    nj = n // tn
    o_spec = pl.BlockSpec((tm, k), lambda i, j: (i, 0))
    w_spec = pl.BlockSpec((None, k, tn), lambda i, j: (layer, 0, j))
    g_specs = [pl.BlockSpec((tm, tn), functools.partial(lambda i, j, br: (i, gb + br * nj + j), br=br))
               for br in range(3)]
    return pl.pallas_call(
        _merge_kernel,
        grid=(m // tm, nj),
        in_specs=[o_spec, o_spec, o_spec, w_spec, w_spec, w_spec] + g_specs,
        out_specs=pl.BlockSpec((tm, tn), lambda i, j: (i, j)),
        out_shape=jax.ShapeDtypeStruct((m, n), BF16),
        compiler_params=_cparams(("parallel", "parallel")),
        name="merge",
    )(o_a, o_b, o_c, wa, wb, wc, u, u, u)


def _cmp_kernel(seg_ref, w1_ref, pe_ref, w2_ref, o_ref, *, nseg):
    w1 = w1_ref[...]
    ab = _dot(seg_ref[...], w1, HI)
    pe = _dot(pe_ref[...], w1, HI)
    cst = pe[0:1, :HD] + pe[1:2, HD:]
    nxt = pltpu.roll(ab[:, HD:], shift=nseg - 1, axis=0)
    pre = ab[:, :HD] + nxt + cst
    o_ref[...] = _dot(_silu(pre), w2_ref[...], HI)


def compress_blocks(seg, w1cat, pe2, w2, nseg):
    _, r, kdim = seg.shape
    return pl.pallas_call(
        functools.partial(_cmp_kernel, nseg=nseg),
        grid=(2, r // nseg),
        in_specs=[pl.BlockSpec((None, nseg, kdim), lambda s, i: (s, i, 0)),
                  pl.BlockSpec((None, kdim, 2 * HD), lambda s, i: (s, 0, 0)),
                  pl.BlockSpec((None, 8, kdim), lambda s, i: (s, 0, 0)),
                  pl.BlockSpec((None, HD, HD), lambda s, i: (s, 0, 0))],
        out_specs=pl.BlockSpec((None, nseg, HD), lambda s, i: (s, i, 0)),
        out_shape=jax.ShapeDtypeStruct((2, r, HD), F32),
        compiler_params=_cparams(("parallel", "parallel")),
        name="compress",
    )(seg, w1cat, pe2, w2)


def _nsa_prompt_kernel(qt_ref, ag_ref, az_ref, kst_ref, vst_ref, kwt_ref, vwt_ref, kc_ref, vc_ref, o_ref,
                       ks_scr, vs_scr, kw_scr, vw_scr, bias_scr, *, t, tq, ck, nseg):
    g = pl.program_id(1)
    i = pl.program_id(2)
    s0 = i * tq
    ns = t // SLC_BLK
    nsr = bias_scr.shape[0]
    cols = NSA_REP * tq

    @pl.when(i == 0)
    def _():
        ks_scr[...] = jnp.transpose(kst_ref[...]).astype(BF16)
        kw_scr[...] = jnp.transpose(kwt_ref[...]).astype(BF16)
        ones = jnp.ones((HD, t), BF16)
        vs_scr[...] = jnp.concatenate([vst_ref[...].astype(BF16), ones], axis=0)
        vw_scr[...] = jnp.concatenate([vwt_ref[...].astype(BF16), ones], axis=0)

    qt = qt_ref[...]
    q2 = jnp.concatenate([qt[r * HD:(r + 1) * HD, :] for r in range(NSA_REP)], axis=1) * (HD ** -0.5 * LOG2E)
    q2b = q2.astype(BF16)
    tpos1 = s0 + lax.broadcasted_iota(jnp.int32, (1, tq), 1)
    tpos = jnp.concatenate([tpos1] * NSA_REP, axis=1)

    crow = lax.broadcasted_iota(jnp.int32, (nseg, 1), 0)
    m_c = (crow * CMP_STRIDE + (CMP_LEN - 1) <= tpos) & (crow < nseg - 1)
    s_c = jnp.where(m_c, _dot(kc_ref[...], q2, HI), NEG)
    p_c = jnp.where(m_c, jnp.exp2(s_c - jnp.max(s_c, axis=0, keepdims=True)), 0.0)
    l_c = jnp.sum(p_c, axis=0, keepdims=True)
    p_c = p_c / jnp.where(l_c > 0.0, l_c, 1.0)
    o_c = _dot(jnp.transpose(vc_ref[...]), p_c)
    psum = p_c[:, 0:tq]
    for r in range(1, NSA_REP):
        psum = psum + p_c[:, r * tq:(r + 1) * tq]

    srow = lax.broadcasted_iota(jnp.int32, (nsr, 1), 0)
    ccol = lax.broadcasted_iota(jnp.int32, (1, nseg), 1) * CMP_STRIDE
    cover = jnp.where((ccol + CMP_LEN > srow * SLC_BLK) & (ccol < (srow + 1) * SLC_BLK), 1.0, 0.0)
    imp = _dot(cover, psum, HI)
    cur = tpos1 // SLC_BLK
    valid = (srow * SLC_BLK <= tpos1) & (srow < ns)
    forced = (srow == 0) | (srow == cur) | (srow == cur - 1)
    score = jnp.where(valid, jnp.where(forced, FORCE, imp), NEG)
    rank = jnp.zeros((nsr, tq), F32)
    for j in range(ns):
        sj = score[j:j + 1, :]
        rank = rank + jnp.where((sj > score) | ((sj == score) & (j < srow)), 1.0, 0.0)
    sel = (rank < float(min(SLC_TOPK, ns))) & valid
    bias_scr[...] = jnp.where(sel, 0.0, NEG)

    def attend(k_rows, v_cols, bias1, m, acc):
        s = _dot(k_rows, q2b)
        s = jnp.concatenate([s[:, r * tq:(r + 1) * tq] + bias1 for r in range(NSA_REP)], axis=1)
        m_new = jnp.maximum(m, jnp.max(s, axis=0, keepdims=True))
        p = jnp.exp2(s - m_new)
        return m_new, jnp.exp2(m - m_new) * acc + _dot(v_cols, p)

    def finish(acc):
        l = acc[HD:HD + 1, :]
        return acc[:HD, :] / jnp.where(l > 0.0, l, 1.0)

    blk_per_chunk = ck // SLC_BLK
    m0 = jnp.full((1, cols), NEG, F32)
    acc0 = jnp.zeros((2 * HD, cols), F32)

    def sel_chunk(j, carry):
        k0 = pl.multiple_of(j * ck, ck)
        npos = k0 + lax.broadcasted_iota(jnp.int32, (ck, 1), 0)
        rows = [jnp.broadcast_to(bias_scr[pl.ds(j * blk_per_chunk + k, 1), :], (SLC_BLK, tq))
                for k in range(blk_per_chunk)]
        bias1 = jnp.where(npos <= tpos1, jnp.concatenate(rows, axis=0), NEG)
        return attend(ks_scr[pl.ds(k0, ck), :], vs_scr[:, pl.ds(k0, ck)], bias1, *carry)

    n_chunks = (s0 + tq + ck - 1) // ck
    o_s = finish(lax.fori_loop(0, n_chunks, sel_chunk, (m0, acc0))[1])

    ww = min(WINDOW + tq, t)
    w0 = pl.multiple_of(jnp.clip(s0 - WINDOW, 0, t - ww), LANE)
    wpos = w0 + lax.broadcasted_iota(jnp.int32, (ww, 1), 0)
    bias_w = jnp.where((wpos <= tpos1) & (wpos > tpos1 - WINDOW), 0.0, NEG)
    o_w = finish(attend(kw_scr[pl.ds(w0, ww), :], vw_scr[:, pl.ds(w0, ww)], bias_w, m0, acc0)[1])

    outs = []
    for r in range(NSA_REP):
        sl = slice(r * tq, (r + 1) * tq)
        gt = [_sigmoid(ag_ref[pl.ds(3 * (NSA_REP * g + r) + j, 1), :]) for j in range(3)]
        outs.append(jnp.transpose(gt[0] * o_c[:, sl] + gt[1] * o_s[:, sl] + gt[2] * o_w[:, sl]))
    o = jnp.concatenate(outs, axis=1) * _silu(az_ref[...])
    o_ref[...] = o.astype(o_ref.dtype)


def nsa_prompt(u, qgt, stacks, layer, kc, nb, t, *, tq, ck):
    tq = min(tq, t)
    nq = t // tq
    nseg = t // CMP_STRIDE
    ck = min(ck, t)
    gw = NSA_REP * HD
    nsr = -(-(t // SLC_BLK) // 8) * 8
    kv_spec = pl.BlockSpec((None, None, HD, t), lambda b, g, i: (layer, b, g, 0))
    return pl.pallas_call(
        functools.partial(_nsa_prompt_kernel, t=t, tq=tq, ck=ck, nseg=nseg),
        grid=(nb, NSA_KV, nq),
        in_specs=[pl.BlockSpec((None, gw, tq), lambda b, g, i: (b, g, i)),
                  pl.BlockSpec((None, LANE, tq), lambda b, g, i: (b, TQ_AG // LANE, i)),
                  pl.BlockSpec((tq, gw), lambda b, g, i: (b * nq + i, _MAIN_OFF["a_z"] // gw + g)),
                  kv_spec, kv_spec, kv_spec, kv_spec,
                  pl.BlockSpec((None, nseg, HD), lambda b, g, i: (0, b * NSA_KV + g, 0)),
                  pl.BlockSpec((None, nseg, HD), lambda b, g, i: (1, b * NSA_KV + g, 0))],
        out_specs=pl.BlockSpec((tq, gw), lambda b, g, i: (b * nq + i, g)),
        out_shape=jax.ShapeDtypeStruct((nb * t, NSA_HEADS * HD), BF16),
        scratch_shapes=[pltpu.VMEM((t, HD), BF16), pltpu.VMEM((2 * HD, t), BF16),
                        pltpu.VMEM((t, HD), BF16), pltpu.VMEM((2 * HD, t), BF16),
                        pltpu.VMEM((nsr, tq), F32)],
        compiler_params=_cparams(("parallel", "parallel", "arbitrary")),
        name="nsa_prompt",
    )(qgt, qgt, u, stacks[2], stacks[3], stacks[4], stacks[5], kc, kc)


def _gla_kernel(q_ref, k_ref, v_ref, z_ref, sm_ref, s0_ref, wa2_ref, ba_ref, gn_ref, o_ref, sout_ref, s_scr,
                *, c, sub, t_valid):
    ci = pl.program_id(1)
    rows = c * sub

    @pl.when(ci == 0)
    def _():
        s_scr[...] = s0_ref[...]

    row = ci * rows + lax.broadcasted_iota(jnp.int32, (rows, 1), 0)
    gate_in = sm_ref[...][:, SM_BA:SM_BA + GLA_RANK]
    log_a = _log_sigmoid(_dot(gate_in, wa2_ref[...], HI) + ba_ref[...]) / GLA_TAU
    log_a = jnp.where(row < t_valid, log_a, 0.0)
    ri = lax.broadcasted_iota(jnp.int32, (rows, rows), 0)
    li = lax.broadcasted_iota(jnp.int32, (rows, rows), 1)
    tri = jnp.where((ri >= li) & (ri // c == li // c), 1.0, 0.0)
    cb_all = _dot(tri, log_a, HI)
    causal = lax.broadcasted_iota(jnp.int32, (c, c), 0) >= lax.broadcasted_iota(jnp.int32, (c, c), 1)
    for h in range(GLA_HEADS):
        ks = slice(h * GLA_HK, (h + 1) * GLA_HK)
        vs = slice(h * GLA_HV, (h + 1) * GLA_HV)
        state = s_scr[h]
        for j in range(sub):
            rs = slice(j * c, (j + 1) * c)
            cb = cb_all[rs, ks]
            ecb = jnp.exp(cb)
            qd = q_ref[rs, ks] * (GLA_HK ** -0.5) * ecb
            kh = k_ref[rs, ks]
            vh = v_ref[rs, vs]
            att = jnp.where(causal, _dot(qd, kh * jnp.exp(-cb), nt=True), 0.0)
            o = _dot(att, vh) + _dot(qd, state)
            c_end = cb[c - 1:c, :]
            k_dec = kh * jnp.exp(c_end - cb)
            e_col = jnp.transpose(ecb)[:, c - 1:c]
            state = e_col * state + _dot(jnp.transpose(k_dec), vh)
            y = o * lax.rsqrt(jnp.mean(o * o, axis=-1, keepdims=True) + EPS) * gn_ref[...]
            o_ref[rs, vs] = (y * _silu(z_ref[rs, vs])).astype(o_ref.dtype)
        s_scr[h] = state

    @pl.when(ci == pl.num_programs(1) - 1)
    def _():
        sout_ref[...] = s_scr[...]


def gla_mixer(u, s0, wa2, ba, gnorm, nb, t, *, c, sub, t_valid):
    rows = c * sub
    nc = t // rows

    def u_spec(name, width):
        return pl.BlockSpec((rows, width), lambda b, i: (b * nc + i, _MAIN_OFF[name] // width))

    s_spec = pl.BlockSpec((None, GLA_HEADS, GLA_HK, GLA_HV), lambda b, i: (b, 0, 0, 0))
    return pl.pallas_call(
        functools.partial(_gla_kernel, c=c, sub=sub, t_valid=t_valid),
        grid=(nb, nc),
        in_specs=[u_spec("b_q", GLA_DK), u_spec("b_k", GLA_DK), u_spec("b_v", GLA_DV), u_spec("b_z", GLA_DV),
                  pl.BlockSpec((rows, LANE), lambda b, i: (b * nc + i, OFF_SMALL // LANE)),
                  s_spec,
                  pl.BlockSpec((GLA_RANK, GLA_DK), lambda b, i: (0, 0)),
                  pl.BlockSpec((1, GLA_DK), lambda b, i: (0, 0)),
                  pl.BlockSpec((1, GLA_HV), lambda b, i: (0, 0))],
        out_specs=[pl.BlockSpec((rows, GLA_DV), lambda b, i: (b * nc + i, 0)), s_spec],
        out_shape=[jax.ShapeDtypeStruct((nb * t, GLA_DV), BF16),
                   jax.ShapeDtypeStruct((nb, GLA_HEADS, GLA_HK, GLA_HV), F32)],
        scratch_shapes=[pltpu.VMEM((GLA_HEADS, GLA_HK, GLA_HV), F32)],
        compiler_params=_cparams(("parallel", "arbitrary")),
        name="gla",
    )(u, u, u, u, u, s0, wa2, ba.reshape(1, GLA_DK), gnorm.reshape(1, GLA_HV))


def _ssd_kernel(xbc_ref, z_ref, sm_ref, s0_ref, cbuf_ref, cw_ref, cbias_ref, dtb_ref, alog_ref, dsk_ref, mn_ref,
                o_ref, sout_ref, cout_ref, s_scr, f_scr, y_scr, *, q, t_valid, n_chunks):
    ci = pl.program_id(1)
    npad = 8
    hist = M_CONV - 1

    @pl.when(ci == 0)
    def _():
        s_scr[...] = s0_ref[...]
        f_scr[npad - hist:npad, :] = cbuf_ref[...]

    f_scr[npad:npad + q, :] = xbc_ref[...]
    conv = cbias_ref[...]
    for i in range(M_CONV):
        conv = conv + f_scr[npad - hist + i:npad - hist + i + q, :] * cw_ref[i:i + 1, :]
    xbc = _silu(conv)

    last_valid = t_valid - (n_chunks - 1) * q

    @pl.when(ci == n_chunks - 1)
    def _():
        cout_ref[...] = f_scr[npad - hist + last_valid:npad + last_valid, :]

    f_scr[npad - hist:npad, :] = f_scr[npad - hist + q:npad + q, :]

    lane = lax.broadcasted_iota(jnp.int32, (1, LANE), 1)
    row = ci * q + lax.broadcasted_iota(jnp.int32, (q, 1), 0)
    head_lane = (lane >= SM_DT) & (lane < SM_DT + M_HEADS)
    dt = jnp.where(head_lane & (row < t_valid), _softplus(sm_ref[...] + dtb_ref[...]), 0.0)
    a = dt * (-jnp.exp(alog_ref[...]))
    tri = jnp.where(lax.broadcasted_iota(jnp.int32, (q, q), 0) >= lax.broadcasted_iota(jnp.int32, (q, q), 1), 1.0, 0.0)
    causal = tri > 0.5
    cum = _dot(tri, a, HI)
    cum_t = jnp.transpose(cum)
    dt_t = jnp.transpose(dt)
    nbc = M_GROUPS * M_STATE
    rep = M_HEADS // M_GROUPS
    for g in range(M_GROUPS):
        bg = xbc[:, M_DINNER + g * M_STATE:M_DINNER + (g + 1) * M_STATE]
        cg = xbc[:, M_DINNER + nbc + g * M_STATE:M_DINNER + nbc + (g + 1) * M_STATE]
        cbm = _dot(cg, bg, nt=True)
        for r in range(rep):
            h = g * rep + r
            hl = SM_DT + h
            xh = xbc[:, h * M_HD:(h + 1) * M_HD]
            cum_c = cum[:, hl:hl + 1]
            decay = jnp.exp(jnp.where(causal, cum_c - cum_t[hl:hl + 1, :], -jnp.inf))
            s_prev = s_scr[h]
            y = _dot(cbm * decay * dt_t[hl:hl + 1, :], xh)
            y = y + jnp.exp(cum_c) * _dot(cg, s_prev, nt=True)
            c_end = cum[q - 1:q, hl:hl + 1]
            w = jnp.exp(c_end - cum_c) * dt[:, hl:hl + 1]
            s_scr[h] = jnp.exp(c_end) * s_prev + _dot(xh * w, bg, tn=True)
            y_scr[:, h * M_HD:(h + 1) * M_HD] = y + dsk_ref[:, hl:hl + 1] * xh
    gw = M_DINNER // M_GROUPS
    for g in range(M_GROUPS):
        sl = slice(g * gw, (g + 1) * gw)
        yg = y_scr[:, sl] * _silu(z_ref[...][:, sl])
        yn = yg * lax.rsqrt(jnp.mean(yg * yg, axis=-1, keepdims=True) + EPS) * mn_ref[:, sl]
        o_ref[:, sl] = yn.astype(o_ref.dtype)

    @pl.when(ci == n_chunks - 1)
    def _():
        sout_ref[...] = s_scr[...]


def _pad_heads(v):
    return jnp.zeros((1, LANE), F32).at[0, SM_DT:SM_DT + M_HEADS].set(v)


def ssd_mixer(u, s0, cbuf, conv_w, conv_b, dt_bias, a_log, d_skip, m_norm, nb, t, *, q, t_valid):
    nc = t // q
    n_chunks = -(-t_valid // q)
    assert n_chunks == nc
    hist = M_CONV - 1

    def u_spec(name, width):
        return pl.BlockSpec((q, width), lambda b, i: (b * nc + i, _MAIN_OFF[name] // width))

    def full(shape):
        return pl.BlockSpec(shape, lambda b, i: (0,) * len(shape))

    s_spec = pl.BlockSpec((None, M_HEADS, M_HD, M_STATE), lambda b, i: (b, 0, 0, 0))
    c_spec = pl.BlockSpec((None, hist, M_CONVDIM), lambda b, i: (b, 0, 0))
    return pl.pallas_call(
        functools.partial(_ssd_kernel, q=q, t_valid=t_valid, n_chunks=n_chunks),
        grid=(nb, nc),
        in_specs=[u_spec("c_xbc", M_CONVDIM), u_spec("c_z", M_DINNER),
                  pl.BlockSpec((q, LANE), lambda b, i: (b * nc + i, OFF_SMALL // LANE)),
                  s_spec, c_spec,
                  full((M_CONV, M_CONVDIM)), full((1, M_CONVDIM)), full((1, LANE)), full((1, LANE)),
                  full((1, LANE)), full((1, M_DINNER))],
        out_specs=[pl.BlockSpec((q, M_DINNER), lambda b, i: (b * nc + i, 0)), s_spec, c_spec],
        out_shape=[jax.ShapeDtypeStruct((nb * t, M_DINNER), BF16),
                   jax.ShapeDtypeStruct((nb, M_HEADS, M_HD, M_STATE), F32),
                   jax.ShapeDtypeStruct((nb, hist, M_CONVDIM), F32)],
        scratch_shapes=[pltpu.VMEM((M_HEADS, M_HD, M_STATE), F32),
                        pltpu.VMEM((q + 8, M_CONVDIM), F32),
                        pltpu.VMEM((q, M_DINNER), F32)],
        compiler_params=_cparams(("parallel", "arbitrary")),
        name="ssd",
    )(u, u, u, s0, cbuf, conv_w, conv_b.reshape(1, M_CONVDIM), _pad_heads(dt_bias), _pad_heads(a_log),
      _pad_heads(d_skip), m_norm.reshape(1, M_DINNER))


SEG_PER_PAGE = PAGE // CMP_STRIDE


def _page_cmp_kernel(pt_ref, *refs, npg):
    wk_ref, wv_ref, ok_ref, ov_ref, scr = refs[2 * npg:]
    n_lane_blk = NSA_KV * HD // LANE
    for pages, w_ref, o_ref in ((refs[:npg], wk_ref, ok_ref), (refs[npg:2 * npg], wv_ref, ov_ref)):
        for k in range(npg):
            page_t = pages[k][...].reshape(NSA_KV * HD, PAGE)
            pos_major = jnp.transpose(page_t)
            for c in range(n_lane_blk):
                scr[c, k * PAGE:(k + 1) * PAGE, :] = pos_major[:, c * LANE:(c + 1) * LANE]
        acc = jnp.zeros((npg * SEG_PER_PAGE, 2 * HD * NSA_KV), F32)
        for j in range(CMP_STRIDE):
            rows = pl.ds(j, npg * SEG_PER_PAGE, stride=CMP_STRIDE)
            xj = jnp.concatenate([scr[c, rows, :] for c in range(n_lane_blk)], axis=1)
            acc = acc + _dot(xj, w_ref[j])
        o_ref[...] = acc


def page_compress(cache_k_t, cache_v_t, page_table, layer, wbd, *, npg):
    nb, n_pages = page_table.shape
    steps = n_pages // npg

    def page_spec(k):
        return pl.BlockSpec((None, None, NSA_KV, HD, PAGE),
                            lambda b, i, pt: (layer, pt[b, i * npg + k], 0, 0, 0))

    def w_spec(s):
        return pl.BlockSpec((None, None, CMP_STRIDE, NSA_KV * HD, NSA_KV * 2 * HD),
                            lambda b, i, pt: (layer, s, 0, 0, 0))

    o_spec = pl.BlockSpec((None, npg * SEG_PER_PAGE, NSA_KV * 2 * HD), lambda b, i, pt: (b, i, 0))
    o_shape = jax.ShapeDtypeStruct((nb, n_pages * SEG_PER_PAGE, NSA_KV * 2 * HD), F32)
    return pl.pallas_call(
        functools.partial(_page_cmp_kernel, npg=npg),
        grid_spec=pltpu.PrefetchScalarGridSpec(
            num_scalar_prefetch=1,
            grid=(nb, steps),
            in_specs=[page_spec(k) for k in range(npg)] * 2 + [w_spec(0), w_spec(1)],
            out_specs=[o_spec, o_spec],
            scratch_shapes=[pltpu.VMEM((NSA_KV * HD // LANE, npg * PAGE, LANE), F32)]),
        out_shape=[o_shape, o_shape],
        compiler_params=_cparams(("parallel", "arbitrary")),
        name="page_compress",
    )(page_table, *([cache_k_t] * npg), *([cache_v_t] * npg), wbd, wbd)


def _masked_softmax_rows(s, mask):
    s = jnp.where(mask, s, NEG)
    m = jnp.max(s, axis=-1, keepdims=True)
    p = jnp.where(mask, jnp.exp(s - m), 0.0)
    l = jnp.sum(p, axis=-1, keepdims=True)
    return p / jnp.where(l > 0.0, l, 1.0)


def _nsa_sample_cmp_kernel(abk_ref, abv_ref, kvs_ref, q_ref, w1_ref, pe_ref, w2_ref, cover_ref, oc_ref, sel_ref,
                           cst_scr, *, p_len, nc, sl):
    g = pl.program_id(1)
    ns = p_len // SLC_BLK + 1
    rowi = lax.broadcasted_iota(jnp.int32, (nc, 1), 0)

    @pl.when((pl.program_id(0) == 0) & (g == 0))
    def _():
        for idx in range(2):
            pe = _dot(pe_ref[idx], w1_ref[idx], HI)
            cst_scr[idx:idx + 1, :] = pe[0:1, :HD] + pe[1:2, HD:]

    def compressed(ab_ref, idx):
        ab = ab_ref[...]
        new = kvs_ref[pl.ds(idx * NSA_KV + g, 1), :]
        b_new = _dot(new, w1_ref[idx][0:HD, HD:])
        nxt = jnp.where(rowi == nc - 1, b_new, pltpu.roll(ab[:, HD:], shift=nc - 1, axis=0))
        return _dot(_silu(ab[:, :HD] + nxt + cst_scr[idx:idx + 1, :]), w2_ref[idx])

    kc = compressed(abk_ref, 0)
    vc = compressed(abv_ref, 1)
    q4 = q_ref[...] * (HD ** -0.5)
    cidx = lax.broadcasted_iota(jnp.int32, (1, nc), 1)
    m_c = cidx * CMP_STRIDE + (CMP_LEN - 1) <= p_len
    p_c = _masked_softmax_rows(_dot(q4, kc, HI, nt=True), m_c)
    oc_ref[...] = _dot(p_c, vc)
    imp = _dot_exact_rhs(jnp.sum(p_c, axis=0, keepdims=True), cover_ref[...])
    sidx = lax.broadcasted_iota(jnp.int32, (1, sl), 1)
    cur = p_len // SLC_BLK
    valid = (sidx * SLC_BLK <= p_len) & (sidx < ns)
    forced = (sidx == 0) | (sidx == cur) | (sidx == cur - 1)
    score = jnp.where(valid, jnp.where(forced, FORCE, imp), NEG)
    ri = lax.broadcasted_iota(jnp.int32, (sl, sl), 0)
    li = lax.broadcasted_iota(jnp.int32, (sl, sl), 1)
    score_b = jnp.broadcast_to(score, (sl, sl))
    score_col = jnp.sum(jnp.where(ri == li, score_b, 0.0), axis=1, keepdims=True)
    ahead = (score_b > score_col) | ((score_b == score_col) & (li < ri))
    rank_col = jnp.sum(jnp.where(ahead, 1.0, 0.0), axis=1, keepdims=True)
    kl = lax.broadcasted_iota(jnp.int32, (sl, LANE), 1).astype(F32)
    sv = lax.broadcasted_iota(jnp.int32, (sl, LANE), 0).astype(F32)
    picked = jnp.sum(jnp.where(rank_col == kl, sv, 0.0), axis=0, keepdims=True)
    sel_ref[...] = picked.astype(jnp.int32)


def _cover_matrix(nc, sl):
    c0 = np.arange(nc)[:, None] * CMP_STRIDE
    s0 = np.arange(sl)[None, :] * SLC_BLK
    return jnp.asarray(((c0 + CMP_LEN > s0) & (c0 < s0 + SLC_BLK)).astype(np.float32), dtype=BF16)


def nsa_sample_cmp(abk, abv, kvs, q3, w1cat, pe2, w2, p_len):
    nb, nc, _ = abk.shape
    ns = p_len // SLC_BLK + 1
    sl = -(-ns // LANE) * LANE
    cover = _cover_matrix(nc, sl)

    def ab_spec():
        return pl.BlockSpec((None, nc, 2 * HD), lambda b, g: (b, 0, g))

    def full(a):
        return pl.BlockSpec(a.shape, lambda b, g: (0,) * a.ndim)

    row_spec = pl.BlockSpec((None, None, NSA_REP, HD), lambda b, g: (b, g, 0, 0))
    return pl.pallas_call(
        functools.partial(_nsa_sample_cmp_kernel, p_len=p_len, nc=nc, sl=sl),
        grid=(nb, NSA_KV),
        in_specs=[ab_spec(), ab_spec(), pl.BlockSpec((None, 6 * NSA_KV, HD), lambda b, g: (b, 0, 0)), row_spec,
                  full(w1cat), full(pe2), full(w2), full(cover)],
        out_specs=[row_spec, pl.BlockSpec((None, None, 1, LANE), lambda b, g: (b, g, 0, 0))],
        out_shape=[jax.ShapeDtypeStruct((nb, NSA_KV, NSA_REP, HD), F32),
                   jax.ShapeDtypeStruct((nb, NSA_KV, 1, LANE), jnp.int32)],
        scratch_shapes=[pltpu.VMEM((8, HD), F32)],
        compiler_params=_cparams(("arbitrary", "arbitrary")),
        name="nsa_sample_cmp",
    )(abk, abv, kvs, q3, w1cat, pe2, w2, cover)


def _nsa_sample_sel_kernel(sel_ref, pt_ref, *refs, p_len, nb, n_buf):
    nk = SLC_TOPK
    kpages, vpages = refs[:nk], refs[nk:2 * nk]
    (kwb_ref, vwb_ref, kvt_ref, kvs_ref, q_ref, az_ref, sm_ref, oc_ref,
     o_ref, kwo_ref, vwo_ref) = refs[2 * nk:]
    b = pl.program_id(0)
    g = pl.program_id(1)
    n_past = p_len // SLC_BLK
    q4 = q_ref[...] * (HD ** -0.5)
    lane_b = lax.broadcasted_iota(jnp.int32, (1, nb), 1) == b

    def new_row(i):
        return kvs_ref[pl.ds(i * NSA_KV + g, 1), :]

    def new_col(i):
        blk = kvt_ref[pl.ds(pl.multiple_of((i * NSA_KV + g) * HD, HD), HD), :]
        return jnp.sum(jnp.where(lane_b, blk, 0.0), axis=1, keepdims=True)

    half_of_lane = lax.broadcasted_iota(jnp.int32, (1, PAGE), 1) // SLC_BLK
    s_parts, m_parts = [], []
    has_new = jnp.zeros((1, 1), jnp.int32)
    for k in range(nk):
        blk = sel_ref[b, g, k]
        m_parts.append((half_of_lane == blk % 2) & (blk < n_past))
        s_parts.append(_dot(q4, kpages[k][...]))
        has_new = has_new + jnp.where(blk == n_past, 1, 0)
    s_sel = jnp.concatenate(s_parts, axis=1)
    m_sel = jnp.concatenate(m_parts, axis=1)
    new_ok = has_new > 0
    s_new = jnp.where(new_ok, jnp.sum(q4 * new_row(2), axis=1, keepdims=True), NEG)
    s_sel = jnp.where(m_sel, s_sel, NEG)
    mx = jnp.maximum(jnp.max(s_sel, axis=1, keepdims=True), s_new)
    p_sel = jnp.where(m_sel, jnp.exp(s_sel - mx), 0.0)
    p_new = jnp.where(new_ok, jnp.exp(s_new - mx), 0.0)
    den = jnp.sum(p_sel, axis=1, keepdims=True) + p_new
    acc = p_new * new_row(3)
    for k in range(nk):
        acc = acc + _dot(p_sel[:, k * PAGE:(k + 1) * PAGE], vpages[k][...], nt=True)
    o_s = acc / jnp.where(den > 0.0, den, 1.0)

    kwb = kwb_ref[...]
    vwb = vwb_ref[...]
    wlane = lax.broadcasted_iota(jnp.int32, (1, n_buf), 1)
    m_w = wlane > n_buf - WINDOW
    s_w = jnp.where(m_w, _dot(q4, kwb), NEG)
    s_wn = jnp.sum(q4 * new_row(4), axis=1, keepdims=True)
    mw = jnp.maximum(jnp.max(s_w, axis=1, keepdims=True), s_wn)
    p_w = jnp.where(m_w, jnp.exp(s_w - mw), 0.0)
    p_wn = jnp.exp(s_wn - mw)
    o_w = (_dot(p_w, vwb, nt=True) + p_wn * new_row(5)) / (jnp.sum(p_w, axis=1, keepdims=True) + p_wn)
    kwo_ref[...] = jnp.where(wlane == n_buf - 1, new_col(4), pltpu.roll(kwb, shift=n_buf - 1, axis=1))
    vwo_ref[...] = jnp.where(wlane == n_buf - 1, new_col(5), pltpu.roll(vwb, shift=n_buf - 1, axis=1))

    gate_row = _sigmoid(pltpu.roll(sm_ref[pl.ds(b, 1), :], shift=(LANE - 3 * NSA_REP * g) % LANE, axis=1))
    gl = lax.broadcasted_iota(jnp.int32, (NSA_REP, LANE), 1)
    gr = lax.broadcasted_iota(jnp.int32, (NSA_REP, LANE), 0)

    def gate(j):
        return jnp.sum(jnp.where(gl == 3 * gr + j, gate_row, 0.0), axis=1, keepdims=True)

    o = gate(0) * oc_ref[...] + gate(1) * o_s + gate(2) * o_w
    o_ref[...] = o * _silu(az_ref[...])


def nsa_sample_sel(sel, page_table, slc_k_t, slc_v_t, win_k_t, win_v_t, layer, kvt, kvs, q3, az3, u, o_cmp, p_len):
    nb = page_table.shape[0]
    n_buf = win_k_t.shape[-1]
    last_blk = p_len // SLC_BLK - 1

    def page_spec(k):
        def imap(b, g, sel_r, pt_r):
            blk = jnp.minimum(sel_r[b, g, k], last_blk)
            return (layer, pt_r[b, blk // (PAGE // SLC_BLK)], g, 0, 0)
        return pl.BlockSpec((None, None, None, HD, PAGE), imap)

    win_spec = pl.BlockSpec((None, None, None, HD, n_buf), lambda b, g, s, p: (layer, b, g, 0, 0))
    wout_spec = pl.BlockSpec((None, None, HD, n_buf), lambda b, g, s, p: (b, g, 0, 0))
    row_spec = pl.BlockSpec((None, None, NSA_REP, HD), lambda b, g, s, p: (b, g, 0, 0))

    def full(a):
        return pl.BlockSpec(a.shape, lambda b, g, s, p: (0,) * a.ndim)

    small = pl.BlockSpec((nb, LANE), lambda b, g, s, p: (0, OFF_SMALL // LANE))
    return pl.pallas_call(
        functools.partial(_nsa_sample_sel_kernel, p_len=p_len, nb=nb, n_buf=n_buf),
        grid_spec=pltpu.PrefetchScalarGridSpec(
            num_scalar_prefetch=2,
            grid=(nb, NSA_KV),
            in_specs=[page_spec(k) for k in range(SLC_TOPK)] * 2
                     + [win_spec, win_spec, full(kvt), pl.BlockSpec((None, 6 * NSA_KV, HD), lambda b, g, s, p: (b, 0, 0)),
                        row_spec, row_spec, small, row_spec],
            out_specs=[row_spec, wout_spec, wout_spec]),
        out_shape=[jax.ShapeDtypeStruct((nb, NSA_KV, NSA_REP, HD), F32),
                   jax.ShapeDtypeStruct((nb, NSA_KV, HD, n_buf), F32),
                   jax.ShapeDtypeStruct((nb, NSA_KV, HD, n_buf), F32)],
        compiler_params=_cparams(("parallel", "arbitrary")),
        name="nsa_sample_sel",
    )(sel, page_table, *([slc_k_t] * SLC_TOPK), *([slc_v_t] * SLC_TOPK), win_k_t, win_v_t, kvt, kvs, q3, az3, u, o_cmp)


def _block_diag_w1(w1cat):
    d = w1cat.shape[0]
    wj = w1cat.reshape(d, 2, CMP_STRIDE, HD, 2 * HD)
    eye = jnp.eye(NSA_KV, dtype=w1cat.dtype)
    bd = jnp.einsum("lsjdo,gh->lsjgdho", wj, eye)
    return bd.reshape(d, 2, CMP_STRIDE, NSA_KV * HD, NSA_KV * 2 * HD).astype(BF16)


def _prep_weights(w_in, cmp_pe_k, cmp_w1_k, cmp_w2_k, cmp_pe_v, cmp_w1_v, cmp_w2_v, w_br_a, w_br_b, w_br_c, w_out):
    wt = jnp.transpose(w_in, (0, 2, 1))

    def gather_rows(order, total):
        parts = [wt[:, _IN_OFF[n]:_IN_OFF[n] + _IN_SZ[n]] for n in order]
        used = sum(_IN_SZ[n] for n in order)
        parts.append(jnp.zeros((wt.shape[0], total - used, wt.shape[2]), wt.dtype))
        return jnp.concatenate(parts, axis=1).astype(BF16)

    half = CMP_STRIDE * HD

    def cat(w1):
        return jnp.concatenate([w1[:, :half], w1[:, half:]], axis=2)

    def pe_rows(pe):
        d = pe.shape[0]
        rows = pe.reshape(d, 2, half)
        return jnp.concatenate([rows, jnp.zeros((d, 6, half), pe.dtype)], axis=1)

    w1cat = jnp.stack([cat(cmp_w1_k), cat(cmp_w1_v)], axis=1)
    pe2 = jnp.stack([pe_rows(cmp_pe_k), pe_rows(cmp_pe_v)], axis=1)
    w2 = jnp.stack([cmp_w2_k, cmp_w2_v], axis=1)
    return dict(wt_main=gather_rows(_MAIN_ORDER, NP_MAIN), wt_kv=gather_rows(("a_kv",), KV_ROWS),
                wt_q=gather_rows(_TQ_ORDER, TQ_ROWS),
                w1cat=w1cat, pe2=pe2, w2=w2, wbd=_block_diag_w1(w1cat),
                wa=w_br_a.astype(BF16), wb=w_br_b.astype(BF16), wc=w_br_c.astype(BF16), wo=w_out.astype(BF16))


def _prompt_layer(x, h, nb, t, layer, w, p, next_norm, tiles, stacks):
    u = matmul(h, w["wt_main"], nt=True, tm=tiles["mm_tm"], tn=tiles["in_tn"], name="in_proj", b_layer=layer)
    stacks = kv_proj_stacked(w["wt_kv"], layer, h, nb, t, stacks, tn=tiles["kv_tn"])
    qgt = matmul_t(w["wt_q"], layer, h, nb, t, tm=TQ_TM, tn=tiles["q_tn"], name="q_proj_t")
    nseg = t // CMP_STRIDE
    seg = jnp.stack([stacks[0][layer], stacks[1][layer]]).reshape(2, nb, NSA_KV, HD, t)
    seg = jnp.transpose(seg, (0, 1, 2, 4, 3)).reshape(2, nb * NSA_KV * nseg, CMP_STRIDE * HD)
    kc = compress_blocks(seg, w["w1cat"][layer], w["pe2"][layer], w["w2"][layer], nseg)
    o_a = nsa_prompt(u, qgt, stacks, layer, kc, nb, t, tq=tiles["tq"], ck=tiles["ck"])
    zeros_gla = jnp.zeros((nb, GLA_HEADS, GLA_HK, GLA_HV), F32)
    zeros_ssm = jnp.zeros((nb, M_HEADS, M_HD, M_STATE), F32)
    zeros_conv = jnp.zeros((nb, M_CONV - 1, M_CONVDIM), F32)
    o_b, gla_s = gla_mixer(u, zeros_gla, p["gla_wa2"], p["gla_ba"], p["gla_norm"], nb, t,
                           c=tiles["gla_c"], sub=min(tiles["gla_sub"], t // tiles["gla_c"]), t_valid=t)
    o_c, ssm_s, conv_s = ssd_mixer(u, zeros_ssm, zeros_conv, p["conv_w"], p["conv_b"], p["dt_bias"], p["a_log"],
                                   p["d_skip"], p["m_norm"], nb, t, q=min(tiles["ssd_q"], t), t_valid=t)
    mix = merge_branches(o_a, o_b, o_c, w["wa"], w["wb"], w["wc"], layer, u, tm=tiles["mg_tm"], tn=tiles["mg_tn"])
    x_out, h_out = out_proj_norm(mix, w["wo"], layer, x, *next_norm, tm=tiles["out_tm"])
    return x_out, h_out, stacks, (gla_s, ssm_s, conv_s)


ROW_PAD = 8


def _sample_layer(x, h, layer, w, p, next_norm, caches, states, page_table, tiles):
    nb = x.shape[0]
    p_len = page_table.shape[1] * PAGE
    ck_t, cv_t, sk_t, sv_t, wk_t, wv_t = caches
    u = matmul(h, w["wt_main"], nt=True, tm=nb, tn=MAIN_TN, name="in_proj_s", b_layer=layer)
    kv = matmul(h, w["wt_kv"], nt=True, tm=nb, tn=512, name="kv_proj_s", b_layer=layer)
    kvt = matmul(w["wt_kv"], h, nt=True, tm=512, tn=nb, name="kv_proj_st", a_layer=layer)
    qg = matmul(h, w["wt_q"], nt=True, tm=nb, tn=TQ_TM, name="q_proj_s", b_layer=layer)
    kvs = kv.reshape(nb, 6 * NSA_KV, HD)
    u_pad = jnp.pad(u.reshape(nb, 1, NP_MAIN), ((0, 0), (0, ROW_PAD - 1), (0, 0))).reshape(nb * ROW_PAD, NP_MAIN)
    o_b, gla_s = gla_mixer(u_pad, states[0], p["gla_wa2"], p["gla_ba"], p["gla_norm"], nb, ROW_PAD,
                           c=ROW_PAD, sub=1, t_valid=1)
    o_c, ssm_s, conv_s = ssd_mixer(u_pad, states[1], states[2], p["conv_w"], p["conv_b"], p["dt_bias"], p["a_log"],
                                   p["d_skip"], p["m_norm"], nb, ROW_PAD, q=ROW_PAD, t_valid=1)
    o_b = o_b.reshape(nb, ROW_PAD, GLA_DV)[:, 0]
    o_c = o_c.reshape(nb, ROW_PAD, M_DINNER)[:, 0]
    abk, abv = page_compress(ck_t, cv_t, page_table, layer, w["wbd"], npg=min(tiles["npg"], page_table.shape[1]))
    q3 = qg[:, :TQ_AG].reshape(nb, NSA_KV, NSA_REP, HD)
    az3 = u[:, _MAIN_OFF["a_z"]:_MAIN_OFF["a_z"] + MIX_W].reshape(nb, NSA_KV, NSA_REP, HD)
    o_cmp, sel = nsa_sample_cmp(abk, abv, kvs, q3, w["w1cat"][layer], w["pe2"][layer], w["w2"][layer], p_len)
    o_a, win_k, win_v = nsa_sample_sel(sel.reshape(nb, NSA_KV, LANE), page_table, sk_t, sv_t, wk_t, wv_t, layer,
                                       kvt, kvs, q3, az3, u, o_cmp, p_len)
    mix = merge_branches(o_a.reshape(nb, MIX_W).astype(BF16), o_b, o_c, w["wa"], w["wb"], w["wc"], layer, u,
                         tm=nb, tn=tiles["mg_tn"])
    x_out, h_out = out_proj_norm(mix, w["wo"], layer, x, *next_norm, tm=nb)
    kv4 = kv.reshape(nb, 6, 1, NSA_KV, HD)
    return x_out, h_out, (kv4[:, 0], kv4[:, 1], kv4[:, 2], kv4[:, 3], win_k, win_v, gla_s, ssm_s, conv_s)


_TILES = dict(rms_tm=256, mm_tm=1024, in_tn=1536, kv_tn=512, q_tn=2048, tq=256, ck=512, gla_c=64, gla_sub=4,
              ssd_q=256, mg_tm=1024, mg_tn=512, out_tm=512, npg=16)
_PARAM_NAMES = ("ln_w", "gla_wa2", "gla_ba", "gla_norm", "conv_w", "conv_b", "dt_bias", "a_log", "d_skip", "m_norm")


def kernel(x_prompt, x_sample, cache_cmp_k, cache_cmp_v, cache_slc_k, cache_slc_v, cache_win_k, cache_win_v,
           state_gla, state_ssm, state_conv, page_table, ln_w, w_in, cmp_pe_k, cmp_w1_k, cmp_w2_k, cmp_pe_v,
           cmp_w1_v, cmp_w2_v, gla_wa2, gla_ba, gla_norm, conv_w, conv_b, dt_bias, a_log, d_skip, m_norm,
           w_br_a, w_br_b, w_br_c, w_out, final_norm):
    nbp, t, d = x_prompt.shape
    nbs = x_sample.shape[0]
    depth = w_in.shape[0]
    weights = _prep_weights(w_in, cmp_pe_k, cmp_w1_k, cmp_w2_k, cmp_pe_v, cmp_w1_v, cmp_w2_v,
                            w_br_a, w_br_b, w_br_c, w_out)
    params = dict(zip(_PARAM_NAMES, (ln_w, gla_wa2, gla_ba, gla_norm, conv_w, conv_b, dt_bias, a_log, d_skip, m_norm)))
    caches = tuple(jnp.transpose(c, (0, 1, 3, 4, 2))
                   for c in (cache_cmp_k, cache_cmp_v, cache_slc_k, cache_slc_v, cache_win_k, cache_win_v))
    xp = x_prompt.reshape(nbp * t, d)
    xs = x_sample.reshape(nbs, d)
    p_states, s_out, stacks = [], [], None
    hp = rmsnorm_rows(xp, ln_w[0], BF16, _TILES["rms_tm"])
    hs = rmsnorm_rows(xs, ln_w[0], BF16, nbs)
    for l in range(depth):
        p = {k: v[l] for k, v in params.items()}
        next_norm = (ln_w[l + 1], BF16) if l + 1 < depth else (final_norm, F32)
        xp, hp, stacks, st_p = _prompt_layer(xp, hp, nbp, t, l, weights, p, next_norm, _TILES, stacks)
        xs, hs, st_s = _sample_layer(xs, hs, l, weights, p, next_norm, caches,
                                     (state_gla[l], state_ssm[l], state_conv[l]), page_table, _TILES)
        p_states.append(st_p)
        s_out.append(st_s)
    y_prompt = hp.reshape(nbp, t, d)
    y_sample = hs.reshape(nbs, 1, d)

    def from_kv_layout(a):
        return jnp.transpose(a, (0, 1, 4, 2, 3))

    n_w = min(WINDOW, t)
    kv5 = [a.reshape(depth, nbp, NSA_KV, HD, t) for a in stacks]
    p_leaves = [from_kv_layout(a) for a in kv5[:4]] + [from_kv_layout(a[..., t - n_w:]) for a in kv5[4:]]
    p_leaves += [jnp.stack([st[i] for st in p_states]) for i in range(3)]
    s_leaves = [jnp.stack([o[i] for o in s_out]) for i in range(9)]
    s_leaves[4], s_leaves[5] = from_kv_layout(s_leaves[4]), from_kv_layout(s_leaves[5])
    return (y_prompt, y_sample, *p_leaves, *s_leaves)
```

```python
import functools
import math

import jax
import jax.numpy as jnp
import numpy as np
from jax import lax
from jax.experimental import pallas as pl
from jax.experimental.pallas import tpu as pltpu

F32 = jnp.float32
BF16 = jnp.bfloat16
HI = lax.Precision.HIGHEST

D_MODEL = 2048
PAGE = 128
MIX_W = D_MODEL // 2
HD = 64
NSA_HEADS = MIX_W // HD
NSA_KV = 4
NSA_REP = NSA_HEADS // NSA_KV
CMP_LEN = 32
CMP_STRIDE = 16
SLC_BLK = 64
SLC_TOPK = 16
WINDOW = 512
GLA_HEADS = 4
GLA_DK = MIX_W // 2
GLA_DV = MIX_W
GLA_HK = GLA_DK // GLA_HEADS
GLA_HV = GLA_DV // GLA_HEADS
GLA_RANK = 16
GLA_TAU = 16.0
M_DINNER = MIX_W
M_HD = 64
M_HEADS = M_DINNER // M_HD
M_GROUPS = 4
M_STATE = 128
M_CONV = 4
M_CONVDIM = M_DINNER + 2 * M_GROUPS * M_STATE
EPS = 1e-6
NEG = -1e30
FORCE = 1e4
LOG2E = math.log2(math.e)
ONES_ROWS = 16

_IN_NAMES = ("a_q", "a_kv", "a_g", "a_z", "b_q", "b_k", "b_v", "b_a", "b_z", "c_xbc", "c_dt", "c_z", "m_g")
_IN_SIZES = (NSA_HEADS * HD, 6 * NSA_KV * HD, 3 * NSA_HEADS, MIX_W, GLA_DK, GLA_DK, GLA_DV, GLA_RANK, GLA_DV,
             M_CONVDIM, M_HEADS, M_DINNER, 3 * D_MODEL)
_IN_OFF = dict(zip(_IN_NAMES, np.cumsum((0,) + _IN_SIZES)[:-1].tolist()))
_IN_SZ = dict(zip(_IN_NAMES, _IN_SIZES))

LANE = 128
_UG_ORDER = ("m_g", "a_z", "b_v", "b_z", "c_z", "b_q", "b_k")
_UX_ORDER = ("c_xbc", "a_g", "b_a", "c_dt")


def _offsets(order):
    off, o = {}, 0
    for n in order:
        off[n] = o
        o += _IN_SZ[n]
    return off, o


_UG_OFF, NP_UG = _offsets(_UG_ORDER)
_UX_OFF, _ux_used = _offsets(_UX_ORDER)
NP_UX = -(-_ux_used // LANE) * LANE
OFF_SMALL = _UX_OFF["a_g"]
SM_AG, SM_BA, SM_DT = 0, 3 * NSA_HEADS, 3 * NSA_HEADS + GLA_RANK
KV_ROWS = _IN_SZ["a_kv"]
GRP_ROWS = NSA_KV * HD
_TQ_ORDER = ("a_q", "a_g")
TQ_AG = _IN_SZ["a_q"]
TQ_TM = 384
TQ_ROWS = -(-(TQ_AG + _IN_SZ["a_g"]) // TQ_TM) * TQ_TM
VMEM_LIMIT = 56 * 1024 * 1024


def _cparams(sem):
    return pltpu.CompilerParams(dimension_semantics=sem, vmem_limit_bytes=VMEM_LIMIT)


def _dot(a, b, prec=None, nt=False, tn=False):
    if prec is None:
        a, b = a.astype(BF16), b.astype(BF16)
    dn = (((0,) if tn else (1,), (1,) if nt else (0,)), ((), ()))
    return lax.dot_general(a, b, dn, precision=prec, preferred_element_type=F32)


def _dot_exact_rhs(a, b_bf16):
    hi = a.astype(BF16)
    lo = (a - hi.astype(F32)).astype(BF16)
    return _dot(hi, b_bf16) + _dot(lo, b_bf16)


def _sigmoid(x):
    return 0.5 * jnp.tanh(0.5 * x) + 0.5


def _silu(x):
    return x * _sigmoid(x)


def _log_sigmoid(x):
    return jnp.minimum(x, 0.0) - jnp.log(1.0 + jnp.exp(-jnp.abs(x)))


def _softplus(x):
    return jnp.maximum(x, 0.0) + jnp.log(1.0 + jnp.exp(-jnp.abs(x)))


def _stacked_spec(block, imap, layer):
    if layer is None:
        return pl.BlockSpec(block, imap)
    return pl.BlockSpec((None,) + block, lambda *idx: (layer,) + imap(*idx))


def _rms_kernel(x_ref, w_ref, o_ref):
    x = x_ref[...]
    y = x * lax.rsqrt(jnp.mean(x * x, axis=-1, keepdims=True) + EPS)
    o_ref[...] = (y * w_ref[...]).astype(o_ref.dtype)


def rmsnorm_rows(x, w, out_dtype, tm):
    m, d = x.shape
    return pl.pallas_call(
        _rms_kernel,
        grid=(m // tm,),
        in_specs=[pl.BlockSpec((tm, d), lambda i: (i, 0)), pl.BlockSpec((1, d), lambda i: (0, 0))],
        out_specs=pl.BlockSpec((tm, d), lambda i: (i, 0)),
        out_shape=jax.ShapeDtypeStruct((m, d), out_dtype),
        compiler_params=_cparams(("parallel",)),
        name="rmsnorm",
    )(x, w.reshape(1, d))


def _mm_kernel(*refs, nt, has_res):
    a_ref, b_ref, o_ref = refs[0], refs[1], refs[-1]
    acc = _dot(a_ref[...], b_ref[...], nt=nt)
    if has_res:
        acc = acc + refs[2][...]
    o_ref[...] = acc.astype(o_ref.dtype)


def matmul(a, b, *, nt, tm, tn, res=None, out_dtype=F32, name="matmul", a_layer=None, b_layer=None):
    m, k = a.shape[-2:]
    n = b.shape[-2] if nt else b.shape[-1]
    tm, tn = min(tm, m), min(tn, n)
    if nt:
        b_spec = _stacked_spec((tn, k), lambda i, j: (j, 0), b_layer)
    else:
        b_spec = _stacked_spec((k, tn), lambda i, j: (0, j), b_layer)
    in_specs = [_stacked_spec((tm, k), lambda i, j: (i, 0), a_layer), b_spec]
    args = [a, b]
    if res is not None:
        in_specs.append(pl.BlockSpec((tm, tn), lambda i, j: (i, j)))
        args.append(res)
    return pl.pallas_call(
        functools.partial(_mm_kernel, nt=nt, has_res=res is not None),
        grid=(m // tm, n // tn),
        in_specs=in_specs,
        out_specs=pl.BlockSpec((tm, tn), lambda i, j: (i, j)),
        out_shape=jax.ShapeDtypeStruct((m, n), out_dtype),
        compiler_params=_cparams(("parallel", "parallel")),
        name=name,
    )(*args)


def matmul_t(wt, layer, h, nb, t, *, tm, tn, name):
    _, r, k = wt.shape
    tn = min(tn, t)
    nj = t // tn
    return pl.pallas_call(
        functools.partial(_mm_kernel, nt=True, has_res=False),
        grid=(nb, nj, r // tm),
        in_specs=[pl.BlockSpec((None, tm, k), lambda b, j, i: (layer, i, 0)),
                  pl.BlockSpec((tn, k), lambda b, j, i: (b * nj + j, 0))],
        out_specs=pl.BlockSpec((None, tm, tn), lambda b, j, i: (b, i, j)),
        out_shape=jax.ShapeDtypeStruct((nb, r, t), F32),
        compiler_params=_cparams(("parallel", "parallel", "parallel")),
        name=name,
    )(wt, h)


def _kv_stack_kernel(w_ref, h_ref, *refs):
    outs = refs[-6:]
    hb = h_ref[...]
    for i in range(6):
        outs[i][...] = _dot(w_ref[i * GRP_ROWS:(i + 1) * GRP_ROWS, :], hb, nt=True)


def kv_proj_stacked(wt_kv, layer, h, nb, t, stacks, *, tn):
    depth, _, k = wt_kv.shape
    tn = min(tn, t)
    nj = t // tn
    in_specs = [pl.BlockSpec((None, KV_ROWS, k), lambda b, j: (layer, 0, 0)),
                pl.BlockSpec((tn, k), lambda b, j: (b * nj + j, 0))]
    args = [wt_kv, h]
    aliases = {}
    if stacks is not None:
        in_specs += [pl.BlockSpec(memory_space=pl.ANY)] * 6
        args += list(stacks)
        aliases = {2 + i: i for i in range(6)}
    o_spec = pl.BlockSpec((None, None, GRP_ROWS, tn), lambda b, j: (layer, b, 0, j))
    o_shape = jax.ShapeDtypeStruct((depth, nb, GRP_ROWS, t), F32)
    return pl.pallas_call(
        _kv_stack_kernel,
        grid=(nb, nj),
        in_specs=in_specs,
        out_specs=[o_spec] * 6,
        out_shape=[o_shape] * 6,
        input_output_aliases=aliases,
        compiler_params=_cparams(("parallel", "parallel")),
        name="kv_proj_t",
    )(*args)


def _out_norm_kernel(a_ref, w_ref, res_ref, nw_ref, x_ref, h_ref):
    x = res_ref[...] + _dot(a_ref[...], w_ref[...])
    x_ref[...] = x
    y = x * lax.rsqrt(jnp.mean(x * x, axis=-1, keepdims=True) + EPS)
    h_ref[...] = (y * nw_ref[...]).astype(h_ref.dtype)


def out_proj_norm(mix, wo, layer, res, norm_w, h_dtype, *, tm):
    m, k = mix.shape
    n = wo.shape[-1]
    tm = min(tm, m)
    row = lambda i: (i, 0)
    return pl.pallas_call(
        _out_norm_kernel,
        grid=(m // tm,),
        in_specs=[pl.BlockSpec((tm, k), row), pl.BlockSpec((None, k, n), lambda i: (layer, 0, 0)),
                  pl.BlockSpec((tm, n), row), pl.BlockSpec((1, n), lambda i: (0, 0))],
        out_specs=[pl.BlockSpec((tm, n), row), pl.BlockSpec((tm, n), row)],
        out_shape=[jax.ShapeDtypeStruct((m, n), F32), jax.ShapeDtypeStruct((m, n), h_dtype)],
        compiler_params=_cparams(("parallel",)),
        name="out_proj",
    )(mix, wo, res, norm_w.reshape(1, n))


def _merge_kernel(oa_ref, ob_ref, oc_ref, wa_ref, wb_ref, wc_ref, g0_ref, g1_ref, g2_ref, o_ref):
    acc = _sigmoid(g0_ref[...].astype(F32)) * _dot(oa_ref[...], wa_ref[...])
    acc = acc + _sigmoid(g1_ref[...].astype(F32)) * _dot(ob_ref[...], wb_ref[...])
    acc = acc + _sigmoid(g2_ref[...].astype(F32)) * _dot(oc_ref[...], wc_ref[...])
    o_ref[...] = acc.astype(o_ref.dtype)


def merge_branches(o_a, o_b, o_c, wa, wb, wc, layer, ug, *, tm, tn):
    m, k = o_a.shape
    n = wa.shape[-1]
    tm = min(tm, m)
    gb = _UG_OFF["m_g"] // tn
    nj = n // tn
    o_spec = pl.BlockSpec((tm, k), lambda i, j: (i, 0))
    w_spec = pl.BlockSpec((None, k, tn), lambda i, j: (layer, 0, j))
    g_specs = [pl.BlockSpec((tm, tn), functools.partial(lambda i, j, br: (i, gb + br * nj + j), br=br))
               for br in range(3)]
    return pl.pallas_call(
        _merge_kernel,
        grid=(m // tm, nj),
        in_specs=[o_spec, o_spec, o_spec, w_spec, w_spec, w_spec] + g_specs,
        out_specs=pl.BlockSpec((tm, tn), lambda i, j: (i, j)),
        out_shape=jax.ShapeDtypeStruct((m, n), BF16),
        compiler_params=_cparams(("parallel", "parallel")),
        name="merge",
    )(o_a, o_b, o_c, wa, wb, wc, ug, ug, ug)


def _cmp_kernel(seg_ref, w1_ref, pe_ref, w2_ref, o_ref, *, nseg):
    w1 = w1_ref[...]
    ab = _dot(seg_ref[...], w1, HI)
    pe = _dot(pe_ref[...], w1, HI)
    cst = pe[0:1, :HD] + pe[1:2, HD:]
    nxt = pltpu.roll(ab[:, HD:], shift=nseg - 1, axis=0)
    pre = ab[:, :HD] + nxt + cst
    o_ref[...] = _dot(_silu(pre), w2_ref[...], HI)


def compress_blocks(seg, w1cat, pe2, w2, nseg):
    _, r, kdim = seg.shape
    return pl.pallas_call(
        functools.partial(_cmp_kernel, nseg=nseg),
        grid=(2, r // nseg),
        in_specs=[pl.BlockSpec((None, nseg, kdim), lambda s, i: (s, i, 0)),
                  pl.BlockSpec((None, kdim, 2 * HD), lambda s, i: (s, 0, 0)),
                  pl.BlockSpec((None, 8, kdim), lambda s, i: (s, 0, 0)),
                  pl.BlockSpec((None, HD, HD), lambda s, i: (s, 0, 0))],
        out_specs=pl.BlockSpec((None, nseg, HD), lambda s, i: (s, i, 0)),
        out_shape=jax.ShapeDtypeStruct((2, r, HD), F32),
        compiler_params=_cparams(("parallel", "parallel")),
        name="compress",
    )(seg, w1cat, pe2, w2)


def _nsa_prompt_kernel(qt_ref, ag_ref, az_ref, kst_ref, vst_ref, kwt_ref, vwt_ref, kc_ref, vc_ref, o_ref,
                       ks_scr, vs_scr, kw_scr, vw_scr, bias_scr, *, t, tq, ck, nseg):
    g = pl.program_id(1)
    i = pl.program_id(2)
    s0 = i * tq
    ns = t // SLC_BLK
    nsr = bias_scr.shape[0]
    cols = NSA_REP * tq

    @pl.when(i == 0)
    def _():
        ks_scr[...] = jnp.transpose(kst_ref[...]).astype(BF16)
        kw_scr[...] = jnp.transpose(kwt_ref[...]).astype(BF16)
        ones = jnp.ones((ONES_ROWS, t), BF16)
        vs_scr[...] = jnp.concatenate([vst_ref[...].astype(BF16), ones], axis=0)
        vw_scr[...] = jnp.concatenate([vwt_ref[...].astype(BF16), ones], axis=0)

    qt = qt_ref[...]
    q2 = jnp.concatenate([qt[r * HD:(r + 1) * HD, :] for r in range(NSA_REP)], axis=1) * (HD ** -0.5 * LOG2E)
    q2b = q2.astype(BF16)
    tpos1 = s0 + lax.broadcasted_iota(jnp.int32, (1, tq), 1)
    tpos = jnp.concatenate([tpos1] * NSA_REP, axis=1)

    crow = lax.broadcasted_iota(jnp.int32, (nseg, 1), 0)
    m_c = (crow * CMP_STRIDE + (CMP_LEN - 1) <= tpos) & (crow < nseg - 1)
    s_c = jnp.where(m_c, _dot(kc_ref[...], q2, HI), NEG)
    p_c = jnp.where(m_c, jnp.exp2(s_c - jnp.max(s_c, axis=0, keepdims=True)), 0.0)
    l_c = jnp.sum(p_c, axis=0, keepdims=True)
    p_c = p_c / jnp.where(l_c > 0.0, l_c, 1.0)
    o_c = _dot(jnp.transpose(vc_ref[...]), p_c)
    psum = p_c[:, 0:tq]
    for r in range(1, NSA_REP):
        psum = psum + p_c[:, r * tq:(r + 1) * tq]

    srow = lax.broadcasted_iota(jnp.int32, (nsr, 1), 0)
    ccol = lax.broadcasted_iota(jnp.int32, (1, nseg), 1) * CMP_STRIDE
    cover = jnp.where((ccol + CMP_LEN > srow * SLC_BLK) & (ccol < (srow + 1) * SLC_BLK), 1.0, 0.0)
    imp = _dot(cover, psum, HI)
    cur = tpos1 // SLC_BLK
    valid = (srow * SLC_BLK <= tpos1) & (srow < ns)
    forced = (srow == 0) | (srow == cur) | (srow == cur - 1)
    score = jnp.where(valid, jnp.where(forced, FORCE, imp), NEG)
    rank = jnp.zeros((nsr, tq), F32)
    for j in range(ns):
        sj = score[j:j + 1, :]
        rank = rank + jnp.where((sj > score) | ((sj == score) & (j < srow)), 1.0, 0.0)
    sel = (rank < float(min(SLC_TOPK, ns))) & valid
    bias_scr[...] = jnp.where(sel, 0.0, NEG)

    def attend(k_rows, v_cols, bias1, m, acc):
        s = _dot(k_rows, q2b)
        s = jnp.concatenate([s[:, r * tq:(r + 1) * tq] + bias1 for r in range(NSA_REP)], axis=1)
        m_new = jnp.maximum(m, jnp.max(s, axis=0, keepdims=True))
        p = jnp.exp2(s - m_new)
        return m_new, jnp.exp2(m - m_new) * acc + _dot(v_cols, p)

    def finish(acc):
        l = acc[HD:HD + 1, :]
        return acc[:HD, :] / jnp.where(l > 0.0, l, 1.0)

    blk_per_chunk = ck // SLC_BLK
    m0 = jnp.full((1, cols), NEG, F32)
    acc0 = jnp.zeros((HD + ONES_ROWS, cols), F32)

    def sel_chunk(j, carry):
        k0 = pl.multiple_of(j * ck, ck)
        npos = k0 + lax.broadcasted_iota(jnp.int32, (ck, 1), 0)
        rows = [jnp.broadcast_to(bias_scr[pl.ds(j * blk_per_chunk + k, 1), :], (SLC_BLK, tq))
                for k in range(blk_per_chunk)]
        bias1 = jnp.where(npos <= tpos1, jnp.concatenate(rows, axis=0), NEG)
        return attend(ks_scr[pl.ds(k0, ck), :], vs_scr[:, pl.ds(k0, ck)], bias1, *carry)

    n_chunks = (s0 + tq + ck - 1) // ck
    o_s = finish(lax.fori_loop(0, n_chunks, sel_chunk, (m0, acc0))[1])

    ww = min(WINDOW + tq, t)
    w0 = pl.multiple_of(jnp.clip(s0 - WINDOW, 0, t - ww), LANE)
    wpos = w0 + lax.broadcasted_iota(jnp.int32, (ww, 1), 0)
    bias_w = jnp.where((wpos <= tpos1) & (wpos > tpos1 - WINDOW), 0.0, NEG)
    o_w = finish(attend(kw_scr[pl.ds(w0, ww), :], vw_scr[:, pl.ds(w0, ww)], bias_w, m0, acc0)[1])

    outs = []
    for r in range(NSA_REP):
        sl = slice(r * tq, (r + 1) * tq)
        gt = [_sigmoid(ag_ref[pl.ds(3 * (NSA_REP * g + r) + j, 1), :]) for j in range(3)]
        outs.append(jnp.transpose(gt[0] * o_c[:, sl] + gt[1] * o_s[:, sl] + gt[2] * o_w[:, sl]))
    o = jnp.concatenate(outs, axis=1) * _silu(az_ref[...].astype(F32))
    o_ref[...] = o.astype(o_ref.dtype)


def nsa_prompt(ug, qgt, stacks, layer, kc, nb, t, *, tq, ck):
    tq = min(tq, t)
    nq = t // tq
    nseg = t // CMP_STRIDE
    ck = min(ck, t)
    gw = NSA_REP * HD
    nsr = -(-(t // SLC_BLK) // 8) * 8
    kv_spec = pl.BlockSpec((None, None, HD, t), lambda b, g, i: (layer, b, g, 0))
    return pl.pallas_call(
        functools.partial(_nsa_prompt_kernel, t=t, tq=tq, ck=ck, nseg=nseg),
        grid=(nb, NSA_KV, nq),
        in_specs=[pl.BlockSpec((None, gw, tq), lambda b, g, i: (b, g, i)),
                  pl.BlockSpec((None, LANE, tq), lambda b, g, i: (b, TQ_AG // LANE, i)),
                  pl.BlockSpec((tq, gw), lambda b, g, i: (b * nq + i, _UG_OFF["a_z"] // gw + g)),
                  kv_spec, kv_spec, kv_spec, kv_spec,
                  pl.BlockSpec((None, nseg, HD), lambda b, g, i: (0, b * NSA_KV + g, 0)),
                  pl.BlockSpec((None, nseg, HD), lambda b, g, i: (1, b * NSA_KV + g, 0))],
        out_specs=pl.BlockSpec((tq, gw), lambda b, g, i: (b * nq + i, g)),
        out_shape=jax.ShapeDtypeStruct((nb * t, NSA_HEADS * HD), BF16),
        scratch_shapes=[pltpu.VMEM((t, HD), BF16), pltpu.VMEM((HD + ONES_ROWS, t), BF16),
                        pltpu.VMEM((t, HD), BF16), pltpu.VMEM((HD + ONES_ROWS, t), BF16),
                        pltpu.VMEM((nsr, tq), F32)],
        compiler_params=_cparams(("parallel", "parallel", "arbitrary")),
        name="nsa_prompt",
    )(qgt, qgt, ug, stacks[2], stacks[3], stacks[4], stacks[5], kc, kc)


def _gla_kernel(q_ref, k_ref, v_ref, z_ref, sm_ref, s0_ref, wa2_ref, ba_ref, gn_ref, o_ref, sout_ref, s_scr,
                *, c, sub, t_valid):
    ci = pl.program_id(1)
    rows = c * sub

    @pl.when(ci == 0)
    def _():
        s_scr[...] = s0_ref[...]

    row = ci * rows + lax.broadcasted_iota(jnp.int32, (rows, 1), 0)
    gate_in = sm_ref[...][:, SM_BA:SM_BA + GLA_RANK]
    log_a = _log_sigmoid(_dot(gate_in, wa2_ref[...], HI) + ba_ref[...]) / GLA_TAU
    log_a = jnp.where(row < t_valid, log_a, 0.0)
    ri = lax.broadcasted_iota(jnp.int32, (rows, rows), 0)
    li = lax.broadcasted_iota(jnp.int32, (rows, rows), 1)
    tri = jnp.where((ri >= li) & (ri // c == li // c), 1.0, 0.0)
    cb_all = _dot(tri, log_a, HI)
    causal = lax.broadcasted_iota(jnp.int32, (c, c), 0) >= lax.broadcasted_iota(jnp.int32, (c, c), 1)
    for h in range(GLA_HEADS):
        ks = slice(h * GLA_HK, (h + 1) * GLA_HK)
        vs = slice(h * GLA_HV, (h + 1) * GLA_HV)
        state = s_scr[h]
        for j in range(sub):
            rs = slice(j * c, (j + 1) * c)
            cb = cb_all[rs, ks]
            ecb = jnp.exp(cb)
            qd = q_ref[rs, ks].astype(F32) * (GLA_HK ** -0.5) * ecb
            kh = k_ref[rs, ks].astype(F32)
            vh = v_ref[rs, vs]
            att = jnp.where(causal, _dot(qd, kh * jnp.exp(-cb), nt=True), 0.0)
            o = _dot(att, vh) + _dot(qd, state)
            c_end = cb[c - 1:c, :]
            k_dec = kh * jnp.exp(c_end - cb)
            e_col = jnp.transpose(ecb)[:, c - 1:c]
            state = e_col * state + _dot(jnp.transpose(k_dec), vh)
            y = o * lax.rsqrt(jnp.mean(o * o, axis=-1, keepdims=True) + EPS) * gn_ref[...]
            o_ref[rs, vs] = (y * _silu(z_ref[rs, vs].astype(F32))).astype(o_ref.dtype)
        s_scr[h] = state

    @pl.when(ci == pl.num_programs(1) - 1)
    def _():
        sout_ref[...] = s_scr[...]


def gla_mixer(ug, ux, s0, wa2, ba, gnorm, nb, t, *, c, sub, t_valid):
    rows = c * sub
    nc = t // rows

    def u_spec(name, width):
        return pl.BlockSpec((rows, width), lambda b, i: (b * nc + i, _UG_OFF[name] // width))

    s_spec = pl.BlockSpec((None, GLA_HEADS, GLA_HK, GLA_HV), lambda b, i: (b, 0, 0, 0))
    return pl.pallas_call(
        functools.partial(_gla_kernel, c=c, sub=sub, t_valid=t_valid),
        grid=(nb, nc),
        in_specs=[u_spec("b_q", GLA_DK), u_spec("b_k", GLA_DK), u_spec("b_v", GLA_DV), u_spec("b_z", GLA_DV),
                  pl.BlockSpec((rows, LANE), lambda b, i: (b * nc + i, OFF_SMALL // LANE)),
                  s_spec,
                  pl.BlockSpec((GLA_RANK, GLA_DK), lambda b, i: (0, 0)),
                  pl.BlockSpec((1, GLA_DK), lambda b, i: (0, 0)),
                  pl.BlockSpec((1, GLA_HV), lambda b, i: (0, 0))],
        out_specs=[pl.BlockSpec((rows, GLA_DV), lambda b, i: (b * nc + i, 0)), s_spec],
        out_shape=[jax.ShapeDtypeStruct((nb * t, GLA_DV), BF16),
                   jax.ShapeDtypeStruct((nb, GLA_HEADS, GLA_HK, GLA_HV), F32)],
        scratch_shapes=[pltpu.VMEM((GLA_HEADS, GLA_HK, GLA_HV), F32)],
        compiler_params=_cparams(("parallel", "arbitrary")),
        name="gla",
    )(ug, ug, ug, ug, ux, s0, wa2, ba.reshape(1, GLA_DK), gnorm.reshape(1, GLA_HV))


def _ssd_kernel(xbc_ref, z_ref, sm_ref, s0_ref, cbuf_ref, cw_ref, cbias_ref, dtb_ref, alog_ref, dsk_ref, mn_ref,
                o_ref, sout_ref, cout_ref, s_scr, f_scr, y_scr, *, q, t_valid, n_chunks):
    ci = pl.program_id(1)
    npad = 8
    hist = M_CONV - 1

    @pl.when(ci == 0)
    def _():
        s_scr[...] = s0_ref[...]
        f_scr[npad - hist:npad, :] = cbuf_ref[...]

    f_scr[npad:npad + q, :] = xbc_ref[...]
    conv = cbias_ref[...]
    for i in range(M_CONV):
        conv = conv + f_scr[npad - hist + i:npad - hist + i + q, :] * cw_ref[i:i + 1, :]
    xbc = _silu(conv)

    last_valid = t_valid - (n_chunks - 1) * q

    @pl.when(ci == n_chunks - 1)
    def _():
        cout_ref[...] = f_scr[npad - hist + last_valid:npad + last_valid, :]

    f_scr[npad - hist:npad, :] = f_scr[npad - hist + q:npad + q, :]

    lane = lax.broadcasted_iota(jnp.int32, (1, LANE), 1)
    row = ci * q + lax.broadcasted_iota(jnp.int32, (q, 1), 0)
    head_lane = (lane >= SM_DT) & (lane < SM_DT + M_HEADS)
    dt = jnp.where(head_lane & (row < t_valid), _softplus(sm_ref[...] + dtb_ref[...]), 0.0)
    a = dt * (-jnp.exp(alog_ref[...]))
    tri = jnp.where(lax.broadcasted_iota(jnp.int32, (q, q), 0) >= lax.broadcasted_iota(jnp.int32, (q, q), 1), 1.0, 0.0)
    causal = tri > 0.5
    cum = _dot(tri, a, HI)
    cum_t = jnp.transpose(cum)
    dt_t = jnp.transpose(dt)
    nbc = M_GROUPS * M_STATE
    rep = M_HEADS // M_GROUPS
    for g in range(M_GROUPS):
        bg = xbc[:, M_DINNER + g * M_STATE:M_DINNER + (g + 1) * M_STATE]
        cg = xbc[:, M_DINNER + nbc + g * M_STATE:M_DINNER + nbc + (g + 1) * M_STATE]
        cbm = _dot(cg, bg, nt=True)
        for r in range(rep):
            h = g * rep + r
            hl = SM_DT + h
            xh = xbc[:, h * M_HD:(h + 1) * M_HD]
            cum_c = cum[:, hl:hl + 1]
            decay = jnp.exp(jnp.where(causal, cum_c - cum_t[hl:hl + 1, :], -jnp.inf))
            s_prev = s_scr[h]
            y = _dot(cbm * decay * dt_t[hl:hl + 1, :], xh)
            y = y + jnp.exp(cum_c) * _dot(cg, s_prev, nt=True)
            c_end = cum[q - 1:q, hl:hl + 1]
            w = jnp.exp(c_end - cum_c) * dt[:, hl:hl + 1]
            s_scr[h] = jnp.exp(c_end) * s_prev + _dot(xh * w, bg, tn=True)
            y_scr[:, h * M_HD:(h + 1) * M_HD] = y + dsk_ref[:, hl:hl + 1] * xh
    gw = M_DINNER // M_GROUPS
    for g in range(M_GROUPS):
        sl = slice(g * gw, (g + 1) * gw)
        yg = y_scr[:, sl] * _silu(z_ref[:, sl].astype(F32))
        yn = yg * lax.rsqrt(jnp.mean(yg * yg, axis=-1, keepdims=True) + EPS) * mn_ref[:, sl]
        o_ref[:, sl] = yn.astype(o_ref.dtype)

    @pl.when(ci == n_chunks - 1)
    def _():
        sout_ref[...] = s_scr[...]


def _pad_heads(v):
    return jnp.zeros((1, LANE), F32).at[0, SM_DT:SM_DT + M_HEADS].set(v)


def ssd_mixer(ug, ux, s0, cbuf, conv_w, conv_b, dt_bias, a_log, d_skip, m_norm, nb, t, *, q, t_valid):
    nc = t // q
    n_chunks = -(-t_valid // q)
    assert n_chunks == nc
    hist = M_CONV - 1

    def u_spec(off, name, width):
        return pl.BlockSpec((q, width), lambda b, i: (b * nc + i, off[name] // width))

    def full(shape):
        return pl.BlockSpec(shape, lambda b, i: (0,) * len(shape))

    s_spec = pl.BlockSpec((None, M_HEADS, M_HD, M_STATE), lambda b, i: (b, 0, 0, 0))
    c_spec = pl.BlockSpec((None, hist, M_CONVDIM), lambda b, i: (b, 0, 0))
    return pl.pallas_call(
        functools.partial(_ssd_kernel, q=q, t_valid=t_valid, n_chunks=n_chunks),
        grid=(nb, nc),
        in_specs=[u_spec(_UX_OFF, "c_xbc", M_CONVDIM), u_spec(_UG_OFF, "c_z", M_DINNER),
                  pl.BlockSpec((q, LANE), lambda b, i: (b * nc + i, OFF_SMALL // LANE)),
                  s_spec, c_spec,
                  full((M_CONV, M_CONVDIM)), full((1, M_CONVDIM)), full((1, LANE)), full((1, LANE)),
                  full((1, LANE)), full((1, M_DINNER))],
        out_specs=[pl.BlockSpec((q, M_DINNER), lambda b, i: (b * nc + i, 0)), s_spec, c_spec],
        out_shape=[jax.ShapeDtypeStruct((nb * t, M_DINNER), BF16),
                   jax.ShapeDtypeStruct((nb, M_HEADS, M_HD, M_STATE), F32),
                   jax.ShapeDtypeStruct((nb, hist, M_CONVDIM), F32)],
        scratch_shapes=[pltpu.VMEM((M_HEADS, M_HD, M_STATE), F32),
                        pltpu.VMEM((q + 8, M_CONVDIM), F32),
                        pltpu.VMEM((q, M_DINNER), F32)],
        compiler_params=_cparams(("parallel", "arbitrary")),
        name="ssd",
    )(ux, ug, ux, s0, cbuf, conv_w, conv_b.reshape(1, M_CONVDIM), _pad_heads(dt_bias), _pad_heads(a_log),
      _pad_heads(d_skip), m_norm.reshape(1, M_DINNER))


SEG_PER_PAGE = PAGE // CMP_STRIDE


def _page_cmp_kernel(pt_ref, *refs, npg):
    wk_ref, wv_ref, ok_ref, ov_ref, scr = refs[2 * npg:]
    n_lane_blk = NSA_KV * HD // LANE
    for pages, w_ref, o_ref in ((refs[:npg], wk_ref, ok_ref), (refs[npg:2 * npg], wv_ref, ov_ref)):
        for k in range(npg):
            page_t = pages[k][...].reshape(NSA_KV * HD, PAGE)
            pos_major = jnp.transpose(page_t)
            for c in range(n_lane_blk):
                scr[c, k * PAGE:(k + 1) * PAGE, :] = pos_major[:, c * LANE:(c + 1) * LANE]
        acc = jnp.zeros((npg * SEG_PER_PAGE, 2 * HD * NSA_KV), F32)
        for j in range(CMP_STRIDE):
            rows = pl.ds(j, npg * SEG_PER_PAGE, stride=CMP_STRIDE)
            xj = jnp.concatenate([scr[c, rows, :] for c in range(n_lane_blk)], axis=1)
            acc = acc + _dot(xj, w_ref[j])
        o_ref[...] = acc


def page_compress(cache_k_t, cache_v_t, page_table, layer, wbd, *, npg):
    nb, n_pages = page_table.shape
    steps = n_pages // npg

    def page_spec(k):
        return pl.BlockSpec((None, None, NSA_KV, HD, PAGE),
                            lambda b, i, pt: (layer, pt[b, i * npg + k], 0, 0, 0))

    def w_spec(s):
        return pl.BlockSpec((None, None, CMP_STRIDE, NSA_KV * HD, NSA_KV * 2 * HD),
                            lambda b, i, pt: (layer, s, 0, 0, 0))

    o_spec = pl.BlockSpec((None, npg * SEG_PER_PAGE, NSA_KV * 2 * HD), lambda b, i, pt: (b, i, 0))
    o_shape = jax.ShapeDtypeStruct((nb, n_pages * SEG_PER_PAGE, NSA_KV * 2 * HD), F32)
    return pl.pallas_call(
        functools.partial(_page_cmp_kernel, npg=npg),
        grid_spec=pltpu.PrefetchScalarGridSpec(
            num_scalar_prefetch=1,
            grid=(nb, steps),
            in_specs=[page_spec(k) for k in range(npg)] * 2 + [w_spec(0), w_spec(1)],
            out_specs=[o_spec, o_spec],
            scratch_shapes=[pltpu.VMEM((NSA_KV * HD // LANE, npg * PAGE, LANE), F32)]),
        out_shape=[o_shape, o_shape],
        compiler_params=_cparams(("parallel", "arbitrary")),
        name="page_compress",
    )(page_table, *([cache_k_t] * npg), *([cache_v_t] * npg), wbd, wbd)


def _masked_softmax_rows(s, mask):
    s = jnp.where(mask, s, NEG)
    m = jnp.max(s, axis=-1, keepdims=True)
    p = jnp.where(mask, jnp.exp(s - m), 0.0)
    l = jnp.sum(p, axis=-1, keepdims=True)
    return p / jnp.where(l > 0.0, l, 1.0)


def _nsa_sample_cmp_kernel(abk_ref, abv_ref, kvs_ref, q_ref, w1_ref, pe_ref, w2_ref, cover_ref, oc_ref, sel_ref,
                           cst_scr, *, p_len, nc, sl):
    g = pl.program_id(1)
    ns = p_len // SLC_BLK + 1
    rowi = lax.broadcasted_iota(jnp.int32, (nc, 1), 0)

    @pl.when((pl.program_id(0) == 0) & (g == 0))
    def _():
        for idx in range(2):
            pe = _dot(pe_ref[idx], w1_ref[idx], HI)
            cst_scr[idx:idx + 1, :] = pe[0:1, :HD] + pe[1:2, HD:]

    def compressed(ab_ref, idx):
        ab = ab_ref[...]
        new = kvs_ref[pl.ds(idx * NSA_KV + g, 1), :]
        b_new = _dot(new, w1_ref[idx][0:HD, HD:])
        nxt = jnp.where(rowi == nc - 1, b_new, pltpu.roll(ab[:, HD:], shift=nc - 1, axis=0))
        return _dot(_silu(ab[:, :HD] + nxt + cst_scr[idx:idx + 1, :]), w2_ref[idx])

    kc = compressed(abk_ref, 0)
    vc = compressed(abv_ref, 1)
    q4 = q_ref[...] * (HD ** -0.5)
    cidx = lax.broadcasted_iota(jnp.int32, (1, nc), 1)
    m_c = cidx * CMP_STRIDE + (CMP_LEN - 1) <= p_len
    p_c = _masked_softmax_rows(_dot(q4, kc, HI, nt=True), m_c)
    oc_ref[...] = _dot(p_c, vc)
    imp = _dot_exact_rhs(jnp.sum(p_c, axis=0, keepdims=True), cover_ref[...])
    sidx = lax.broadcasted_iota(jnp.int32, (1, sl), 1)
    cur = p_len // SLC_BLK
    valid = (sidx * SLC_BLK <= p_len) & (sidx < ns)
    forced = (sidx == 0) | (sidx == cur) | (sidx == cur - 1)
    score = jnp.where(valid, jnp.where(forced, FORCE, imp), NEG)
    ri = lax.broadcasted_iota(jnp.int32, (sl, sl), 0)
    li = lax.broadcasted_iota(jnp.int32, (sl, sl), 1)
    score_b = jnp.broadcast_to(score, (sl, sl))
    score_col = jnp.sum(jnp.where(ri == li, score_b, 0.0), axis=1, keepdims=True)
    ahead = (score_b > score_col) | ((score_b == score_col) & (li < ri))
    rank_col = jnp.sum(jnp.where(ahead, 1.0, 0.0), axis=1, keepdims=True)
    kl = lax.broadcasted_iota(jnp.int32, (sl, LANE), 1).astype(F32)
    sv = lax.broadcasted_iota(jnp.int32, (sl, LANE), 0).astype(F32)
    picked = jnp.sum(jnp.where(rank_col == kl, sv, 0.0), axis=0, keepdims=True)
    sel_ref[...] = picked.astype(jnp.int32)


def _cover_matrix(nc, sl):
    c0 = np.arange(nc)[:, None] * CMP_STRIDE
    s0 = np.arange(sl)[None, :] * SLC_BLK
    return jnp.asarray(((c0 + CMP_LEN > s0) & (c0 < s0 + SLC_BLK)).astype(np.float32), dtype=BF16)


def nsa_sample_cmp(abk, abv, kvs, q3, w1cat, pe2, w2, p_len):
    nb, nc, _ = abk.shape
    ns = p_len // SLC_BLK + 1
    sl = -(-ns // LANE) * LANE
    cover = _cover_matrix(nc, sl)

    def ab_spec():
        return pl.BlockSpec((None, nc, 2 * HD), lambda b, g: (b, 0, g))

    def full(a):
        return pl.BlockSpec(a.shape, lambda b, g: (0,) * a.ndim)

    row_spec = pl.BlockSpec((None, None, NSA_REP, HD), lambda b, g: (b, g, 0, 0))
    return pl.pallas_call(
        functools.partial(_nsa_sample_cmp_kernel, p_len=p_len, nc=nc, sl=sl),
        grid=(nb, NSA_KV),
        in_specs=[ab_spec(), ab_spec(), pl.BlockSpec((None, 6 * NSA_KV, HD), lambda b, g: (b, 0, 0)), row_spec,
                  full(w1cat), full(pe2), full(w2), full(cover)],
        out_specs=[row_spec, pl.BlockSpec((None, None, 1, LANE), lambda b, g: (b, g, 0, 0))],
        out_shape=[jax.ShapeDtypeStruct((nb, NSA_KV, NSA_REP, HD), F32),
                   jax.ShapeDtypeStruct((nb, NSA_KV, 1, LANE), jnp.int32)],
        scratch_shapes=[pltpu.VMEM((8, HD), F32)],
        compiler_params=_cparams(("arbitrary", "arbitrary")),
        name="nsa_sample_cmp",
    )(abk, abv, kvs, q3, w1cat, pe2, w2, cover)


def _nsa_sample_sel_kernel(sel_ref, pt_ref, *refs, p_len, nb, n_buf):
    nk = SLC_TOPK
    kpages, vpages = refs[:nk], refs[nk:2 * nk]
    (kwb_ref, vwb_ref, kvt_ref, kvs_ref, q_ref, az_ref, sm_ref, oc_ref,
     o_ref, kwo_ref, vwo_ref) = refs[2 * nk:]
    b = pl.program_id(0)
    g = pl.program_id(1)
    n_past = p_len // SLC_BLK
    q4 = q_ref[...] * (HD ** -0.5)
    lane_b = lax.broadcasted_iota(jnp.int32, (1, nb), 1) == b

    def new_row(i):
        return kvs_ref[pl.ds(i * NSA_KV + g, 1), :]

    def new_col(i):
        blk = kvt_ref[pl.ds(pl.multiple_of((i * NSA_KV + g) * HD, HD), HD), :]
        return jnp.sum(jnp.where(lane_b, blk, 0.0), axis=1, keepdims=True)

    half_of_lane = lax.broadcasted_iota(jnp.int32, (1, PAGE), 1) // SLC_BLK
    s_parts, m_parts = [], []
    has_new = jnp.zeros((1, 1), jnp.int32)
    for k in range(nk):
        blk = sel_ref[b, g, k]
        m_parts.append((half_of_lane == blk % 2) & (blk < n_past))
        s_parts.append(_dot(q4, kpages[k][...]))
        has_new = has_new + jnp.where(blk == n_past, 1, 0)
    s_sel = jnp.concatenate(s_parts, axis=1)
    m_sel = jnp.concatenate(m_parts, axis=1)
    new_ok = has_new > 0
    s_new = jnp.where(new_ok, jnp.sum(q4 * new_row(2), axis=1, keepdims=True), NEG)
    s_sel = jnp.where(m_sel, s_sel, NEG)
    mx = jnp.maximum(jnp.max(s_sel, axis=1, keepdims=True), s_new)
    p_sel = jnp.where(m_sel, jnp.exp(s_sel - mx), 0.0)
    p_new = jnp.where(new_ok, jnp.exp(s_new - mx), 0.0)
    den = jnp.sum(p_sel, axis=1, keepdims=True) + p_new
    acc = p_new * new_row(3)
    for k in range(nk):
        acc = acc + _dot(p_sel[:, k * PAGE:(k + 1) * PAGE], vpages[k][...], nt=True)
    o_s = acc / jnp.where(den > 0.0, den, 1.0)

    kwb = kwb_ref[...]
    vwb = vwb_ref[...]
    wlane = lax.broadcasted_iota(jnp.int32, (1, n_buf), 1)
    m_w = wlane > n_buf - WINDOW
    s_w = jnp.where(m_w, _dot(q4, kwb), NEG)
    s_wn = jnp.sum(q4 * new_row(4), axis=1, keepdims=True)
    mw = jnp.maximum(jnp.max(s_w, axis=1, keepdims=True), s_wn)
    p_w = jnp.where(m_w, jnp.exp(s_w - mw), 0.0)
    p_wn = jnp.exp(s_wn - mw)
    o_w = (_dot(p_w, vwb, nt=True) + p_wn * new_row(5)) / (jnp.sum(p_w, axis=1, keepdims=True) + p_wn)
    kwo_ref[...] = jnp.where(wlane == n_buf - 1, new_col(4), pltpu.roll(kwb, shift=n_buf - 1, axis=1))
    vwo_ref[...] = jnp.where(wlane == n_buf - 1, new_col(5), pltpu.roll(vwb, shift=n_buf - 1, axis=1))

    gate_row = _sigmoid(pltpu.roll(sm_ref[pl.ds(b, 1), :], shift=(LANE - 3 * NSA_REP * g) % LANE, axis=1))
    gl = lax.broadcasted_iota(jnp.int32, (NSA_REP, LANE), 1)
    gr = lax.broadcasted_iota(jnp.int32, (NSA_REP, LANE), 0)

    def gate(j):
        return jnp.sum(jnp.where(gl == 3 * gr + j, gate_row, 0.0), axis=1, keepdims=True)

    o = gate(0) * oc_ref[...] + gate(1) * o_s + gate(2) * o_w
    o_ref[...] = o * _silu(az_ref[...])


def nsa_sample_sel(sel, page_table, slc_k_t, slc_v_t, win_k_t, win_v_t, layer, kvt, kvs, q3, az3, ux, o_cmp, p_len):
    nb = page_table.shape[0]
    n_buf = win_k_t.shape[-1]
    last_blk = p_len // SLC_BLK - 1

    def page_spec(k):
        def imap(b, g, sel_r, pt_r):
            blk = jnp.minimum(sel_r[b, g, k], last_blk)
            return (layer, pt_r[b, blk // (PAGE // SLC_BLK)], g, 0, 0)
        return pl.BlockSpec((None, None, None, HD, PAGE), imap)

    win_spec = pl.BlockSpec((None, None, None, HD, n_buf), lambda b, g, s, p: (layer, b, g, 0, 0))
    wout_spec = pl.BlockSpec((None, None, HD, n_buf), lambda b, g, s, p: (b, g, 0, 0))
    row_spec = pl.BlockSpec((None, None, NSA_REP, HD), lambda b, g, s, p: (b, g, 0, 0))

    def full(a):
        return pl.BlockSpec(a.shape, lambda b, g, s, p: (0,) * a.ndim)

    small = pl.BlockSpec((nb, LANE), lambda b, g, s, p: (0, OFF_SMALL // LANE))
    return pl.pallas_call(
        functools.partial(_nsa_sample_sel_kernel, p_len=p_len, nb=nb, n_buf=n_buf),
        grid_spec=pltpu.PrefetchScalarGridSpec(
            num_scalar_prefetch=2,
            grid=(nb, NSA_KV),
            in_specs=[page_spec(k) for k in range(SLC_TOPK)] * 2
                     + [win_spec, win_spec, full(kvt), pl.BlockSpec((None, 6 * NSA_KV, HD), lambda b, g, s, p: (b, 0, 0)),
                        row_spec, row_spec, small, row_spec],
            out_specs=[row_spec, wout_spec, wout_spec]),
        out_shape=[jax.ShapeDtypeStruct((nb, NSA_KV, NSA_REP, HD), F32),
                   jax.ShapeDtypeStruct((nb, NSA_KV, HD, n_buf), F32),
                   jax.ShapeDtypeStruct((nb, NSA_KV, HD, n_buf), F32)],
        compiler_params=_cparams(("parallel", "arbitrary")),
        name="nsa_sample_sel",
    )(sel, page_table, *([slc_k_t] * SLC_TOPK), *([slc_v_t] * SLC_TOPK), win_k_t, win_v_t, kvt, kvs, q3, az3, ux, o_cmp)


def _block_diag_w1(w1cat):
    d = w1cat.shape[0]
    wj = w1cat.reshape(d, 2, CMP_STRIDE, HD, 2 * HD)
    eye = jnp.eye(NSA_KV, dtype=w1cat.dtype)
    bd = jnp.einsum("lsjdo,gh->lsjgdho", wj, eye)
    return bd.reshape(d, 2, CMP_STRIDE, NSA_KV * HD, NSA_KV * 2 * HD).astype(BF16)


def _prep_weights(w_in, cmp_pe_k, cmp_w1_k, cmp_w2_k, cmp_pe_v, cmp_w1_v, cmp_w2_v, w_br_a, w_br_b, w_br_c, w_out):
    wt = jnp.transpose(w_in, (0, 2, 1))

    def gather_rows(order, total):
        parts = [wt[:, _IN_OFF[n]:_IN_OFF[n] + _IN_SZ[n]] for n in order]
        used = sum(_IN_SZ[n] for n in order)
        parts.append(jnp.zeros((wt.shape[0], total - used, wt.shape[2]), wt.dtype))
        return jnp.concatenate(parts, axis=1).astype(BF16)

    half = CMP_STRIDE * HD

    def cat(w1):
        return jnp.concatenate([w1[:, :half], w1[:, half:]], axis=2)

    def pe_rows(pe):
        d = pe.shape[0]
        rows = pe.reshape(d, 2, half)
        return jnp.concatenate([rows, jnp.zeros((d, 6, half), pe.dtype)], axis=1)

    w1cat = jnp.stack([cat(cmp_w1_k), cat(cmp_w1_v)], axis=1)
    pe2 = jnp.stack([pe_rows(cmp_pe_k), pe_rows(cmp_pe_v)], axis=1)
    w2 = jnp.stack([cmp_w2_k, cmp_w2_v], axis=1)
    return dict(wt_ug=gather_rows(_UG_ORDER, NP_UG), wt_ux=gather_rows(_UX_ORDER, NP_UX), wt_kv=gather_rows(("a_kv",), KV_ROWS),
                wt_q=gather_rows(_TQ_ORDER, TQ_ROWS),
                w1cat=w1cat, pe2=pe2, w2=w2, wbd=_block_diag_w1(w1cat),
                wa=w_br_a.astype(BF16), wb=w_br_b.astype(BF16), wc=w_br_c.astype(BF16), wo=w_out.astype(BF16))


def _prompt_layer(x, h, nb, t, layer, w, p, next_norm, tiles, stacks):
    ug = matmul(h, w["wt_ug"], nt=True, tm=tiles["mm_tm"], tn=tiles["in_tn"], out_dtype=BF16, name="in_proj",
                b_layer=layer)
    ux = matmul(h, w["wt_ux"], nt=True, tm=tiles["mm_tm"], tn=NP_UX, name="in_proj_x", b_layer=layer)
    stacks = kv_proj_stacked(w["wt_kv"], layer, h, nb, t, stacks, tn=tiles["kv_tn"])
    qgt = matmul_t(w["wt_q"], layer, h, nb, t, tm=TQ_TM, tn=tiles["q_tn"], name="q_proj_t")
    nseg = t // CMP_STRIDE
    seg = jnp.stack([stacks[0][layer], stacks[1][layer]]).reshape(2, nb, NSA_KV, HD, t)
    seg = jnp.transpose(seg, (0, 1, 2, 4, 3)).reshape(2, nb * NSA_KV * nseg, CMP_STRIDE * HD)
    kc = compress_blocks(seg, w["w1cat"][layer], w["pe2"][layer], w["w2"][layer], nseg)
    o_a = nsa_prompt(ug, qgt, stacks, layer, kc, nb, t, tq=tiles["tq"], ck=tiles["ck"])
    zeros_gla = jnp.zeros((nb, GLA_HEADS, GLA_HK, GLA_HV), F32)
    zeros_ssm = jnp.zeros((nb, M_HEADS, M_HD, M_STATE), F32)
    zeros_conv = jnp.zeros((nb, M_CONV - 1, M_CONVDIM), F32)
    o_b, gla_s = gla_mixer(ug, ux, zeros_gla, p["gla_wa2"], p["gla_ba"], p["gla_norm"], nb, t,
                           c=tiles["gla_c"], sub=min(tiles["gla_sub"], t // tiles["gla_c"]), t_valid=t)
    o_c, ssm_s, conv_s = ssd_mixer(ug, ux, zeros_ssm, zeros_conv, p["conv_w"], p["conv_b"], p["dt_bias"], p["a_log"],
                                   p["d_skip"], p["m_norm"], nb, t, q=min(tiles["ssd_q"], t), t_valid=t)
    mix = merge_branches(o_a, o_b, o_c, w["wa"], w["wb"], w["wc"], layer, ug, tm=tiles["mg_tm"], tn=tiles["mg_tn"])
    x_out, h_out = out_proj_norm(mix, w["wo"], layer, x, *next_norm, tm=tiles["out_tm"])
    return x_out, h_out, stacks, (gla_s, ssm_s, conv_s)


ROW_PAD = 8


def _sample_layer(x, h, layer, w, p, next_norm, caches, states, page_table, tiles):
    nb = x.shape[0]
    p_len = page_table.shape[1] * PAGE
    ck_t, cv_t, sk_t, sv_t, wk_t, wv_t = caches
    ug = matmul(h, w["wt_ug"], nt=True, tm=nb, tn=tiles["in_tn"], out_dtype=BF16, name="in_proj_s", b_layer=layer)
    ux = matmul(h, w["wt_ux"], nt=True, tm=nb, tn=NP_UX, name="in_proj_xs", b_layer=layer)
    kv = matmul(h, w["wt_kv"], nt=True, tm=nb, tn=512, name="kv_proj_s", b_layer=layer)
    kvt = matmul(w["wt_kv"], h, nt=True, tm=512, tn=nb, name="kv_proj_st", a_layer=layer)
    qg = matmul(h, w["wt_q"], nt=True, tm=nb, tn=TQ_TM, name="q_proj_s", b_layer=layer)
    kvs = kv.reshape(nb, 6 * NSA_KV, HD)
    def pad_rows(a):
        return jnp.pad(a[:, None, :], ((0, 0), (0, ROW_PAD - 1), (0, 0))).reshape(nb * ROW_PAD, a.shape[1])

    ug_pad, ux_pad = pad_rows(ug), pad_rows(ux)
    o_b, gla_s = gla_mixer(ug_pad, ux_pad, states[0], p["gla_wa2"], p["gla_ba"], p["gla_norm"], nb, ROW_PAD,
                           c=ROW_PAD, sub=1, t_valid=1)
    o_c, ssm_s, conv_s = ssd_mixer(ug_pad, ux_pad, states[1], states[2], p["conv_w"], p["conv_b"], p["dt_bias"], p["a_log"],
                                   p["d_skip"], p["m_norm"], nb, ROW_PAD, q=ROW_PAD, t_valid=1)
    o_b = o_b.reshape(nb, ROW_PAD, GLA_DV)[:, 0]
    o_c = o_c.reshape(nb, ROW_PAD, M_DINNER)[:, 0]
    abk, abv = page_compress(ck_t, cv_t, page_table, layer, w["wbd"], npg=min(tiles["npg"], page_table.shape[1]))
    q3 = qg[:, :TQ_AG].reshape(nb, NSA_KV, NSA_REP, HD)
    az3 = ug[:, _UG_OFF["a_z"]:_UG_OFF["a_z"] + MIX_W].astype(F32).reshape(nb, NSA_KV, NSA_REP, HD)
    o_cmp, sel = nsa_sample_cmp(abk, abv, kvs, q3, w["w1cat"][layer], w["pe2"][layer], w["w2"][layer], p_len)
    o_a, win_k, win_v = nsa_sample_sel(sel.reshape(nb, NSA_KV, LANE), page_table, sk_t, sv_t, wk_t, wv_t, layer,
                                       kvt, kvs, q3, az3, ux, o_cmp, p_len)
    mix = merge_branches(o_a.reshape(nb, MIX_W).astype(BF16), o_b, o_c, w["wa"], w["wb"], w["wc"], layer, ug,
                         tm=nb, tn=tiles["mg_tn"])
    x_out, h_out = out_proj_norm(mix, w["wo"], layer, x, *next_norm, tm=nb)
    kv4 = kv.reshape(nb, 6, 1, NSA_KV, HD)
    return x_out, h_out, (kv4[:, 0], kv4[:, 1], kv4[:, 2], kv4[:, 3], win_k, win_v, gla_s, ssm_s, conv_s)


_TILES = dict(rms_tm=256, mm_tm=1024, in_tn=1024, kv_tn=512, q_tn=2048, tq=256, ck=512, gla_c=64, gla_sub=4,
              ssd_q=256, mg_tm=1024, mg_tn=512, out_tm=512, npg=16)
_PARAM_NAMES = ("ln_w", "gla_wa2", "gla_ba", "gla_norm", "conv_w", "conv_b", "dt_bias", "a_log", "d_skip", "m_norm")


def kernel(x_prompt, x_sample, cache_cmp_k, cache_cmp_v, cache_slc_k, cache_slc_v, cache_win_k, cache_win_v,
           state_gla, state_ssm, state_conv, page_table, ln_w, w_in, cmp_pe_k, cmp_w1_k, cmp_w2_k, cmp_pe_v,
           cmp_w1_v, cmp_w2_v, gla_wa2, gla_ba, gla_norm, conv_w, conv_b, dt_bias, a_log, d_skip, m_norm,
           w_br_a, w_br_b, w_br_c, w_out, final_norm):
    nbp, t, d = x_prompt.shape
    nbs = x_sample.shape[0]
    depth = w_in.shape[0]
    weights = _prep_weights(w_in, cmp_pe_k, cmp_w1_k, cmp_w2_k, cmp_pe_v, cmp_w1_v, cmp_w2_v,
                            w_br_a, w_br_b, w_br_c, w_out)
    params = dict(zip(_PARAM_NAMES, (ln_w, gla_wa2, gla_ba, gla_norm, conv_w, conv_b, dt_bias, a_log, d_skip, m_norm)))
    caches = tuple(jnp.transpose(c, (0, 1, 3, 4, 2))
                   for c in (cache_cmp_k, cache_cmp_v, cache_slc_k, cache_slc_v, cache_win_k, cache_win_v))
    xp = x_prompt.reshape(nbp * t, d)
    xs = x_sample.reshape(nbs, d)
    p_states, s_out, stacks = [], [], None
    hp = rmsnorm_rows(xp, ln_w[0], BF16, _TILES["rms_tm"])
    hs = rmsnorm_rows(xs, ln_w[0], BF16, nbs)
    for l in range(depth):
        p = {k: v[l] for k, v in params.items()}
        next_norm = (ln_w[l + 1], BF16) if l + 1 < depth else (final_norm, F32)
        xp, hp, stacks, st_p = _prompt_layer(xp, hp, nbp, t, l, weights, p, next_norm, _TILES, stacks)
        xs, hs, st_s = _sample_layer(xs, hs, l, weights, p, next_norm, caches,
                                     (state_gla[l], state_ssm[l], state_conv[l]), page_table, _TILES)
        p_states.append(st_p)
        s_out.append(st_s)
    y_prompt = hp.reshape(nbp, t, d)
    y_sample = hs.reshape(nbs, 1, d)

    def from_kv_layout(a):
        return jnp.transpose(a, (0, 1, 4, 2, 3))

    n_w = min(WINDOW, t)
    kv5 = [a.reshape(depth, nbp, NSA_KV, HD, t) for a in stacks]
    p_leaves = [from_kv_layout(a) for a in kv5[:4]] + [from_kv_layout(a[..., t - n_w:]) for a in kv5[4:]]
    p_leaves += [jnp.stack([st[i] for st in p_states]) for i in range(3)]
    s_leaves = [jnp.stack([o[i] for o in s_out]) for i in range(9)]
    s_leaves[4], s_leaves[5] = from_kv_layout(s_leaves[4]), from_kv_layout(s_leaves[5])
    return (y_prompt, y_sample, *p_leaves, *s_leaves)
```

```python
import functools
import math

import jax
import jax.numpy as jnp
import numpy as np
from jax import lax
from jax.experimental import pallas as pl
from jax.experimental.pallas import tpu as pltpu

F32 = jnp.float32
BF16 = jnp.bfloat16
HI = lax.Precision.HIGHEST

D_MODEL = 2048
PAGE = 128
MIX_W = D_MODEL // 2
HD = 64
NSA_HEADS = MIX_W // HD
NSA_KV = 4
NSA_REP = NSA_HEADS // NSA_KV
CMP_LEN = 32
CMP_STRIDE = 16
SLC_BLK = 64
SLC_TOPK = 16
WINDOW = 512
GLA_HEADS = 4
GLA_DK = MIX_W // 2
GLA_DV = MIX_W
GLA_HK = GLA_DK // GLA_HEADS
GLA_HV = GLA_DV // GLA_HEADS
GLA_RANK = 16
GLA_TAU = 16.0
M_DINNER = MIX_W
M_HD = 64
M_HEADS = M_DINNER // M_HD
M_GROUPS = 4
M_STATE = 128
M_CONV = 4
M_CONVDIM = M_DINNER + 2 * M_GROUPS * M_STATE
EPS = 1e-6
NEG = -1e30
FORCE = 1e4
LOG2E = math.log2(math.e)
ONES_ROWS = 16

_IN_NAMES = ("a_q", "a_kv", "a_g", "a_z", "b_q", "b_k", "b_v", "b_a", "b_z", "c_xbc", "c_dt", "c_z", "m_g")
_IN_SIZES = (NSA_HEADS * HD, 6 * NSA_KV * HD, 3 * NSA_HEADS, MIX_W, GLA_DK, GLA_DK, GLA_DV, GLA_RANK, GLA_DV,
             M_CONVDIM, M_HEADS, M_DINNER, 3 * D_MODEL)
_IN_OFF = dict(zip(_IN_NAMES, np.cumsum((0,) + _IN_SIZES)[:-1].tolist()))
_IN_SZ = dict(zip(_IN_NAMES, _IN_SIZES))

LANE = 128
_UG_ORDER = ("m_g", "a_z", "b_v", "b_z", "c_z", "b_q", "b_k")
_UX_ORDER = ("c_xbc", "a_g", "b_a", "c_dt")


def _offsets(order):
    off, o = {}, 0
    for n in order:
        off[n] = o
        o += _IN_SZ[n]
    return off, o


_UG_OFF, NP_UG = _offsets(_UG_ORDER)
_UX_OFF, _ux_used = _offsets(_UX_ORDER)
NP_UX = -(-_ux_used // LANE) * LANE
OFF_SMALL = _UX_OFF["a_g"]
SM_AG, SM_BA, SM_DT = 0, 3 * NSA_HEADS, 3 * NSA_HEADS + GLA_RANK
KV_ROWS = _IN_SZ["a_kv"]
GRP_ROWS = NSA_KV * HD
_TQ_ORDER = ("a_q", "a_g")
TQ_AG = _IN_SZ["a_q"]
TQ_TM = 384
TQ_ROWS = -(-(TQ_AG + _IN_SZ["a_g"]) // TQ_TM) * TQ_TM
VMEM_LIMIT = 56 * 1024 * 1024


def _cparams(sem):
    return pltpu.CompilerParams(dimension_semantics=sem, vmem_limit_bytes=VMEM_LIMIT)


def _dot(a, b, prec=None, nt=False, tn=False):
    if prec is None:
        a, b = a.astype(BF16), b.astype(BF16)
    dn = (((0,) if tn else (1,), (1,) if nt else (0,)), ((), ()))
    return lax.dot_general(a, b, dn, precision=prec, preferred_element_type=F32)


def _dot_exact_rhs(a, b_bf16):
    hi = a.astype(BF16)
    lo = (a - hi.astype(F32)).astype(BF16)
    return _dot(hi, b_bf16) + _dot(lo, b_bf16)


def _sigmoid(x):
    return 0.5 * jnp.tanh(0.5 * x) + 0.5


def _silu(x):
    return x * _sigmoid(x)


def _log_sigmoid(x):
    return jnp.minimum(x, 0.0) - jnp.log(1.0 + jnp.exp(-jnp.abs(x)))


def _softplus(x):
    return jnp.maximum(x, 0.0) + jnp.log(1.0 + jnp.exp(-jnp.abs(x)))


def _stacked_spec(block, imap, layer):
    if layer is None:
        return pl.BlockSpec(block, imap)
    return pl.BlockSpec((None,) + block, lambda *idx: (layer,) + imap(*idx))


def _rms_kernel(x_ref, w_ref, o_ref):
    x = x_ref[...]
    y = x * lax.rsqrt(jnp.mean(x * x, axis=-1, keepdims=True) + EPS)
    o_ref[...] = (y * w_ref[...]).astype(o_ref.dtype)


def rmsnorm_rows(x, w, out_dtype, tm):
    m, d = x.shape
    return pl.pallas_call(
        _rms_kernel,
        grid=(m // tm,),
        in_specs=[pl.BlockSpec((tm, d), lambda i: (i, 0)), pl.BlockSpec((1, d), lambda i: (0, 0))],
        out_specs=pl.BlockSpec((tm, d), lambda i: (i, 0)),
        out_shape=jax.ShapeDtypeStruct((m, d), out_dtype),
        compiler_params=_cparams(("parallel",)),
        name="rmsnorm",
    )(x, w.reshape(1, d))


def _mm_kernel(*refs, nt, has_res):
    a_ref, b_ref, o_ref = refs[0], refs[1], refs[-1]
    acc = _dot(a_ref[...], b_ref[...], nt=nt)
    if has_res:
        acc = acc + refs[2][...]
    o_ref[...] = acc.astype(o_ref.dtype)


def matmul(a, b, *, nt, tm, tn, res=None, out_dtype=F32, name="matmul", a_layer=None, b_layer=None):
    m, k = a.shape[-2:]
    n = b.shape[-2] if nt else b.shape[-1]
    tm, tn = min(tm, m), min(tn, n)
    if nt:
        b_spec = _stacked_spec((tn, k), lambda i, j: (j, 0), b_layer)
    else:
        b_spec = _stacked_spec((k, tn), lambda i, j: (0, j), b_layer)
    in_specs = [_stacked_spec((tm, k), lambda i, j: (i, 0), a_layer), b_spec]
    args = [a, b]
    if res is not None:
        in_specs.append(pl.BlockSpec((tm, tn), lambda i, j: (i, j)))
        args.append(res)
    return pl.pallas_call(
        functools.partial(_mm_kernel, nt=nt, has_res=res is not None),
        grid=(m // tm, n // tn),
        in_specs=in_specs,
        out_specs=pl.BlockSpec((tm, tn), lambda i, j: (i, j)),
        out_shape=jax.ShapeDtypeStruct((m, n), out_dtype),
        compiler_params=_cparams(("parallel", "parallel")),
        name=name,
    )(*args)


def matmul_t(wt, layer, h, nb, t, *, tm, tn, name):
    _, r, k = wt.shape
    tn = min(tn, t)
    nj = t // tn
    return pl.pallas_call(
        functools.partial(_mm_kernel, nt=True, has_res=False),
        grid=(nb, nj, r // tm),
        in_specs=[pl.BlockSpec((None, tm, k), lambda b, j, i: (layer, i, 0)),
                  pl.BlockSpec((tn, k), lambda b, j, i: (b * nj + j, 0))],
        out_specs=pl.BlockSpec((None, tm, tn), lambda b, j, i: (b, i, j)),
        out_shape=jax.ShapeDtypeStruct((nb, r, t), F32),
        compiler_params=_cparams(("parallel", "parallel", "parallel")),
        name=name,
    )(wt, h)


def _kv_stack_kernel(w_ref, h_ref, *refs):
    outs = refs[-6:]
    hb = h_ref[...]
    for i in range(6):
        outs[i][...] = _dot(w_ref[i * GRP_ROWS:(i + 1) * GRP_ROWS, :], hb, nt=True)


def kv_proj_stacked(wt_kv, layer, h, nb, t, stacks, *, tn):
    depth, _, k = wt_kv.shape
    tn = min(tn, t)
    nj = t // tn
    in_specs = [pl.BlockSpec((None, KV_ROWS, k), lambda b, j: (layer, 0, 0)),
                pl.BlockSpec((tn, k), lambda b, j: (b * nj + j, 0))]
    args = [wt_kv, h]
    aliases = {}
    if stacks is not None:
        in_specs += [pl.BlockSpec(memory_space=pl.ANY)] * 6
        args += list(stacks)
        aliases = {2 + i: i for i in range(6)}
    o_spec = pl.BlockSpec((None, None, GRP_ROWS, tn), lambda b, j: (layer, b, 0, j))
    o_shape = jax.ShapeDtypeStruct((depth, nb, GRP_ROWS, t), F32)
    return pl.pallas_call(
        _kv_stack_kernel,
        grid=(nb, nj),
        in_specs=in_specs,
        out_specs=[o_spec] * 6,
        out_shape=[o_shape] * 6,
        input_output_aliases=aliases,
        compiler_params=_cparams(("parallel", "parallel")),
        name="kv_proj_t",
    )(*args)


def _out_norm_kernel(a_ref, w_ref, res_ref, nw_ref, x_ref, h_ref):
    x = res_ref[...] + _dot(a_ref[...], w_ref[...])
    x_ref[...] = x
    y = x * lax.rsqrt(jnp.mean(x * x, axis=-1, keepdims=True) + EPS)
    h_ref[...] = (y * nw_ref[...]).astype(h_ref.dtype)


def out_proj_norm(mix, wo, layer, res, norm_w, h_dtype, *, tm):
    m, k = mix.shape
    n = wo.shape[-1]
    tm = min(tm, m)
    row = lambda i: (i, 0)
    return pl.pallas_call(
        _out_norm_kernel,
        grid=(m // tm,),
        in_specs=[pl.BlockSpec((tm, k), row), pl.BlockSpec((None, k, n), lambda i: (layer, 0, 0)),
                  pl.BlockSpec((tm, n), row), pl.BlockSpec((1, n), lambda i: (0, 0))],
        out_specs=[pl.BlockSpec((tm, n), row), pl.BlockSpec((tm, n), row)],
        out_shape=[jax.ShapeDtypeStruct((m, n), F32), jax.ShapeDtypeStruct((m, n), h_dtype)],
        compiler_params=_cparams(("parallel",)),
        name="out_proj",
    )(mix, wo, res, norm_w.reshape(1, n))


def _merge_kernel(oa_ref, ob_ref, oc_ref, wa_ref, wb_ref, wc_ref, g0_ref, g1_ref, g2_ref, o_ref):
    acc = _sigmoid(g0_ref[...].astype(F32)) * _dot(oa_ref[...], wa_ref[...])
    acc = acc + _sigmoid(g1_ref[...].astype(F32)) * _dot(ob_ref[...], wb_ref[...])
    acc = acc + _sigmoid(g2_ref[...].astype(F32)) * _dot(oc_ref[...], wc_ref[...])
    o_ref[...] = acc.astype(o_ref.dtype)


def merge_branches(o_a, o_b, o_c, wa, wb, wc, layer, ug, *, tm, tn):
    m, k = o_a.shape
    n = wa.shape[-1]
    tm = min(tm, m)
    gb = _UG_OFF["m_g"] // tn
    nj = n // tn
    o_spec = pl.BlockSpec((tm, k), lambda i, j: (i, 0))
    w_spec = pl.BlockSpec((None, k, tn), lambda i, j: (layer, 0, j))
    g_specs = [pl.BlockSpec((tm, tn), functools.partial(lambda i, j, br: (i, gb + br * nj + j), br=br))
               for br in range(3)]
    return pl.pallas_call(
        _merge_kernel,
        grid=(m // tm, nj),
        in_specs=[o_spec, o_spec, o_spec, w_spec, w_spec, w_spec] + g_specs,
        out_specs=pl.BlockSpec((tm, tn), lambda i, j: (i, j)),
        out_shape=jax.ShapeDtypeStruct((m, n), BF16),
        compiler_params=_cparams(("parallel", "parallel")),
        name="merge",
    )(o_a, o_b, o_c, wa, wb, wc, ug, ug, ug)


def _cmp_kernel(seg_ref, w1_ref, pe_ref, w2_ref, o_ref, *, nseg):
    w1 = w1_ref[...]
    ab = _dot(seg_ref[...], w1, HI)
    pe = _dot(pe_ref[...], w1, HI)
    cst = pe[0:1, :HD] + pe[1:2, HD:]
    nxt = pltpu.roll(ab[:, HD:], shift=nseg - 1, axis=0)
    pre = ab[:, :HD] + nxt + cst
    o_ref[...] = _dot(_silu(pre), w2_ref[...], HI)


def compress_blocks(seg, w1cat, pe2, w2, nseg):
    _, r, kdim = seg.shape
    return pl.pallas_call(
        functools.partial(_cmp_kernel, nseg=nseg),
        grid=(2, r // nseg),
        in_specs=[pl.BlockSpec((None, nseg, kdim), lambda s, i: (s, i, 0)),
                  pl.BlockSpec((None, kdim, 2 * HD), lambda s, i: (s, 0, 0)),
                  pl.BlockSpec((None, 8, kdim), lambda s, i: (s, 0, 0)),
                  pl.BlockSpec((None, HD, HD), lambda s, i: (s, 0, 0))],
        out_specs=pl.BlockSpec((None, nseg, HD), lambda s, i: (s, i, 0)),
        out_shape=jax.ShapeDtypeStruct((2, r, HD), F32),
        compiler_params=_cparams(("parallel", "parallel")),
        name="compress",
    )(seg, w1cat, pe2, w2)


def _nsa_prompt_kernel(qt_ref, ag_ref, az_ref, kst_ref, vst_ref, kwt_ref, vwt_ref, kc_ref, vc_ref, o_ref,
                       ks_scr, vs_scr, kw_scr, vw_scr, bias_scr, *, t, tq, ck, nseg):
    g = pl.program_id(1)
    i = pl.program_id(2)
    s0 = i * tq
    ns = t // SLC_BLK
    nsr = bias_scr.shape[0]
    cols = NSA_REP * tq

    @pl.when(i == 0)
    def _():
        ks_scr[...] = jnp.transpose(kst_ref[...]).astype(BF16)
        kw_scr[...] = jnp.transpose(kwt_ref[...]).astype(BF16)
        ones = jnp.ones((ONES_ROWS, t), BF16)
        vs_scr[...] = jnp.concatenate([vst_ref[...].astype(BF16), ones], axis=0)
        vw_scr[...] = jnp.concatenate([vwt_ref[...].astype(BF16), ones], axis=0)

    qt = qt_ref[...]
    q2 = jnp.concatenate([qt[r * HD:(r + 1) * HD, :] for r in range(NSA_REP)], axis=1) * (HD ** -0.5 * LOG2E)
    q2b = q2.astype(BF16)
    tpos1 = s0 + lax.broadcasted_iota(jnp.int32, (1, tq), 1)
    tpos = jnp.concatenate([tpos1] * NSA_REP, axis=1)

    crow = lax.broadcasted_iota(jnp.int32, (nseg, 1), 0)
    m_c = (crow * CMP_STRIDE + (CMP_LEN - 1) <= tpos) & (crow < nseg - 1)
    s_c = jnp.where(m_c, _dot(kc_ref[...], q2, HI), NEG)
    p_c = jnp.where(m_c, jnp.exp2(s_c - jnp.max(s_c, axis=0, keepdims=True)), 0.0)
    l_c = jnp.sum(p_c, axis=0, keepdims=True)
    p_c = p_c / jnp.where(l_c > 0.0, l_c, 1.0)
    o_c = _dot(jnp.transpose(vc_ref[...]), p_c)
    psum = p_c[:, 0:tq]
    for r in range(1, NSA_REP):
        psum = psum + p_c[:, r * tq:(r + 1) * tq]

    srow = lax.broadcasted_iota(jnp.int32, (nsr, 1), 0)
    ccol = lax.broadcasted_iota(jnp.int32, (1, nseg), 1) * CMP_STRIDE
    cover = jnp.where((ccol + CMP_LEN > srow * SLC_BLK) & (ccol < (srow + 1) * SLC_BLK), 1.0, 0.0)
    imp = _dot(cover, psum, HI)
    cur = tpos1 // SLC_BLK
    valid = (srow * SLC_BLK <= tpos1) & (srow < ns)
    forced = (srow == 0) | (srow == cur) | (srow == cur - 1)
    score = jnp.where(valid, jnp.where(forced, FORCE, imp), NEG)
    rank = jnp.zeros((nsr, tq), F32)
    for j in range(ns):
        sj = score[j:j + 1, :]
        rank = rank + jnp.where((sj > score) | ((sj == score) & (j < srow)), 1.0, 0.0)
    sel = (rank < float(min(SLC_TOPK, ns))) & valid
    bias_scr[...] = jnp.where(sel, 0.0, NEG)

    def attend(k_rows, v_cols, bias1, m, acc):
        s = _dot(k_rows, q2b)
        s = jnp.concatenate([s[:, r * tq:(r + 1) * tq] + bias1 for r in range(NSA_REP)], axis=1)
        m_new = jnp.maximum(m, jnp.max(s, axis=0, keepdims=True))
        p = jnp.exp2(s - m_new)
        return m_new, jnp.exp2(m - m_new) * acc + _dot(v_cols, p)

    def finish(acc):
        l = acc[HD:HD + 1, :]
        return acc[:HD, :] / jnp.where(l > 0.0, l, 1.0)

    blk_per_chunk = ck // SLC_BLK
    m0 = jnp.full((1, cols), NEG, F32)
    acc0 = jnp.zeros((HD + ONES_ROWS, cols), F32)

    def sel_chunk(j, carry):
        k0 = pl.multiple_of(j * ck, ck)
        npos = k0 + lax.broadcasted_iota(jnp.int32, (ck, 1), 0)
        rows = [jnp.broadcast_to(bias_scr[pl.ds(j * blk_per_chunk + k, 1), :], (SLC_BLK, tq))
                for k in range(blk_per_chunk)]
        bias1 = jnp.where(npos <= tpos1, jnp.concatenate(rows, axis=0), NEG)
        return attend(ks_scr[pl.ds(k0, ck), :], vs_scr[:, pl.ds(k0, ck)], bias1, *carry)

    n_chunks = (s0 + tq + ck - 1) // ck
    o_s = finish(lax.fori_loop(0, n_chunks, sel_chunk, (m0, acc0))[1])

    ww = min(WINDOW + tq, t)
    w0 = pl.multiple_of(jnp.clip(s0 - WINDOW, 0, t - ww), LANE)
    wpos = w0 + lax.broadcasted_iota(jnp.int32, (ww, 1), 0)
    bias_w = jnp.where((wpos <= tpos1) & (wpos > tpos1 - WINDOW), 0.0, NEG)
    o_w = finish(attend(kw_scr[pl.ds(w0, ww), :], vw_scr[:, pl.ds(w0, ww)], bias_w, m0, acc0)[1])

    outs = []
    for r in range(NSA_REP):
        sl = slice(r * tq, (r + 1) * tq)
        gt = [_sigmoid(ag_ref[pl.ds(3 * (NSA_REP * g + r) + j, 1), :]) for j in range(3)]
        outs.append(jnp.transpose(gt[0] * o_c[:, sl] + gt[1] * o_s[:, sl] + gt[2] * o_w[:, sl]))
    o = jnp.concatenate(outs, axis=1) * _silu(az_ref[...].astype(F32))
    o_ref[...] = o.astype(o_ref.dtype)


def nsa_prompt(ug, qgt, stacks, layer, kc, nb, t, *, tq, ck):
    tq = min(tq, t)
    nq = t // tq
    nseg = t // CMP_STRIDE
    ck = min(ck, t)
    gw = NSA_REP * HD
    nsr = -(-(t // SLC_BLK) // 8) * 8
    kv_spec = pl.BlockSpec((None, None, HD, t), lambda b, g, i: (layer, b, g, 0))
    return pl.pallas_call(
        functools.partial(_nsa_prompt_kernel, t=t, tq=tq, ck=ck, nseg=nseg),
        grid=(nb, NSA_KV, nq),
        in_specs=[pl.BlockSpec((None, gw, tq), lambda b, g, i: (b, g, i)),
                  pl.BlockSpec((None, LANE, tq), lambda b, g, i: (b, TQ_AG // LANE, i)),
                  pl.BlockSpec((tq, gw), lambda b, g, i: (b * nq + i, _UG_OFF["a_z"] // gw + g)),
                  kv_spec, kv_spec, kv_spec, kv_spec,
                  pl.BlockSpec((None, nseg, HD), lambda b, g, i: (0, b * NSA_KV + g, 0)),
                  pl.BlockSpec((None, nseg, HD), lambda b, g, i: (1, b * NSA_KV + g, 0))],
        out_specs=pl.BlockSpec((tq, gw), lambda b, g, i: (b * nq + i, g)),
        out_shape=jax.ShapeDtypeStruct((nb * t, NSA_HEADS * HD), BF16),
        scratch_shapes=[pltpu.VMEM((t, HD), BF16), pltpu.VMEM((HD + ONES_ROWS, t), BF16),
                        pltpu.VMEM((t, HD), BF16), pltpu.VMEM((HD + ONES_ROWS, t), BF16),
                        pltpu.VMEM((nsr, tq), F32)],
        compiler_params=_cparams(("parallel", "parallel", "arbitrary")),
        name="nsa_prompt",
    )(qgt, qgt, ug, stacks[2], stacks[3], stacks[4], stacks[5], kc, kc)


def _gla_kernel(q_ref, k_ref, v_ref, z_ref, sm_ref, s0_ref, wa2_ref, ba_ref, gn_ref, o_ref, sout_ref, s_scr,
                *, c, sub, t_valid):
    ci = pl.program_id(1)
    rows = c * sub

    @pl.when(ci == 0)
    def _():
        s_scr[...] = s0_ref[...]

    row = ci * rows + lax.broadcasted_iota(jnp.int32, (rows, 1), 0)
    gate_in = sm_ref[...][:, SM_BA:SM_BA + GLA_RANK]
    log_a = _log_sigmoid(_dot(gate_in, wa2_ref[...], HI) + ba_ref[...]) / GLA_TAU
    log_a = jnp.where(row < t_valid, log_a, 0.0)
    causal = lax.broadcasted_iota(jnp.int32, (c, c), 0) >= lax.broadcasted_iota(jnp.int32, (c, c), 1)
    tri = jnp.where(causal, 1.0, 0.0)
    cb_all = jnp.concatenate([_dot(tri, log_a[j * c:(j + 1) * c, :], HI) for j in range(sub)], axis=0)
    for h in range(GLA_HEADS):
        ks = slice(h * GLA_HK, (h + 1) * GLA_HK)
        vs = slice(h * GLA_HV, (h + 1) * GLA_HV)
        state = s_scr[h]
        outs = []
        for j in range(sub):
            rs = slice(j * c, (j + 1) * c)
            cb = cb_all[rs, ks]
            ecb = jnp.exp(cb)
            qd = q_ref[rs, ks].astype(F32) * (GLA_HK ** -0.5) * ecb
            kh = k_ref[rs, ks].astype(F32)
            vh = v_ref[rs, vs]
            att = jnp.where(causal, _dot(qd, kh * jnp.exp(-cb), nt=True), 0.0)
            outs.append(_dot(att, vh) + _dot(qd, state))
            c_end = cb[c - 1:c, :]
            k_dec = kh * jnp.exp(c_end - cb)
            e_col = jnp.transpose(ecb[c - 8:c, :])[:, 7:8]
            state = e_col * state + _dot(k_dec, vh, tn=True)
        o = jnp.concatenate(outs, axis=0)
        y = o * lax.rsqrt(jnp.mean(o * o, axis=-1, keepdims=True) + EPS) * gn_ref[...]
        o_ref[:, vs] = (y * _silu(z_ref[:, vs].astype(F32))).astype(o_ref.dtype)
        s_scr[h] = state

    @pl.when(ci == pl.num_programs(1) - 1)
    def _():
        sout_ref[...] = s_scr[...]


def gla_mixer(ug, ux, s0, wa2, ba, gnorm, nb, t, *, c, sub, t_valid):
    rows = c * sub
    nc = t // rows

    def u_spec(name, width):
        return pl.BlockSpec((rows, width), lambda b, i: (b * nc + i, _UG_OFF[name] // width))

    s_spec = pl.BlockSpec((None, GLA_HEADS, GLA_HK, GLA_HV), lambda b, i: (b, 0, 0, 0))
    return pl.pallas_call(
        functools.partial(_gla_kernel, c=c, sub=sub, t_valid=t_valid),
        grid=(nb, nc),
        in_specs=[u_spec("b_q", GLA_DK), u_spec("b_k", GLA_DK), u_spec("b_v", GLA_DV), u_spec("b_z", GLA_DV),
                  pl.BlockSpec((rows, LANE), lambda b, i: (b * nc + i, OFF_SMALL // LANE)),
                  s_spec,
                  pl.BlockSpec((GLA_RANK, GLA_DK), lambda b, i: (0, 0)),
                  pl.BlockSpec((1, GLA_DK), lambda b, i: (0, 0)),
                  pl.BlockSpec((1, GLA_HV), lambda b, i: (0, 0))],
        out_specs=[pl.BlockSpec((rows, GLA_DV), lambda b, i: (b * nc + i, 0)), s_spec],
        out_shape=[jax.ShapeDtypeStruct((nb * t, GLA_DV), BF16),
                   jax.ShapeDtypeStruct((nb, GLA_HEADS, GLA_HK, GLA_HV), F32)],
        scratch_shapes=[pltpu.VMEM((GLA_HEADS, GLA_HK, GLA_HV), F32)],
        compiler_params=_cparams(("parallel", "arbitrary")),
        name="gla",
    )(ug, ug, ug, ug, ux, s0, wa2, ba.reshape(1, GLA_DK), gnorm.reshape(1, GLA_HV))


def _ssd_kernel(xbc_ref, z_ref, sm_ref, s0_ref, cbuf_ref, cw_ref, cbias_ref, dtb_ref, alog_ref, dsk_ref, mn_ref,
                o_ref, sout_ref, cout_ref, s_scr, f_scr, *, q, t_valid, n_chunks):
    ci = pl.program_id(1)
    npad = 8
    hist = M_CONV - 1

    @pl.when(ci == 0)
    def _():
        s_scr[...] = s0_ref[...].reshape(s_scr.shape)
        f_scr[npad - hist:npad, :] = cbuf_ref[...]

    f_scr[npad:npad + q, :] = xbc_ref[...]
    conv = cbias_ref[...]
    for i in range(M_CONV):
        conv = conv + f_scr[npad - hist + i:npad - hist + i + q, :] * cw_ref[i:i + 1, :]
    xbc = _silu(conv)

    last_valid = t_valid - (n_chunks - 1) * q

    @pl.when(ci == n_chunks - 1)
    def _():
        cout_ref[...] = f_scr[npad - hist + last_valid:npad + last_valid, :]

    f_scr[npad - hist:npad, :] = f_scr[npad - hist + q:npad + q, :]

    lane = lax.broadcasted_iota(jnp.int32, (1, LANE), 1)
    row = ci * q + lax.broadcasted_iota(jnp.int32, (q, 1), 0)
    head_lane = (lane >= SM_DT) & (lane < SM_DT + M_HEADS)
    dt = jnp.where(head_lane & (row < t_valid), _softplus(sm_ref[...] + dtb_ref[...]), 0.0)
    a = dt * (-jnp.exp(alog_ref[...]))
    tri = jnp.where(lax.broadcasted_iota(jnp.int32, (q, q), 0) >= lax.broadcasted_iota(jnp.int32, (q, q), 1), 1.0, 0.0)
    causal = tri > 0.5
    cum = _dot(tri, a, HI)
    cum_t = jnp.transpose(cum)
    dt_t = jnp.transpose(dt)
    nbc = M_GROUPS * M_STATE
    rep = M_HEADS // M_GROUPS
    gw = rep * M_HD
    head_of_lane = lax.broadcasted_iota(jnp.int32, (1, gw), 1) // M_HD
    head_of_row = lax.broadcasted_iota(jnp.int32, (gw, 1), 0) // M_HD
    for g in range(M_GROUPS):
        sl = slice(g * gw, (g + 1) * gw)
        bg = xbc[:, M_DINNER + g * M_STATE:M_DINNER + (g + 1) * M_STATE]
        cg = xbc[:, M_DINNER + nbc + g * M_STATE:M_DINNER + nbc + (g + 1) * M_STATE]
        xg = xbc[:, sl]
        cbm = _dot(cg, bg, nt=True)
        s_prev = s_scr[g]
        y = jnp.zeros((q, gw), F32)
        e_lanes = jnp.zeros((q, gw), F32)
        w_lanes = jnp.zeros((q, gw), F32)
        skip = jnp.zeros((1, gw), F32)
        keep = jnp.zeros((gw, 1), F32)
        for r in range(rep):
            hl = SM_DT + g * rep + r
            mine = head_of_lane == r
            cum_c = cum[:, hl:hl + 1]
            c_end = cum[q - 1:q, hl:hl + 1]
            decay = jnp.exp(jnp.where(causal, cum_c - cum_t[hl:hl + 1, :], -jnp.inf))
            y = y + _dot(cbm * decay * dt_t[hl:hl + 1, :], jnp.where(mine, xg, 0.0))
            e_lanes = jnp.where(mine, jnp.exp(cum_c), e_lanes)
            w_lanes = jnp.where(mine, jnp.exp(c_end - cum_c) * dt[:, hl:hl + 1], w_lanes)
            skip = jnp.where(mine, dsk_ref[:, hl:hl + 1], skip)
            keep = jnp.where(head_of_row == r, jnp.exp(c_end), keep)
        y = y + e_lanes * _dot(cg, s_prev, nt=True) + skip * xg
        s_scr[g] = keep * s_prev + _dot(xg * w_lanes, bg, tn=True)
        yg = y * _silu(z_ref[:, sl].astype(F32))
        yn = yg * lax.rsqrt(jnp.mean(yg * yg, axis=-1, keepdims=True) + EPS) * mn_ref[:, sl]
        o_ref[:, sl] = yn.astype(o_ref.dtype)

    @pl.when(ci == n_chunks - 1)
    def _():
        sout_ref[...] = s_scr[...].reshape(M_HEADS, M_HD, M_STATE)


def _pad_heads(v):
    return jnp.zeros((1, LANE), F32).at[0, SM_DT:SM_DT + M_HEADS].set(v)


def ssd_mixer(ug, ux, s0, cbuf, conv_w, conv_b, dt_bias, a_log, d_skip, m_norm, nb, t, *, q, t_valid):
    nc = t // q
    n_chunks = -(-t_valid // q)
    assert n_chunks == nc
    hist = M_CONV - 1

    def u_spec(off, name, width):
        return pl.BlockSpec((q, width), lambda b, i: (b * nc + i, off[name] // width))

    def full(shape):
        return pl.BlockSpec(shape, lambda b, i: (0,) * len(shape))

    s_spec = pl.BlockSpec((None, M_HEADS, M_HD, M_STATE), lambda b, i: (b, 0, 0, 0))
    c_spec = pl.BlockSpec((None, hist, M_CONVDIM), lambda b, i: (b, 0, 0))
    return pl.pallas_call(
        functools.partial(_ssd_kernel, q=q, t_valid=t_valid, n_chunks=n_chunks),
        grid=(nb, nc),
        in_specs=[u_spec(_UX_OFF, "c_xbc", M_CONVDIM), u_spec(_UG_OFF, "c_z", M_DINNER),
                  pl.BlockSpec((q, LANE), lambda b, i: (b * nc + i, OFF_SMALL // LANE)),
                  s_spec, c_spec,
                  full((M_CONV, M_CONVDIM)), full((1, M_CONVDIM)), full((1, LANE)), full((1, LANE)),
                  full((1, LANE)), full((1, M_DINNER))],
        out_specs=[pl.BlockSpec((q, M_DINNER), lambda b, i: (b * nc + i, 0)), s_spec, c_spec],
        out_shape=[jax.ShapeDtypeStruct((nb * t, M_DINNER), BF16),
                   jax.ShapeDtypeStruct((nb, M_HEADS, M_HD, M_STATE), F32),
                   jax.ShapeDtypeStruct((nb, hist, M_CONVDIM), F32)],
        scratch_shapes=[pltpu.VMEM((M_GROUPS, M_HEADS // M_GROUPS * M_HD, M_STATE), F32),
                        pltpu.VMEM((q + 8, M_CONVDIM), F32)],
        compiler_params=_cparams(("parallel", "arbitrary")),
        name="ssd",
    )(ux, ug, ux, s0, cbuf, conv_w, conv_b.reshape(1, M_CONVDIM), _pad_heads(dt_bias), _pad_heads(a_log),
      _pad_heads(d_skip), m_norm.reshape(1, M_DINNER))


SEG_PER_PAGE = PAGE // CMP_STRIDE


def _page_cmp_kernel(pt_ref, *refs, npg):
    w1_ref, ok_ref, ov_ref, scr, wbd_scr = refs[2 * npg:]
    n_lane_blk = NSA_KV * HD // LANE

    @pl.when((pl.program_id(0) == 0) & (pl.program_id(1) == 0))
    def _():
        wbd_scr[...] = jnp.zeros(wbd_scr.shape, wbd_scr.dtype)
        for s in range(2):
            for j in range(CMP_STRIDE):
                wj = w1_ref[s, j * HD:(j + 1) * HD, :].astype(BF16)
                for g in range(NSA_KV):
                    wbd_scr[s, j, g * HD:(g + 1) * HD, g * 2 * HD:(g + 1) * 2 * HD] = wj

    for s, (pages, o_ref) in enumerate(((refs[:npg], ok_ref), (refs[npg:2 * npg], ov_ref))):
        for k in range(npg):
            page_t = pages[k][...].reshape(NSA_KV * HD, PAGE)
            pos_major = jnp.transpose(page_t)
            for c in range(n_lane_blk):
                scr[c, k * PAGE:(k + 1) * PAGE, :] = pos_major[:, c * LANE:(c + 1) * LANE]
        acc = jnp.zeros((npg * SEG_PER_PAGE, 2 * HD * NSA_KV), F32)
        for j in range(CMP_STRIDE):
            rows = pl.ds(j, npg * SEG_PER_PAGE, stride=CMP_STRIDE)
            xj = jnp.concatenate([scr[c, rows, :] for c in range(n_lane_blk)], axis=1)
            acc = acc + _dot(xj, wbd_scr[s, j])
        o_ref[...] = acc


def page_compress(cache_k_t, cache_v_t, page_table, layer, w1cat, *, npg):
    nb, n_pages = page_table.shape
    steps = n_pages // npg

    def page_spec(k):
        return pl.BlockSpec((None, None, NSA_KV, HD, PAGE),
                            lambda b, i, pt: (layer, pt[b, i * npg + k], 0, 0, 0))

    w_spec = pl.BlockSpec((None, 2, CMP_STRIDE * HD, 2 * HD), lambda b, i, pt: (layer, 0, 0, 0))
    o_spec = pl.BlockSpec((None, npg * SEG_PER_PAGE, NSA_KV * 2 * HD), lambda b, i, pt: (b, i, 0))
    o_shape = jax.ShapeDtypeStruct((nb, n_pages * SEG_PER_PAGE, NSA_KV * 2 * HD), F32)
    return pl.pallas_call(
        functools.partial(_page_cmp_kernel, npg=npg),
        grid_spec=pltpu.PrefetchScalarGridSpec(
            num_scalar_prefetch=1,
            grid=(nb, steps),
            in_specs=[page_spec(k) for k in range(npg)] * 2 + [w_spec],
            out_specs=[o_spec, o_spec],
            scratch_shapes=[pltpu.VMEM((NSA_KV * HD // LANE, npg * PAGE, LANE), F32),
                            pltpu.VMEM((2, CMP_STRIDE, NSA_KV * HD, NSA_KV * 2 * HD), BF16)]),
        out_shape=[o_shape, o_shape],
        compiler_params=_cparams(("arbitrary", "arbitrary")),
        name="page_compress",
    )(page_table, *([cache_k_t] * npg), *([cache_v_t] * npg), w1cat)


def _masked_softmax_rows(s, mask):
    s = jnp.where(mask, s, NEG)
    m = jnp.max(s, axis=-1, keepdims=True)
    p = jnp.where(mask, jnp.exp(s - m), 0.0)
    l = jnp.sum(p, axis=-1, keepdims=True)
    return p / jnp.where(l > 0.0, l, 1.0)


def _nsa_sample_cmp_kernel(abk_ref, abv_ref, kvs_ref, q_ref, w1_ref, pe_ref, w2_ref, cover_ref, oc_ref, sel_ref,
                           cst_scr, *, p_len, nc, sl):
    g = pl.program_id(1)
    ns = p_len // SLC_BLK + 1
    rowi = lax.broadcasted_iota(jnp.int32, (nc, 1), 0)

    @pl.when((pl.program_id(0) == 0) & (g == 0))
    def _():
        for idx in range(2):
            pe = _dot(pe_ref[idx], w1_ref[idx], HI)
            cst_scr[idx:idx + 1, :] = pe[0:1, :HD] + pe[1:2, HD:]

    def compressed(ab_ref, idx):
        ab = ab_ref[...]
        new = kvs_ref[pl.ds(idx * NSA_KV + g, 1), :]
        b_new = _dot(new, w1_ref[idx][0:HD, HD:])
        nxt = jnp.where(rowi == nc - 1, b_new, pltpu.roll(ab[:, HD:], shift=nc - 1, axis=0))
        return _dot(_silu(ab[:, :HD] + nxt + cst_scr[idx:idx + 1, :]), w2_ref[idx])

    kc = compressed(abk_ref, 0)
    vc = compressed(abv_ref, 1)
    q4 = q_ref[...] * (HD ** -0.5)
    cidx = lax.broadcasted_iota(jnp.int32, (1, nc), 1)
    m_c = cidx * CMP_STRIDE + (CMP_LEN - 1) <= p_len
    p_c = _masked_softmax_rows(_dot(q4, kc, HI, nt=True), m_c)
    oc_ref[...] = _dot(p_c, vc)
    imp = _dot_exact_rhs(jnp.sum(p_c, axis=0, keepdims=True), cover_ref[...])
    sidx = lax.broadcasted_iota(jnp.int32, (1, sl), 1)
    cur = p_len // SLC_BLK
    valid = (sidx * SLC_BLK <= p_len) & (sidx < ns)
    forced = (sidx == 0) | (sidx == cur) | (sidx == cur - 1)
    score = jnp.where(valid, jnp.where(forced, FORCE, imp), NEG)
    ri = lax.broadcasted_iota(jnp.int32, (sl, sl), 0)
    li = lax.broadcasted_iota(jnp.int32, (sl, sl), 1)
    score_b = jnp.broadcast_to(score, (sl, sl))
    score_col = jnp.sum(jnp.where(ri == li, score_b, 0.0), axis=1, keepdims=True)
    ahead = (score_b > score_col) | ((score_b == score_col) & (li < ri))
    rank_col = jnp.sum(jnp.where(ahead, 1.0, 0.0), axis=1, keepdims=True)
    kl = lax.broadcasted_iota(jnp.int32, (sl, LANE), 1).astype(F32)
    sv = lax.broadcasted_iota(jnp.int32, (sl, LANE), 0).astype(F32)
    picked = jnp.sum(jnp.where(rank_col == kl, sv, 0.0), axis=0, keepdims=True)
    sel_ref[...] = picked.astype(jnp.int32)


def _cover_matrix(nc, sl):
    c0 = np.arange(nc)[:, None] * CMP_STRIDE
    s0 = np.arange(sl)[None, :] * SLC_BLK
    return jnp.asarray(((c0 + CMP_LEN > s0) & (c0 < s0 + SLC_BLK)).astype(np.float32), dtype=BF16)


def nsa_sample_cmp(abk, abv, kvs, q3, w1cat, pe2, w2, p_len):
    nb, nc, _ = abk.shape
    ns = p_len // SLC_BLK + 1
    sl = -(-ns // LANE) * LANE
    cover = _cover_matrix(nc, sl)

    def ab_spec():
        return pl.BlockSpec((None, nc, 2 * HD), lambda b, g: (b, 0, g))

    def full(a):
        return pl.BlockSpec(a.shape, lambda b, g: (0,) * a.ndim)

    row_spec = pl.BlockSpec((None, None, NSA_REP, HD), lambda b, g: (b, g, 0, 0))
    return pl.pallas_call(
        functools.partial(_nsa_sample_cmp_kernel, p_len=p_len, nc=nc, sl=sl),
        grid=(nb, NSA_KV),
        in_specs=[ab_spec(), ab_spec(), pl.BlockSpec((None, 6 * NSA_KV, HD), lambda b, g: (b, 0, 0)), row_spec,
                  full(w1cat), full(pe2), full(w2), full(cover)],
        out_specs=[row_spec, pl.BlockSpec((None, None, 1, LANE), lambda b, g: (b, g, 0, 0))],
        out_shape=[jax.ShapeDtypeStruct((nb, NSA_KV, NSA_REP, HD), F32),
                   jax.ShapeDtypeStruct((nb, NSA_KV, 1, LANE), jnp.int32)],
        scratch_shapes=[pltpu.VMEM((8, HD), F32)],
        compiler_params=_cparams(("arbitrary", "arbitrary")),
        name="nsa_sample_cmp",
    )(abk, abv, kvs, q3, w1cat, pe2, w2, cover)


def _nsa_sample_sel_kernel(sel_ref, pt_ref, *refs, p_len, nb, n_buf):
    nk = SLC_TOPK
    kpages, vpages = refs[:nk], refs[nk:2 * nk]
    (kwb_ref, vwb_ref, kvt_ref, kvs_ref, q_ref, az_ref, sm_ref, oc_ref,
     o_ref, kwo_ref, vwo_ref) = refs[2 * nk:]
    b = pl.program_id(0)
    g = pl.program_id(1)
    n_past = p_len // SLC_BLK
    q4 = q_ref[...] * (HD ** -0.5)
    lane_b = lax.broadcasted_iota(jnp.int32, (1, nb), 1) == b

    def new_row(i):
        return kvs_ref[pl.ds(i * NSA_KV + g, 1), :]

    def new_col(i):
        blk = kvt_ref[pl.ds(pl.multiple_of((i * NSA_KV + g) * HD, HD), HD), :]
        return jnp.sum(jnp.where(lane_b, blk, 0.0), axis=1, keepdims=True)

    half_of_lane = lax.broadcasted_iota(jnp.int32, (1, PAGE), 1) // SLC_BLK
    s_parts, m_parts = [], []
    has_new = jnp.zeros((1, 1), jnp.int32)
    for k in range(nk):
        blk = sel_ref[b, g, k]
        m_parts.append((half_of_lane == blk % 2) & (blk < n_past))
        s_parts.append(_dot(q4, kpages[k][...]))
        has_new = has_new + jnp.where(blk == n_past, 1, 0)
    s_sel = jnp.concatenate(s_parts, axis=1)
    m_sel = jnp.concatenate(m_parts, axis=1)
    new_ok = has_new > 0
    s_new = jnp.where(new_ok, jnp.sum(q4 * new_row(2), axis=1, keepdims=True), NEG)
    s_sel = jnp.where(m_sel, s_sel, NEG)
    mx = jnp.maximum(jnp.max(s_sel, axis=1, keepdims=True), s_new)
    p_sel = jnp.where(m_sel, jnp.exp(s_sel - mx), 0.0)
    p_new = jnp.where(new_ok, jnp.exp(s_new - mx), 0.0)
    den = jnp.sum(p_sel, axis=1, keepdims=True) + p_new
    acc = p_new * new_row(3)
    for k in range(nk):
        acc = acc + _dot(p_sel[:, k * PAGE:(k + 1) * PAGE], vpages[k][...], nt=True)
    o_s = acc / jnp.where(den > 0.0, den, 1.0)

    kwb = kwb_ref[...]
    vwb = vwb_ref[...]
    wlane = lax.broadcasted_iota(jnp.int32, (1, n_buf), 1)
    m_w = wlane > n_buf - WINDOW
    s_w = jnp.where(m_w, _dot(q4, kwb), NEG)
    s_wn = jnp.sum(q4 * new_row(4), axis=1, keepdims=True)
    mw = jnp.maximum(jnp.max(s_w, axis=1, keepdims=True), s_wn)
    p_w = jnp.where(m_w, jnp.exp(s_w - mw), 0.0)
    p_wn = jnp.exp(s_wn - mw)
    o_w = (_dot(p_w, vwb, nt=True) + p_wn * new_row(5)) / (jnp.sum(p_w, axis=1, keepdims=True) + p_wn)
    kwo_ref[...] = jnp.where(wlane == n_buf - 1, new_col(4), pltpu.roll(kwb, shift=n_buf - 1, axis=1))
    vwo_ref[...] = jnp.where(wlane == n_buf - 1, new_col(5), pltpu.roll(vwb, shift=n_buf - 1, axis=1))

    gate_row = _sigmoid(pltpu.roll(sm_ref[pl.ds(b, 1), :], shift=(LANE - 3 * NSA_REP * g) % LANE, axis=1))
    gl = lax.broadcasted_iota(jnp.int32, (NSA_REP, LANE), 1)
    gr = lax.broadcasted_iota(jnp.int32, (NSA_REP, LANE), 0)

    def gate(j):
        return jnp.sum(jnp.where(gl == 3 * gr + j, gate_row, 0.0), axis=1, keepdims=True)

    o = gate(0) * oc_ref[...] + gate(1) * o_s + gate(2) * o_w
    o_ref[...] = o * _silu(az_ref[...])


def nsa_sample_sel(sel, page_table, slc_k_t, slc_v_t, win_k_t, win_v_t, layer, kvt, kvs, q3, az3, ux, o_cmp, p_len):
    nb = page_table.shape[0]
    n_buf = win_k_t.shape[-1]
    last_blk = p_len // SLC_BLK - 1

    def page_spec(k):
        def imap(b, g, sel_r, pt_r):
            blk = jnp.minimum(sel_r[b, g, k], last_blk)
            return (layer, pt_r[b, blk // (PAGE // SLC_BLK)], g, 0, 0)
        return pl.BlockSpec((None, None, None, HD, PAGE), imap)

    win_spec = pl.BlockSpec((None, None, None, HD, n_buf), lambda b, g, s, p: (layer, b, g, 0, 0))
    wout_spec = pl.BlockSpec((None, None, HD, n_buf), lambda b, g, s, p: (b, g, 0, 0))
    row_spec = pl.BlockSpec((None, None, NSA_REP, HD), lambda b, g, s, p: (b, g, 0, 0))

    def full(a):
        return pl.BlockSpec(a.shape, lambda b, g, s, p: (0,) * a.ndim)

    small = pl.BlockSpec((nb, LANE), lambda b, g, s, p: (0, OFF_SMALL // LANE))
    return pl.pallas_call(
        functools.partial(_nsa_sample_sel_kernel, p_len=p_len, nb=nb, n_buf=n_buf),
        grid_spec=pltpu.PrefetchScalarGridSpec(
            num_scalar_prefetch=2,
            grid=(nb, NSA_KV),
            in_specs=[page_spec(k) for k in range(SLC_TOPK)] * 2
                     + [win_spec, win_spec, full(kvt), pl.BlockSpec((None, 6 * NSA_KV, HD), lambda b, g, s, p: (b, 0, 0)),
                        row_spec, row_spec, small, row_spec],
            out_specs=[row_spec, wout_spec, wout_spec]),
        out_shape=[jax.ShapeDtypeStruct((nb, NSA_KV, NSA_REP, HD), F32),
                   jax.ShapeDtypeStruct((nb, NSA_KV, HD, n_buf), F32),
                   jax.ShapeDtypeStruct((nb, NSA_KV, HD, n_buf), F32)],
        compiler_params=_cparams(("parallel", "arbitrary")),
        name="nsa_sample_sel",
    )(sel, page_table, *([slc_k_t] * SLC_TOPK), *([slc_v_t] * SLC_TOPK), win_k_t, win_v_t, kvt, kvs, q3, az3, ux, o_cmp)


def _prep_weights(w_in, cmp_pe_k, cmp_w1_k, cmp_w2_k, cmp_pe_v, cmp_w1_v, cmp_w2_v, w_br_a, w_br_b, w_br_c, w_out):
    wt = jnp.transpose(w_in, (0, 2, 1))

    def gather_rows(order, total):
        parts = [wt[:, _IN_OFF[n]:_IN_OFF[n] + _IN_SZ[n]] for n in order]
        used = sum(_IN_SZ[n] for n in order)
        parts.append(jnp.zeros((wt.shape[0], total - used, wt.shape[2]), wt.dtype))
        return jnp.concatenate(parts, axis=1).astype(BF16)

    half = CMP_STRIDE * HD

    def cat(w1):
        return jnp.concatenate([w1[:, :half], w1[:, half:]], axis=2)

    def pe_rows(pe):
        d = pe.shape[0]
        rows = pe.reshape(d, 2, half)
        return jnp.concatenate([rows, jnp.zeros((d, 6, half), pe.dtype)], axis=1)

    w1cat = jnp.stack([cat(cmp_w1_k), cat(cmp_w1_v)], axis=1)
    pe2 = jnp.stack([pe_rows(cmp_pe_k), pe_rows(cmp_pe_v)], axis=1)
    w2 = jnp.stack([cmp_w2_k, cmp_w2_v], axis=1)
    return dict(wt_ug=gather_rows(_UG_ORDER, NP_UG), wt_ux=gather_rows(_UX_ORDER, NP_UX), wt_kv=gather_rows(("a_kv",), KV_ROWS),
                wt_q=gather_rows(_TQ_ORDER, TQ_ROWS),
                w1cat=w1cat, pe2=pe2, w2=w2,
                wa=w_br_a.astype(BF16), wb=w_br_b.astype(BF16), wc=w_br_c.astype(BF16), wo=w_out.astype(BF16))


def _prompt_layer(x, h, nb, t, layer, w, p, next_norm, tiles, stacks):
    ug = matmul(h, w["wt_ug"], nt=True, tm=tiles["mm_tm"], tn=tiles["in_tn"], out_dtype=BF16, name="in_proj",
                b_layer=layer)
    ux = matmul(h, w["wt_ux"], nt=True, tm=tiles["mm_tm"], tn=NP_UX, name="in_proj_x", b_layer=layer)
    stacks = kv_proj_stacked(w["wt_kv"], layer, h, nb, t, stacks, tn=tiles["kv_tn"])
    qgt = matmul_t(w["wt_q"], layer, h, nb, t, tm=TQ_TM, tn=tiles["q_tn"], name="q_proj_t")
    nseg = t // CMP_STRIDE
    seg = jnp.stack([stacks[0][layer], stacks[1][layer]]).reshape(2, nb, NSA_KV, HD, t)
    seg = jnp.transpose(seg, (0, 1, 2, 4, 3)).reshape(2, nb * NSA_KV * nseg, CMP_STRIDE * HD)
    kc = compress_blocks(seg, w["w1cat"][layer], w["pe2"][layer], w["w2"][layer], nseg)
    o_a = nsa_prompt(ug, qgt, stacks, layer, kc, nb, t, tq=tiles["tq"], ck=tiles["ck"])
    zeros_gla = jnp.zeros((nb, GLA_HEADS, GLA_HK, GLA_HV), F32)
    zeros_ssm = jnp.zeros((nb, M_HEADS, M_HD, M_STATE), F32)
    zeros_conv = jnp.zeros((nb, M_CONV - 1, M_CONVDIM), F32)
    o_b, gla_s = gla_mixer(ug, ux, zeros_gla, p["gla_wa2"], p["gla_ba"], p["gla_norm"], nb, t,
                           c=tiles["gla_c"], sub=min(tiles["gla_sub"], t // tiles["gla_c"]), t_valid=t)
    o_c, ssm_s, conv_s = ssd_mixer(ug, ux, zeros_ssm, zeros_conv, p["conv_w"], p["conv_b"], p["dt_bias"], p["a_log"],
                                   p["d_skip"], p["m_norm"], nb, t, q=min(tiles["ssd_q"], t), t_valid=t)
    mix = merge_branches(o_a, o_b, o_c, w["wa"], w["wb"], w["wc"], layer, ug, tm=tiles["mg_tm"], tn=tiles["mg_tn"])
    x_out, h_out = out_proj_norm(mix, w["wo"], layer, x, *next_norm, tm=tiles["out_tm"])
    return x_out, h_out, stacks, (gla_s, ssm_s, conv_s)


ROW_PAD = 8


def _sample_layer(x, h, layer, w, p, next_norm, caches, states, page_table, tiles):
    nb = x.shape[0]
    p_len = page_table.shape[1] * PAGE
    ck_t, cv_t, sk_t, sv_t, wk_t, wv_t = caches
    ug = matmul(h, w["wt_ug"], nt=True, tm=nb, tn=tiles["in_tn"], out_dtype=BF16, name="in_proj_s", b_layer=layer)
    ux = matmul(h, w["wt_ux"], nt=True, tm=nb, tn=NP_UX, name="in_proj_xs", b_layer=layer)
    kv = matmul(h, w["wt_kv"], nt=True, tm=nb, tn=512, name="kv_proj_s", b_layer=layer)
    kvt = matmul(w["wt_kv"], h, nt=True, tm=512, tn=nb, name="kv_proj_st", a_layer=layer)
    qg = matmul(h, w["wt_q"], nt=True, tm=nb, tn=TQ_TM, name="q_proj_s", b_layer=layer)
    kvs = kv.reshape(nb, 6 * NSA_KV, HD)
    def pad_rows(a):
        return jnp.pad(a[:, None, :], ((0, 0), (0, ROW_PAD - 1), (0, 0))).reshape(nb * ROW_PAD, a.shape[1])

    ug_pad, ux_pad = pad_rows(ug), pad_rows(ux)
    o_b, gla_s = gla_mixer(ug_pad, ux_pad, states[0], p["gla_wa2"], p["gla_ba"], p["gla_norm"], nb, ROW_PAD,
                           c=ROW_PAD, sub=1, t_valid=1)
    o_c, ssm_s, conv_s = ssd_mixer(ug_pad, ux_pad, states[1], states[2], p["conv_w"], p["conv_b"], p["dt_bias"], p["a_log"],
                                   p["d_skip"], p["m_norm"], nb, ROW_PAD, q=ROW_PAD, t_valid=1)
    o_b = o_b.reshape(nb, ROW_PAD, GLA_DV)[:, 0]
    o_c = o_c.reshape(nb, ROW_PAD, M_DINNER)[:, 0]
    abk, abv = page_compress(ck_t, cv_t, page_table, layer, w["w1cat"], npg=min(tiles["npg"], page_table.shape[1]))
    q3 = qg[:, :TQ_AG].reshape(nb, NSA_KV, NSA_REP, HD)
    az3 = ug[:, _UG_OFF["a_z"]:_UG_OFF["a_z"] + MIX_W].astype(F32).reshape(nb, NSA_KV, NSA_REP, HD)
    o_cmp, sel = nsa_sample_cmp(abk, abv, kvs, q3, w["w1cat"][layer], w["pe2"][layer], w["w2"][layer], p_len)
    o_a, win_k, win_v = nsa_sample_sel(sel.reshape(nb, NSA_KV, LANE), page_table, sk_t, sv_t, wk_t, wv_t, layer,
                                       kvt, kvs, q3, az3, ux, o_cmp, p_len)
    mix = merge_branches(o_a.reshape(nb, MIX_W).astype(BF16), o_b, o_c, w["wa"], w["wb"], w["wc"], layer, ug,
                         tm=nb, tn=tiles["mg_tn"])
    x_out, h_out = out_proj_norm(mix, w["wo"], layer, x, *next_norm, tm=nb)
    kv4 = kv.reshape(nb, 6, 1, NSA_KV, HD)
    return x_out, h_out, (kv4[:, 0], kv4[:, 1], kv4[:, 2], kv4[:, 3], win_k, win_v, gla_s, ssm_s, conv_s)


_TILES = dict(rms_tm=256, mm_tm=1024, in_tn=1024, kv_tn=512, q_tn=2048, tq=256, ck=512, gla_c=64, gla_sub=4,
              ssd_q=256, mg_tm=1024, mg_tn=512, out_tm=512, npg=16)
_PARAM_NAMES = ("ln_w", "gla_wa2", "gla_ba", "gla_norm", "conv_w", "conv_b", "dt_bias", "a_log", "d_skip", "m_norm")


def kernel(x_prompt, x_sample, cache_cmp_k, cache_cmp_v, cache_slc_k, cache_slc_v, cache_win_k, cache_win_v,
           state_gla, state_ssm, state_conv, page_table, ln_w, w_in, cmp_pe_k, cmp_w1_k, cmp_w2_k, cmp_pe_v,
           cmp_w1_v, cmp_w2_v, gla_wa2, gla_ba, gla_norm, conv_w, conv_b, dt_bias, a_log, d_skip, m_norm,
           w_br_a, w_br_b, w_br_c, w_out, final_norm):
    nbp, t, d = x_prompt.shape
    nbs = x_sample.shape[0]
    depth = w_in.shape[0]
    weights = _prep_weights(w_in, cmp_pe_k, cmp_w1_k, cmp_w2_k, cmp_pe_v, cmp_w1_v, cmp_w2_v,
                            w_br_a, w_br_b, w_br_c, w_out)
    params = dict(zip(_PARAM_NAMES, (ln_w, gla_wa2, gla_ba, gla_norm, conv_w, conv_b, dt_bias, a_log, d_skip, m_norm)))
    caches = tuple(jnp.transpose(c, (0, 1, 3, 4, 2))
                   for c in (cache_cmp_k, cache_cmp_v, cache_slc_k, cache_slc_v, cache_win_k, cache_win_v))
    xp = x_prompt.reshape(nbp * t, d)
    xs = x_sample.reshape(nbs, d)
    p_states, s_out, stacks = [], [], None
    hp = rmsnorm_rows(xp, ln_w[0], BF16, _TILES["rms_tm"])
    hs = rmsnorm_rows(xs, ln_w[0], BF16, nbs)
    for l in range(depth):
        p = {k: v[l] for k, v in params.items()}
        next_norm = (ln_w[l + 1], BF16) if l + 1 < depth else (final_norm, F32)
        xp, hp, stacks, st_p = _prompt_layer(xp, hp, nbp, t, l, weights, p, next_norm, _TILES, stacks)
        xs, hs, st_s = _sample_layer(xs, hs, l, weights, p, next_norm, caches,
                                     (state_gla[l], state_ssm[l], state_conv[l]), page_table, _TILES)
        p_states.append(st_p)
        s_out.append(st_s)
    y_prompt = hp.reshape(nbp, t, d)
    y_sample = hs.reshape(nbs, 1, d)

    def from_kv_layout(a):
        return jnp.transpose(a, (0, 1, 4, 2, 3))

    n_w = min(WINDOW, t)
    kv5 = [a.reshape(depth, nbp, NSA_KV, HD, t) for a in stacks]
    p_leaves = [from_kv_layout(a) for a in kv5[:4]] + [from_kv_layout(a[..., t - n_w:]) for a in kv5[4:]]
    p_leaves += [jnp.stack([st[i] for st in p_states]) for i in range(3)]
    s_leaves = [jnp.stack([o[i] for o in s_out]) for i in range(9)]
    s_leaves[4], s_leaves[5] = from_kv_layout(s_leaves[4]), from_kv_layout(s_leaves[5])
    return (y_prompt, y_sample, *p_leaves, *s_leaves)
```

```python
import functools
import math

import jax
import jax.numpy as jnp
import numpy as np
from jax import lax
from jax.experimental import pallas as pl
from jax.experimental.pallas import tpu as pltpu

F32 = jnp.float32
BF16 = jnp.bfloat16
HI = lax.Precision.HIGHEST

D_MODEL = 2048
PAGE = 128
MIX_W = D_MODEL // 2
HD = 64
NSA_HEADS = MIX_W // HD
NSA_KV = 4
NSA_REP = NSA_HEADS // NSA_KV
CMP_LEN = 32
CMP_STRIDE = 16
SLC_BLK = 64
SLC_TOPK = 16
WINDOW = 512
GLA_HEADS = 4
GLA_DK = MIX_W // 2
GLA_DV = MIX_W
GLA_HK = GLA_DK // GLA_HEADS
GLA_HV = GLA_DV // GLA_HEADS
GLA_RANK = 16
GLA_TAU = 16.0
M_DINNER = MIX_W
M_HD = 64
M_HEADS = M_DINNER // M_HD
M_GROUPS = 4
M_STATE = 128
M_CONV = 4
M_CONVDIM = M_DINNER + 2 * M_GROUPS * M_STATE
EPS = 1e-6
NEG = -1e30
FORCE = 1e4
LOG2E = math.log2(math.e)
ONES_ROWS = 16

_IN_NAMES = ("a_q", "a_kv", "a_g", "a_z", "b_q", "b_k", "b_v", "b_a", "b_z", "c_xbc", "c_dt", "c_z", "m_g")
_IN_SIZES = (NSA_HEADS * HD, 6 * NSA_KV * HD, 3 * NSA_HEADS, MIX_W, GLA_DK, GLA_DK, GLA_DV, GLA_RANK, GLA_DV,
             M_CONVDIM, M_HEADS, M_DINNER, 3 * D_MODEL)
_IN_OFF = dict(zip(_IN_NAMES, np.cumsum((0,) + _IN_SIZES)[:-1].tolist()))
_IN_SZ = dict(zip(_IN_NAMES, _IN_SIZES))

LANE = 128
_UG_ORDER = ("m_g", "a_z", "b_v", "b_z", "c_z", "b_q", "b_k")
_UX_ORDER = ("c_xbc", "a_g", "b_a", "c_dt")


def _offsets(order):
    off, o = {}, 0
    for n in order:
        off[n] = o
        o += _IN_SZ[n]
    return off, o


_UG_OFF, NP_UG = _offsets(_UG_ORDER)
_UX_OFF, _ux_used = _offsets(_UX_ORDER)
NP_UX = -(-_ux_used // LANE) * LANE
OFF_SMALL = _UX_OFF["a_g"]
SM_AG, SM_BA, SM_DT = 0, 3 * NSA_HEADS, 3 * NSA_HEADS + GLA_RANK
KV_ROWS = _IN_SZ["a_kv"]
GRP_ROWS = NSA_KV * HD
_TQ_ORDER = ("a_q", "a_g")
TQ_AG = _IN_SZ["a_q"]
TQ_TM = 384
TQ_ROWS = -(-(TQ_AG + _IN_SZ["a_g"]) // TQ_TM) * TQ_TM
VMEM_LIMIT = 56 * 1024 * 1024


def _cparams(sem):
    return pltpu.CompilerParams(dimension_semantics=sem, vmem_limit_bytes=VMEM_LIMIT)


def _dot(a, b, prec=None, nt=False, tn=False):
    if prec is None:
        a, b = a.astype(BF16), b.astype(BF16)
    dn = (((0,) if tn else (1,), (1,) if nt else (0,)), ((), ()))
    return lax.dot_general(a, b, dn, precision=prec, preferred_element_type=F32)


def _split(a):
    hi = a.astype(BF16)
    return hi, (a - hi.astype(F32)).astype(BF16)


def _dot_exact_rhs(a, b_bf16):
    hi, lo = _split(a)
    return _dot(hi, b_bf16) + _dot(lo, b_bf16)


def _dot_exact_lhs(a_bf16, b):
    hi, lo = _split(b)
    return _dot(a_bf16, hi) + _dot(a_bf16, lo)


def _dot3(a, b, **kw):
    ah, al = _split(a)
    bh, bl = _split(b)
    return _dot(ah, bh, **kw) + _dot(ah, bl, **kw) + _dot(al, bh, **kw)


def _sigmoid(x):
    return 0.5 * jnp.tanh(0.5 * x) + 0.5


def _silu(x):
    return x * _sigmoid(x)


def _log_sigmoid(x):
    return jnp.minimum(x, 0.0) - jnp.log(1.0 + jnp.exp(-jnp.abs(x)))


def _softplus(x):
    return jnp.maximum(x, 0.0) + jnp.log(1.0 + jnp.exp(-jnp.abs(x)))


def _stacked_spec(block, imap, layer):
    if layer is None:
        return pl.BlockSpec(block, imap)
    return pl.BlockSpec((None,) + block, lambda *idx: (layer,) + imap(*idx))


def _rms_kernel(x_ref, w_ref, o_ref):
    x = x_ref[...]
    y = x * lax.rsqrt(jnp.mean(x * x, axis=-1, keepdims=True) + EPS)
    o_ref[...] = (y * w_ref[...]).astype(o_ref.dtype)


def rmsnorm_rows(x, w, out_dtype, tm):
    m, d = x.shape
    return pl.pallas_call(
        _rms_kernel,
        grid=(m // tm,),
        in_specs=[pl.BlockSpec((tm, d), lambda i: (i, 0)), pl.BlockSpec((1, d), lambda i: (0, 0))],
        out_specs=pl.BlockSpec((tm, d), lambda i: (i, 0)),
        out_shape=jax.ShapeDtypeStruct((m, d), out_dtype),
        compiler_params=_cparams(("parallel",)),
        name="rmsnorm",
    )(x, w.reshape(1, d))


def _mm_kernel(*refs, nt, has_res):
    a_ref, b_ref, o_ref = refs[0], refs[1], refs[-1]
    acc = _dot(a_ref[...], b_ref[...], nt=nt)
    if has_res:
        acc = acc + refs[2][...]
    o_ref[...] = acc.astype(o_ref.dtype)


def matmul(a, b, *, nt, tm, tn, res=None, out_dtype=F32, name="matmul", a_layer=None, b_layer=None):
    m, k = a.shape[-2:]
    n = b.shape[-2] if nt else b.shape[-1]
    tm, tn = min(tm, m), min(tn, n)
    if nt:
        b_spec = _stacked_spec((tn, k), lambda i, j: (j, 0), b_layer)
    else:
        b_spec = _stacked_spec((k, tn), lambda i, j: (0, j), b_layer)
    in_specs = [_stacked_spec((tm, k), lambda i, j: (i, 0), a_layer), b_spec]
    args = [a, b]
    if res is not None:
        in_specs.append(pl.BlockSpec((tm, tn), lambda i, j: (i, j)))
        args.append(res)
    return pl.pallas_call(
        functools.partial(_mm_kernel, nt=nt, has_res=res is not None),
        grid=(m // tm, n // tn),
        in_specs=in_specs,
        out_specs=pl.BlockSpec((tm, tn), lambda i, j: (i, j)),
        out_shape=jax.ShapeDtypeStruct((m, n), out_dtype),
        compiler_params=_cparams(("parallel", "parallel")),
        name=name,
    )(*args)


def matmul_t(wt, layer, h, nb, t, *, tm, tn, name):
    _, r, k = wt.shape
    tn = min(tn, t)
    nj = t // tn
    return pl.pallas_call(
        functools.partial(_mm_kernel, nt=True, has_res=False),
        grid=(nb, nj, r // tm),
        in_specs=[pl.BlockSpec((None, tm, k), lambda b, j, i: (layer, i, 0)),
                  pl.BlockSpec((tn, k), lambda b, j, i: (b * nj + j, 0))],
        out_specs=pl.BlockSpec((None, tm, tn), lambda b, j, i: (b, i, j)),
        out_shape=jax.ShapeDtypeStruct((nb, r, t), F32),
        compiler_params=_cparams(("parallel", "parallel", "parallel")),
        name=name,
    )(wt, h)


def _kv_stack_kernel(w_ref, h_ref, *refs):
    outs = refs[-6:]
    hb = h_ref[...]
    for i in range(6):
        outs[i][...] = _dot(w_ref[i * GRP_ROWS:(i + 1) * GRP_ROWS, :], hb, nt=True)


def kv_proj_stacked(wt_kv, layer, h, nb, t, stacks, *, tn):
    depth, _, k = wt_kv.shape
    tn = min(tn, t)
    nj = t // tn
    in_specs = [pl.BlockSpec((None, KV_ROWS, k), lambda b, j: (layer, 0, 0)),
                pl.BlockSpec((tn, k), lambda b, j: (b * nj + j, 0))]
    args = [wt_kv, h]
    aliases = {}
    if stacks is not None:
        in_specs += [pl.BlockSpec(memory_space=pl.ANY)] * 6
        args += list(stacks)
        aliases = {2 + i: i for i in range(6)}
    o_spec = pl.BlockSpec((None, None, GRP_ROWS, tn), lambda b, j: (layer, b, 0, j))
    o_shape = jax.ShapeDtypeStruct((depth, nb, GRP_ROWS, t), F32)
    return pl.pallas_call(
        _kv_stack_kernel,
        grid=(nb, nj),
        in_specs=in_specs,
        out_specs=[o_spec] * 6,
        out_shape=[o_shape] * 6,
        input_output_aliases=aliases,
        compiler_params=_cparams(("parallel", "parallel")),
        name="kv_proj_t",
    )(*args)


def _out_norm_kernel(a_ref, w_ref, res_ref, nw_ref, x_ref, h_ref):
    x = res_ref[...] + _dot(a_ref[...], w_ref[...])
    x_ref[...] = x
    y = x * lax.rsqrt(jnp.mean(x * x, axis=-1, keepdims=True) + EPS)
    h_ref[...] = (y * nw_ref[...]).astype(h_ref.dtype)


def out_proj_norm(mix, wo, layer, res, norm_w, h_dtype, *, tm):
    m, k = mix.shape
    n = wo.shape[-1]
    tm = min(tm, m)
    row = lambda i: (i, 0)
    return pl.pallas_call(
        _out_norm_kernel,
        grid=(m // tm,),
        in_specs=[pl.BlockSpec((tm, k), row), pl.BlockSpec((None, k, n), lambda i: (layer, 0, 0)),
                  pl.BlockSpec((tm, n), row), pl.BlockSpec((1, n), lambda i: (0, 0))],
        out_specs=[pl.BlockSpec((tm, n), row), pl.BlockSpec((tm, n), row)],
        out_shape=[jax.ShapeDtypeStruct((m, n), F32), jax.ShapeDtypeStruct((m, n), h_dtype)],
        compiler_params=_cparams(("parallel",)),
        name="out_proj",
    )(mix, wo, res, norm_w.reshape(1, n))


def _merge_kernel(oa_ref, ob_ref, oc_ref, wa_ref, wb_ref, wc_ref, g0_ref, g1_ref, g2_ref, o_ref):
    acc = _sigmoid(g0_ref[...].astype(F32)) * _dot(oa_ref[...], wa_ref[...])
    acc = acc + _sigmoid(g1_ref[...].astype(F32)) * _dot(ob_ref[...], wb_ref[...])
    acc = acc + _sigmoid(g2_ref[...].astype(F32)) * _dot(oc_ref[...], wc_ref[...])
    o_ref[...] = acc.astype(o_ref.dtype)


def merge_branches(o_a, o_b, o_c, wa, wb, wc, layer, ug, *, tm, tn):
    m, k = o_a.shape
    n = wa.shape[-1]
    tm = min(tm, m)
    gb = _UG_OFF["m_g"] // tn
    nj = n // tn
    o_spec = pl.BlockSpec((tm, k), lambda i, j: (i, 0))
    w_spec = pl.BlockSpec((None, k, tn), lambda i, j: (layer, 0, j))
    g_specs = [pl.BlockSpec((tm, tn), functools.partial(lambda i, j, br: (i, gb + br * nj + j), br=br))
               for br in range(3)]
    return pl.pallas_call(
        _merge_kernel,
        grid=(m // tm, nj),
        in_specs=[o_spec, o_spec, o_spec, w_spec, w_spec, w_spec] + g_specs,
        out_specs=pl.BlockSpec((tm, tn), lambda i, j: (i, j)),
        out_shape=jax.ShapeDtypeStruct((m, n), BF16),
        compiler_params=_cparams(("parallel", "parallel")),
        name="merge",
    )(o_a, o_b, o_c, wa, wb, wc, ug, ug, ug)


def _cmp_kernel(seg_ref, w1_ref, pe_ref, w2_ref, o_ref, *, nseg):
    w1 = w1_ref[...]
    ab = _dot3(seg_ref[...], w1)
    pe = _dot3(pe_ref[...], w1)
    cst = pe[0:1, :HD] + pe[1:2, HD:]
    nxt = pltpu.roll(ab[:, HD:], shift=nseg - 1, axis=0)
    pre = ab[:, :HD] + nxt + cst
    o_ref[...] = _dot3(_silu(pre), w2_ref[...])


def compress_blocks(seg, w1cat, pe2, w2, nseg):
    _, r, kdim = seg.shape
    return pl.pallas_call(
        functools.partial(_cmp_kernel, nseg=nseg),
        grid=(2, r // nseg),
        in_specs=[pl.BlockSpec((None, nseg, kdim), lambda s, i: (s, i, 0)),
                  pl.BlockSpec((None, kdim, 2 * HD), lambda s, i: (s, 0, 0)),
                  pl.BlockSpec((None, 8, kdim), lambda s, i: (s, 0, 0)),
                  pl.BlockSpec((None, HD, HD), lambda s, i: (s, 0, 0))],
        out_specs=pl.BlockSpec((None, nseg, HD), lambda s, i: (s, i, 0)),
        out_shape=jax.ShapeDtypeStruct((2, r, HD), F32),
        compiler_params=_cparams(("parallel", "parallel")),
        name="compress",
    )(seg, w1cat, pe2, w2)


def _nsa_prompt_kernel(qt_ref, ag_ref, az_ref, kst_ref, vst_ref, kwt_ref, vwt_ref, kc_ref, vc_ref, o_ref,
                       ks_scr, vs_scr, kw_scr, vw_scr, bias_scr, *, t, tq, ck, nseg):
    g = pl.program_id(1)
    i = pl.program_id(2)
    s0 = i * tq
    ns = t // SLC_BLK
    nsr = bias_scr.shape[0]
    cols = NSA_REP * tq

    @pl.when(i == 0)
    def _():
        ks_scr[...] = jnp.transpose(kst_ref[...]).astype(BF16)
        kw_scr[...] = jnp.transpose(kwt_ref[...]).astype(BF16)
        ones = jnp.ones((ONES_ROWS, t), BF16)
        vs_scr[...] = jnp.concatenate([vst_ref[...].astype(BF16), ones], axis=0)
        vw_scr[...] = jnp.concatenate([vwt_ref[...].astype(BF16), ones], axis=0)

    qt = qt_ref[...]
    q2 = jnp.concatenate([qt[r * HD:(r + 1) * HD, :] for r in range(NSA_REP)], axis=1) * (HD ** -0.5 * LOG2E)
    q2b = q2.astype(BF16)
    tpos1 = s0 + lax.broadcasted_iota(jnp.int32, (1, tq), 1)
    tpos = jnp.concatenate([tpos1] * NSA_REP, axis=1)

    crow = lax.broadcasted_iota(jnp.int32, (nseg, 1), 0)
    m_c = (crow * CMP_STRIDE + (CMP_LEN - 1) <= tpos) & (crow < nseg - 1)
    s_c = jnp.where(m_c, _dot3(kc_ref[...], q2), NEG)
    p_c = jnp.where(m_c, jnp.exp2(s_c - jnp.max(s_c, axis=0, keepdims=True)), 0.0)
    l_c = jnp.sum(p_c, axis=0, keepdims=True)
    p_c = p_c / jnp.where(l_c > 0.0, l_c, 1.0)
    o_c = _dot(jnp.transpose(vc_ref[...]), p_c)
    psum = p_c[:, 0:tq]
    for r in range(1, NSA_REP):
        psum = psum + p_c[:, r * tq:(r + 1) * tq]

    srow = lax.broadcasted_iota(jnp.int32, (nsr, 1), 0)
    ccol = lax.broadcasted_iota(jnp.int32, (1, nseg), 1) * CMP_STRIDE
    cover = jnp.where((ccol + CMP_LEN > srow * SLC_BLK) & (ccol < (srow + 1) * SLC_BLK), 1.0, 0.0).astype(BF16)
    imp = _dot_exact_lhs(cover, psum)
    cur = tpos1 // SLC_BLK
    valid = (srow * SLC_BLK <= tpos1) & (srow < ns)
    forced = (srow == 0) | (srow == cur) | (srow == cur - 1)
    score = jnp.where(valid, jnp.where(forced, FORCE, imp), NEG)
    rank = jnp.zeros((nsr, tq), F32)
    for j in range(ns):
        sj = score[j:j + 1, :]
        rank = rank + jnp.where((sj > score) | ((sj == score) & (j < srow)), 1.0, 0.0)
    sel = (rank < float(min(SLC_TOPK, ns))) & valid
    bias_scr[...] = jnp.where(sel, 0.0, NEG)

    def attend(k_rows, v_cols, bias1, m, acc):
        s = _dot(k_rows, q2b)
        s = jnp.concatenate([s[:, r * tq:(r + 1) * tq] + bias1 for r in range(NSA_REP)], axis=1)
        m_new = jnp.maximum(m, jnp.max(s, axis=0, keepdims=True))
        p = jnp.exp2(s - m_new)
        return m_new, jnp.exp2(m - m_new) * acc + _dot(v_cols, p)

    def finish(acc):
        l = acc[HD:HD + 1, :]
        return acc[:HD, :] / jnp.where(l > 0.0, l, 1.0)

    blk_per_chunk = ck // SLC_BLK
    m0 = jnp.full((1, cols), NEG, F32)
    acc0 = jnp.zeros((HD + ONES_ROWS, cols), F32)

    def sel_chunk(j, carry):
        k0 = pl.multiple_of(j * ck, ck)
        npos = k0 + lax.broadcasted_iota(jnp.int32, (ck, 1), 0)
        rows = [jnp.broadcast_to(bias_scr[pl.ds(j * blk_per_chunk + k, 1), :], (SLC_BLK, tq))
                for k in range(blk_per_chunk)]
        bias1 = jnp.where(npos <= tpos1, jnp.concatenate(rows, axis=0), NEG)
        return attend(ks_scr[pl.ds(k0, ck), :], vs_scr[:, pl.ds(k0, ck)], bias1, *carry)

    n_chunks = (s0 + tq + ck - 1) // ck
    o_s = finish(lax.fori_loop(0, n_chunks, sel_chunk, (m0, acc0))[1])

    ww = min(WINDOW + tq, t)
    w0 = pl.multiple_of(jnp.clip(s0 - WINDOW, 0, t - ww), LANE)
    wpos = w0 + lax.broadcasted_iota(jnp.int32, (ww, 1), 0)
    bias_w = jnp.where((wpos <= tpos1) & (wpos > tpos1 - WINDOW), 0.0, NEG)
    o_w = finish(attend(kw_scr[pl.ds(w0, ww), :], vw_scr[:, pl.ds(w0, ww)], bias_w, m0, acc0)[1])

    outs = []
    for r in range(NSA_REP):
        sl = slice(r * tq, (r + 1) * tq)
        gt = [_sigmoid(ag_ref[pl.ds(3 * (NSA_REP * g + r) + j, 1), :]) for j in range(3)]
        outs.append(jnp.transpose(gt[0] * o_c[:, sl] + gt[1] * o_s[:, sl] + gt[2] * o_w[:, sl]))
    o = jnp.concatenate(outs, axis=1) * _silu(az_ref[...].astype(F32))
    o_ref[...] = o.astype(o_ref.dtype)


def nsa_prompt(ug, qgt, stacks, layer, kc, nb, t, *, tq, ck):
    tq = min(tq, t)
    nq = t // tq
    nseg = t // CMP_STRIDE
    ck = min(ck, t)
    gw = NSA_REP * HD
    nsr = -(-(t // SLC_BLK) // 8) * 8
    kv_spec = pl.BlockSpec((None, None, HD, t), lambda b, g, i: (layer, b, g, 0))
    return pl.pallas_call(
        functools.partial(_nsa_prompt_kernel, t=t, tq=tq, ck=ck, nseg=nseg),
        grid=(nb, NSA_KV, nq),
        in_specs=[pl.BlockSpec((None, gw, tq), lambda b, g, i: (b, g, i)),
                  pl.BlockSpec((None, LANE, tq), lambda b, g, i: (b, TQ_AG // LANE, i)),
                  pl.BlockSpec((tq, gw), lambda b, g, i: (b * nq + i, _UG_OFF["a_z"] // gw + g)),
                  kv_spec, kv_spec, kv_spec, kv_spec,
                  pl.BlockSpec((None, nseg, HD), lambda b, g, i: (0, b * NSA_KV + g, 0)),
                  pl.BlockSpec((None, nseg, HD), lambda b, g, i: (1, b * NSA_KV + g, 0))],
        out_specs=pl.BlockSpec((tq, gw), lambda b, g, i: (b * nq + i, g)),
        out_shape=jax.ShapeDtypeStruct((nb * t, NSA_HEADS * HD), BF16),
        scratch_shapes=[pltpu.VMEM((t, HD), BF16), pltpu.VMEM((HD + ONES_ROWS, t), BF16),
                        pltpu.VMEM((t, HD), BF16), pltpu.VMEM((HD + ONES_ROWS, t), BF16),
                        pltpu.VMEM((nsr, tq), F32)],
        compiler_params=_cparams(("parallel", "parallel", "arbitrary")),
        name="nsa_prompt",
    )(qgt, qgt, ug, stacks[2], stacks[3], stacks[4], stacks[5], kc, kc)


def _gla_kernel(q_ref, k_ref, v_ref, z_ref, sm_ref, s0_ref, wa2_ref, ba_ref, gn_ref, o_ref, sout_ref, s_scr,
                *, c, sub, t_valid):
    ci = pl.program_id(1)
    rows = c * sub

    @pl.when(ci == 0)
    def _():
        s_scr[...] = s0_ref[...]

    row = ci * rows + lax.broadcasted_iota(jnp.int32, (rows, 1), 0)
    gate_in = sm_ref[...][:, SM_BA:SM_BA + GLA_RANK]
    log_a = _log_sigmoid(_dot3(gate_in, wa2_ref[...]) + ba_ref[...]) / GLA_TAU
    log_a = jnp.where(row < t_valid, log_a, 0.0)
    causal = lax.broadcasted_iota(jnp.int32, (c, c), 0) >= lax.broadcasted_iota(jnp.int32, (c, c), 1)
    tri = jnp.where(causal, 1.0, 0.0).astype(BF16)
    cb_all = jnp.concatenate([_dot_exact_lhs(tri, log_a[j * c:(j + 1) * c, :]) for j in range(sub)], axis=0)
    for h in range(GLA_HEADS):
        ks = slice(h * GLA_HK, (h + 1) * GLA_HK)
        vs = slice(h * GLA_HV, (h + 1) * GLA_HV)
        state = s_scr[h]
        outs = []
        for j in range(sub):
            rs = slice(j * c, (j + 1) * c)
            cb = cb_all[rs, ks]
            ecb = jnp.exp(cb)
            qd = q_ref[rs, ks].astype(F32) * (GLA_HK ** -0.5) * ecb
            kh = k_ref[rs, ks].astype(F32)
            vh = v_ref[rs, vs]
            att = jnp.where(causal, _dot(qd, kh * jnp.exp(-cb), nt=True), 0.0)
            outs.append(_dot(att, vh) + _dot(qd, state))
            c_end = cb[c - 1:c, :]
            k_dec = kh * jnp.exp(c_end - cb)
            e_col = jnp.transpose(ecb[c - 8:c, :])[:, 7:8]
            state = e_col * state + _dot(k_dec, vh, tn=True)
        o = jnp.concatenate(outs, axis=0)
        y = o * lax.rsqrt(jnp.mean(o * o, axis=-1, keepdims=True) + EPS) * gn_ref[...]
        o_ref[:, vs] = (y * _silu(z_ref[:, vs].astype(F32))).astype(o_ref.dtype)
        s_scr[h] = state

    @pl.when(ci == pl.num_programs(1) - 1)
    def _():
        sout_ref[...] = s_scr[...]


def gla_mixer(ug, ux, s0, wa2, ba, gnorm, nb, t, *, c, sub, t_valid):
    rows = c * sub
    nc = t // rows

    def u_spec(name, width):
        return pl.BlockSpec((rows, width), lambda b, i: (b * nc + i, _UG_OFF[name] // width))

    s_spec = pl.BlockSpec((None, GLA_HEADS, GLA_HK, GLA_HV), lambda b, i: (b, 0, 0, 0))
    return pl.pallas_call(
        functools.partial(_gla_kernel, c=c, sub=sub, t_valid=t_valid),
        grid=(nb, nc),
        in_specs=[u_spec("b_q", GLA_DK), u_spec("b_k", GLA_DK), u_spec("b_v", GLA_DV), u_spec("b_z", GLA_DV),
                  pl.BlockSpec((rows, LANE), lambda b, i: (b * nc + i, OFF_SMALL // LANE)),
                  s_spec,
                  pl.BlockSpec((GLA_RANK, GLA_DK), lambda b, i: (0, 0)),
                  pl.BlockSpec((1, GLA_DK), lambda b, i: (0, 0)),
                  pl.BlockSpec((1, GLA_HV), lambda b, i: (0, 0))],
        out_specs=[pl.BlockSpec((rows, GLA_DV), lambda b, i: (b * nc + i, 0)), s_spec],
        out_shape=[jax.ShapeDtypeStruct((nb * t, GLA_DV), BF16),
                   jax.ShapeDtypeStruct((nb, GLA_HEADS, GLA_HK, GLA_HV), F32)],
        scratch_shapes=[pltpu.VMEM((GLA_HEADS, GLA_HK, GLA_HV), F32)],
        compiler_params=_cparams(("parallel", "arbitrary")),
        name="gla",
    )(ug, ug, ug, ug, ux, s0, wa2, ba.reshape(1, GLA_DK), gnorm.reshape(1, GLA_HV))


def _ssd_kernel(xbc_ref, z_ref, sm_ref, s0_ref, cbuf_ref, cw_ref, cbias_ref, dtb_ref, alog_ref, dsk_ref, mn_ref,
                o_ref, sout_ref, cout_ref, s_scr, f_scr, *, q, t_valid, n_chunks):
    ci = pl.program_id(1)
    npad = 8
    hist = M_CONV - 1

    @pl.when(ci == 0)
    def _():
        s_scr[...] = s0_ref[...].reshape(s_scr.shape)
        f_scr[npad - hist:npad, :] = cbuf_ref[...]

    f_scr[npad:npad + q, :] = xbc_ref[...]
    conv = cbias_ref[...]
    for i in range(M_CONV):
        conv = conv + f_scr[npad - hist + i:npad - hist + i + q, :] * cw_ref[i:i + 1, :]
    xbc = _silu(conv)

    last_valid = t_valid - (n_chunks - 1) * q

    @pl.when(ci == n_chunks - 1)
    def _():
        cout_ref[...] = f_scr[npad - hist + last_valid:npad + last_valid, :]

    f_scr[npad - hist:npad, :] = f_scr[npad - hist + q:npad + q, :]

    lane = lax.broadcasted_iota(jnp.int32, (1, LANE), 1)
    row = ci * q + lax.broadcasted_iota(jnp.int32, (q, 1), 0)
    head_lane = (lane >= SM_DT) & (lane < SM_DT + M_HEADS)
    dt = jnp.where(head_lane & (row < t_valid), _softplus(sm_ref[...] + dtb_ref[...]), 0.0)
    a = dt * (-jnp.exp(alog_ref[...]))
    tri = jnp.where(lax.broadcasted_iota(jnp.int32, (q, q), 0) >= lax.broadcasted_iota(jnp.int32, (q, q), 1), 1.0, 0.0)
    causal = tri > 0.5
    cum = _dot_exact_lhs(tri.astype(BF16), a)
    cum_t = jnp.transpose(cum)
    dt_t = jnp.transpose(dt)
    nbc = M_GROUPS * M_STATE
    rep = M_HEADS // M_GROUPS
    gw = rep * M_HD
    head_of_lane = lax.broadcasted_iota(jnp.int32, (1, gw), 1) // M_HD
    head_of_row = lax.broadcasted_iota(jnp.int32, (gw, 1), 0) // M_HD
    for g in range(M_GROUPS):
        sl = slice(g * gw, (g + 1) * gw)
        bg = xbc[:, M_DINNER + g * M_STATE:M_DINNER + (g + 1) * M_STATE]
        cg = xbc[:, M_DINNER + nbc + g * M_STATE:M_DINNER + nbc + (g + 1) * M_STATE]
        xg = xbc[:, sl]
        cbm = _dot(cg, bg, nt=True)
        s_prev = s_scr[g]
        y = jnp.zeros((q, gw), F32)
        e_lanes = jnp.zeros((q, gw), F32)
        w_lanes = jnp.zeros((q, gw), F32)
        skip = jnp.zeros((1, gw), F32)
        keep = jnp.zeros((gw, 1), F32)
        for r in range(rep):
            hl = SM_DT + g * rep + r
            mine = head_of_lane == r
            cum_c = cum[:, hl:hl + 1]
            c_end = cum[q - 1:q, hl:hl + 1]
            decay = jnp.exp(jnp.where(causal, cum_c - cum_t[hl:hl + 1, :], -jnp.inf))
            y = y + _dot(cbm * decay * dt_t[hl:hl + 1, :], jnp.where(mine, xg, 0.0))
            e_lanes = jnp.where(mine, jnp.exp(cum_c), e_lanes)
            w_lanes = jnp.where(mine, jnp.exp(c_end - cum_c) * dt[:, hl:hl + 1], w_lanes)
            skip = jnp.where(mine, dsk_ref[:, hl:hl + 1], skip)
            keep = jnp.where(head_of_row == r, jnp.exp(c_end), keep)
        y = y + e_lanes * _dot(cg, s_prev, nt=True) + skip * xg
        s_scr[g] = keep * s_prev + _dot(xg * w_lanes, bg, tn=True)
        yg = y * _silu(z_ref[:, sl].astype(F32))
        yn = yg * lax.rsqrt(jnp.mean(yg * yg, axis=-1, keepdims=True) + EPS) * mn_ref[:, sl]
        o_ref[:, sl] = yn.astype(o_ref.dtype)

    @pl.when(ci == n_chunks - 1)
    def _():
        sout_ref[...] = s_scr[...].reshape(M_HEADS, M_HD, M_STATE)


def _pad_heads(v):
    return jnp.zeros((1, LANE), F32).at[0, SM_DT:SM_DT + M_HEADS].set(v)


def ssd_mixer(ug, ux, s0, cbuf, conv_w, conv_b, dt_bias, a_log, d_skip, m_norm, nb, t, *, q, t_valid):
    nc = t // q
    n_chunks = -(-t_valid // q)
    assert n_chunks == nc
    hist = M_CONV - 1

    def u_spec(off, name, width):
        return pl.BlockSpec((q, width), lambda b, i: (b * nc + i, off[name] // width))

    def full(shape):
        return pl.BlockSpec(shape, lambda b, i: (0,) * len(shape))

    s_spec = pl.BlockSpec((None, M_HEADS, M_HD, M_STATE), lambda b, i: (b, 0, 0, 0))
    c_spec = pl.BlockSpec((None, hist, M_CONVDIM), lambda b, i: (b, 0, 0))
    return pl.pallas_call(
        functools.partial(_ssd_kernel, q=q, t_valid=t_valid, n_chunks=n_chunks),
        grid=(nb, nc),
        in_specs=[u_spec(_UX_OFF, "c_xbc", M_CONVDIM), u_spec(_UG_OFF, "c_z", M_DINNER),
                  pl.BlockSpec((q, LANE), lambda b, i: (b * nc + i, OFF_SMALL // LANE)),
                  s_spec, c_spec,
                  full((M_CONV, M_CONVDIM)), full((1, M_CONVDIM)), full((1, LANE)), full((1, LANE)),
                  full((1, LANE)), full((1, M_DINNER))],
        out_specs=[pl.BlockSpec((q, M_DINNER), lambda b, i: (b * nc + i, 0)), s_spec, c_spec],
        out_shape=[jax.ShapeDtypeStruct((nb * t, M_DINNER), BF16),
                   jax.ShapeDtypeStruct((nb, M_HEADS, M_HD, M_STATE), F32),
                   jax.ShapeDtypeStruct((nb, hist, M_CONVDIM), F32)],
        scratch_shapes=[pltpu.VMEM((M_GROUPS, M_HEADS // M_GROUPS * M_HD, M_STATE), F32),
                        pltpu.VMEM((q + 8, M_CONVDIM), F32)],
        compiler_params=_cparams(("parallel", "arbitrary")),
        name="ssd",
    )(ux, ug, ux, s0, cbuf, conv_w, conv_b.reshape(1, M_CONVDIM), _pad_heads(dt_bias), _pad_heads(a_log),
      _pad_heads(d_skip), m_norm.reshape(1, M_DINNER))


SEG_PER_PAGE = PAGE // CMP_STRIDE


def _page_cmp_kernel(pt_ref, *refs, npg):
    w1_ref, ok_ref, ov_ref, scr, wbd_scr = refs[2 * npg:]

    @pl.when((pl.program_id(0) == 0) & (pl.program_id(1) == 0))
    def _():
        wbd_scr[...] = jnp.zeros(wbd_scr.shape, wbd_scr.dtype)
        for s in range(2):
            for j in range(CMP_STRIDE):
                wj = w1_ref[s, j * HD:(j + 1) * HD, :].astype(BF16)
                for g in range(NSA_KV):
                    wbd_scr[s, j, g * HD:(g + 1) * HD, g * 2 * HD:(g + 1) * 2 * HD] = wj

    row = lax.broadcasted_iota(jnp.int32, (PAGE, PAGE), 0)
    pos = lax.broadcasted_iota(jnp.int32, (PAGE, PAGE), 1)
    regroup = jnp.where(pos == CMP_STRIDE * (row % SEG_PER_PAGE) + row // SEG_PER_PAGE, 1.0, 0.0).astype(BF16)
    for s, (pages, o_ref) in enumerate(((refs[:npg], ok_ref), (refs[npg:2 * npg], ov_ref))):
        for k in range(npg):
            by_j = _dot(regroup, pages[k][...].reshape(NSA_KV * HD, PAGE), nt=True)
            for j in range(CMP_STRIDE):
                scr[j, k * SEG_PER_PAGE:(k + 1) * SEG_PER_PAGE, :] = by_j[j * SEG_PER_PAGE:(j + 1) * SEG_PER_PAGE, :]
        acc = jnp.zeros((npg * SEG_PER_PAGE, 2 * HD * NSA_KV), F32)
        for j in range(CMP_STRIDE):
            acc = acc + _dot(scr[j], wbd_scr[s, j])
        o_ref[...] = acc


def page_compress(cache_k_t, cache_v_t, page_table, layer, w1cat, *, npg):
    nb, n_pages = page_table.shape
    steps = n_pages // npg

    def page_spec(k):
        return pl.BlockSpec((None, None, NSA_KV, HD, PAGE),
                            lambda b, i, pt: (layer, pt[b, i * npg + k], 0, 0, 0))

    w_spec = pl.BlockSpec((None, 2, CMP_STRIDE * HD, 2 * HD), lambda b, i, pt: (layer, 0, 0, 0))
    o_spec = pl.BlockSpec((None, npg * SEG_PER_PAGE, NSA_KV * 2 * HD), lambda b, i, pt: (b, i, 0))
    o_shape = jax.ShapeDtypeStruct((nb, n_pages * SEG_PER_PAGE, NSA_KV * 2 * HD), F32)
    return pl.pallas_call(
        functools.partial(_page_cmp_kernel, npg=npg),
        grid_spec=pltpu.PrefetchScalarGridSpec(
            num_scalar_prefetch=1,
            grid=(nb, steps),
            in_specs=[page_spec(k) for k in range(npg)] * 2 + [w_spec],
            out_specs=[o_spec, o_spec],
            scratch_shapes=[pltpu.VMEM((CMP_STRIDE, npg * SEG_PER_PAGE, NSA_KV * HD), F32),
                            pltpu.VMEM((2, CMP_STRIDE, NSA_KV * HD, NSA_KV * 2 * HD), BF16)]),
        out_shape=[o_shape, o_shape],
        compiler_params=_cparams(("arbitrary", "arbitrary")),
        name="page_compress",
    )(page_table, *([cache_k_t] * npg), *([cache_v_t] * npg), w1cat)


def _masked_softmax_rows(s, mask):
    s = jnp.where(mask, s, NEG)
    m = jnp.max(s, axis=-1, keepdims=True)
    p = jnp.where(mask, jnp.exp(s - m), 0.0)
    l = jnp.sum(p, axis=-1, keepdims=True)
    return p / jnp.where(l > 0.0, l, 1.0)


def _nsa_sample_cmp_kernel(abk_ref, abv_ref, kvs_ref, q_ref, w1_ref, pe_ref, w2_ref, cover_ref, oc_ref, sel_ref,
                           cst_scr, *, p_len, nc, sl):
    g = pl.program_id(1)
    ns = p_len // SLC_BLK + 1
    rowi = lax.broadcasted_iota(jnp.int32, (nc, 1), 0)

    @pl.when((pl.program_id(0) == 0) & (g == 0))
    def _():
        for idx in range(2):
            pe = _dot(pe_ref[idx], w1_ref[idx], HI)
            cst_scr[idx:idx + 1, :] = pe[0:1, :HD] + pe[1:2, HD:]

    def compressed(ab_ref, idx):
        ab = ab_ref[...]
        new = kvs_ref[pl.ds(idx * NSA_KV + g, 1), :]
        b_new = _dot(new, w1_ref[idx][0:HD, HD:])
        nxt = jnp.where(rowi == nc - 1, b_new, pltpu.roll(ab[:, HD:], shift=nc - 1, axis=0))
        return _dot(_silu(ab[:, :HD] + nxt + cst_scr[idx:idx + 1, :]), w2_ref[idx])

    kc = compressed(abk_ref, 0)
    vc = compressed(abv_ref, 1)
    q4 = q_ref[...] * (HD ** -0.5)
    cidx = lax.broadcasted_iota(jnp.int32, (1, nc), 1)
    m_c = cidx * CMP_STRIDE + (CMP_LEN - 1) <= p_len
    p_c = _masked_softmax_rows(_dot(q4, kc, HI, nt=True), m_c)
    oc_ref[...] = _dot(p_c, vc)
    imp = _dot_exact_rhs(jnp.sum(p_c, axis=0, keepdims=True), cover_ref[...])
    sidx = lax.broadcasted_iota(jnp.int32, (1, sl), 1)
    cur = p_len // SLC_BLK
    valid = (sidx * SLC_BLK <= p_len) & (sidx < ns)
    forced = (sidx == 0) | (sidx == cur) | (sidx == cur - 1)
    score = jnp.where(valid, jnp.where(forced, FORCE, imp), NEG)
    ri = lax.broadcasted_iota(jnp.int32, (sl, sl), 0)
    li = lax.broadcasted_iota(jnp.int32, (sl, sl), 1)
    score_b = jnp.broadcast_to(score, (sl, sl))
    score_col = jnp.sum(jnp.where(ri == li, score_b, 0.0), axis=1, keepdims=True)
    ahead = (score_b > score_col) | ((score_b == score_col) & (li < ri))
    rank_col = jnp.sum(jnp.where(ahead, 1.0, 0.0), axis=1, keepdims=True)
    kl = lax.broadcasted_iota(jnp.int32, (sl, LANE), 1).astype(F32)
    sv = lax.broadcasted_iota(jnp.int32, (sl, LANE), 0).astype(F32)
    picked = jnp.sum(jnp.where(rank_col == kl, sv, 0.0), axis=0, keepdims=True)
    sel_ref[...] = picked.astype(jnp.int32)


def _cover_matrix(nc, sl):
    c0 = np.arange(nc)[:, None] * CMP_STRIDE
    s0 = np.arange(sl)[None, :] * SLC_BLK
    return jnp.asarray(((c0 + CMP_LEN > s0) & (c0 < s0 + SLC_BLK)).astype(np.float32), dtype=BF16)


def nsa_sample_cmp(abk, abv, kvs, q3, w1cat, pe2, w2, p_len):
    nb, nc, _ = abk.shape
    ns = p_len // SLC_BLK + 1
    sl = -(-ns // LANE) * LANE
    cover = _cover_matrix(nc, sl)

    def ab_spec():
        return pl.BlockSpec((None, nc, 2 * HD), lambda b, g: (b, 0, g))

    def full(a):
        return pl.BlockSpec(a.shape, lambda b, g: (0,) * a.ndim)

    row_spec = pl.BlockSpec((None, None, NSA_REP, HD), lambda b, g: (b, g, 0, 0))
    return pl.pallas_call(
        functools.partial(_nsa_sample_cmp_kernel, p_len=p_len, nc=nc, sl=sl),
        grid=(nb, NSA_KV),
        in_specs=[ab_spec(), ab_spec(), pl.BlockSpec((None, 6 * NSA_KV, HD), lambda b, g: (b, 0, 0)), row_spec,
                  full(w1cat), full(pe2), full(w2), full(cover)],
        out_specs=[row_spec, pl.BlockSpec((None, None, 1, LANE), lambda b, g: (b, g, 0, 0))],
        out_shape=[jax.ShapeDtypeStruct((nb, NSA_KV, NSA_REP, HD), F32),
                   jax.ShapeDtypeStruct((nb, NSA_KV, 1, LANE), jnp.int32)],
        scratch_shapes=[pltpu.VMEM((8, HD), F32)],
        compiler_params=_cparams(("arbitrary", "arbitrary")),
        name="nsa_sample_cmp",
    )(abk, abv, kvs, q3, w1cat, pe2, w2, cover)


def _nsa_sample_sel_kernel(sel_ref, pt_ref, *refs, p_len, nb, n_buf):
    nk = SLC_TOPK
    kpages, vpages = refs[:nk], refs[nk:2 * nk]
    (kwb_ref, vwb_ref, kvt_ref, kvs_ref, q_ref, az_ref, sm_ref, oc_ref,
     o_ref, kwo_ref, vwo_ref) = refs[2 * nk:]
    b = pl.program_id(0)
    g = pl.program_id(1)
    n_past = p_len // SLC_BLK
    q4 = q_ref[...] * (HD ** -0.5)
    lane_b = lax.broadcasted_iota(jnp.int32, (1, nb), 1) == b

    def new_row(i):
        return kvs_ref[pl.ds(i * NSA_KV + g, 1), :]

    def new_col(i):
        blk = kvt_ref[pl.ds(pl.multiple_of((i * NSA_KV + g) * HD, HD), HD), :]
        return jnp.sum(jnp.where(lane_b, blk, 0.0), axis=1, keepdims=True)

    half_of_lane = lax.broadcasted_iota(jnp.int32, (1, PAGE), 1) // SLC_BLK
    s_parts, m_parts = [], []
    has_new = jnp.zeros((1, 1), jnp.int32)
    for k in range(nk):
        blk = sel_ref[b, g, k]
        m_parts.append((half_of_lane == blk % 2) & (blk < n_past))
        s_parts.append(_dot(q4, kpages[k][...]))
        has_new = has_new + jnp.where(blk == n_past, 1, 0)
    s_sel = jnp.concatenate(s_parts, axis=1)
    m_sel = jnp.concatenate(m_parts, axis=1)
    new_ok = has_new > 0
    s_new = jnp.where(new_ok, jnp.sum(q4 * new_row(2), axis=1, keepdims=True), NEG)
    s_sel = jnp.where(m_sel, s_sel, NEG)
    mx = jnp.maximum(jnp.max(s_sel, axis=1, keepdims=True), s_new)
    p_sel = jnp.where(m_sel, jnp.exp(s_sel - mx), 0.0)
    p_new = jnp.where(new_ok, jnp.exp(s_new - mx), 0.0)
    den = jnp.sum(p_sel, axis=1, keepdims=True) + p_new
    acc = p_new * new_row(3)
    for k in range(nk):
        acc = acc + _dot(p_sel[:, k * PAGE:(k + 1) * PAGE], vpages[k][...], nt=True)
    o_s = acc / jnp.where(den > 0.0, den, 1.0)

    kwb = kwb_ref[...]
    vwb = vwb_ref[...]
    wlane = lax.broadcasted_iota(jnp.int32, (1, n_buf), 1)
    m_w = wlane > n_buf - WINDOW
    s_w = jnp.where(m_w, _dot(q4, kwb), NEG)
    s_wn = jnp.sum(q4 * new_row(4), axis=1, keepdims=True)
    mw = jnp.maximum(jnp.max(s_w, axis=1, keepdims=True), s_wn)
    p_w = jnp.where(m_w, jnp.exp(s_w - mw), 0.0)
    p_wn = jnp.exp(s_wn - mw)
    o_w = (_dot(p_w, vwb, nt=True) + p_wn * new_row(5)) / (jnp.sum(p_w, axis=1, keepdims=True) + p_wn)
    kwo_ref[...] = jnp.where(wlane == n_buf - 1, new_col(4), pltpu.roll(kwb, shift=n_buf - 1, axis=1))
    vwo_ref[...] = jnp.where(wlane == n_buf - 1, new_col(5), pltpu.roll(vwb, shift=n_buf - 1, axis=1))

    gate_row = _sigmoid(pltpu.roll(sm_ref[pl.ds(b, 1), :], shift=(LANE - 3 * NSA_REP * g) % LANE, axis=1))
    gl = lax.broadcasted_iota(jnp.int32, (NSA_REP, LANE), 1)
    gr = lax.broadcasted_iota(jnp.int32, (NSA_REP, LANE), 0)

    def gate(j):
        return jnp.sum(jnp.where(gl == 3 * gr + j, gate_row, 0.0), axis=1, keepdims=True)

    o = gate(0) * oc_ref[...] + gate(1) * o_s + gate(2) * o_w
    o_ref[...] = o * _silu(az_ref[...])


def nsa_sample_sel(sel, page_table, slc_k_t, slc_v_t, win_k_t, win_v_t, layer, kvt, kvs, q3, az3, ux, o_cmp, p_len):
    nb = page_table.shape[0]
    n_buf = win_k_t.shape[-1]
    last_blk = p_len // SLC_BLK - 1

    def page_spec(k):
        def imap(b, g, sel_r, pt_r):
            blk = jnp.minimum(sel_r[b, g, k], last_blk)
            return (layer, pt_r[b, blk // (PAGE // SLC_BLK)], g, 0, 0)
        return pl.BlockSpec((None, None, None, HD, PAGE), imap)

    win_spec = pl.BlockSpec((None, None, None, HD, n_buf), lambda b, g, s, p: (layer, b, g, 0, 0))
    wout_spec = pl.BlockSpec((None, None, HD, n_buf), lambda b, g, s, p: (b, g, 0, 0))
    row_spec = pl.BlockSpec((None, None, NSA_REP, HD), lambda b, g, s, p: (b, g, 0, 0))

    def full(a):
        return pl.BlockSpec(a.shape, lambda b, g, s, p: (0,) * a.ndim)

    small = pl.BlockSpec((nb, LANE), lambda b, g, s, p: (0, OFF_SMALL // LANE))
    return pl.pallas_call(
        functools.partial(_nsa_sample_sel_kernel, p_len=p_len, nb=nb, n_buf=n_buf),
        grid_spec=pltpu.PrefetchScalarGridSpec(
            num_scalar_prefetch=2,
            grid=(nb, NSA_KV),
            in_specs=[page_spec(k) for k in range(SLC_TOPK)] * 2
                     + [win_spec, win_spec, full(kvt), pl.BlockSpec((None, 6 * NSA_KV, HD), lambda b, g, s, p: (b, 0, 0)),
                        row_spec, row_spec, small, row_spec],
            out_specs=[row_spec, wout_spec, wout_spec]),
        out_shape=[jax.ShapeDtypeStruct((nb, NSA_KV, NSA_REP, HD), F32),
                   jax.ShapeDtypeStruct((nb, NSA_KV, HD, n_buf), F32),
                   jax.ShapeDtypeStruct((nb, NSA_KV, HD, n_buf), F32)],
        compiler_params=_cparams(("parallel", "arbitrary")),
        name="nsa_sample_sel",
    )(sel, page_table, *([slc_k_t] * SLC_TOPK), *([slc_v_t] * SLC_TOPK), win_k_t, win_v_t, kvt, kvs, q3, az3, ux, o_cmp)


def _prep_weights(w_in, cmp_pe_k, cmp_w1_k, cmp_w2_k, cmp_pe_v, cmp_w1_v, cmp_w2_v, w_br_a, w_br_b, w_br_c, w_out):
    wt = jnp.transpose(w_in, (0, 2, 1))

    def gather_rows(order, total):
        parts = [wt[:, _IN_OFF[n]:_IN_OFF[n] + _IN_SZ[n]] for n in order]
        used = sum(_IN_SZ[n] for n in order)
        parts.append(jnp.zeros((wt.shape[0], total - used, wt.shape[2]), wt.dtype))
        return jnp.concatenate(parts, axis=1).astype(BF16)

    half = CMP_STRIDE * HD

    def cat(w1):
        return jnp.concatenate([w1[:, :half], w1[:, half:]], axis=2)

    def pe_rows(pe):
        d = pe.shape[0]
        rows = pe.reshape(d, 2, half)
        return jnp.concatenate([rows, jnp.zeros((d, 6, half), pe.dtype)], axis=1)

    w1cat = jnp.stack([cat(cmp_w1_k), cat(cmp_w1_v)], axis=1)
    pe2 = jnp.stack([pe_rows(cmp_pe_k), pe_rows(cmp_pe_v)], axis=1)
    w2 = jnp.stack([cmp_w2_k, cmp_w2_v], axis=1)
    return dict(wt_ug=gather_rows(_UG_ORDER, NP_UG), wt_ux=gather_rows(_UX_ORDER, NP_UX), wt_kv=gather_rows(("a_kv",), KV_ROWS),
                wt_q=gather_rows(_TQ_ORDER, TQ_ROWS),
                w1cat=w1cat, pe2=pe2, w2=w2,
                wa=w_br_a.astype(BF16), wb=w_br_b.astype(BF16), wc=w_br_c.astype(BF16), wo=w_out.astype(BF16))


def _prompt_layer(x, h, nb, t, layer, w, p, next_norm, tiles, stacks):
    ug = matmul(h, w["wt_ug"], nt=True, tm=tiles["mm_tm"], tn=tiles["in_tn"], out_dtype=BF16, name="in_proj",
                b_layer=layer)
    ux = matmul(h, w["wt_ux"], nt=True, tm=tiles["mm_tm"], tn=NP_UX, name="in_proj_x", b_layer=layer)
    stacks = kv_proj_stacked(w["wt_kv"], layer, h, nb, t, stacks, tn=tiles["kv_tn"])
    qgt = matmul_t(w["wt_q"], layer, h, nb, t, tm=TQ_TM, tn=tiles["q_tn"], name="q_proj_t")
    nseg = t // CMP_STRIDE
    seg = jnp.stack([stacks[0][layer], stacks[1][layer]]).reshape(2, nb, NSA_KV, HD, t)
    seg = jnp.transpose(seg, (0, 1, 2, 4, 3)).reshape(2, nb * NSA_KV * nseg, CMP_STRIDE * HD)
    kc = compress_blocks(seg, w["w1cat"][layer], w["pe2"][layer], w["w2"][layer], nseg)
    o_a = nsa_prompt(ug, qgt, stacks, layer, kc, nb, t, tq=tiles["tq"], ck=tiles["ck"])
    zeros_gla = jnp.zeros((nb, GLA_HEADS, GLA_HK, GLA_HV), F32)
    zeros_ssm = jnp.zeros((nb, M_HEADS, M_HD, M_STATE), F32)
    zeros_conv = jnp.zeros((nb, M_CONV - 1, M_CONVDIM), F32)
    o_b, gla_s = gla_mixer(ug, ux, zeros_gla, p["gla_wa2"], p["gla_ba"], p["gla_norm"], nb, t,
                           c=tiles["gla_c"], sub=min(tiles["gla_sub"], t // tiles["gla_c"]), t_valid=t)
    o_c, ssm_s, conv_s = ssd_mixer(ug, ux, zeros_ssm, zeros_conv, p["conv_w"], p["conv_b"], p["dt_bias"], p["a_log"],
                                   p["d_skip"], p["m_norm"], nb, t, q=min(tiles["ssd_q"], t), t_valid=t)
    mix = merge_branches(o_a, o_b, o_c, w["wa"], w["wb"], w["wc"], layer, ug, tm=tiles["mg_tm"], tn=tiles["mg_tn"])
    x_out, h_out = out_proj_norm(mix, w["wo"], layer, x, *next_norm, tm=tiles["out_tm"])
    return x_out, h_out, stacks, (gla_s, ssm_s, conv_s)


ROW_PAD = 8


def _sample_layer(x, h, layer, w, p, next_norm, caches, states, page_table, tiles):
    nb = x.shape[0]
    p_len = page_table.shape[1] * PAGE
    ck_t, cv_t, sk_t, sv_t, wk_t, wv_t = caches
    ug = matmul(h, w["wt_ug"], nt=True, tm=nb, tn=tiles["in_tn"], out_dtype=BF16, name="in_proj_s", b_layer=layer)
    ux = matmul(h, w["wt_ux"], nt=True, tm=nb, tn=NP_UX, name="in_proj_xs", b_layer=layer)
    kv = matmul(h, w["wt_kv"], nt=True, tm=nb, tn=512, name="kv_proj_s", b_layer=layer)
    kvt = matmul(w["wt_kv"], h, nt=True, tm=512, tn=nb, name="kv_proj_st", a_layer=layer)
    qg = matmul(h, w["wt_q"], nt=True, tm=nb, tn=TQ_TM, name="q_proj_s", b_layer=layer)
    kvs = kv.reshape(nb, 6 * NSA_KV, HD)
    def pad_rows(a):
        return jnp.pad(a[:, None, :], ((0, 0), (0, ROW_PAD - 1), (0, 0))).reshape(nb * ROW_PAD, a.shape[1])

    ug_pad, ux_pad = pad_rows(ug), pad_rows(ux)
    o_b, gla_s = gla_mixer(ug_pad, ux_pad, states[0], p["gla_wa2"], p["gla_ba"], p["gla_norm"], nb, ROW_PAD,
                           c=ROW_PAD, sub=1, t_valid=1)
    o_c, ssm_s, conv_s = ssd_mixer(ug_pad, ux_pad, states[1], states[2], p["conv_w"], p["conv_b"], p["dt_bias"], p["a_log"],
                                   p["d_skip"], p["m_norm"], nb, ROW_PAD, q=ROW_PAD, t_valid=1)
    o_b = o_b.reshape(nb, ROW_PAD, GLA_DV)[:, 0]
    o_c = o_c.reshape(nb, ROW_PAD, M_DINNER)[:, 0]
    abk, abv = page_compress(ck_t, cv_t, page_table, layer, w["w1cat"], npg=min(tiles["npg"], page_table.shape[1]))
    q3 = qg[:, :TQ_AG].reshape(nb, NSA_KV, NSA_REP, HD)
    az3 = ug[:, _UG_OFF["a_z"]:_UG_OFF["a_z"] + MIX_W].astype(F32).reshape(nb, NSA_KV, NSA_REP, HD)
    o_cmp, sel = nsa_sample_cmp(abk, abv, kvs, q3, w["w1cat"][layer], w["pe2"][layer], w["w2"][layer], p_len)
    o_a, win_k, win_v = nsa_sample_sel(sel.reshape(nb, NSA_KV, LANE), page_table, sk_t, sv_t, wk_t, wv_t, layer,
                                       kvt, kvs, q3, az3, ux, o_cmp, p_len)
    mix = merge_branches(o_a.reshape(nb, MIX_W).astype(BF16), o_b, o_c, w["wa"], w["wb"], w["wc"], layer, ug,
                         tm=nb, tn=tiles["mg_tn"])
    x_out, h_out = out_proj_norm(mix, w["wo"], layer, x, *next_norm, tm=nb)
    kv4 = kv.reshape(nb, 6, 1, NSA_KV, HD)
    return x_out, h_out, (kv4[:, 0], kv4[:, 1], kv4[:, 2], kv4[:, 3], win_k, win_v, gla_s, ssm_s, conv_s)


_TILES = dict(rms_tm=256, mm_tm=1024, in_tn=1024, kv_tn=512, q_tn=2048, tq=256, ck=512, gla_c=64, gla_sub=4,
              ssd_q=256, mg_tm=1024, mg_tn=512, out_tm=512, npg=32)
_PARAM_NAMES = ("ln_w", "gla_wa2", "gla_ba", "gla_norm", "conv_w", "conv_b", "dt_bias", "a_log", "d_skip", "m_norm")


def kernel(x_prompt, x_sample, cache_cmp_k, cache_cmp_v, cache_slc_k, cache_slc_v, cache_win_k, cache_win_v,
           state_gla, state_ssm, state_conv, page_table, ln_w, w_in, cmp_pe_k, cmp_w1_k, cmp_w2_k, cmp_pe_v,
           cmp_w1_v, cmp_w2_v, gla_wa2, gla_ba, gla_norm, conv_w, conv_b, dt_bias, a_log, d_skip, m_norm,
           w_br_a, w_br_b, w_br_c, w_out, final_norm):
    nbp, t, d = x_prompt.shape
    nbs = x_sample.shape[0]
    depth = w_in.shape[0]
    weights = _prep_weights(w_in, cmp_pe_k, cmp_w1_k, cmp_w2_k, cmp_pe_v, cmp_w1_v, cmp_w2_v,
                            w_br_a, w_br_b, w_br_c, w_out)
    params = dict(zip(_PARAM_NAMES, (ln_w, gla_wa2, gla_ba, gla_norm, conv_w, conv_b, dt_bias, a_log, d_skip, m_norm)))
    caches = tuple(jnp.transpose(c, (0, 1, 3, 4, 2))
                   for c in (cache_cmp_k, cache_cmp_v, cache_slc_k, cache_slc_v, cache_win_k, cache_win_v))
    xp = x_prompt.reshape(nbp * t, d)
    xs = x_sample.reshape(nbs, d)
    p_states, s_out, stacks = [], [], None
    hp = rmsnorm_rows(xp, ln_w[0], BF16, _TILES["rms_tm"])
    hs = rmsnorm_rows(xs, ln_w[0], BF16, nbs)
    for l in range(depth):
        p = {k: v[l] for k, v in params.items()}
        next_norm = (ln_w[l + 1], BF16) if l + 1 < depth else (final_norm, F32)
        xp, hp, stacks, st_p = _prompt_layer(xp, hp, nbp, t, l, weights, p, next_norm, _TILES, stacks)
        xs, hs, st_s = _sample_layer(xs, hs, l, weights, p, next_norm, caches,
                                     (state_gla[l], state_ssm[l], state_conv[l]), page_table, _TILES)
        p_states.append(st_p)
        s_out.append(st_s)
    y_prompt = hp.reshape(nbp, t, d)
    y_sample = hs.reshape(nbs, 1, d)

    def from_kv_layout(a):
        return jnp.transpose(a, (0, 1, 4, 2, 3))

    n_w = min(WINDOW, t)
    kv5 = [a.reshape(depth, nbp, NSA_KV, HD, t) for a in stacks]
    p_leaves = [from_kv_layout(a) for a in kv5[:4]] + [from_kv_layout(a[..., t - n_w:]) for a in kv5[4:]]
    p_leaves += [jnp.stack([st[i] for st in p_states]) for i in range(3)]
    s_leaves = [jnp.stack([o[i] for o in s_out]) for i in range(9)]
    s_leaves[4], s_leaves[5] = from_kv_layout(s_leaves[4]), from_kv_layout(s_leaves[5])
    return (y_prompt, y_sample, *p_leaves, *s_leaves)
```

```python
import functools
import math

import jax
import jax.numpy as jnp
import numpy as np
from jax import lax
from jax.experimental import pallas as pl
from jax.experimental.pallas import tpu as pltpu

F32 = jnp.float32
BF16 = jnp.bfloat16

D_MODEL = 2048
PAGE = 128
MIX_W = D_MODEL // 2
HD = 64
NSA_HEADS = MIX_W // HD
NSA_KV = 4
NSA_REP = NSA_HEADS // NSA_KV
CMP_LEN = 32
CMP_STRIDE = 16
SLC_BLK = 64
SLC_TOPK = 16
WINDOW = 512
GLA_HEADS = 4
GLA_DK = MIX_W // 2
GLA_DV = MIX_W
GLA_HK = GLA_DK // GLA_HEADS
GLA_HV = GLA_DV // GLA_HEADS
GLA_RANK = 16
GLA_TAU = 16.0
M_DINNER = MIX_W
M_HD = 64
M_HEADS = M_DINNER // M_HD
M_GROUPS = 4
M_STATE = 128
M_CONV = 4
M_CONVDIM = M_DINNER + 2 * M_GROUPS * M_STATE
EPS = 1e-6
NEG = -1e30
FORCE = 1e4
LOG2E = math.log2(math.e)
ONES_ROWS = 16

_IN_NAMES = ("a_q", "a_kv", "a_g", "a_z", "b_q", "b_k", "b_v", "b_a", "b_z", "c_xbc", "c_dt", "c_z", "m_g")
_IN_SIZES = (NSA_HEADS * HD, 6 * NSA_KV * HD, 3 * NSA_HEADS, MIX_W, GLA_DK, GLA_DK, GLA_DV, GLA_RANK, GLA_DV,
             M_CONVDIM, M_HEADS, M_DINNER, 3 * D_MODEL)
_IN_OFF = dict(zip(_IN_NAMES, np.cumsum((0,) + _IN_SIZES)[:-1].tolist()))
_IN_SZ = dict(zip(_IN_NAMES, _IN_SIZES))

LANE = 128
_UG_ORDER = ("m_g", "a_z", "b_v", "b_z", "c_z", "b_q", "b_k")
_UX_ORDER = ("c_xbc", "a_g", "b_a", "c_dt")


def _offsets(order):
    off, o = {}, 0
    for n in order:
        off[n] = o
        o += _IN_SZ[n]
    return off, o


_UG_OFF, NP_UG = _offsets(_UG_ORDER)
_UX_OFF, _ux_used = _offsets(_UX_ORDER)
NP_UX = -(-_ux_used // LANE) * LANE
OFF_SMALL = _UX_OFF["a_g"]
SM_AG, SM_BA, SM_DT = 0, 3 * NSA_HEADS, 3 * NSA_HEADS + GLA_RANK
KV_ROWS = _IN_SZ["a_kv"]
GRP_ROWS = NSA_KV * HD
_TQ_ORDER = ("a_q", "a_g")
TQ_AG = _IN_SZ["a_q"]
TQ_TM = 384
TQ_ROWS = -(-(TQ_AG + _IN_SZ["a_g"]) // TQ_TM) * TQ_TM
VMEM_LIMIT = 56 * 1024 * 1024


def _cparams(sem):
    return pltpu.CompilerParams(dimension_semantics=sem, vmem_limit_bytes=VMEM_LIMIT)


def _dot(a, b, prec=None, nt=False, tn=False):
    if prec is None:
        a, b = a.astype(BF16), b.astype(BF16)
    dn = (((0,) if tn else (1,), (1,) if nt else (0,)), ((), ()))
    return lax.dot_general(a, b, dn, precision=prec, preferred_element_type=F32)


def _split(a):
    hi = a.astype(BF16)
    return hi, (a - hi.astype(F32)).astype(BF16)


def _dot_exact_rhs(a, b_bf16):
    hi, lo = _split(a)
    return _dot(hi, b_bf16) + _dot(lo, b_bf16)


def _dot_exact_lhs(a_bf16, b):
    hi, lo = _split(b)
    return _dot(a_bf16, hi) + _dot(a_bf16, lo)


def _dot3(a, b, **kw):
    ah, al = _split(a)
    bh, bl = _split(b)
    return _dot(ah, bh, **kw) + _dot(ah, bl, **kw) + _dot(al, bh, **kw)


def _sigmoid(x):
    return 0.5 * jnp.tanh(0.5 * x) + 0.5


def _silu(x):
    return x * _sigmoid(x)


def _log_sigmoid(x):
    return jnp.minimum(x, 0.0) - jnp.log(1.0 + jnp.exp(-jnp.abs(x)))


def _softplus(x):
    return jnp.maximum(x, 0.0) + jnp.log(1.0 + jnp.exp(-jnp.abs(x)))


def _stacked_spec(block, imap, layer):
    if layer is None:
        return pl.BlockSpec(block, imap)
    return pl.BlockSpec((None,) + block, lambda *idx: (layer,) + imap(*idx))


def _rms_kernel(x_ref, w_ref, o_ref):
    x = x_ref[...]
    y = x * lax.rsqrt(jnp.mean(x * x, axis=-1, keepdims=True) + EPS)
    o_ref[...] = (y * w_ref[...]).astype(o_ref.dtype)


def rmsnorm_rows(x, w, out_dtype, tm):
    m, d = x.shape
    return pl.pallas_call(
        _rms_kernel,
        grid=(m // tm,),
        in_specs=[pl.BlockSpec((tm, d), lambda i: (i, 0)), pl.BlockSpec((1, d), lambda i: (0, 0))],
        out_specs=pl.BlockSpec((tm, d), lambda i: (i, 0)),
        out_shape=jax.ShapeDtypeStruct((m, d), out_dtype),
        compiler_params=_cparams(("parallel",)),
        name="rmsnorm",
    )(x, w.reshape(1, d))


def _mm_kernel(*refs, nt, has_res):
    a_ref, b_ref, o_ref = refs[0], refs[1], refs[-1]
    acc = _dot(a_ref[...], b_ref[...], nt=nt)
    if has_res:
        acc = acc + refs[2][...]
    o_ref[...] = acc.astype(o_ref.dtype)


def matmul(a, b, *, nt, tm, tn, res=None, out_dtype=F32, name="matmul", a_layer=None, b_layer=None):
    m, k = a.shape[-2:]
    n = b.shape[-2] if nt else b.shape[-1]
    tm, tn = min(tm, m), min(tn, n)
    if nt:
        b_spec = _stacked_spec((tn, k), lambda i, j: (j, 0), b_layer)
    else:
        b_spec = _stacked_spec((k, tn), lambda i, j: (0, j), b_layer)
    in_specs = [_stacked_spec((tm, k), lambda i, j: (i, 0), a_layer), b_spec]
    args = [a, b]
    if res is not None:
        in_specs.append(pl.BlockSpec((tm, tn), lambda i, j: (i, j)))
        args.append(res)
    return pl.pallas_call(
        functools.partial(_mm_kernel, nt=nt, has_res=res is not None),
        grid=(m // tm, n // tn),
        in_specs=in_specs,
        out_specs=pl.BlockSpec((tm, tn), lambda i, j: (i, j)),
        out_shape=jax.ShapeDtypeStruct((m, n), out_dtype),
        compiler_params=_cparams(("parallel", "parallel")),
        name=name,
    )(*args)


def matmul_t(wt, layer, h, nb, t, *, tm, tn, name):
    _, r, k = wt.shape
    tn = min(tn, t)
    nj = t // tn
    return pl.pallas_call(
        functools.partial(_mm_kernel, nt=True, has_res=False),
        grid=(nb, nj, r // tm),
        in_specs=[pl.BlockSpec((None, tm, k), lambda b, j, i: (layer, i, 0)),
                  pl.BlockSpec((tn, k), lambda b, j, i: (b * nj + j, 0))],
        out_specs=pl.BlockSpec((None, tm, tn), lambda b, j, i: (b, i, j)),
        out_shape=jax.ShapeDtypeStruct((nb, r, t), F32),
        compiler_params=_cparams(("parallel", "parallel", "parallel")),
        name=name,
    )(wt, h)


def _kv_stack_kernel(w_ref, h_ref, *refs):
    outs = refs[-6:]
    hb = h_ref[...]
    for i in range(6):
        outs[i][...] = _dot(w_ref[i * GRP_ROWS:(i + 1) * GRP_ROWS, :], hb, nt=True)


def kv_proj_stacked(wt_kv, layer, h, nb, t, stacks, *, tn):
    depth, _, k = wt_kv.shape
    tn = min(tn, t)
    nj = t // tn
    in_specs = [pl.BlockSpec((None, KV_ROWS, k), lambda b, j: (layer, 0, 0)),
                pl.BlockSpec((tn, k), lambda b, j: (b * nj + j, 0))] + [pl.BlockSpec(memory_space=pl.ANY)] * 6
    args = [wt_kv, h] + list(stacks)
    aliases = {2 + i: i for i in range(6)}
    o_spec = pl.BlockSpec((None, None, GRP_ROWS, tn), lambda b, j: (layer, b, 0, j))
    o_shape = jax.ShapeDtypeStruct((depth, nb, GRP_ROWS, t), F32)
    return pl.pallas_call(
        _kv_stack_kernel,
        grid=(nb, nj),
        in_specs=in_specs,
        out_specs=[o_spec] * 6,
        out_shape=[o_shape] * 6,
        input_output_aliases=aliases,
        compiler_params=_cparams(("parallel", "parallel")),
        name="kv_proj_t",
    )(*args)


def _out_norm_kernel(a_ref, w_ref, res_ref, nw_ref, x_ref, h_ref):
    x = res_ref[...] + _dot(a_ref[...], w_ref[...])
    x_ref[...] = x
    y = x * lax.rsqrt(jnp.mean(x * x, axis=-1, keepdims=True) + EPS)
    h_ref[...] = (y * nw_ref[...]).astype(h_ref.dtype)


def out_proj_norm(mix, wo, layer, res, norm_w, h_dtype, *, tm):
    m, k = mix.shape
    n = wo.shape[-1]
    tm = min(tm, m)
    row = lambda i: (i, 0)
    return pl.pallas_call(
        _out_norm_kernel,
        grid=(m // tm,),
        in_specs=[pl.BlockSpec((tm, k), row), pl.BlockSpec((None, k, n), lambda i: (layer, 0, 0)),
                  pl.BlockSpec((tm, n), row), pl.BlockSpec((1, n), lambda i: (0, 0))],
        out_specs=[pl.BlockSpec((tm, n), row), pl.BlockSpec((tm, n), row)],
        out_shape=[jax.ShapeDtypeStruct((m, n), F32), jax.ShapeDtypeStruct((m, n), h_dtype)],
        compiler_params=_cparams(("parallel",)),
        name="out_proj",
    )(mix, wo, res, norm_w.reshape(1, n))


def _merge_kernel(oa_ref, ob_ref, oc_ref, wa_ref, wb_ref, wc_ref, g0_ref, g1_ref, g2_ref, o_ref):
    acc = _sigmoid(g0_ref[...].astype(F32)) * _dot(oa_ref[...], wa_ref[...])
    acc = acc + _sigmoid(g1_ref[...].astype(F32)) * _dot(ob_ref[...], wb_ref[...])
    acc = acc + _sigmoid(g2_ref[...].astype(F32)) * _dot(oc_ref[...], wc_ref[...])
    o_ref[...] = acc.astype(o_ref.dtype)


def merge_branches(o_a, o_b, o_c, wa, wb, wc, layer, ug, *, tm, tn):
    m, k = o_a.shape
    n = wa.shape[-1]
    tm = min(tm, m)
    gb = _UG_OFF["m_g"] // tn
    nj = n // tn
    o_spec = pl.BlockSpec((tm, k), lambda i, j: (i, 0))
    w_spec = pl.BlockSpec((None, k, tn), lambda i, j: (layer, 0, j))
    g_specs = [pl.BlockSpec((tm, tn), functools.partial(lambda i, j, br: (i, gb + br * nj + j), br=br))
               for br in range(3)]
    return pl.pallas_call(
        _merge_kernel,
        grid=(m // tm, nj),
        in_specs=[o_spec, o_spec, o_spec, w_spec, w_spec, w_spec] + g_specs,
        out_specs=pl.BlockSpec((tm, tn), lambda i, j: (i, j)),
        out_shape=jax.ShapeDtypeStruct((m, n), BF16),
        compiler_params=_cparams(("parallel", "parallel")),
        name="merge",
    )(o_a, o_b, o_c, wa, wb, wc, ug, ug, ug)


def _cmp_kernel(seg_ref, w1_ref, pe_ref, w2_ref, o_ref, *, nseg):
    w1 = w1_ref[...]
    ab = _dot3(seg_ref[...], w1)
    pe = _dot3(pe_ref[...], w1)
    cst = pe[0:1, :HD] + pe[1:2, HD:]
    nxt = pltpu.roll(ab[:, HD:], shift=nseg - 1, axis=0)
    pre = ab[:, :HD] + nxt + cst
    o_ref[...] = _dot3(_silu(pre), w2_ref[...])


def compress_blocks(seg, w1cat, pe2, w2, nseg):
    _, r, kdim = seg.shape
    return pl.pallas_call(
        functools.partial(_cmp_kernel, nseg=nseg),
        grid=(2, r // nseg),
        in_specs=[pl.BlockSpec((None, nseg, kdim), lambda s, i: (s, i, 0)),
                  pl.BlockSpec((None, kdim, 2 * HD), lambda s, i: (s, 0, 0)),
                  pl.BlockSpec((None, 8, kdim), lambda s, i: (s, 0, 0)),
                  pl.BlockSpec((None, HD, HD), lambda s, i: (s, 0, 0))],
        out_specs=pl.BlockSpec((None, nseg, HD), lambda s, i: (s, i, 0)),
        out_shape=jax.ShapeDtypeStruct((2, r, HD), F32),
        compiler_params=_cparams(("parallel", "parallel")),
        name="compress",
    )(seg, w1cat, pe2, w2)


def _nsa_prompt_kernel(qt_ref, ag_ref, az_ref, kst_ref, vst_ref, kwt_ref, vwt_ref, kc_ref, vc_ref, o_ref,
                       ks_scr, vs_scr, kw_scr, vw_scr, bias_scr, *, t, tq, ck, nseg):
    g = pl.program_id(1)
    i = pl.program_id(2)
    s0 = i * tq
    ns = t // SLC_BLK
    nsr = bias_scr.shape[0]
    cols = NSA_REP * tq

    @pl.when(i == 0)
    def _():
        ks_scr[...] = jnp.transpose(kst_ref[...]).astype(BF16)
        kw_scr[...] = jnp.transpose(kwt_ref[...]).astype(BF16)
        ones = jnp.ones((ONES_ROWS, t), BF16)
        vs_scr[...] = jnp.concatenate([vst_ref[...].astype(BF16), ones], axis=0)
        vw_scr[...] = jnp.concatenate([vwt_ref[...].astype(BF16), ones], axis=0)

    qt = qt_ref[...]
    q2 = jnp.concatenate([qt[r * HD:(r + 1) * HD, :] for r in range(NSA_REP)], axis=1) * (HD ** -0.5 * LOG2E)
    q2b = q2.astype(BF16)
    tpos1 = s0 + lax.broadcasted_iota(jnp.int32, (1, tq), 1)
    tpos = jnp.concatenate([tpos1] * NSA_REP, axis=1)

    crow = lax.broadcasted_iota(jnp.int32, (nseg, 1), 0)
    m_c = (crow * CMP_STRIDE + (CMP_LEN - 1) <= tpos) & (crow < nseg - 1)
    s_c = jnp.where(m_c, _dot3(kc_ref[...], q2), NEG)
    p_c = jnp.where(m_c, jnp.exp2(s_c - jnp.max(s_c, axis=0, keepdims=True)), 0.0)
    l_c = jnp.sum(p_c, axis=0, keepdims=True)
    p_c = p_c / jnp.where(l_c > 0.0, l_c, 1.0)
    o_c = _dot(jnp.transpose(vc_ref[...]), p_c)
    psum = p_c[:, 0:tq]
    for r in range(1, NSA_REP):
        psum = psum + p_c[:, r * tq:(r + 1) * tq]

    srow = lax.broadcasted_iota(jnp.int32, (nsr, 1), 0)
    ccol = lax.broadcasted_iota(jnp.int32, (1, nseg), 1) * CMP_STRIDE
    cover = jnp.where((ccol + CMP_LEN > srow * SLC_BLK) & (ccol < (srow + 1) * SLC_BLK), 1.0, 0.0).astype(BF16)
    imp = _dot_exact_lhs(cover, psum)
    cur = tpos1 // SLC_BLK
    valid = (srow * SLC_BLK <= tpos1) & (srow < ns)
    forced = (srow == 0) | (srow == cur) | (srow == cur - 1)
    score = jnp.where(valid, jnp.where(forced, FORCE, imp), NEG)
    rank = jnp.zeros((nsr, tq), F32)
    for j in range(ns):
        sj = score[j:j + 1, :]
        rank = rank + jnp.where((sj > score) | ((sj == score) & (j < srow)), 1.0, 0.0)
    sel = (rank < float(min(SLC_TOPK, ns))) & valid
    bias_scr[...] = jnp.where(sel, 0.0, NEG)

    def attend(k_rows, v_cols, bias1, m, acc):
        s = _dot(k_rows, q2b)
        s = jnp.concatenate([s[:, r * tq:(r + 1) * tq] + bias1 for r in range(NSA_REP)], axis=1)
        m_new = jnp.maximum(m, jnp.max(s, axis=0, keepdims=True))
        p = jnp.exp2(s - m_new)
        return m_new, jnp.exp2(m - m_new) * acc + _dot(v_cols, p)

    def finish(acc):
        l = acc[HD:HD + 1, :]
        return acc[:HD, :] / jnp.where(l > 0.0, l, 1.0)

    blk_per_chunk = ck // SLC_BLK
    m0 = jnp.full((1, cols), NEG, F32)
    acc0 = jnp.zeros((HD + ONES_ROWS, cols), F32)

    def sel_chunk(j, carry):
        k0 = pl.multiple_of(j * ck, ck)
        npos = k0 + lax.broadcasted_iota(jnp.int32, (ck, 1), 0)
        rows = [jnp.broadcast_to(bias_scr[pl.ds(j * blk_per_chunk + k, 1), :], (SLC_BLK, tq))
                for k in range(blk_per_chunk)]
        bias1 = jnp.where(npos <= tpos1, jnp.concatenate(rows, axis=0), NEG)
        return attend(ks_scr[pl.ds(k0, ck), :], vs_scr[:, pl.ds(k0, ck)], bias1, *carry)

    n_chunks = (s0 + tq + ck - 1) // ck
    o_s = finish(lax.fori_loop(0, n_chunks, sel_chunk, (m0, acc0))[1])

    ww = min(WINDOW + tq, t)
    w0 = pl.multiple_of(jnp.clip(s0 - WINDOW, 0, t - ww), LANE)
    wpos = w0 + lax.broadcasted_iota(jnp.int32, (ww, 1), 0)
    bias_w = jnp.where((wpos <= tpos1) & (wpos > tpos1 - WINDOW), 0.0, NEG)
    o_w = finish(attend(kw_scr[pl.ds(w0, ww), :], vw_scr[:, pl.ds(w0, ww)], bias_w, m0, acc0)[1])

    outs = []
    for r in range(NSA_REP):
        sl = slice(r * tq, (r + 1) * tq)
        gt = [_sigmoid(ag_ref[pl.ds(3 * (NSA_REP * g + r) + j, 1), :]) for j in range(3)]
        outs.append(jnp.transpose(gt[0] * o_c[:, sl] + gt[1] * o_s[:, sl] + gt[2] * o_w[:, sl]))
    o = jnp.concatenate(outs, axis=1) * _silu(az_ref[...].astype(F32))
    o_ref[...] = o.astype(o_ref.dtype)


def nsa_prompt(ug, qgt, stacks, layer, kc, nb, t, *, tq, ck):
    tq = min(tq, t)
    nq = t // tq
    nseg = t // CMP_STRIDE
    ck = min(ck, t)
    gw = NSA_REP * HD
    nsr = -(-(t // SLC_BLK) // 8) * 8
    kv_spec = pl.BlockSpec((None, None, HD, t), lambda b, g, i: (layer, b, g, 0))
    return pl.pallas_call(
        functools.partial(_nsa_prompt_kernel, t=t, tq=tq, ck=ck, nseg=nseg),
        grid=(nb, NSA_KV, nq),
        in_specs=[pl.BlockSpec((None, gw, tq), lambda b, g, i: (b, g, i)),
                  pl.BlockSpec((None, LANE, tq), lambda b, g, i: (b, TQ_AG // LANE, i)),
                  pl.BlockSpec((tq, gw), lambda b, g, i: (b * nq + i, _UG_OFF["a_z"] // gw + g)),
                  kv_spec, kv_spec, kv_spec, kv_spec,
                  pl.BlockSpec((None, nseg, HD), lambda b, g, i: (0, b * NSA_KV + g, 0)),
                  pl.BlockSpec((None, nseg, HD), lambda b, g, i: (1, b * NSA_KV + g, 0))],
        out_specs=pl.BlockSpec((tq, gw), lambda b, g, i: (b * nq + i, g)),
        out_shape=jax.ShapeDtypeStruct((nb * t, NSA_HEADS * HD), BF16),
        scratch_shapes=[pltpu.VMEM((t, HD), BF16), pltpu.VMEM((HD + ONES_ROWS, t), BF16),
                        pltpu.VMEM((t, HD), BF16), pltpu.VMEM((HD + ONES_ROWS, t), BF16),
                        pltpu.VMEM((nsr, tq), F32)],
        compiler_params=_cparams(("parallel", "parallel", "arbitrary")),
        name="nsa_prompt",
    )(qgt, qgt, ug, stacks[2], stacks[3], stacks[4], stacks[5], kc, kc)


def _gla_kernel(q_ref, k_ref, v_ref, z_ref, sm_ref, s0_ref, wa2_ref, ba_ref, gn_ref, o_ref, sout_ref, s_scr,
                *, c, sub, t_valid):
    ci = pl.program_id(1)
    rows = c * sub

    @pl.when(ci == 0)
    def _():
        s_scr[...] = s0_ref[...]

    row = ci * rows + lax.broadcasted_iota(jnp.int32, (rows, 1), 0)
    gate_in = sm_ref[...][:, SM_BA:SM_BA + GLA_RANK]
    log_a = _log_sigmoid(_dot3(gate_in, wa2_ref[...]) + ba_ref[...]) / GLA_TAU
    log_a = jnp.where(row < t_valid, log_a, 0.0)
    causal = lax.broadcasted_iota(jnp.int32, (c, c), 0) >= lax.broadcasted_iota(jnp.int32, (c, c), 1)
    tri = jnp.where(causal, 1.0, 0.0).astype(BF16)
    cb_all = jnp.concatenate([_dot_exact_lhs(tri, log_a[j * c:(j + 1) * c, :]) for j in range(sub)], axis=0)
    for h in range(GLA_HEADS):
        ks = slice(h * GLA_HK, (h + 1) * GLA_HK)
        vs = slice(h * GLA_HV, (h + 1) * GLA_HV)
        state = s_scr[h]
        outs = []
        for j in range(sub):
            rs = slice(j * c, (j + 1) * c)
            cb = cb_all[rs, ks]
            ecb = jnp.exp(cb)
            qd = q_ref[rs, ks].astype(F32) * (GLA_HK ** -0.5) * ecb
            kh = k_ref[rs, ks].astype(F32)
            vh = v_ref[rs, vs]
            att = jnp.where(causal, _dot(qd, kh * jnp.exp(-cb), nt=True), 0.0)
            outs.append(_dot(att, vh) + _dot(qd, state))
            c_end = cb[c - 1:c, :]
            k_dec = kh * jnp.exp(c_end - cb)
            e_col = jnp.transpose(ecb[c - 8:c, :])[:, 7:8]
            state = e_col * state + _dot(k_dec, vh, tn=True)
        o = jnp.concatenate(outs, axis=0)
        y = o * lax.rsqrt(jnp.mean(o * o, axis=-1, keepdims=True) + EPS) * gn_ref[...]
        o_ref[:, vs] = (y * _silu(z_ref[:, vs].astype(F32))).astype(o_ref.dtype)
        s_scr[h] = state

    @pl.when(ci == pl.num_programs(1) - 1)
    def _():
        sout_ref[...] = s_scr[...]


def gla_mixer(ug, ux, s0, wa2, ba, gnorm, nb, t, *, c, sub, t_valid):
    rows = c * sub
    nc = t // rows

    def u_spec(name, width):
        return pl.BlockSpec((rows, width), lambda b, i: (b * nc + i, _UG_OFF[name] // width))

    s_spec = pl.BlockSpec((None, GLA_HEADS, GLA_HK, GLA_HV), lambda b, i: (b, 0, 0, 0))
    return pl.pallas_call(
        functools.partial(_gla_kernel, c=c, sub=sub, t_valid=t_valid),
        grid=(nb, nc),
        in_specs=[u_spec("b_q", GLA_DK), u_spec("b_k", GLA_DK), u_spec("b_v", GLA_DV), u_spec("b_z", GLA_DV),
                  pl.BlockSpec((rows, LANE), lambda b, i: (b * nc + i, OFF_SMALL // LANE)),
                  s_spec,
                  pl.BlockSpec((GLA_RANK, GLA_DK), lambda b, i: (0, 0)),
                  pl.BlockSpec((1, GLA_DK), lambda b, i: (0, 0)),
                  pl.BlockSpec((1, GLA_HV), lambda b, i: (0, 0))],
        out_specs=[pl.BlockSpec((rows, GLA_DV), lambda b, i: (b * nc + i, 0)), s_spec],
        out_shape=[jax.ShapeDtypeStruct((nb * t, GLA_DV), BF16),
                   jax.ShapeDtypeStruct((nb, GLA_HEADS, GLA_HK, GLA_HV), F32)],
        scratch_shapes=[pltpu.VMEM((GLA_HEADS, GLA_HK, GLA_HV), F32)],
        compiler_params=_cparams(("parallel", "arbitrary")),
        name="gla",
    )(ug, ug, ug, ug, ux, s0, wa2, ba.reshape(1, GLA_DK), gnorm.reshape(1, GLA_HV))


def _ssd_kernel(xbc_ref, z_ref, sm_ref, s0_ref, cbuf_ref, cw_ref, cbias_ref, dtb_ref, alog_ref, dsk_ref, mn_ref,
                o_ref, sout_ref, cout_ref, s_scr, f_scr, *, q, t_valid, n_chunks):
    ci = pl.program_id(1)
    npad = 8
    hist = M_CONV - 1

    @pl.when(ci == 0)
    def _():
        s_scr[...] = s0_ref[...].reshape(s_scr.shape)
        f_scr[npad - hist:npad, :] = cbuf_ref[...]

    f_scr[npad:npad + q, :] = xbc_ref[...]
    conv = cbias_ref[...]
    for i in range(M_CONV):
        conv = conv + f_scr[npad - hist + i:npad - hist + i + q, :] * cw_ref[i:i + 1, :]
    xbc = _silu(conv)

    last_valid = t_valid - (n_chunks - 1) * q

    @pl.when(ci == n_chunks - 1)
    def _():
        cout_ref[...] = f_scr[npad - hist + last_valid:npad + last_valid, :]

    f_scr[npad - hist:npad, :] = f_scr[npad - hist + q:npad + q, :]

    lane = lax.broadcasted_iota(jnp.int32, (1, LANE), 1)
    row = ci * q + lax.broadcasted_iota(jnp.int32, (q, 1), 0)
    head_lane = (lane >= SM_DT) & (lane < SM_DT + M_HEADS)
    dt = jnp.where(head_lane & (row < t_valid), _softplus(sm_ref[...] + dtb_ref[...]), 0.0)
    a = dt * (-jnp.exp(alog_ref[...]))
    tri = jnp.where(lax.broadcasted_iota(jnp.int32, (q, q), 0) >= lax.broadcasted_iota(jnp.int32, (q, q), 1), 1.0, 0.0)
    causal = tri > 0.5
    cum = _dot_exact_lhs(tri.astype(BF16), a)
    cum_t = jnp.transpose(cum)
    dt_t = jnp.transpose(dt)
    nbc = M_GROUPS * M_STATE
    rep = M_HEADS // M_GROUPS
    gw = rep * M_HD
    head_of_lane = lax.broadcasted_iota(jnp.int32, (1, gw), 1) // M_HD
    head_of_row = lax.broadcasted_iota(jnp.int32, (gw, 1), 0) // M_HD
    for g in range(M_GROUPS):
        sl = slice(g * gw, (g + 1) * gw)
        bg = xbc[:, M_DINNER + g * M_STATE:M_DINNER + (g + 1) * M_STATE]
        cg = xbc[:, M_DINNER + nbc + g * M_STATE:M_DINNER + nbc + (g + 1) * M_STATE]
        xg = xbc[:, sl]
        cbm = _dot(cg, bg, nt=True)
        s_prev = s_scr[g]
        y = jnp.zeros((q, gw), F32)
        e_lanes = jnp.zeros((q, gw), F32)
        w_lanes = jnp.zeros((q, gw), F32)
        skip = jnp.zeros((1, gw), F32)
        keep = jnp.zeros((gw, 1), F32)
        for r in range(rep):
            hl = SM_DT + g * rep + r
            mine = head_of_lane == r
            cum_c = cum[:, hl:hl + 1]
            c_end = cum[q - 1:q, hl:hl + 1]
            decay = jnp.exp(jnp.where(causal, cum_c - cum_t[hl:hl + 1, :], -jnp.inf))
            y = y + _dot(cbm * decay * dt_t[hl:hl + 1, :], jnp.where(mine, xg, 0.0))
            e_lanes = jnp.where(mine, jnp.exp(cum_c), e_lanes)
            w_lanes = jnp.where(mine, jnp.exp(c_end - cum_c) * dt[:, hl:hl + 1], w_lanes)
            skip = jnp.where(mine, dsk_ref[:, hl:hl + 1], skip)
            keep = jnp.where(head_of_row == r, jnp.exp(c_end), keep)
        y = y + e_lanes * _dot(cg, s_prev, nt=True) + skip * xg
        s_scr[g] = keep * s_prev + _dot(xg * w_lanes, bg, tn=True)
        yg = y * _silu(z_ref[:, sl].astype(F32))
        yn = yg * lax.rsqrt(jnp.mean(yg * yg, axis=-1, keepdims=True) + EPS) * mn_ref[:, sl]
        o_ref[:, sl] = yn.astype(o_ref.dtype)

    @pl.when(ci == n_chunks - 1)
    def _():
        sout_ref[...] = s_scr[...].reshape(M_HEADS, M_HD, M_STATE)


def _pad_heads(v):
    return jnp.zeros((1, LANE), F32).at[0, SM_DT:SM_DT + M_HEADS].set(v)


def ssd_mixer(ug, ux, s0, cbuf, conv_w, conv_b, dt_bias, a_log, d_skip, m_norm, nb, t, *, q, t_valid):
    nc = t // q
    n_chunks = -(-t_valid // q)
    assert n_chunks == nc
    hist = M_CONV - 1

    def u_spec(off, name, width):
        return pl.BlockSpec((q, width), lambda b, i: (b * nc + i, off[name] // width))

    def full(shape):
        return pl.BlockSpec(shape, lambda b, i: (0,) * len(shape))

    s_spec = pl.BlockSpec((None, M_HEADS, M_HD, M_STATE), lambda b, i: (b, 0, 0, 0))
    c_spec = pl.BlockSpec((None, hist, M_CONVDIM), lambda b, i: (b, 0, 0))
    return pl.pallas_call(
        functools.partial(_ssd_kernel, q=q, t_valid=t_valid, n_chunks=n_chunks),
        grid=(nb, nc),
        in_specs=[u_spec(_UX_OFF, "c_xbc", M_CONVDIM), u_spec(_UG_OFF, "c_z", M_DINNER),
                  pl.BlockSpec((q, LANE), lambda b, i: (b * nc + i, OFF_SMALL // LANE)),
                  s_spec, c_spec,
                  full((M_CONV, M_CONVDIM)), full((1, M_CONVDIM)), full((1, LANE)), full((1, LANE)),
                  full((1, LANE)), full((1, M_DINNER))],
        out_specs=[pl.BlockSpec((q, M_DINNER), lambda b, i: (b * nc + i, 0)), s_spec, c_spec],
        out_shape=[jax.ShapeDtypeStruct((nb * t, M_DINNER), BF16),
                   jax.ShapeDtypeStruct((nb, M_HEADS, M_HD, M_STATE), F32),
                   jax.ShapeDtypeStruct((nb, hist, M_CONVDIM), F32)],
        scratch_shapes=[pltpu.VMEM((M_GROUPS, M_HEADS // M_GROUPS * M_HD, M_STATE), F32),
                        pltpu.VMEM((q + 8, M_CONVDIM), F32)],
        compiler_params=_cparams(("parallel", "arbitrary")),
        name="ssd",
    )(ux, ug, ux, s0, cbuf, conv_w, conv_b.reshape(1, M_CONVDIM), _pad_heads(dt_bias), _pad_heads(a_log),
      _pad_heads(d_skip), m_norm.reshape(1, M_DINNER))


SEG_PER_PAGE = PAGE // CMP_STRIDE


def _page_cmp_kernel(pt_ref, *refs, npg):
    w1_ref, ok_ref, ov_ref, scr, wbd_scr = refs[2 * npg:]

    @pl.when((pl.program_id(0) == 0) & (pl.program_id(1) == 0))
    def _():
        wbd_scr[...] = jnp.zeros(wbd_scr.shape, wbd_scr.dtype)
        for s in range(2):
            for j in range(CMP_STRIDE):
                wj = w1_ref[s, j * HD:(j + 1) * HD, :].astype(BF16)
                for g in range(NSA_KV):
                    wbd_scr[s, j, g * HD:(g + 1) * HD, g * 2 * HD:(g + 1) * 2 * HD] = wj

    row = lax.broadcasted_iota(jnp.int32, (PAGE, PAGE), 0)
    pos = lax.broadcasted_iota(jnp.int32, (PAGE, PAGE), 1)
    regroup = jnp.where(pos == CMP_STRIDE * (row % SEG_PER_PAGE) + row // SEG_PER_PAGE, 1.0, 0.0).astype(BF16)
    for s, (pages, o_ref) in enumerate(((refs[:npg], ok_ref), (refs[npg:2 * npg], ov_ref))):
        for k in range(npg):
            by_j = _dot(regroup, pages[k][...].reshape(NSA_KV * HD, PAGE), nt=True)
            for j in range(CMP_STRIDE):
                scr[j, k * SEG_PER_PAGE:(k + 1) * SEG_PER_PAGE, :] = by_j[j * SEG_PER_PAGE:(j + 1) * SEG_PER_PAGE, :]
        acc = jnp.zeros((npg * SEG_PER_PAGE, 2 * HD * NSA_KV), F32)
        for j in range(CMP_STRIDE):
            acc = acc + _dot(scr[j], wbd_scr[s, j])
        o_ref[...] = acc


def page_compress(cache_k_t, cache_v_t, page_table, layer, w1cat, *, npg):
    nb, n_pages = page_table.shape
    steps = n_pages // npg

    def page_spec(k):
        return pl.BlockSpec((None, None, NSA_KV, HD, PAGE),
                            lambda b, i, pt: (layer, pt[b, i * npg + k], 0, 0, 0))

    w_spec = pl.BlockSpec((None, 2, CMP_STRIDE * HD, 2 * HD), lambda b, i, pt: (layer, 0, 0, 0))
    o_spec = pl.BlockSpec((None, npg * SEG_PER_PAGE, NSA_KV * 2 * HD), lambda b, i, pt: (b, i, 0))
    o_shape = jax.ShapeDtypeStruct((nb, n_pages * SEG_PER_PAGE, NSA_KV * 2 * HD), F32)
    return pl.pallas_call(
        functools.partial(_page_cmp_kernel, npg=npg),
        grid_spec=pltpu.PrefetchScalarGridSpec(
            num_scalar_prefetch=1,
            grid=(nb, steps),
            in_specs=[page_spec(k) for k in range(npg)] * 2 + [w_spec],
            out_specs=[o_spec, o_spec],
            scratch_shapes=[pltpu.VMEM((CMP_STRIDE, npg * SEG_PER_PAGE, NSA_KV * HD), F32),
                            pltpu.VMEM((2, CMP_STRIDE, NSA_KV * HD, NSA_KV * 2 * HD), BF16)]),
        out_shape=[o_shape, o_shape],
        compiler_params=_cparams(("arbitrary", "arbitrary")),
        name="page_compress",
    )(page_table, *([cache_k_t] * npg), *([cache_v_t] * npg), w1cat)


def _masked_softmax_rows(s, mask):
    s = jnp.where(mask, s, NEG)
    m = jnp.max(s, axis=-1, keepdims=True)
    p = jnp.where(mask, jnp.exp(s - m), 0.0)
    l = jnp.sum(p, axis=-1, keepdims=True)
    return p / jnp.where(l > 0.0, l, 1.0)


def _nsa_sample_cmp_kernel(abk_ref, abv_ref, kvs_ref, q_ref, w1_ref, pe_ref, w2_ref, cover_ref, oc_ref, sel_ref,
                           cst_scr, *, p_len, nc, sl):
    ns = p_len // SLC_BLK + 1
    rowi = lax.broadcasted_iota(jnp.int32, (nc, 1), 0)

    @pl.when(pl.program_id(0) == 0)
    def _():
        for idx in range(2):
            pe = _dot3(pe_ref[idx], w1_ref[idx])
            cst_scr[idx:idx + 1, :] = pe[0:1, :HD] + pe[1:2, HD:]

    cidx = lax.broadcasted_iota(jnp.int32, (1, nc), 1)
    m_c = cidx * CMP_STRIDE + (CMP_LEN - 1) <= p_len
    sidx = lax.broadcasted_iota(jnp.int32, (1, sl), 1)
    cur = p_len // SLC_BLK
    valid = (sidx * SLC_BLK <= p_len) & (sidx < ns)
    forced = (sidx == 0) | (sidx == cur) | (sidx == cur - 1)
    ri = lax.broadcasted_iota(jnp.int32, (sl, sl), 0)
    li = lax.broadcasted_iota(jnp.int32, (sl, sl), 1)
    kl = lax.broadcasted_iota(jnp.int32, (sl, LANE), 1).astype(F32)
    sv = lax.broadcasted_iota(jnp.int32, (sl, LANE), 0).astype(F32)

    for g in range(NSA_KV):
        def compressed(ab_ref, idx):
            ab = ab_ref[:, g * 2 * HD:(g + 1) * 2 * HD]
            new = kvs_ref[idx * NSA_KV + g:idx * NSA_KV + g + 1, :]
            b_new = _dot(new, w1_ref[idx][0:HD, HD:])
            nxt = jnp.where(rowi == nc - 1, b_new, pltpu.roll(ab[:, HD:], shift=nc - 1, axis=0))
            return _dot(_silu(ab[:, :HD] + nxt + cst_scr[idx:idx + 1, :]), w2_ref[idx])

        kc = compressed(abk_ref, 0)
        vc = compressed(abv_ref, 1)
        p_c = _masked_softmax_rows(_dot3(q_ref[g] * (HD ** -0.5), kc, nt=True), m_c)
        oc_ref[g] = _dot(p_c, vc)
        imp = _dot_exact_rhs(jnp.sum(p_c, axis=0, keepdims=True), cover_ref[...])
        score = jnp.where(valid, jnp.where(forced, FORCE, imp), NEG)
        score_b = jnp.broadcast_to(score, (sl, sl))
        score_col = jnp.sum(jnp.where(ri == li, score_b, 0.0), axis=1, keepdims=True)
        ahead = (score_b > score_col) | ((score_b == score_col) & (li < ri))
        rank_col = jnp.sum(jnp.where(ahead, 1.0, 0.0), axis=1, keepdims=True)
        picked = jnp.sum(jnp.where(rank_col == kl, sv, 0.0), axis=0, keepdims=True)
        sel_ref[g] = picked.astype(jnp.int32)


def _cover_matrix(nc, sl):
    c0 = np.arange(nc)[:, None] * CMP_STRIDE
    s0 = np.arange(sl)[None, :] * SLC_BLK
    return jnp.asarray(((c0 + CMP_LEN > s0) & (c0 < s0 + SLC_BLK)).astype(np.float32), dtype=BF16)


def nsa_sample_cmp(abk, abv, kvs, q3, w1cat, pe2, w2, p_len):
    nb, nc, _ = abk.shape
    ns = p_len // SLC_BLK + 1
    sl = -(-ns // LANE) * LANE
    cover = _cover_matrix(nc, sl)

    ab_spec = pl.BlockSpec((None, nc, NSA_KV * 2 * HD), lambda b: (b, 0, 0))

    def full(a):
        return pl.BlockSpec(a.shape, lambda b: (0,) * a.ndim)

    row_spec = pl.BlockSpec((None, NSA_KV, NSA_REP, HD), lambda b: (b, 0, 0, 0))
    return pl.pallas_call(
        functools.partial(_nsa_sample_cmp_kernel, p_len=p_len, nc=nc, sl=sl),
        grid=(nb,),
        in_specs=[ab_spec, ab_spec, pl.BlockSpec((None, 6 * NSA_KV, HD), lambda b: (b, 0, 0)), row_spec,
                  full(w1cat), full(pe2), full(w2), full(cover)],
        out_specs=[row_spec, pl.BlockSpec((None, NSA_KV, 1, LANE), lambda b: (b, 0, 0, 0))],
        out_shape=[jax.ShapeDtypeStruct((nb, NSA_KV, NSA_REP, HD), F32),
                   jax.ShapeDtypeStruct((nb, NSA_KV, 1, LANE), jnp.int32)],
        scratch_shapes=[pltpu.VMEM((8, HD), F32)],
        compiler_params=_cparams(("arbitrary",)),
        name="nsa_sample_cmp",
    )(abk, abv, kvs, q3, w1cat, pe2, w2, cover)


def _nsa_sample_sel_kernel(sel_ref, pt_ref, *refs, p_len, nb, n_buf):
    nk = SLC_TOPK
    kpages, vpages = refs[:nk], refs[nk:2 * nk]
    (kwb_ref, vwb_ref, kvt_ref, kvs_ref, q_ref, az_ref, sm_ref, oc_ref,
     o_ref, kwo_ref, vwo_ref) = refs[2 * nk:]
    b = pl.program_id(0)
    g = pl.program_id(1)
    n_past = p_len // SLC_BLK
    q4 = q_ref[...] * (HD ** -0.5)
    lane_b = lax.broadcasted_iota(jnp.int32, (1, nb), 1) == b

    def new_row(i):
        return kvs_ref[pl.ds(i * NSA_KV + g, 1), :]

    def new_col(i):
        blk = kvt_ref[pl.ds(pl.multiple_of((i * NSA_KV + g) * HD, HD), HD), :]
        return jnp.sum(jnp.where(lane_b, blk, 0.0), axis=1, keepdims=True)

    half_of_lane = lax.broadcasted_iota(jnp.int32, (1, PAGE), 1) // SLC_BLK
    s_parts, m_parts = [], []
    has_new = jnp.zeros((1, 1), jnp.int32)
    for k in range(nk):
        blk = sel_ref[b, g, k]
        m_parts.append((half_of_lane == blk % 2) & (blk < n_past))
        s_parts.append(_dot(q4, kpages[k][...]))
        has_new = has_new + jnp.where(blk == n_past, 1, 0)
    s_sel = jnp.concatenate(s_parts, axis=1)
    m_sel = jnp.concatenate(m_parts, axis=1)
    new_ok = has_new > 0
    s_new = jnp.where(new_ok, jnp.sum(q4 * new_row(2), axis=1, keepdims=True), NEG)
    s_sel = jnp.where(m_sel, s_sel, NEG)
    mx = jnp.maximum(jnp.max(s_sel, axis=1, keepdims=True), s_new)
    p_sel = jnp.where(m_sel, jnp.exp(s_sel - mx), 0.0)
    p_new = jnp.where(new_ok, jnp.exp(s_new - mx), 0.0)
    den = jnp.sum(p_sel, axis=1, keepdims=True) + p_new
    acc = p_new * new_row(3)
    for k in range(nk):
        acc = acc + _dot(p_sel[:, k * PAGE:(k + 1) * PAGE], vpages[k][...], nt=True)
    o_s = acc / jnp.where(den > 0.0, den, 1.0)

    kwb = kwb_ref[...]
    vwb = vwb_ref[...]
    wlane = lax.broadcasted_iota(jnp.int32, (1, n_buf), 1)
    m_w = wlane > n_buf - WINDOW
    s_w = jnp.where(m_w, _dot(q4, kwb), NEG)
    s_wn = jnp.sum(q4 * new_row(4), axis=1, keepdims=True)
    mw = jnp.maximum(jnp.max(s_w, axis=1, keepdims=True), s_wn)
    p_w = jnp.where(m_w, jnp.exp(s_w - mw), 0.0)
    p_wn = jnp.exp(s_wn - mw)
    o_w = (_dot(p_w, vwb, nt=True) + p_wn * new_row(5)) / (jnp.sum(p_w, axis=1, keepdims=True) + p_wn)
    kwo_ref[...] = jnp.where(wlane == n_buf - 1, new_col(4), pltpu.roll(kwb, shift=n_buf - 1, axis=1))
    vwo_ref[...] = jnp.where(wlane == n_buf - 1, new_col(5), pltpu.roll(vwb, shift=n_buf - 1, axis=1))

    gate_row = _sigmoid(pltpu.roll(sm_ref[pl.ds(b, 1), :], shift=(LANE - 3 * NSA_REP * g) % LANE, axis=1))
    gl = lax.broadcasted_iota(jnp.int32, (NSA_REP, LANE), 1)
    gr = lax.broadcasted_iota(jnp.int32, (NSA_REP, LANE), 0)

    def gate(j):
        return jnp.sum(jnp.where(gl == 3 * gr + j, gate_row, 0.0), axis=1, keepdims=True)

    o = gate(0) * oc_ref[...] + gate(1) * o_s + gate(2) * o_w
    o_ref[...] = o * _silu(az_ref[...])


def nsa_sample_sel(sel, page_table, slc_k_t, slc_v_t, win_k_t, win_v_t, layer, kvt, kvs, q3, az3, ux, o_cmp, p_len):
    nb = page_table.shape[0]
    n_buf = win_k_t.shape[-1]
    last_blk = p_len // SLC_BLK - 1

    def page_spec(k):
        def imap(b, g, sel_r, pt_r):
            blk = jnp.minimum(sel_r[b, g, k], last_blk)
            return (layer, pt_r[b, blk // (PAGE // SLC_BLK)], g, 0, 0)
        return pl.BlockSpec((None, None, None, HD, PAGE), imap)

    win_spec = pl.BlockSpec((None, None, None, HD, n_buf), lambda b, g, s, p: (layer, b, g, 0, 0))
    wout_spec = pl.BlockSpec((None, None, HD, n_buf), lambda b, g, s, p: (b, g, 0, 0))
    row_spec = pl.BlockSpec((None, None, NSA_REP, HD), lambda b, g, s, p: (b, g, 0, 0))

    def full(a):
        return pl.BlockSpec(a.shape, lambda b, g, s, p: (0,) * a.ndim)

    small = pl.BlockSpec((nb, LANE), lambda b, g, s, p: (0, OFF_SMALL // LANE))
    return pl.pallas_call(
        functools.partial(_nsa_sample_sel_kernel, p_len=p_len, nb=nb, n_buf=n_buf),
        grid_spec=pltpu.PrefetchScalarGridSpec(
            num_scalar_prefetch=2,
            grid=(nb, NSA_KV),
            in_specs=[page_spec(k) for k in range(SLC_TOPK)] * 2
                     + [win_spec, win_spec, full(kvt), pl.BlockSpec((None, 6 * NSA_KV, HD), lambda b, g, s, p: (b, 0, 0)),
                        row_spec, row_spec, small, row_spec],
            out_specs=[row_spec, wout_spec, wout_spec]),
        out_shape=[jax.ShapeDtypeStruct((nb, NSA_KV, NSA_REP, HD), F32),
                   jax.ShapeDtypeStruct((nb, NSA_KV, HD, n_buf), F32),
                   jax.ShapeDtypeStruct((nb, NSA_KV, HD, n_buf), F32)],
        compiler_params=_cparams(("parallel", "arbitrary")),
        name="nsa_sample_sel",
    )(sel, page_table, *([slc_k_t] * SLC_TOPK), *([slc_v_t] * SLC_TOPK), win_k_t, win_v_t, kvt, kvs, q3, az3, ux, o_cmp)


def _prep_weights(w_in, cmp_pe_k, cmp_w1_k, cmp_w2_k, cmp_pe_v, cmp_w1_v, cmp_w2_v, w_br_a, w_br_b, w_br_c, w_out):
    wt = jnp.transpose(w_in, (0, 2, 1))

    def gather_rows(order, total):
        parts = [wt[:, _IN_OFF[n]:_IN_OFF[n] + _IN_SZ[n]] for n in order]
        used = sum(_IN_SZ[n] for n in order)
        parts.append(jnp.zeros((wt.shape[0], total - used, wt.shape[2]), wt.dtype))
        return jnp.concatenate(parts, axis=1).astype(BF16)

    half = CMP_STRIDE * HD

    def cat(w1):
        return jnp.concatenate([w1[:, :half], w1[:, half:]], axis=2)

    def pe_rows(pe):
        d = pe.shape[0]
        rows = pe.reshape(d, 2, half)
        return jnp.concatenate([rows, jnp.zeros((d, 6, half), pe.dtype)], axis=1)

    w1cat = jnp.stack([cat(cmp_w1_k), cat(cmp_w1_v)], axis=1)
    pe2 = jnp.stack([pe_rows(cmp_pe_k), pe_rows(cmp_pe_v)], axis=1)
    w2 = jnp.stack([cmp_w2_k, cmp_w2_v], axis=1)
    return dict(wt_ug=gather_rows(_UG_ORDER, NP_UG), wt_ux=gather_rows(_UX_ORDER, NP_UX), wt_kv=gather_rows(("a_kv",), KV_ROWS),
                wt_q=gather_rows(_TQ_ORDER, TQ_ROWS),
                w1cat=w1cat, pe2=pe2, w2=w2,
                wa=w_br_a.astype(BF16), wb=w_br_b.astype(BF16), wc=w_br_c.astype(BF16), wo=w_out.astype(BF16))


def _prompt_layer(x, h, nb, t, layer, w, p, next_norm, tiles, stacks):
    ug = matmul(h, w["wt_ug"], nt=True, tm=tiles["mm_tm"], tn=tiles["in_tn"], out_dtype=BF16, name="in_proj",
                b_layer=layer)
    ux = matmul(h, w["wt_ux"], nt=True, tm=tiles["mm_tm"], tn=NP_UX, name="in_proj_x", b_layer=layer)
    stacks = kv_proj_stacked(w["wt_kv"], layer, h, nb, t, stacks, tn=tiles["kv_tn"])
    qgt = matmul_t(w["wt_q"], layer, h, nb, t, tm=TQ_TM, tn=tiles["q_tn"], name="q_proj_t")
    nseg = t // CMP_STRIDE
    seg = jnp.stack([stacks[0][layer], stacks[1][layer]]).reshape(2, nb, NSA_KV, HD, t)
    seg = jnp.transpose(seg, (0, 1, 2, 4, 3)).reshape(2, nb * NSA_KV * nseg, CMP_STRIDE * HD)
    kc = compress_blocks(seg, w["w1cat"][layer], w["pe2"][layer], w["w2"][layer], nseg)
    o_a = nsa_prompt(ug, qgt, stacks, layer, kc, nb, t, tq=tiles["tq"], ck=tiles["ck"])
    zeros_gla = jnp.zeros((nb, GLA_HEADS, GLA_HK, GLA_HV), F32)
    zeros_ssm = jnp.zeros((nb, M_HEADS, M_HD, M_STATE), F32)
    zeros_conv = jnp.zeros((nb, M_CONV - 1, M_CONVDIM), F32)
    o_b, gla_s = gla_mixer(ug, ux, zeros_gla, p["gla_wa2"], p["gla_ba"], p["gla_norm"], nb, t,
                           c=tiles["gla_c"], sub=min(tiles["gla_sub"], t // tiles["gla_c"]), t_valid=t)
    o_c, ssm_s, conv_s = ssd_mixer(ug, ux, zeros_ssm, zeros_conv, p["conv_w"], p["conv_b"], p["dt_bias"], p["a_log"],
                                   p["d_skip"], p["m_norm"], nb, t, q=min(tiles["ssd_q"], t), t_valid=t)
    mix = merge_branches(o_a, o_b, o_c, w["wa"], w["wb"], w["wc"], layer, ug, tm=tiles["mg_tm"], tn=tiles["mg_tn"])
    x_out, h_out = out_proj_norm(mix, w["wo"], layer, x, *next_norm, tm=tiles["out_tm"])
    return x_out, h_out, stacks, (gla_s, ssm_s, conv_s)


ROW_PAD = 8


def _sample_layer(x, h, layer, w, p, next_norm, caches, states, page_table, tiles):
    nb = x.shape[0]
    p_len = page_table.shape[1] * PAGE
    ck_t, cv_t, sk_t, sv_t, wk_t, wv_t = caches
    ug = matmul(h, w["wt_ug"], nt=True, tm=nb, tn=tiles["in_tn"], out_dtype=BF16, name="in_proj_s", b_layer=layer)
    ux = matmul(h, w["wt_ux"], nt=True, tm=nb, tn=NP_UX, name="in_proj_xs", b_layer=layer)
    kv = matmul(h, w["wt_kv"], nt=True, tm=nb, tn=512, name="kv_proj_s", b_layer=layer)
    kvt = matmul(w["wt_kv"], h, nt=True, tm=512, tn=nb, name="kv_proj_st", a_layer=layer)
    qg = matmul(h, w["wt_q"], nt=True, tm=nb, tn=TQ_TM, name="q_proj_s", b_layer=layer)
    kvs = kv.reshape(nb, 6 * NSA_KV, HD)
    def pad_rows(a):
        return jnp.pad(a[:, None, :], ((0, 0), (0, ROW_PAD - 1), (0, 0))).reshape(nb * ROW_PAD, a.shape[1])

    ug_pad, ux_pad = pad_rows(ug), pad_rows(ux)
    o_b, gla_s = gla_mixer(ug_pad, ux_pad, states[0], p["gla_wa2"], p["gla_ba"], p["gla_norm"], nb, ROW_PAD,
                           c=ROW_PAD, sub=1, t_valid=1)
    o_c, ssm_s, conv_s = ssd_mixer(ug_pad, ux_pad, states[1], states[2], p["conv_w"], p["conv_b"], p["dt_bias"], p["a_log"],
                                   p["d_skip"], p["m_norm"], nb, ROW_PAD, q=ROW_PAD, t_valid=1)
    o_b = o_b.reshape(nb, ROW_PAD, GLA_DV)[:, 0]
    o_c = o_c.reshape(nb, ROW_PAD, M_DINNER)[:, 0]
    abk, abv = page_compress(ck_t, cv_t, page_table, layer, w["w1cat"], npg=min(tiles["npg"], page_table.shape[1]))
    q3 = qg[:, :TQ_AG].reshape(nb, NSA_KV, NSA_REP, HD)
    az3 = ug[:, _UG_OFF["a_z"]:_UG_OFF["a_z"] + MIX_W].astype(F32).reshape(nb, NSA_KV, NSA_REP, HD)
    o_cmp, sel = nsa_sample_cmp(abk, abv, kvs, q3, w["w1cat"][layer], w["pe2"][layer], w["w2"][layer], p_len)
    o_a, win_k, win_v = nsa_sample_sel(sel.reshape(nb, NSA_KV, LANE), page_table, sk_t, sv_t, wk_t, wv_t, layer,
                                       kvt, kvs, q3, az3, ux, o_cmp, p_len)
    mix = merge_branches(o_a.reshape(nb, MIX_W).astype(BF16), o_b, o_c, w["wa"], w["wb"], w["wc"], layer, ug,
                         tm=nb, tn=tiles["mg_tn"])
    x_out, h_out = out_proj_norm(mix, w["wo"], layer, x, *next_norm, tm=nb)
    kv4 = kv.reshape(nb, 6, 1, NSA_KV, HD)
    return x_out, h_out, (kv4[:, 0], kv4[:, 1], kv4[:, 2], kv4[:, 3], win_k, win_v, gla_s, ssm_s, conv_s)


_TILES = dict(rms_tm=256, mm_tm=1024, in_tn=1024, kv_tn=512, q_tn=2048, tq=512, ck=512, gla_c=64, gla_sub=4,
              ssd_q=256, mg_tm=1024, mg_tn=512, out_tm=512, npg=32)
_PARAM_NAMES = ("ln_w", "gla_wa2", "gla_ba", "gla_norm", "conv_w", "conv_b", "dt_bias", "a_log", "d_skip", "m_norm")


def kernel(x_prompt, x_sample, cache_cmp_k, cache_cmp_v, cache_slc_k, cache_slc_v, cache_win_k, cache_win_v,
           state_gla, state_ssm, state_conv, page_table, ln_w, w_in, cmp_pe_k, cmp_w1_k, cmp_w2_k, cmp_pe_v,
           cmp_w1_v, cmp_w2_v, gla_wa2, gla_ba, gla_norm, conv_w, conv_b, dt_bias, a_log, d_skip, m_norm,
           w_br_a, w_br_b, w_br_c, w_out, final_norm):
    nbp, t, d = x_prompt.shape
    nbs = x_sample.shape[0]
    depth = w_in.shape[0]
    weights = _prep_weights(w_in, cmp_pe_k, cmp_w1_k, cmp_w2_k, cmp_pe_v, cmp_w1_v, cmp_w2_v,
                            w_br_a, w_br_b, w_br_c, w_out)
    params = dict(zip(_PARAM_NAMES, (ln_w, gla_wa2, gla_ba, gla_norm, conv_w, conv_b, dt_bias, a_log, d_skip, m_norm)))
    caches = tuple(jnp.transpose(c, (0, 1, 3, 4, 2))
                   for c in (cache_cmp_k, cache_cmp_v, cache_slc_k, cache_slc_v, cache_win_k, cache_win_v))
    xp = x_prompt.reshape(nbp * t, d)
    xs = x_sample.reshape(nbs, d)
    p_states, s_out = [], []
    stacks = [jnp.zeros((depth, nbp, GRP_ROWS, t), F32) for _ in range(6)]
    hp = rmsnorm_rows(xp, ln_w[0], BF16, _TILES["rms_tm"])
    hs = rmsnorm_rows(xs, ln_w[0], BF16, nbs)
    for l in range(depth):
        p = {k: v[l] for k, v in params.items()}
        next_norm = (ln_w[l + 1], BF16) if l + 1 < depth else (final_norm, F32)
        xp, hp, stacks, st_p = _prompt_layer(xp, hp, nbp, t, l, weights, p, next_norm, _TILES, stacks)
        xs, hs, st_s = _sample_layer(xs, hs, l, weights, p, next_norm, caches,
                                     (state_gla[l], state_ssm[l], state_conv[l]), page_table, _TILES)
        p_states.append(st_p)
        s_out.append(st_s)
    y_prompt = hp.reshape(nbp, t, d)
    y_sample = hs.reshape(nbs, 1, d)

    def from_kv_layout(a):
        return jnp.transpose(a, (0, 1, 4, 2, 3))

    n_w = min(WINDOW, t)
    kv5 = [a.reshape(depth, nbp, NSA_KV, HD, t) for a in stacks]
    p_leaves = [from_kv_layout(a) for a in kv5[:4]] + [from_kv_layout(a[..., t - n_w:]) for a in kv5[4:]]
    p_leaves += [jnp.stack([st[i] for st in p_states]) for i in range(3)]
    s_leaves = [jnp.stack([o[i] for o in s_out]) for i in range(9)]
    s_leaves[4], s_leaves[5] = from_kv_layout(s_leaves[4]), from_kv_layout(s_leaves[5])
    return (y_prompt, y_sample, *p_leaves, *s_leaves)
```

```python
import functools
import math

import jax
import jax.numpy as jnp
import numpy as np
from jax import lax
from jax.experimental import pallas as pl
from jax.experimental.pallas import tpu as pltpu

F32 = jnp.float32
BF16 = jnp.bfloat16

D_MODEL = 2048
PAGE = 128
MIX_W = D_MODEL // 2
HD = 64
NSA_HEADS = MIX_W // HD
NSA_KV = 4
NSA_REP = NSA_HEADS // NSA_KV
CMP_LEN = 32
CMP_STRIDE = 16
SLC_BLK = 64
SLC_TOPK = 16
WINDOW = 512
GLA_HEADS = 4
GLA_DK = MIX_W // 2
GLA_DV = MIX_W
GLA_HK = GLA_DK // GLA_HEADS
GLA_HV = GLA_DV // GLA_HEADS
GLA_RANK = 16
GLA_TAU = 16.0
M_DINNER = MIX_W
M_HD = 64
M_HEADS = M_DINNER // M_HD
M_GROUPS = 4
M_STATE = 128
M_CONV = 4
M_CONVDIM = M_DINNER + 2 * M_GROUPS * M_STATE
EPS = 1e-6
NEG = -1e30
FORCE = 1e4
LOG2E = math.log2(math.e)
WIN_Q = 128
ONES_ROWS = 16

_IN_NAMES = ("a_q", "a_kv", "a_g", "a_z", "b_q", "b_k", "b_v", "b_a", "b_z", "c_xbc", "c_dt", "c_z", "m_g")
_IN_SIZES = (NSA_HEADS * HD, 6 * NSA_KV * HD, 3 * NSA_HEADS, MIX_W, GLA_DK, GLA_DK, GLA_DV, GLA_RANK, GLA_DV,
             M_CONVDIM, M_HEADS, M_DINNER, 3 * D_MODEL)
_IN_OFF = dict(zip(_IN_NAMES, np.cumsum((0,) + _IN_SIZES)[:-1].tolist()))
_IN_SZ = dict(zip(_IN_NAMES, _IN_SIZES))

LANE = 128
_UG_ORDER = ("m_g", "a_z", "b_v", "b_z", "c_z", "b_q", "b_k")
_UX_ORDER = ("c_xbc", "a_g", "b_a", "c_dt")


def _offsets(order):
    off, o = {}, 0
    for n in order:
        off[n] = o
        o += _IN_SZ[n]
    return off, o


_UG_OFF, NP_UG = _offsets(_UG_ORDER)
_UX_OFF, _ux_used = _offsets(_UX_ORDER)
NP_UX = -(-_ux_used // LANE) * LANE
OFF_SMALL = _UX_OFF["a_g"]
SM_AG, SM_BA, SM_DT = 0, 3 * NSA_HEADS, 3 * NSA_HEADS + GLA_RANK
KV_ROWS = _IN_SZ["a_kv"]
GRP_ROWS = NSA_KV * HD
_TQ_ORDER = ("a_q", "a_g")
TQ_AG = _IN_SZ["a_q"]
TQ_TM = 384
TQ_ROWS = -(-(TQ_AG + _IN_SZ["a_g"]) // TQ_TM) * TQ_TM
VMEM_LIMIT = 56 * 1024 * 1024


def _cparams(sem):
    return pltpu.CompilerParams(dimension_semantics=sem, vmem_limit_bytes=VMEM_LIMIT)


def _dot(a, b, prec=None, nt=False, tn=False):
    if prec is None:
        a, b = a.astype(BF16), b.astype(BF16)
    dn = (((0,) if tn else (1,), (1,) if nt else (0,)), ((), ()))
    return lax.dot_general(a, b, dn, precision=prec, preferred_element_type=F32)


def _split(a):
    hi = a.astype(BF16)
    return hi, (a - hi.astype(F32)).astype(BF16)


def _dot_exact_rhs(a, b_bf16):
    hi, lo = _split(a)
    return _dot(hi, b_bf16) + _dot(lo, b_bf16)


def _dot_exact_lhs(a_bf16, b):
    hi, lo = _split(b)
    return _dot(a_bf16, hi) + _dot(a_bf16, lo)


def _dot3(a, b, **kw):
    ah, al = _split(a)
    bh, bl = _split(b)
    return _dot(ah, bh, **kw) + _dot(ah, bl, **kw) + _dot(al, bh, **kw)


def _sigmoid(x):
    return 0.5 * jnp.tanh(0.5 * x) + 0.5


def _silu(x):
    return x * _sigmoid(x)


def _log_sigmoid(x):
    return jnp.minimum(x, 0.0) - jnp.log(1.0 + jnp.exp(-jnp.abs(x)))


def _softplus(x):
    return jnp.maximum(x, 0.0) + jnp.log(1.0 + jnp.exp(-jnp.abs(x)))


def _stacked_spec(block, imap, layer):
    if layer is None:
        return pl.BlockSpec(block, imap)
    return pl.BlockSpec((None,) + block, lambda *idx: (layer,) + imap(*idx))


def _rms_kernel(x_ref, w_ref, o_ref):
    x = x_ref[...]
    y = x * lax.rsqrt(jnp.mean(x * x, axis=-1, keepdims=True) + EPS)
    o_ref[...] = (y * w_ref[...]).astype(o_ref.dtype)


def rmsnorm_rows(x, w, out_dtype, tm):
    m, d = x.shape
    return pl.pallas_call(
        _rms_kernel,
        grid=(m // tm,),
        in_specs=[pl.BlockSpec((tm, d), lambda i: (i, 0)), pl.BlockSpec((1, d), lambda i: (0, 0))],
        out_specs=pl.BlockSpec((tm, d), lambda i: (i, 0)),
        out_shape=jax.ShapeDtypeStruct((m, d), out_dtype),
        compiler_params=_cparams(("parallel",)),
        name="rmsnorm",
    )(x, w.reshape(1, d))


def _mm_kernel(*refs, nt, has_res):
    a_ref, b_ref, o_ref = refs[0], refs[1], refs[-1]
    acc = _dot(a_ref[...], b_ref[...], nt=nt)
    if has_res:
        acc = acc + refs[2][...]
    o_ref[...] = acc.astype(o_ref.dtype)


def matmul(a, b, *, nt, tm, tn, res=None, out_dtype=F32, name="matmul", a_layer=None, b_layer=None):
    m, k = a.shape[-2:]
    n = b.shape[-2] if nt else b.shape[-1]
    tm, tn = min(tm, m), min(tn, n)
    if nt:
        b_spec = _stacked_spec((tn, k), lambda i, j: (j, 0), b_layer)
    else:
        b_spec = _stacked_spec((k, tn), lambda i, j: (0, j), b_layer)
    in_specs = [_stacked_spec((tm, k), lambda i, j: (i, 0), a_layer), b_spec]
    args = [a, b]
    if res is not None:
        in_specs.append(pl.BlockSpec((tm, tn), lambda i, j: (i, j)))
        args.append(res)
    return pl.pallas_call(
        functools.partial(_mm_kernel, nt=nt, has_res=res is not None),
        grid=(m // tm, n // tn),
        in_specs=in_specs,
        out_specs=pl.BlockSpec((tm, tn), lambda i, j: (i, j)),
        out_shape=jax.ShapeDtypeStruct((m, n), out_dtype),
        compiler_params=_cparams(("parallel", "parallel")),
        name=name,
    )(*args)


def matmul_t(wt, layer, h, nb, t, *, tm, tn, name):
    _, r, k = wt.shape
    tn = min(tn, t)
    nj = t // tn
    return pl.pallas_call(
        functools.partial(_mm_kernel, nt=True, has_res=False),
        grid=(nb, nj, r // tm),
        in_specs=[pl.BlockSpec((None, tm, k), lambda b, j, i: (layer, i, 0)),
                  pl.BlockSpec((tn, k), lambda b, j, i: (b * nj + j, 0))],
        out_specs=pl.BlockSpec((None, tm, tn), lambda b, j, i: (b, i, j)),
        out_shape=jax.ShapeDtypeStruct((nb, r, t), F32),
        compiler_params=_cparams(("parallel", "parallel", "parallel")),
        name=name,
    )(wt, h)


def _kv_stack_kernel(w_ref, h_ref, *refs):
    outs = refs[-6:]
    hb = h_ref[...]
    for i in range(6):
        outs[i][...] = _dot(w_ref[i * GRP_ROWS:(i + 1) * GRP_ROWS, :], hb, nt=True)


def kv_proj_stacked(wt_kv, layer, h, nb, t, stacks, *, tn):
    depth, _, k = wt_kv.shape
    tn = min(tn, t)
    nj = t // tn
    in_specs = [pl.BlockSpec((None, KV_ROWS, k), lambda b, j: (layer, 0, 0)),
                pl.BlockSpec((tn, k), lambda b, j: (b * nj + j, 0))] + [pl.BlockSpec(memory_space=pl.ANY)] * 6
    args = [wt_kv, h] + list(stacks)
    aliases = {2 + i: i for i in range(6)}
    o_spec = pl.BlockSpec((None, None, GRP_ROWS, tn), lambda b, j: (layer, b, 0, j))
    o_shape = jax.ShapeDtypeStruct((depth, nb, GRP_ROWS, t), F32)
    return pl.pallas_call(
        _kv_stack_kernel,
        grid=(nb, nj),
        in_specs=in_specs,
        out_specs=[o_spec] * 6,
        out_shape=[o_shape] * 6,
        input_output_aliases=aliases,
        compiler_params=_cparams(("parallel", "parallel")),
        name="kv_proj_t",
    )(*args)


def _out_norm_kernel(a_ref, w_ref, res_ref, nw_ref, x_ref, h_ref):
    x = res_ref[...] + _dot(a_ref[...], w_ref[...])
    x_ref[...] = x
    y = x * lax.rsqrt(jnp.mean(x * x, axis=-1, keepdims=True) + EPS)
    h_ref[...] = (y * nw_ref[...]).astype(h_ref.dtype)


def out_proj_norm(mix, wo, layer, res, norm_w, h_dtype, *, tm):
    m, k = mix.shape
    n = wo.shape[-1]
    tm = min(tm, m)
    row = lambda i: (i, 0)
    return pl.pallas_call(
        _out_norm_kernel,
        grid=(m // tm,),
        in_specs=[pl.BlockSpec((tm, k), row), pl.BlockSpec((None, k, n), lambda i: (layer, 0, 0)),
                  pl.BlockSpec((tm, n), row), pl.BlockSpec((1, n), lambda i: (0, 0))],
        out_specs=[pl.BlockSpec((tm, n), row), pl.BlockSpec((tm, n), row)],
        out_shape=[jax.ShapeDtypeStruct((m, n), F32), jax.ShapeDtypeStruct((m, n), h_dtype)],
        compiler_params=_cparams(("parallel",)),
        name="out_proj",
    )(mix, wo, res, norm_w.reshape(1, n))


def _merge_kernel(oa_ref, ob_ref, oc_ref, wa_ref, wb_ref, wc_ref, g0_ref, g1_ref, g2_ref, o_ref):
    acc = _sigmoid(g0_ref[...].astype(F32)) * _dot(oa_ref[...], wa_ref[...])
    acc = acc + _sigmoid(g1_ref[...].astype(F32)) * _dot(ob_ref[...], wb_ref[...])
    acc = acc + _sigmoid(g2_ref[...].astype(F32)) * _dot(oc_ref[...], wc_ref[...])
    o_ref[...] = acc.astype(o_ref.dtype)


def merge_branches(o_a, o_b, o_c, wa, wb, wc, layer, ug, *, tm, tn):
    m, k = o_a.shape
    n = wa.shape[-1]
    tm = min(tm, m)
    gb = _UG_OFF["m_g"] // tn
    nj = n // tn
    o_spec = pl.BlockSpec((tm, k), lambda i, j: (i, 0))
    w_spec = pl.BlockSpec((None, k, tn), lambda i, j: (layer, 0, j))
    g_specs = [pl.BlockSpec((tm, tn), functools.partial(lambda i, j, br: (i, gb + br * nj + j), br=br))
               for br in range(3)]
    return pl.pallas_call(
        _merge_kernel,
        grid=(m // tm, nj),
        in_specs=[o_spec, o_spec, o_spec, w_spec, w_spec, w_spec] + g_specs,
        out_specs=pl.BlockSpec((tm, tn), lambda i, j: (i, j)),
        out_shape=jax.ShapeDtypeStruct((m, n), BF16),
        compiler_params=_cparams(("parallel", "parallel")),
        name="merge",
    )(o_a, o_b, o_c, wa, wb, wc, ug, ug, ug)


def _cmp_kernel(seg_ref, w1_ref, pe_ref, w2_ref, o_ref, *, nseg):
    w1 = w1_ref[...]
    ab = _dot3(seg_ref[...], w1)
    pe = _dot3(pe_ref[...], w1)
    cst = pe[0:1, :HD] + pe[1:2, HD:]
    nxt = pltpu.roll(ab[:, HD:], shift=nseg - 1, axis=0)
    pre = ab[:, :HD] + nxt + cst
    o_ref[...] = _dot3(_silu(pre), w2_ref[...])


def compress_blocks(seg, w1cat, pe2, w2, nseg):
    _, r, kdim = seg.shape
    return pl.pallas_call(
        functools.partial(_cmp_kernel, nseg=nseg),
        grid=(2, r // nseg),
        in_specs=[pl.BlockSpec((None, nseg, kdim), lambda s, i: (s, i, 0)),
                  pl.BlockSpec((None, kdim, 2 * HD), lambda s, i: (s, 0, 0)),
                  pl.BlockSpec((None, 8, kdim), lambda s, i: (s, 0, 0)),
                  pl.BlockSpec((None, HD, HD), lambda s, i: (s, 0, 0))],
        out_specs=pl.BlockSpec((None, nseg, HD), lambda s, i: (s, i, 0)),
        out_shape=jax.ShapeDtypeStruct((2, r, HD), F32),
        compiler_params=_cparams(("parallel", "parallel")),
        name="compress",
    )(seg, w1cat, pe2, w2)


def _nsa_prompt_kernel(qt_ref, ag_ref, az_ref, kst_ref, vst_ref, kwt_ref, vwt_ref, kc_ref, vc_ref, o_ref,
                       ks_scr, vs_scr, kw_scr, vw_scr, bias_scr, *, t, tq, ck, nseg):
    g = pl.program_id(1)
    i = pl.program_id(2)
    s0 = i * tq
    ns = t // SLC_BLK
    nsr = bias_scr.shape[0]
    cols = NSA_REP * tq

    @pl.when(i == 0)
    def _():
        ks_scr[...] = jnp.transpose(kst_ref[...]).astype(BF16)
        kw_scr[...] = jnp.transpose(kwt_ref[...]).astype(BF16)
        ones = jnp.ones((ONES_ROWS, t), BF16)
        vs_scr[...] = jnp.concatenate([vst_ref[...].astype(BF16), ones], axis=0)
        vw_scr[...] = jnp.concatenate([vwt_ref[...].astype(BF16), ones], axis=0)

    qt = qt_ref[...]
    q2 = jnp.concatenate([qt[r * HD:(r + 1) * HD, :] for r in range(NSA_REP)], axis=1) * (HD ** -0.5 * LOG2E)
    q2b = q2.astype(BF16)
    tpos1 = s0 + lax.broadcasted_iota(jnp.int32, (1, tq), 1)
    tpos = jnp.concatenate([tpos1] * NSA_REP, axis=1)

    crow = lax.broadcasted_iota(jnp.int32, (nseg, 1), 0)
    m_c = (crow * CMP_STRIDE + (CMP_LEN - 1) <= tpos) & (crow < nseg - 1)
    s_c = jnp.where(m_c, _dot3(kc_ref[...], q2), NEG)
    p_c = jnp.where(m_c, jnp.exp2(s_c - jnp.max(s_c, axis=0, keepdims=True)), 0.0)
    l_c = jnp.sum(p_c, axis=0, keepdims=True)
    p_c = p_c / jnp.where(l_c > 0.0, l_c, 1.0)
    o_c = _dot(jnp.transpose(vc_ref[...]), p_c)
    psum = p_c[:, 0:tq]
    for r in range(1, NSA_REP):
        psum = psum + p_c[:, r * tq:(r + 1) * tq]

    srow = lax.broadcasted_iota(jnp.int32, (nsr, 1), 0)
    ccol = lax.broadcasted_iota(jnp.int32, (1, nseg), 1) * CMP_STRIDE
    cover = jnp.where((ccol + CMP_LEN > srow * SLC_BLK) & (ccol < (srow + 1) * SLC_BLK), 1.0, 0.0).astype(BF16)
    imp = _dot_exact_lhs(cover, psum)
    cur = tpos1 // SLC_BLK
    valid = (srow * SLC_BLK <= tpos1) & (srow < ns)
    forced = (srow == 0) | (srow == cur) | (srow == cur - 1)
    score = jnp.where(valid, jnp.where(forced, FORCE, imp), NEG)
    rank = jnp.zeros((nsr, tq), F32)
    for j in range(ns):
        sj = score[j:j + 1, :]
        rank = rank + jnp.where((sj > score) | ((sj == score) & (j < srow)), 1.0, 0.0)
    sel = (rank < float(min(SLC_TOPK, ns))) & valid
    bias_scr[...] = jnp.where(sel, 0.0, NEG)

    def attend(k_rows, v_cols, bias1, m, acc, q_cols=q2b):
        w = bias1.shape[1]
        s = _dot(k_rows, q_cols)
        s = jnp.concatenate([s[:, r * w:(r + 1) * w] + bias1 for r in range(NSA_REP)], axis=1)
        m_new = jnp.maximum(m, jnp.max(s, axis=0, keepdims=True))
        p = jnp.exp2(s - m_new)
        return m_new, jnp.exp2(m - m_new) * acc + _dot(v_cols, p)

    def finish(acc):
        l = acc[HD:HD + 1, :]
        return acc[:HD, :] / jnp.where(l > 0.0, l, 1.0)

    blk_per_chunk = ck // SLC_BLK
    m0 = jnp.full((1, cols), NEG, F32)
    acc0 = jnp.zeros((HD + ONES_ROWS, cols), F32)

    def sel_chunk(j, carry):
        k0 = pl.multiple_of(j * ck, ck)
        npos = k0 + lax.broadcasted_iota(jnp.int32, (ck, 1), 0)
        rows = [jnp.broadcast_to(bias_scr[pl.ds(j * blk_per_chunk + k, 1), :], (SLC_BLK, tq))
                for k in range(blk_per_chunk)]
        bias1 = jnp.where(npos <= tpos1, jnp.concatenate(rows, axis=0), NEG)
        return attend(ks_scr[pl.ds(k0, ck), :], vs_scr[:, pl.ds(k0, ck)], bias1, *carry)

    n_chunks = (s0 + tq + ck - 1) // ck
    o_s = finish(lax.fori_loop(0, n_chunks, sel_chunk, (m0, acc0))[1])

    wq = min(WIN_Q, tq)
    ww = min(WINDOW + wq, t)
    parts = []
    for a in range(tq // wq):
        w0 = pl.multiple_of(jnp.clip(s0 + a * wq - WINDOW, 0, t - ww), LANE)
        wpos = w0 + lax.broadcasted_iota(jnp.int32, (ww, 1), 0)
        tsub = tpos1[:, a * wq:(a + 1) * wq]
        bias_w = jnp.where((wpos <= tsub) & (wpos > tsub - WINDOW), 0.0, NEG)
        q_sub = jnp.concatenate([q2b[:, r * tq + a * wq:r * tq + (a + 1) * wq] for r in range(NSA_REP)], axis=1)
        parts.append(finish(attend(kw_scr[pl.ds(w0, ww), :], vw_scr[:, pl.ds(w0, ww)], bias_w,
                                   m0[:, :NSA_REP * wq], acc0[:, :NSA_REP * wq], q_sub)[1]))
    o_w = jnp.concatenate([parts[a][:, r * wq:(r + 1) * wq] for r in range(NSA_REP) for a in range(tq // wq)], axis=1)

    outs = []
    for r in range(NSA_REP):
        sl = slice(r * tq, (r + 1) * tq)
        gt = [_sigmoid(ag_ref[pl.ds(3 * (NSA_REP * g + r) + j, 1), :]) for j in range(3)]
        outs.append(jnp.transpose(gt[0] * o_c[:, sl] + gt[1] * o_s[:, sl] + gt[2] * o_w[:, sl]))
    o = jnp.concatenate(outs, axis=1) * _silu(az_ref[...].astype(F32))
    o_ref[...] = o.astype(o_ref.dtype)


def nsa_prompt(ug, qgt, stacks, layer, kc, nb, t, *, tq, ck):
    tq = min(tq, t)
    nq = t // tq
    nseg = t // CMP_STRIDE
    ck = min(ck, t)
    gw = NSA_REP * HD
    nsr = -(-(t // SLC_BLK) // 8) * 8
    kv_spec = pl.BlockSpec((None, None, HD, t), lambda b, g, i: (layer, b, g, 0))
    return pl.pallas_call(
        functools.partial(_nsa_prompt_kernel, t=t, tq=tq, ck=ck, nseg=nseg),
        grid=(nb, NSA_KV, nq),
        in_specs=[pl.BlockSpec((None, gw, tq), lambda b, g, i: (b, g, i)),
                  pl.BlockSpec((None, LANE, tq), lambda b, g, i: (b, TQ_AG // LANE, i)),
                  pl.BlockSpec((tq, gw), lambda b, g, i: (b * nq + i, _UG_OFF["a_z"] // gw + g)),
                  kv_spec, kv_spec, kv_spec, kv_spec,
                  pl.BlockSpec((None, nseg, HD), lambda b, g, i: (0, b * NSA_KV + g, 0)),
                  pl.BlockSpec((None, nseg, HD), lambda b, g, i: (1, b * NSA_KV + g, 0))],
        out_specs=pl.BlockSpec((tq, gw), lambda b, g, i: (b * nq + i, g)),
        out_shape=jax.ShapeDtypeStruct((nb * t, NSA_HEADS * HD), BF16),
        scratch_shapes=[pltpu.VMEM((t, HD), BF16), pltpu.VMEM((HD + ONES_ROWS, t), BF16),
                        pltpu.VMEM((t, HD), BF16), pltpu.VMEM((HD + ONES_ROWS, t), BF16),
                        pltpu.VMEM((nsr, tq), F32)],
        compiler_params=_cparams(("parallel", "parallel", "arbitrary")),
        name="nsa_prompt",
    )(qgt, qgt, ug, stacks[2], stacks[3], stacks[4], stacks[5], kc, kc)


def _gla_kernel(q_ref, k_ref, v_ref, z_ref, sm_ref, s0_ref, wa2_ref, ba_ref, gn_ref, o_ref, sout_ref, s_scr,
                *, c, sub, t_valid):
    ci = pl.program_id(1)
    rows = c * sub

    @pl.when(ci == 0)
    def _():
        s_scr[...] = s0_ref[...]

    row = ci * rows + lax.broadcasted_iota(jnp.int32, (rows, 1), 0)
    gate_in = sm_ref[...][:, SM_BA:SM_BA + GLA_RANK]
    log_a = _log_sigmoid(_dot3(gate_in, wa2_ref[...]) + ba_ref[...]) / GLA_TAU
    log_a = jnp.where(row < t_valid, log_a, 0.0)
    causal = lax.broadcasted_iota(jnp.int32, (c, c), 0) >= lax.broadcasted_iota(jnp.int32, (c, c), 1)
    tri = jnp.where(causal, 1.0, 0.0).astype(BF16)
    cb_all = jnp.concatenate([_dot_exact_lhs(tri, log_a[j * c:(j + 1) * c, :]) for j in range(sub)], axis=0)
    for h in range(GLA_HEADS):
        ks = slice(h * GLA_HK, (h + 1) * GLA_HK)
        vs = slice(h * GLA_HV, (h + 1) * GLA_HV)
        state = s_scr[h]
        outs = []
        for j in range(sub):
            rs = slice(j * c, (j + 1) * c)
            cb = cb_all[rs, ks]
            ecb = jnp.exp(cb)
            qd = q_ref[rs, ks].astype(F32) * (GLA_HK ** -0.5) * ecb
            kh = k_ref[rs, ks].astype(F32)
            vh = v_ref[rs, vs]
            att = jnp.where(causal, _dot(qd, kh * jnp.exp(-cb), nt=True), 0.0)
            outs.append(_dot(att, vh) + _dot(qd, state))
            c_end = cb[c - 1:c, :]
            k_dec = kh * jnp.exp(c_end - cb)
            e_col = jnp.transpose(ecb[c - 8:c, :])[:, 7:8]
            state = e_col * state + _dot(k_dec, vh, tn=True)
        o = jnp.concatenate(outs, axis=0)
        y = o * lax.rsqrt(jnp.mean(o * o, axis=-1, keepdims=True) + EPS) * gn_ref[...]
        o_ref[:, vs] = (y * _silu(z_ref[:, vs].astype(F32))).astype(o_ref.dtype)
        s_scr[h] = state

    @pl.when(ci == pl.num_programs(1) - 1)
    def _():
        sout_ref[...] = s_scr[...]


def gla_mixer(ug, ux, s0, wa2, ba, gnorm, nb, t, *, c, sub, t_valid):
    rows = c * sub
    nc = t // rows

    def u_spec(name, width):
        return pl.BlockSpec((rows, width), lambda b, i: (b * nc + i, _UG_OFF[name] // width))

    s_spec = pl.BlockSpec((None, GLA_HEADS, GLA_HK, GLA_HV), lambda b, i: (b, 0, 0, 0))
    return pl.pallas_call(
        functools.partial(_gla_kernel, c=c, sub=sub, t_valid=t_valid),
        grid=(nb, nc),
        in_specs=[u_spec("b_q", GLA_DK), u_spec("b_k", GLA_DK), u_spec("b_v", GLA_DV), u_spec("b_z", GLA_DV),
                  pl.BlockSpec((rows, LANE), lambda b, i: (b * nc + i, OFF_SMALL // LANE)),
                  s_spec,
                  pl.BlockSpec((GLA_RANK, GLA_DK), lambda b, i: (0, 0)),
                  pl.BlockSpec((1, GLA_DK), lambda b, i: (0, 0)),
                  pl.BlockSpec((1, GLA_HV), lambda b, i: (0, 0))],
        out_specs=[pl.BlockSpec((rows, GLA_DV), lambda b, i: (b * nc + i, 0)), s_spec],
        out_shape=[jax.ShapeDtypeStruct((nb * t, GLA_DV), BF16),
                   jax.ShapeDtypeStruct((nb, GLA_HEADS, GLA_HK, GLA_HV), F32)],
        scratch_shapes=[pltpu.VMEM((GLA_HEADS, GLA_HK, GLA_HV), F32)],
        compiler_params=_cparams(("parallel", "arbitrary")),
        name="gla",
    )(ug, ug, ug, ug, ux, s0, wa2, ba.reshape(1, GLA_DK), gnorm.reshape(1, GLA_HV))


def _ssd_kernel(xbc_ref, z_ref, sm_ref, s0_ref, cbuf_ref, cw_ref, cbias_ref, dtb_ref, alog_ref, dsk_ref, mn_ref,
                o_ref, sout_ref, cout_ref, s_scr, f_scr, *, q, t_valid, n_chunks):
    ci = pl.program_id(1)
    npad = 8
    hist = M_CONV - 1

    @pl.when(ci == 0)
    def _():
        s_scr[...] = s0_ref[...].reshape(s_scr.shape)
        f_scr[npad - hist:npad, :] = cbuf_ref[...]

    f_scr[npad:npad + q, :] = xbc_ref[...]
    conv = cbias_ref[...] + xbc_ref[...] * cw_ref[hist:hist + 1, :]
    full = f_scr[...]
    for i in range(hist):
        conv = conv + pltpu.roll(full, shift=hist - i, axis=0)[npad:npad + q, :] * cw_ref[i:i + 1, :]
    xbc = _silu(conv)

    last_valid = t_valid - (n_chunks - 1) * q

    @pl.when(ci == n_chunks - 1)
    def _():
        cout_ref[...] = f_scr[npad - hist + last_valid:npad + last_valid, :]

    f_scr[npad - hist:npad, :] = f_scr[npad - hist + q:npad + q, :]

    lane = lax.broadcasted_iota(jnp.int32, (1, LANE), 1)
    row = ci * q + lax.broadcasted_iota(jnp.int32, (q, 1), 0)
    head_lane = (lane >= SM_DT) & (lane < SM_DT + M_HEADS)
    dt = jnp.where(head_lane & (row < t_valid), _softplus(sm_ref[...] + dtb_ref[...]), 0.0)
    a = dt * (-jnp.exp(alog_ref[...]))
    tri = jnp.where(lax.broadcasted_iota(jnp.int32, (q, q), 0) >= lax.broadcasted_iota(jnp.int32, (q, q), 1), 1.0, 0.0)
    causal = tri > 0.5
    cum = _dot_exact_lhs(tri.astype(BF16), a)
    cum_t = jnp.transpose(cum)
    dt_t = jnp.transpose(dt)
    nbc = M_GROUPS * M_STATE
    rep = M_HEADS // M_GROUPS
    gw = rep * M_HD
    head_of_lane = lax.broadcasted_iota(jnp.int32, (1, gw), 1) // M_HD
    head_of_row = lax.broadcasted_iota(jnp.int32, (gw, 1), 0) // M_HD
    for g in range(M_GROUPS):
        sl = slice(g * gw, (g + 1) * gw)
        bg = xbc[:, M_DINNER + g * M_STATE:M_DINNER + (g + 1) * M_STATE]
        cg = xbc[:, M_DINNER + nbc + g * M_STATE:M_DINNER + nbc + (g + 1) * M_STATE]
        xg = xbc[:, sl]
        cbm = _dot(cg, bg, nt=True)
        s_prev = s_scr[g]
        y = jnp.zeros((q, gw), F32)
        e_lanes = jnp.zeros((q, gw), F32)
        w_lanes = jnp.zeros((q, gw), F32)
        skip = jnp.zeros((1, gw), F32)
        keep = jnp.zeros((gw, 1), F32)
        for r in range(rep):
            hl = SM_DT + g * rep + r
            mine = head_of_lane == r
            cum_c = cum[:, hl:hl + 1]
            c_end = cum[q - 1:q, hl:hl + 1]
            decay = jnp.exp(jnp.where(causal, cum_c - cum_t[hl:hl + 1, :], -jnp.inf))
            y = y + _dot(cbm * decay * dt_t[hl:hl + 1, :], jnp.where(mine, xg, 0.0))
            e_lanes = jnp.where(mine, jnp.exp(cum_c), e_lanes)
            w_lanes = jnp.where(mine, jnp.exp(c_end - cum_c) * dt[:, hl:hl + 1], w_lanes)
            skip = jnp.where(mine, dsk_ref[:, hl:hl + 1], skip)
            keep = jnp.where(head_of_row == r, jnp.exp(c_end), keep)
        y = y + e_lanes * _dot(cg, s_prev, nt=True) + skip * xg
        s_scr[g] = keep * s_prev + _dot(xg * w_lanes, bg, tn=True)
        yg = y * _silu(z_ref[:, sl].astype(F32))
        yn = yg * lax.rsqrt(jnp.mean(yg * yg, axis=-1, keepdims=True) + EPS) * mn_ref[:, sl]
        o_ref[:, sl] = yn.astype(o_ref.dtype)

    @pl.when(ci == n_chunks - 1)
    def _():
        sout_ref[...] = s_scr[...].reshape(M_HEADS, M_HD, M_STATE)


def _pad_heads(v):
    return jnp.zeros((1, LANE), F32).at[0, SM_DT:SM_DT + M_HEADS].set(v)


def ssd_mixer(ug, ux, s0, cbuf, conv_w, conv_b, dt_bias, a_log, d_skip, m_norm, nb, t, *, q, t_valid):
    nc = t // q
    n_chunks = -(-t_valid // q)
    assert n_chunks == nc
    hist = M_CONV - 1

    def u_spec(off, name, width):
        return pl.BlockSpec((q, width), lambda b, i: (b * nc + i, off[name] // width))

    def full(shape):
        return pl.BlockSpec(shape, lambda b, i: (0,) * len(shape))

    s_spec = pl.BlockSpec((None, M_HEADS, M_HD, M_STATE), lambda b, i: (b, 0, 0, 0))
    c_spec = pl.BlockSpec((None, hist, M_CONVDIM), lambda b, i: (b, 0, 0))
    return pl.pallas_call(
        functools.partial(_ssd_kernel, q=q, t_valid=t_valid, n_chunks=n_chunks),
        grid=(nb, nc),
        in_specs=[u_spec(_UX_OFF, "c_xbc", M_CONVDIM), u_spec(_UG_OFF, "c_z", M_DINNER),
                  pl.BlockSpec((q, LANE), lambda b, i: (b * nc + i, OFF_SMALL // LANE)),
                  s_spec, c_spec,
                  full((M_CONV, M_CONVDIM)), full((1, M_CONVDIM)), full((1, LANE)), full((1, LANE)),
                  full((1, LANE)), full((1, M_DINNER))],
        out_specs=[pl.BlockSpec((q, M_DINNER), lambda b, i: (b * nc + i, 0)), s_spec, c_spec],
        out_shape=[jax.ShapeDtypeStruct((nb * t, M_DINNER), BF16),
                   jax.ShapeDtypeStruct((nb, M_HEADS, M_HD, M_STATE), F32),
                   jax.ShapeDtypeStruct((nb, hist, M_CONVDIM), F32)],
        scratch_shapes=[pltpu.VMEM((M_GROUPS, M_HEADS // M_GROUPS * M_HD, M_STATE), F32),
                        pltpu.VMEM((q + 8, M_CONVDIM), F32)],
        compiler_params=_cparams(("parallel", "arbitrary")),
        name="ssd",
    )(ux, ug, ux, s0, cbuf, conv_w, conv_b.reshape(1, M_CONVDIM), _pad_heads(dt_bias), _pad_heads(a_log),
      _pad_heads(d_skip), m_norm.reshape(1, M_DINNER))


SEG_PER_PAGE = PAGE // CMP_STRIDE


def _page_cmp_kernel(pt_ref, *refs, npg):
    w1_ref, ok_ref, ov_ref, scr, wbd_scr = refs[2 * npg:]

    @pl.when((pl.program_id(0) == 0) & (pl.program_id(1) == 0))
    def _():
        wbd_scr[...] = jnp.zeros(wbd_scr.shape, wbd_scr.dtype)
        for s in range(2):
            for j in range(CMP_STRIDE):
                wj = w1_ref[s, j * HD:(j + 1) * HD, :].astype(BF16)
                for g in range(NSA_KV):
                    wbd_scr[s, j, g * HD:(g + 1) * HD, g * 2 * HD:(g + 1) * 2 * HD] = wj

    row = lax.broadcasted_iota(jnp.int32, (PAGE, PAGE), 0)
    pos = lax.broadcasted_iota(jnp.int32, (PAGE, PAGE), 1)
    regroup = jnp.where(pos == CMP_STRIDE * (row % SEG_PER_PAGE) + row // SEG_PER_PAGE, 1.0, 0.0).astype(BF16)
    for s, (pages, o_ref) in enumerate(((refs[:npg], ok_ref), (refs[npg:2 * npg], ov_ref))):
        for k in range(npg):
            by_j = _dot(regroup, pages[k][...].reshape(NSA_KV * HD, PAGE), nt=True)
            for j in range(CMP_STRIDE):
                scr[j, k * SEG_PER_PAGE:(k + 1) * SEG_PER_PAGE, :] = by_j[j * SEG_PER_PAGE:(j + 1) * SEG_PER_PAGE, :]
        acc = jnp.zeros((npg * SEG_PER_PAGE, 2 * HD * NSA_KV), F32)
        for j in range(CMP_STRIDE):
            acc = acc + _dot(scr[j], wbd_scr[s, j])
        o_ref[...] = acc


def page_compress(cache_k_t, cache_v_t, page_table, layer, w1cat, *, npg):
    nb, n_pages = page_table.shape
    steps = n_pages // npg

    def page_spec(k):
        return pl.BlockSpec((None, None, NSA_KV, HD, PAGE),
                            lambda b, i, pt: (layer, pt[b, i * npg + k], 0, 0, 0))

    w_spec = pl.BlockSpec((None, 2, CMP_STRIDE * HD, 2 * HD), lambda b, i, pt: (layer, 0, 0, 0))
    o_spec = pl.BlockSpec((None, npg * SEG_PER_PAGE, NSA_KV * 2 * HD), lambda b, i, pt: (b, i, 0))
    o_shape = jax.ShapeDtypeStruct((nb, n_pages * SEG_PER_PAGE, NSA_KV * 2 * HD), F32)
    return pl.pallas_call(
        functools.partial(_page_cmp_kernel, npg=npg),
        grid_spec=pltpu.PrefetchScalarGridSpec(
            num_scalar_prefetch=1,
            grid=(nb, steps),
            in_specs=[page_spec(k) for k in range(npg)] * 2 + [w_spec],
            out_specs=[o_spec, o_spec],
            scratch_shapes=[pltpu.VMEM((CMP_STRIDE, npg * SEG_PER_PAGE, NSA_KV * HD), F32),
                            pltpu.VMEM((2, CMP_STRIDE, NSA_KV * HD, NSA_KV * 2 * HD), BF16)]),
        out_shape=[o_shape, o_shape],
        compiler_params=_cparams(("arbitrary", "arbitrary")),
        name="page_compress",
    )(page_table, *([cache_k_t] * npg), *([cache_v_t] * npg), w1cat)


def _masked_softmax_rows(s, mask):
    s = jnp.where(mask, s, NEG)
    m = jnp.max(s, axis=-1, keepdims=True)
    p = jnp.where(mask, jnp.exp(s - m), 0.0)
    l = jnp.sum(p, axis=-1, keepdims=True)
    return p / jnp.where(l > 0.0, l, 1.0)


def _nsa_sample_cmp_kernel(abk_ref, abv_ref, kvs_ref, q_ref, w1_ref, pe_ref, w2_ref, cover_ref, oc_ref, sel_ref,
                           cst_scr, *, p_len, nc, sl):
    ns = p_len // SLC_BLK + 1
    rowi = lax.broadcasted_iota(jnp.int32, (nc, 1), 0)

    @pl.when(pl.program_id(0) == 0)
    def _():
        for idx in range(2):
            pe = _dot3(pe_ref[idx], w1_ref[idx])
            cst_scr[idx:idx + 1, :] = pe[0:1, :HD] + pe[1:2, HD:]

    cidx = lax.broadcasted_iota(jnp.int32, (1, nc), 1)
    m_c = cidx * CMP_STRIDE + (CMP_LEN - 1) <= p_len
    sidx = lax.broadcasted_iota(jnp.int32, (1, sl), 1)
    cur = p_len // SLC_BLK
    valid = (sidx * SLC_BLK <= p_len) & (sidx < ns)
    forced = (sidx == 0) | (sidx == cur) | (sidx == cur - 1)
    ri = lax.broadcasted_iota(jnp.int32, (sl, sl), 0)
    li = lax.broadcasted_iota(jnp.int32, (sl, sl), 1)
    kl = lax.broadcasted_iota(jnp.int32, (sl, LANE), 1).astype(F32)
    sv = lax.broadcasted_iota(jnp.int32, (sl, LANE), 0).astype(F32)

    for g in range(NSA_KV):
        def compressed(ab_ref, idx):
            ab = ab_ref[:, g * 2 * HD:(g + 1) * 2 * HD]
            new = kvs_ref[idx * NSA_KV + g:idx * NSA_KV + g + 1, :]
            b_new = _dot(new, w1_ref[idx][0:HD, HD:])
            nxt = jnp.where(rowi == nc - 1, b_new, pltpu.roll(ab[:, HD:], shift=nc - 1, axis=0))
            return _dot(_silu(ab[:, :HD] + nxt + cst_scr[idx:idx + 1, :]), w2_ref[idx])

        kc = compressed(abk_ref, 0)
        vc = compressed(abv_ref, 1)
        p_c = _masked_softmax_rows(_dot3(q_ref[g] * (HD ** -0.5), kc, nt=True), m_c)
        oc_ref[g] = _dot(p_c, vc)
        imp = _dot_exact_rhs(jnp.sum(p_c, axis=0, keepdims=True), cover_ref[...])
        score = jnp.where(valid, jnp.where(forced, FORCE, imp), NEG)
        score_b = jnp.broadcast_to(score, (sl, sl))
        score_col = jnp.sum(jnp.where(ri == li, score_b, 0.0), axis=1, keepdims=True)
        ahead = (score_b > score_col) | ((score_b == score_col) & (li < ri))
        rank_col = jnp.sum(jnp.where(ahead, 1.0, 0.0), axis=1, keepdims=True)
        picked = jnp.sum(jnp.where(rank_col == kl, sv, 0.0), axis=0, keepdims=True)
        sel_ref[g] = picked.astype(jnp.int32)


def _cover_matrix(nc, sl):
    c0 = np.arange(nc)[:, None] * CMP_STRIDE
    s0 = np.arange(sl)[None, :] * SLC_BLK
    return jnp.asarray(((c0 + CMP_LEN > s0) & (c0 < s0 + SLC_BLK)).astype(np.float32), dtype=BF16)


def nsa_sample_cmp(abk, abv, kvs, q3, w1cat, pe2, w2, p_len):
    nb, nc, _ = abk.shape
    ns = p_len // SLC_BLK + 1
    sl = -(-ns // LANE) * LANE
    cover = _cover_matrix(nc, sl)

    ab_spec = pl.BlockSpec((None, nc, NSA_KV * 2 * HD), lambda b: (b, 0, 0))

    def full(a):
        return pl.BlockSpec(a.shape, lambda b: (0,) * a.ndim)

    row_spec = pl.BlockSpec((None, NSA_KV, NSA_REP, HD), lambda b: (b, 0, 0, 0))
    return pl.pallas_call(
        functools.partial(_nsa_sample_cmp_kernel, p_len=p_len, nc=nc, sl=sl),
        grid=(nb,),
        in_specs=[ab_spec, ab_spec, pl.BlockSpec((None, 6 * NSA_KV, HD), lambda b: (b, 0, 0)), row_spec,
                  full(w1cat), full(pe2), full(w2), full(cover)],
        out_specs=[row_spec, pl.BlockSpec((None, NSA_KV, 1, LANE), lambda b: (b, 0, 0, 0))],
        out_shape=[jax.ShapeDtypeStruct((nb, NSA_KV, NSA_REP, HD), F32),
                   jax.ShapeDtypeStruct((nb, NSA_KV, 1, LANE), jnp.int32)],
        scratch_shapes=[pltpu.VMEM((8, HD), F32)],
        compiler_params=_cparams(("arbitrary",)),
        name="nsa_sample_cmp",
    )(abk, abv, kvs, q3, w1cat, pe2, w2, cover)


def _nsa_sample_sel_kernel(sel_ref, pt_ref, *refs, p_len, nb, n_buf):
    nk = SLC_TOPK
    kpages, vpages = refs[:nk], refs[nk:2 * nk]
    (kwb_ref, vwb_ref, kvt_ref, kvs_ref, q_ref, az_ref, sm_ref, oc_ref,
     o_ref, kwo_ref, vwo_ref) = refs[2 * nk:]
    b = pl.program_id(0)
    g = pl.program_id(1)
    n_past = p_len // SLC_BLK
    q4 = q_ref[...] * (HD ** -0.5)
    lane_b = lax.broadcasted_iota(jnp.int32, (1, nb), 1) == b

    def new_row(i):
        return kvs_ref[pl.ds(i * NSA_KV + g, 1), :]

    def new_col(i):
        blk = kvt_ref[pl.ds(pl.multiple_of((i * NSA_KV + g) * HD, HD), HD), :]
        return jnp.sum(jnp.where(lane_b, blk, 0.0), axis=1, keepdims=True)

    half_of_lane = lax.broadcasted_iota(jnp.int32, (1, PAGE), 1) // SLC_BLK
    s_parts, m_parts = [], []
    has_new = jnp.zeros((1, 1), jnp.int32)
    for k in range(nk):
        blk = sel_ref[b, g, k]
        m_parts.append((half_of_lane == blk % 2) & (blk < n_past))
        s_parts.append(_dot(q4, kpages[k][...]))
        has_new = has_new + jnp.where(blk == n_past, 1, 0)
    s_sel = jnp.concatenate(s_parts, axis=1)
    m_sel = jnp.concatenate(m_parts, axis=1)
    new_ok = has_new > 0
    s_new = jnp.where(new_ok, jnp.sum(q4 * new_row(2), axis=1, keepdims=True), NEG)
    s_sel = jnp.where(m_sel, s_sel, NEG)
    mx = jnp.maximum(jnp.max(s_sel, axis=1, keepdims=True), s_new)
    p_sel = jnp.where(m_sel, jnp.exp(s_sel - mx), 0.0)
    p_new = jnp.where(new_ok, jnp.exp(s_new - mx), 0.0)
    den = jnp.sum(p_sel, axis=1, keepdims=True) + p_new
    acc = p_new * new_row(3)
    for k in range(nk):
        acc = acc + _dot(p_sel[:, k * PAGE:(k + 1) * PAGE], vpages[k][...], nt=True)
    o_s = acc / jnp.where(den > 0.0, den, 1.0)

    kwb = kwb_ref[...]
    vwb = vwb_ref[...]
    wlane = lax.broadcasted_iota(jnp.int32, (1, n_buf), 1)
    m_w = wlane > n_buf - WINDOW
    s_w = jnp.where(m_w, _dot(q4, kwb), NEG)
    s_wn = jnp.sum(q4 * new_row(4), axis=1, keepdims=True)
    mw = jnp.maximum(jnp.max(s_w, axis=1, keepdims=True), s_wn)
    p_w = jnp.where(m_w, jnp.exp(s_w - mw), 0.0)
    p_wn = jnp.exp(s_wn - mw)
    o_w = (_dot(p_w, vwb, nt=True) + p_wn * new_row(5)) / (jnp.sum(p_w, axis=1, keepdims=True) + p_wn)
    kwo_ref[...] = jnp.where(wlane == n_buf - 1, new_col(4), pltpu.roll(kwb, shift=n_buf - 1, axis=1))
    vwo_ref[...] = jnp.where(wlane == n_buf - 1, new_col(5), pltpu.roll(vwb, shift=n_buf - 1, axis=1))

    gate_row = _sigmoid(pltpu.roll(sm_ref[pl.ds(b, 1), :], shift=(LANE - 3 * NSA_REP * g) % LANE, axis=1))
    gl = lax.broadcasted_iota(jnp.int32, (NSA_REP, LANE), 1)
    gr = lax.broadcasted_iota(jnp.int32, (NSA_REP, LANE), 0)

    def gate(j):
        return jnp.sum(jnp.where(gl == 3 * gr + j, gate_row, 0.0), axis=1, keepdims=True)

    o = gate(0) * oc_ref[...] + gate(1) * o_s + gate(2) * o_w
    o_ref[...] = o * _silu(az_ref[...])


def nsa_sample_sel(sel, page_table, slc_k_t, slc_v_t, win_k_t, win_v_t, layer, kvt, kvs, q3, az3, ux, o_cmp, p_len):
    nb = page_table.shape[0]
    n_buf = win_k_t.shape[-1]
    last_blk = p_len // SLC_BLK - 1

    def page_spec(k):
        def imap(b, g, sel_r, pt_r):
            blk = jnp.minimum(sel_r[b, g, k], last_blk)
            return (layer, pt_r[b, blk // (PAGE // SLC_BLK)], g, 0, 0)
        return pl.BlockSpec((None, None, None, HD, PAGE), imap)

    win_spec = pl.BlockSpec((None, None, None, HD, n_buf), lambda b, g, s, p: (layer, b, g, 0, 0))
    wout_spec = pl.BlockSpec((None, None, HD, n_buf), lambda b, g, s, p: (b, g, 0, 0))
    row_spec = pl.BlockSpec((None, None, NSA_REP, HD), lambda b, g, s, p: (b, g, 0, 0))

    def full(a):
        return pl.BlockSpec(a.shape, lambda b, g, s, p: (0,) * a.ndim)

    small = pl.BlockSpec((nb, LANE), lambda b, g, s, p: (0, OFF_SMALL // LANE))
    return pl.pallas_call(
        functools.partial(_nsa_sample_sel_kernel, p_len=p_len, nb=nb, n_buf=n_buf),
        grid_spec=pltpu.PrefetchScalarGridSpec(
            num_scalar_prefetch=2,
            grid=(nb, NSA_KV),
            in_specs=[page_spec(k) for k in range(SLC_TOPK)] * 2
                     + [win_spec, win_spec, full(kvt), pl.BlockSpec((None, 6 * NSA_KV, HD), lambda b, g, s, p: (b, 0, 0)),
                        row_spec, row_spec, small, row_spec],
            out_specs=[row_spec, wout_spec, wout_spec]),
        out_shape=[jax.ShapeDtypeStruct((nb, NSA_KV, NSA_REP, HD), F32),
                   jax.ShapeDtypeStruct((nb, NSA_KV, HD, n_buf), F32),
                   jax.ShapeDtypeStruct((nb, NSA_KV, HD, n_buf), F32)],
        compiler_params=_cparams(("parallel", "arbitrary")),
        name="nsa_sample_sel",
    )(sel, page_table, *([slc_k_t] * SLC_TOPK), *([slc_v_t] * SLC_TOPK), win_k_t, win_v_t, kvt, kvs, q3, az3, ux, o_cmp)


def _prep_weights(w_in, cmp_pe_k, cmp_w1_k, cmp_w2_k, cmp_pe_v, cmp_w1_v, cmp_w2_v, w_br_a, w_br_b, w_br_c, w_out):
    wt = jnp.transpose(w_in, (0, 2, 1))

    def gather_rows(order, total):
        parts = [wt[:, _IN_OFF[n]:_IN_OFF[n] + _IN_SZ[n]] for n in order]
        used = sum(_IN_SZ[n] for n in order)
        parts.append(jnp.zeros((wt.shape[0], total - used, wt.shape[2]), wt.dtype))
        return jnp.concatenate(parts, axis=1).astype(BF16)

    half = CMP_STRIDE * HD

    def cat(w1):
        return jnp.concatenate([w1[:, :half], w1[:, half:]], axis=2)

    def pe_rows(pe):
        d = pe.shape[0]
        rows = pe.reshape(d, 2, half)
        return jnp.concatenate([rows, jnp.zeros((d, 6, half), pe.dtype)], axis=1)

    w1cat = jnp.stack([cat(cmp_w1_k), cat(cmp_w1_v)], axis=1)
    pe2 = jnp.stack([pe_rows(cmp_pe_k), pe_rows(cmp_pe_v)], axis=1)
    w2 = jnp.stack([cmp_w2_k, cmp_w2_v], axis=1)
    return dict(wt_ug=gather_rows(_UG_ORDER, NP_UG), wt_ux=gather_rows(_UX_ORDER, NP_UX), wt_kv=gather_rows(("a_kv",), KV_ROWS),
                wt_q=gather_rows(_TQ_ORDER, TQ_ROWS),
                w1cat=w1cat, pe2=pe2, w2=w2,
                wa=w_br_a.astype(BF16), wb=w_br_b.astype(BF16), wc=w_br_c.astype(BF16), wo=w_out.astype(BF16))


def _prompt_layer(x, h, nb, t, layer, w, p, next_norm, tiles, stacks):
    ug = matmul(h, w["wt_ug"], nt=True, tm=tiles["mm_tm"], tn=tiles["in_tn"], out_dtype=BF16, name="in_proj",
                b_layer=layer)
    ux = matmul(h, w["wt_ux"], nt=True, tm=tiles["mm_tm"], tn=NP_UX, name="in_proj_x", b_layer=layer)
    stacks = kv_proj_stacked(w["wt_kv"], layer, h, nb, t, stacks, tn=tiles["kv_tn"])
    qgt = matmul_t(w["wt_q"], layer, h, nb, t, tm=TQ_TM, tn=tiles["q_tn"], name="q_proj_t")
    nseg = t // CMP_STRIDE
    seg = jnp.stack([stacks[0][layer], stacks[1][layer]]).reshape(2, nb, NSA_KV, HD, t)
    seg = jnp.transpose(seg, (0, 1, 2, 4, 3)).reshape(2, nb * NSA_KV * nseg, CMP_STRIDE * HD)
    kc = compress_blocks(seg, w["w1cat"][layer], w["pe2"][layer], w["w2"][layer], nseg)
    o_a = nsa_prompt(ug, qgt, stacks, layer, kc, nb, t, tq=tiles["tq"], ck=tiles["ck"])
    zeros_gla = jnp.zeros((nb, GLA_HEADS, GLA_HK, GLA_HV), F32)
    zeros_ssm = jnp.zeros((nb, M_HEADS, M_HD, M_STATE), F32)
    zeros_conv = jnp.zeros((nb, M_CONV - 1, M_CONVDIM), F32)
    o_b, gla_s = gla_mixer(ug, ux, zeros_gla, p["gla_wa2"], p["gla_ba"], p["gla_norm"], nb, t,
                           c=tiles["gla_c"], sub=min(tiles["gla_sub"], t // tiles["gla_c"]), t_valid=t)
    o_c, ssm_s, conv_s = ssd_mixer(ug, ux, zeros_ssm, zeros_conv, p["conv_w"], p["conv_b"], p["dt_bias"], p["a_log"],
                                   p["d_skip"], p["m_norm"], nb, t, q=min(tiles["ssd_q"], t), t_valid=t)
    mix = merge_branches(o_a, o_b, o_c, w["wa"], w["wb"], w["wc"], layer, ug, tm=tiles["mg_tm"], tn=tiles["mg_tn"])
    x_out, h_out = out_proj_norm(mix, w["wo"], layer, x, *next_norm, tm=tiles["out_tm"])
    return x_out, h_out, stacks, (gla_s, ssm_s, conv_s)


ROW_PAD = 8


def _sample_layer(x, h, layer, w, p, next_norm, caches, states, page_table, tiles):
    nb = x.shape[0]
    p_len = page_table.shape[1] * PAGE
    ck_t, cv_t, sk_t, sv_t, wk_t, wv_t = caches
    ug = matmul(h, w["wt_ug"], nt=True, tm=nb, tn=tiles["in_tn"], out_dtype=BF16, name="in_proj_s", b_layer=layer)
    ux = matmul(h, w["wt_ux"], nt=True, tm=nb, tn=NP_UX, name="in_proj_xs", b_layer=layer)
    kv = matmul(h, w["wt_kv"], nt=True, tm=nb, tn=512, name="kv_proj_s", b_layer=layer)
    kvt = matmul(w["wt_kv"], h, nt=True, tm=512, tn=nb, name="kv_proj_st", a_layer=layer)
    qg = matmul(h, w["wt_q"], nt=True, tm=nb, tn=TQ_TM, name="q_proj_s", b_layer=layer)
    kvs = kv.reshape(nb, 6 * NSA_KV, HD)
    def pad_rows(a):
        return jnp.pad(a[:, None, :], ((0, 0), (0, ROW_PAD - 1), (0, 0))).reshape(nb * ROW_PAD, a.shape[1])

    ug_pad, ux_pad = pad_rows(ug), pad_rows(ux)
    o_b, gla_s = gla_mixer(ug_pad, ux_pad, states[0], p["gla_wa2"], p["gla_ba"], p["gla_norm"], nb, ROW_PAD,
                           c=ROW_PAD, sub=1, t_valid=1)
    o_c, ssm_s, conv_s = ssd_mixer(ug_pad, ux_pad, states[1], states[2], p["conv_w"], p["conv_b"], p["dt_bias"], p["a_log"],
                                   p["d_skip"], p["m_norm"], nb, ROW_PAD, q=ROW_PAD, t_valid=1)
    o_b = o_b.reshape(nb, ROW_PAD, GLA_DV)[:, 0]
    o_c = o_c.reshape(nb, ROW_PAD, M_DINNER)[:, 0]
    abk, abv = page_compress(ck_t, cv_t, page_table, layer, w["w1cat"], npg=min(tiles["npg"], page_table.shape[1]))
    q3 = qg[:, :TQ_AG].reshape(nb, NSA_KV, NSA_REP, HD)
    az3 = ug[:, _UG_OFF["a_z"]:_UG_OFF["a_z"] + MIX_W].astype(F32).reshape(nb, NSA_KV, NSA_REP, HD)
    o_cmp, sel = nsa_sample_cmp(abk, abv, kvs, q3, w["w1cat"][layer], w["pe2"][layer], w["w2"][layer], p_len)
    o_a, win_k, win_v = nsa_sample_sel(sel.reshape(nb, NSA_KV, LANE), page_table, sk_t, sv_t, wk_t, wv_t, layer,
                                       kvt, kvs, q3, az3, ux, o_cmp, p_len)
    mix = merge_branches(o_a.reshape(nb, MIX_W).astype(BF16), o_b, o_c, w["wa"], w["wb"], w["wc"], layer, ug,
                         tm=nb, tn=tiles["mg_tn"])
    x_out, h_out = out_proj_norm(mix, w["wo"], layer, x, *next_norm, tm=nb)
    kv4 = kv.reshape(nb, 6, 1, NSA_KV, HD)
    return x_out, h_out, (kv4[:, 0], kv4[:, 1], kv4[:, 2], kv4[:, 3], win_k, win_v, gla_s, ssm_s, conv_s)


_TILES = dict(rms_tm=256, mm_tm=1024, in_tn=1024, kv_tn=512, q_tn=2048, tq=512, ck=512, gla_c=64, gla_sub=4,
              ssd_q=256, mg_tm=1024, mg_tn=512, out_tm=512, npg=32)
_PARAM_NAMES = ("ln_w", "gla_wa2", "gla_ba", "gla_norm", "conv_w", "conv_b", "dt_bias", "a_log", "d_skip", "m_norm")


def kernel(x_prompt, x_sample, cache_cmp_k, cache_cmp_v, cache_slc_k, cache_slc_v, cache_win_k, cache_win_v,
           state_gla, state_ssm, state_conv, page_table, ln_w, w_in, cmp_pe_k, cmp_w1_k, cmp_w2_k, cmp_pe_v,
           cmp_w1_v, cmp_w2_v, gla_wa2, gla_ba, gla_norm, conv_w, conv_b, dt_bias, a_log, d_skip, m_norm,
           w_br_a, w_br_b, w_br_c, w_out, final_norm):
    nbp, t, d = x_prompt.shape
    nbs = x_sample.shape[0]
    depth = w_in.shape[0]
    weights = _prep_weights(w_in, cmp_pe_k, cmp_w1_k, cmp_w2_k, cmp_pe_v, cmp_w1_v, cmp_w2_v,
                            w_br_a, w_br_b, w_br_c, w_out)
    params = dict(zip(_PARAM_NAMES, (ln_w, gla_wa2, gla_ba, gla_norm, conv_w, conv_b, dt_bias, a_log, d_skip, m_norm)))
    caches = tuple(jnp.transpose(c, (0, 1, 3, 4, 2))
                   for c in (cache_cmp_k, cache_cmp_v, cache_slc_k, cache_slc_v, cache_win_k, cache_win_v))
    xp = x_prompt.reshape(nbp * t, d)
    xs = x_sample.reshape(nbs, d)
    p_states, s_out = [], []
    stacks = [jnp.zeros((depth, nbp, GRP_ROWS, t), F32) for _ in range(6)]
    hp = rmsnorm_rows(xp, ln_w[0], BF16, _TILES["rms_tm"])
    hs = rmsnorm_rows(xs, ln_w[0], BF16, nbs)
    for l in range(depth):
        p = {k: v[l] for k, v in params.items()}
        next_norm = (ln_w[l + 1], BF16) if l + 1 < depth else (final_norm, F32)
        xp, hp, stacks, st_p = _prompt_layer(xp, hp, nbp, t, l, weights, p, next_norm, _TILES, stacks)
        xs, hs, st_s = _sample_layer(xs, hs, l, weights, p, next_norm, caches,
                                     (state_gla[l], state_ssm[l], state_conv[l]), page_table, _TILES)
        p_states.append(st_p)
        s_out.append(st_s)
    y_prompt = hp.reshape(nbp, t, d)
    y_sample = hs.reshape(nbs, 1, d)

    def from_kv_layout(a):
        return jnp.transpose(a, (0, 1, 4, 2, 3))

    n_w = min(WINDOW, t)
    kv5 = [a.reshape(depth, nbp, NSA_KV, HD, t) for a in stacks]
    p_leaves = [from_kv_layout(a) for a in kv5[:4]] + [from_kv_layout(a[..., t - n_w:]) for a in kv5[4:]]
    p_leaves += [jnp.stack([st[i] for st in p_states]) for i in range(3)]
    s_leaves = [jnp.stack([o[i] for o in s_out]) for i in range(9)]
    s_leaves[4], s_leaves[5] = from_kv_layout(s_leaves[4]), from_kv_layout(s_leaves[5])
    return (y_prompt, y_sample, *p_leaves, *s_leaves)
```

```python
import functools
import math

import jax
import jax.numpy as jnp
import numpy as np
from jax import lax
from jax.experimental import pallas as pl
from jax.experimental.pallas import tpu as pltpu

F32 = jnp.float32
BF16 = jnp.bfloat16

D_MODEL = 2048
PAGE = 128
MIX_W = D_MODEL // 2
HD = 64
NSA_HEADS = MIX_W // HD
NSA_KV = 4
NSA_REP = NSA_HEADS // NSA_KV
CMP_LEN = 32
CMP_STRIDE = 16
SLC_BLK = 64
SLC_TOPK = 16
WINDOW = 512
GLA_HEADS = 4
GLA_DK = MIX_W // 2
GLA_DV = MIX_W
GLA_HK = GLA_DK // GLA_HEADS
GLA_HV = GLA_DV // GLA_HEADS
GLA_RANK = 16
GLA_TAU = 16.0
M_DINNER = MIX_W
M_HD = 64
M_HEADS = M_DINNER // M_HD
M_GROUPS = 4
M_STATE = 128
M_CONV = 4
M_CONVDIM = M_DINNER + 2 * M_GROUPS * M_STATE
EPS = 1e-6
NEG = -1e30
FORCE = 1e4
LOG2E = math.log2(math.e)
WIN_Q = 128
ONES_ROWS = 16

_IN_NAMES = ("a_q", "a_kv", "a_g", "a_z", "b_q", "b_k", "b_v", "b_a", "b_z", "c_xbc", "c_dt", "c_z", "m_g")
_IN_SIZES = (NSA_HEADS * HD, 6 * NSA_KV * HD, 3 * NSA_HEADS, MIX_W, GLA_DK, GLA_DK, GLA_DV, GLA_RANK, GLA_DV,
             M_CONVDIM, M_HEADS, M_DINNER, 3 * D_MODEL)
_IN_OFF = dict(zip(_IN_NAMES, np.cumsum((0,) + _IN_SIZES)[:-1].tolist()))
_IN_SZ = dict(zip(_IN_NAMES, _IN_SIZES))

LANE = 128
_UG_ORDER = ("m_g", "a_z", "b_v", "b_z", "c_z", "b_q", "b_k")
_UX_ORDER = ("c_xbc", "a_g", "b_a", "c_dt")


def _offsets(order):
    off, o = {}, 0
    for n in order:
        off[n] = o
        o += _IN_SZ[n]
    return off, o


_UG_OFF, NP_UG = _offsets(_UG_ORDER)
_UX_OFF, _ux_used = _offsets(_UX_ORDER)
NP_UX = -(-_ux_used // LANE) * LANE
OFF_SMALL = _UX_OFF["a_g"]
SM_AG, SM_BA, SM_DT = 0, 3 * NSA_HEADS, 3 * NSA_HEADS + GLA_RANK
KV_ROWS = _IN_SZ["a_kv"]
GRP_ROWS = NSA_KV * HD
_TQ_ORDER = ("a_q", "a_g")
TQ_AG = _IN_SZ["a_q"]
TQ_TM = 384
TQ_ROWS = -(-(TQ_AG + _IN_SZ["a_g"]) // TQ_TM) * TQ_TM
VMEM_LIMIT = 56 * 1024 * 1024


def _cparams(sem):
    return pltpu.CompilerParams(dimension_semantics=sem, vmem_limit_bytes=VMEM_LIMIT)


def _dot(a, b, prec=None, nt=False, tn=False):
    if prec is None:
        a, b = a.astype(BF16), b.astype(BF16)
    dn = (((0,) if tn else (1,), (1,) if nt else (0,)), ((), ()))
    return lax.dot_general(a, b, dn, precision=prec, preferred_element_type=F32)


def _split(a):
    hi = a.astype(BF16)
    return hi, (a - hi.astype(F32)).astype(BF16)


def _dot_exact_rhs(a, b_bf16):
    hi, lo = _split(a)
    return _dot(hi, b_bf16) + _dot(lo, b_bf16)


def _dot_exact_lhs(a_bf16, b):
    hi, lo = _split(b)
    return _dot(a_bf16, hi) + _dot(a_bf16, lo)


def _dot3(a, b, **kw):
    ah, al = _split(a)
    bh, bl = _split(b)
    return _dot(ah, bh, **kw) + _dot(ah, bl, **kw) + _dot(al, bh, **kw)


def _sigmoid(x):
    return 0.5 * jnp.tanh(0.5 * x) + 0.5


def _silu(x):
    return x * _sigmoid(x)


def _log_sigmoid(x):
    return jnp.minimum(x, 0.0) - jnp.log(1.0 + jnp.exp(-jnp.abs(x)))


def _softplus(x):
    return jnp.maximum(x, 0.0) + jnp.log(1.0 + jnp.exp(-jnp.abs(x)))


def _stacked_spec(block, imap, layer):
    if layer is None:
        return pl.BlockSpec(block, imap)
    return pl.BlockSpec((None,) + block, lambda *idx: (layer,) + imap(*idx))


def _rms_kernel(x_ref, w_ref, o_ref):
    x = x_ref[...]
    y = x * lax.rsqrt(jnp.mean(x * x, axis=-1, keepdims=True) + EPS)
    o_ref[...] = (y * w_ref[...]).astype(o_ref.dtype)


def rmsnorm_rows(x, w, out_dtype, tm):
    m, d = x.shape
    return pl.pallas_call(
        _rms_kernel,
        grid=(m // tm,),
        in_specs=[pl.BlockSpec((tm, d), lambda i: (i, 0)), pl.BlockSpec((1, d), lambda i: (0, 0))],
        out_specs=pl.BlockSpec((tm, d), lambda i: (i, 0)),
        out_shape=jax.ShapeDtypeStruct((m, d), out_dtype),
        compiler_params=_cparams(("parallel",)),
        name="rmsnorm",
    )(x, w.reshape(1, d))


def _mm_kernel(*refs, nt, has_res):
    a_ref, b_ref, o_ref = refs[0], refs[1], refs[-1]
    acc = _dot(a_ref[...], b_ref[...], nt=nt)
    if has_res:
        acc = acc + refs[2][...]
    o_ref[...] = acc.astype(o_ref.dtype)


def matmul(a, b, *, nt, tm, tn, res=None, out_dtype=F32, name="matmul", a_layer=None, b_layer=None):
    m, k = a.shape[-2:]
    n = b.shape[-2] if nt else b.shape[-1]
    tm, tn = min(tm, m), min(tn, n)
    if nt:
        b_spec = _stacked_spec((tn, k), lambda i, j: (j, 0), b_layer)
    else:
        b_spec = _stacked_spec((k, tn), lambda i, j: (0, j), b_layer)
    in_specs = [_stacked_spec((tm, k), lambda i, j: (i, 0), a_layer), b_spec]
    args = [a, b]
    if res is not None:
        in_specs.append(pl.BlockSpec((tm, tn), lambda i, j: (i, j)))
        args.append(res)
    return pl.pallas_call(
        functools.partial(_mm_kernel, nt=nt, has_res=res is not None),
        grid=(m // tm, n // tn),
        in_specs=in_specs,
        out_specs=pl.BlockSpec((tm, tn), lambda i, j: (i, j)),
        out_shape=jax.ShapeDtypeStruct((m, n), out_dtype),
        compiler_params=_cparams(("parallel", "parallel")),
        name=name,
    )(*args)


def matmul_t(wt, layer, h, nb, t, *, tm, tn, name):
    _, r, k = wt.shape
    tn = min(tn, t)
    nj = t // tn
    return pl.pallas_call(
        functools.partial(_mm_kernel, nt=True, has_res=False),
        grid=(nb, nj, r // tm),
        in_specs=[pl.BlockSpec((None, tm, k), lambda b, j, i: (layer, i, 0)),
                  pl.BlockSpec((tn, k), lambda b, j, i: (b * nj + j, 0))],
        out_specs=pl.BlockSpec((None, tm, tn), lambda b, j, i: (b, i, j)),
        out_shape=jax.ShapeDtypeStruct((nb, r, t), F32),
        compiler_params=_cparams(("parallel", "parallel", "parallel")),
        name=name,
    )(wt, h)


def _kv_stack_kernel(w_ref, h_ref, *refs):
    outs = refs[-6:]
    hb = h_ref[...]
    for i in range(6):
        outs[i][...] = _dot(w_ref[i * GRP_ROWS:(i + 1) * GRP_ROWS, :], hb, nt=True)


def kv_proj_stacked(wt_kv, layer, h, nb, t, stacks, *, tn):
    depth, _, k = wt_kv.shape
    tn = min(tn, t)
    nj = t // tn
    in_specs = [pl.BlockSpec((None, KV_ROWS, k), lambda b, j: (layer, 0, 0)),
                pl.BlockSpec((tn, k), lambda b, j: (b * nj + j, 0))] + [pl.BlockSpec(memory_space=pl.ANY)] * 6
    args = [wt_kv, h] + list(stacks)
    aliases = {2 + i: i for i in range(6)}
    o_spec = pl.BlockSpec((None, None, GRP_ROWS, tn), lambda b, j: (layer, b, 0, j))
    o_shape = jax.ShapeDtypeStruct((depth, nb, GRP_ROWS, t), F32)
    return pl.pallas_call(
        _kv_stack_kernel,
        grid=(nb, nj),
        in_specs=in_specs,
        out_specs=[o_spec] * 6,
        out_shape=[o_shape] * 6,
        input_output_aliases=aliases,
        compiler_params=_cparams(("parallel", "parallel")),
        name="kv_proj_t",
    )(*args)


def _out_norm_kernel(a_ref, w_ref, res_ref, nw_ref, x_ref, h_ref):
    x = res_ref[...] + _dot(a_ref[...], w_ref[...])
    x_ref[...] = x
    y = x * lax.rsqrt(jnp.mean(x * x, axis=-1, keepdims=True) + EPS)
    h_ref[...] = (y * nw_ref[...]).astype(h_ref.dtype)


def out_proj_norm(mix, wo, layer, res, norm_w, h_dtype, *, tm):
    m, k = mix.shape
    n = wo.shape[-1]
    tm = min(tm, m)
    row = lambda i: (i, 0)
    return pl.pallas_call(
        _out_norm_kernel,
        grid=(m // tm,),
        in_specs=[pl.BlockSpec((tm, k), row), pl.BlockSpec((None, k, n), lambda i: (layer, 0, 0)),
                  pl.BlockSpec((tm, n), row), pl.BlockSpec((1, n), lambda i: (0, 0))],
        out_specs=[pl.BlockSpec((tm, n), row), pl.BlockSpec((tm, n), row)],
        out_shape=[jax.ShapeDtypeStruct((m, n), F32), jax.ShapeDtypeStruct((m, n), h_dtype)],
        compiler_params=_cparams(("parallel",)),
        name="out_proj",
    )(mix, wo, res, norm_w.reshape(1, n))


def _merge_kernel(oa_ref, ob_ref, oc_ref, wa_ref, wb_ref, wc_ref, g0_ref, g1_ref, g2_ref, o_ref):
    acc = _sigmoid(g0_ref[...].astype(F32)) * _dot(oa_ref[...], wa_ref[...])
    acc = acc + _sigmoid(g1_ref[...].astype(F32)) * _dot(ob_ref[...], wb_ref[...])
    acc = acc + _sigmoid(g2_ref[...].astype(F32)) * _dot(oc_ref[...], wc_ref[...])
    o_ref[...] = acc.astype(o_ref.dtype)


def merge_branches(o_a, o_b, o_c, wa, wb, wc, layer, ug, *, tm, tn):
    m, k = o_a.shape
    n = wa.shape[-1]
    tm = min(tm, m)
    gb = _UG_OFF["m_g"] // tn
    nj = n // tn
    o_spec = pl.BlockSpec((tm, k), lambda i, j: (i, 0))
    w_spec = pl.BlockSpec((None, k, tn), lambda i, j: (layer, 0, j))
    g_specs = [pl.BlockSpec((tm, tn), functools.partial(lambda i, j, br: (i, gb + br * nj + j), br=br))
               for br in range(3)]
    return pl.pallas_call(
        _merge_kernel,
        grid=(m // tm, nj),
        in_specs=[o_spec, o_spec, o_spec, w_spec, w_spec, w_spec] + g_specs,
        out_specs=pl.BlockSpec((tm, tn), lambda i, j: (i, j)),
        out_shape=jax.ShapeDtypeStruct((m, n), BF16),
        compiler_params=_cparams(("parallel", "parallel")),
        name="merge",
    )(o_a, o_b, o_c, wa, wb, wc, ug, ug, ug)


def _cmp_kernel(seg_ref, w1_ref, pe_ref, w2_ref, o_ref, *, nseg):
    w1 = w1_ref[...]
    ab = _dot3(seg_ref[...], w1)
    pe = _dot3(pe_ref[...], w1)
    cst = pe[0:1, :HD] + pe[1:2, HD:]
    nxt = pltpu.roll(ab[:, HD:], shift=nseg - 1, axis=0)
    pre = ab[:, :HD] + nxt + cst
    o_ref[...] = _dot3(_silu(pre), w2_ref[...])


def compress_blocks(seg, w1cat, pe2, w2, nseg):
    _, r, kdim = seg.shape
    return pl.pallas_call(
        functools.partial(_cmp_kernel, nseg=nseg),
        grid=(2, r // nseg),
        in_specs=[pl.BlockSpec((None, nseg, kdim), lambda s, i: (s, i, 0)),
                  pl.BlockSpec((None, kdim, 2 * HD), lambda s, i: (s, 0, 0)),
                  pl.BlockSpec((None, 8, kdim), lambda s, i: (s, 0, 0)),
                  pl.BlockSpec((None, HD, HD), lambda s, i: (s, 0, 0))],
        out_specs=pl.BlockSpec((None, nseg, HD), lambda s, i: (s, i, 0)),
        out_shape=jax.ShapeDtypeStruct((2, r, HD), F32),
        compiler_params=_cparams(("parallel", "parallel")),
        name="compress",
    )(seg, w1cat, pe2, w2)


def _nsa_prompt_kernel(qt_ref, ag_ref, az_ref, kst_ref, vst_ref, kwt_ref, vwt_ref, kc_ref, vc_ref, o_ref,
                       ks_scr, vs_scr, kw_scr, vw_scr, bias_scr, *, t, tq, ck, nseg):
    g = pl.program_id(1)
    i = pl.program_id(2)
    s0 = i * tq
    ns = t // SLC_BLK
    nsr = bias_scr.shape[0]
    cols = NSA_REP * tq

    @pl.when(i == 0)
    def _():
        ks_scr[...] = jnp.transpose(kst_ref[...]).astype(BF16)
        kw_scr[...] = jnp.transpose(kwt_ref[...]).astype(BF16)
        ones = jnp.ones((ONES_ROWS, t), BF16)
        vs_scr[...] = jnp.concatenate([vst_ref[...].astype(BF16), ones], axis=0)
        vw_scr[...] = jnp.concatenate([vwt_ref[...].astype(BF16), ones], axis=0)

    qt = qt_ref[...]
    q2 = jnp.concatenate([qt[r * HD:(r + 1) * HD, :] for r in range(NSA_REP)], axis=1) * (HD ** -0.5 * LOG2E)
    q2b = q2.astype(BF16)
    tpos1 = s0 + lax.broadcasted_iota(jnp.int32, (1, tq), 1)
    tpos = jnp.concatenate([tpos1] * NSA_REP, axis=1)

    crow = lax.broadcasted_iota(jnp.int32, (nseg, 1), 0)
    m_c = (crow * CMP_STRIDE + (CMP_LEN - 1) <= tpos) & (crow < nseg - 1)
    s_c = jnp.where(m_c, _dot3(kc_ref[...], q2), NEG)
    p_c = jnp.where(m_c, jnp.exp2(s_c - jnp.max(s_c, axis=0, keepdims=True)), 0.0)
    l_c = jnp.sum(p_c, axis=0, keepdims=True)
    p_c = p_c / jnp.where(l_c > 0.0, l_c, 1.0)
    o_c = _dot(jnp.transpose(vc_ref[...]), p_c)
    psum = p_c[:, 0:tq]
    for r in range(1, NSA_REP):
        psum = psum + p_c[:, r * tq:(r + 1) * tq]

    srow = lax.broadcasted_iota(jnp.int32, (nsr, 1), 0)
    ccol = lax.broadcasted_iota(jnp.int32, (1, nseg), 1) * CMP_STRIDE
    cover = jnp.where((ccol + CMP_LEN > srow * SLC_BLK) & (ccol < (srow + 1) * SLC_BLK), 1.0, 0.0).astype(BF16)
    imp = _dot_exact_lhs(cover, psum)
    cur = tpos1 // SLC_BLK
    valid = (srow * SLC_BLK <= tpos1) & (srow < ns)
    forced = (srow == 0) | (srow == cur) | (srow == cur - 1)
    score = jnp.where(valid, jnp.where(forced, FORCE, imp), NEG)
    rank = jnp.zeros((nsr, tq), F32)
    for j in range(ns):
        sj = score[j:j + 1, :]
        rank = rank + jnp.where((sj > score) | ((sj == score) & (j < srow)), 1.0, 0.0)
    sel = (rank < float(min(SLC_TOPK, ns))) & valid
    bias_scr[...] = jnp.where(sel, 0.0, NEG)

    def attend(k_rows, v_cols, bias1, m, acc, q_cols=q2b):
        w = bias1.shape[1]
        s = _dot(k_rows, q_cols)
        s = jnp.concatenate([s[:, r * w:(r + 1) * w] + bias1 for r in range(NSA_REP)], axis=1)
        m_new = jnp.maximum(m, jnp.max(s, axis=0, keepdims=True))
        p = jnp.exp2(s - m_new)
        return m_new, jnp.exp2(m - m_new) * acc + _dot(v_cols, p)

    def finish(acc):
        l = acc[HD:HD + 1, :]
        return acc[:HD, :] / jnp.where(l > 0.0, l, 1.0)

    blk_per_chunk = ck // SLC_BLK
    m0 = jnp.full((1, cols), NEG, F32)
    acc0 = jnp.zeros((HD + ONES_ROWS, cols), F32)

    def sel_chunk(j, carry):
        k0 = pl.multiple_of(j * ck, ck)
        npos = k0 + lax.broadcasted_iota(jnp.int32, (ck, 1), 0)
        rows = [jnp.broadcast_to(bias_scr[pl.ds(j * blk_per_chunk + k, 1), :], (SLC_BLK, tq))
                for k in range(blk_per_chunk)]
        bias1 = jnp.where(npos <= tpos1, jnp.concatenate(rows, axis=0), NEG)
        return attend(ks_scr[pl.ds(k0, ck), :], vs_scr[:, pl.ds(k0, ck)], bias1, *carry)

    n_chunks = (s0 + tq + ck - 1) // ck
    o_s = finish(lax.fori_loop(0, n_chunks, sel_chunk, (m0, acc0))[1])

    wq = min(WIN_Q, tq)
    ww = min(WINDOW + wq, t)
    parts = []
    for a in range(tq // wq):
        w0 = pl.multiple_of(jnp.clip(s0 + a * wq - WINDOW, 0, t - ww), LANE)
        wpos = w0 + lax.broadcasted_iota(jnp.int32, (ww, 1), 0)
        tsub = tpos1[:, a * wq:(a + 1) * wq]
        bias_w = jnp.where((wpos <= tsub) & (wpos > tsub - WINDOW), 0.0, NEG)
        q_sub = jnp.concatenate([q2b[:, r * tq + a * wq:r * tq + (a + 1) * wq] for r in range(NSA_REP)], axis=1)
        parts.append(finish(attend(kw_scr[pl.ds(w0, ww), :], vw_scr[:, pl.ds(w0, ww)], bias_w,
                                   m0[:, :NSA_REP * wq], acc0[:, :NSA_REP * wq], q_sub)[1]))
    o_w = jnp.concatenate([parts[a][:, r * wq:(r + 1) * wq] for r in range(NSA_REP) for a in range(tq // wq)], axis=1)

    outs = []
    for r in range(NSA_REP):
        sl = slice(r * tq, (r + 1) * tq)
        gt = [_sigmoid(ag_ref[pl.ds(3 * (NSA_REP * g + r) + j, 1), :]) for j in range(3)]
        outs.append(jnp.transpose(gt[0] * o_c[:, sl] + gt[1] * o_s[:, sl] + gt[2] * o_w[:, sl]))
    o = jnp.concatenate(outs, axis=1) * _silu(az_ref[...].astype(F32))
    o_ref[...] = o.astype(o_ref.dtype)


def nsa_prompt(ug, qgt, stacks, layer, kc, nb, t, *, tq, ck):
    tq = min(tq, t)
    nq = t // tq
    nseg = t // CMP_STRIDE
    ck = min(ck, t)
    gw = NSA_REP * HD
    nsr = -(-(t // SLC_BLK) // 8) * 8
    kv_spec = pl.BlockSpec((None, None, HD, t), lambda b, g, i: (layer, b, g, 0))
    return pl.pallas_call(
        functools.partial(_nsa_prompt_kernel, t=t, tq=tq, ck=ck, nseg=nseg),
        grid=(nb, NSA_KV, nq),
        in_specs=[pl.BlockSpec((None, gw, tq), lambda b, g, i: (b, g, i)),
                  pl.BlockSpec((None, LANE, tq), lambda b, g, i: (b, TQ_AG // LANE, i)),
                  pl.BlockSpec((tq, gw), lambda b, g, i: (b * nq + i, _UG_OFF["a_z"] // gw + g)),
                  kv_spec, kv_spec, kv_spec, kv_spec,
                  pl.BlockSpec((None, nseg, HD), lambda b, g, i: (0, b * NSA_KV + g, 0)),
                  pl.BlockSpec((None, nseg, HD), lambda b, g, i: (1, b * NSA_KV + g, 0))],
        out_specs=pl.BlockSpec((tq, gw), lambda b, g, i: (b * nq + i, g)),
        out_shape=jax.ShapeDtypeStruct((nb * t, NSA_HEADS * HD), BF16),
        scratch_shapes=[pltpu.VMEM((t, HD), BF16), pltpu.VMEM((HD + ONES_ROWS, t), BF16),
                        pltpu.VMEM((t, HD), BF16), pltpu.VMEM((HD + ONES_ROWS, t), BF16),
                        pltpu.VMEM((nsr, tq), F32)],
        compiler_params=_cparams(("parallel", "parallel", "arbitrary")),
        name="nsa_prompt",
    )(qgt, qgt, ug, stacks[2], stacks[3], stacks[4], stacks[5], kc, kc)


def _gla_kernel(q_ref, k_ref, v_ref, z_ref, sm_ref, s0_ref, wa2_ref, ba_ref, gn_ref, o_ref, sout_ref, s_scr,
                *, c, sub, t_valid):
    ci = pl.program_id(1)
    rows = c * sub

    @pl.when(ci == 0)
    def _():
        s_scr[...] = s0_ref[...]

    row = ci * rows + lax.broadcasted_iota(jnp.int32, (rows, 1), 0)
    gate_in = sm_ref[...][:, SM_BA:SM_BA + GLA_RANK]
    log_a = _log_sigmoid(_dot3(gate_in, wa2_ref[...]) + ba_ref[...]) / GLA_TAU
    log_a = jnp.where(row < t_valid, log_a, 0.0)
    causal = lax.broadcasted_iota(jnp.int32, (c, c), 0) >= lax.broadcasted_iota(jnp.int32, (c, c), 1)
    tri = jnp.where(causal, 1.0, 0.0).astype(BF16)
    cb_all = jnp.concatenate([_dot_exact_lhs(tri, log_a[j * c:(j + 1) * c, :]) for j in range(sub)], axis=0)
    for h in range(GLA_HEADS):
        ks = slice(h * GLA_HK, (h + 1) * GLA_HK)
        vs = slice(h * GLA_HV, (h + 1) * GLA_HV)
        state = s_scr[h]
        outs = []
        for j in range(sub):
            rs = slice(j * c, (j + 1) * c)
            cb = cb_all[rs, ks]
            ecb = jnp.exp(cb)
            qd = q_ref[rs, ks].astype(F32) * (GLA_HK ** -0.5) * ecb
            kh = k_ref[rs, ks].astype(F32)
            vh = v_ref[rs, vs]
            att = jnp.where(causal, _dot(qd, kh * jnp.exp(-cb), nt=True), 0.0)
            outs.append(_dot(att, vh) + _dot(qd, state))
            c_end = cb[c - 1:c, :]
            k_dec = kh * jnp.exp(c_end - cb)
            e_col = jnp.transpose(ecb[c - 8:c, :])[:, 7:8]
            state = e_col * state + _dot(k_dec, vh, tn=True)
        o = jnp.concatenate(outs, axis=0)
        y = o * lax.rsqrt(jnp.mean(o * o, axis=-1, keepdims=True) + EPS) * gn_ref[...]
        o_ref[:, vs] = (y * _silu(z_ref[:, vs].astype(F32))).astype(o_ref.dtype)
        s_scr[h] = state

    @pl.when(ci == pl.num_programs(1) - 1)
    def _():
        sout_ref[...] = s_scr[...]


def gla_mixer(ug, ux, s0, wa2, ba, gnorm, nb, t, *, c, sub, t_valid):
    rows = c * sub
    nc = t // rows

    def u_spec(name, width):
        return pl.BlockSpec((rows, width), lambda b, i: (b * nc + i, _UG_OFF[name] // width))

    s_spec = pl.BlockSpec((None, GLA_HEADS, GLA_HK, GLA_HV), lambda b, i: (b, 0, 0, 0))
    return pl.pallas_call(
        functools.partial(_gla_kernel, c=c, sub=sub, t_valid=t_valid),
        grid=(nb, nc),
        in_specs=[u_spec("b_q", GLA_DK), u_spec("b_k", GLA_DK), u_spec("b_v", GLA_DV), u_spec("b_z", GLA_DV),
                  pl.BlockSpec((rows, LANE), lambda b, i: (b * nc + i, OFF_SMALL // LANE)),
                  s_spec,
                  pl.BlockSpec((GLA_RANK, GLA_DK), lambda b, i: (0, 0)),
                  pl.BlockSpec((1, GLA_DK), lambda b, i: (0, 0)),
                  pl.BlockSpec((1, GLA_HV), lambda b, i: (0, 0))],
        out_specs=[pl.BlockSpec((rows, GLA_DV), lambda b, i: (b * nc + i, 0)), s_spec],
        out_shape=[jax.ShapeDtypeStruct((nb * t, GLA_DV), BF16),
                   jax.ShapeDtypeStruct((nb, GLA_HEADS, GLA_HK, GLA_HV), F32)],
        scratch_shapes=[pltpu.VMEM((GLA_HEADS, GLA_HK, GLA_HV), F32)],
        compiler_params=_cparams(("parallel", "arbitrary")),
        name="gla",
    )(ug, ug, ug, ug, ux, s0, wa2, ba.reshape(1, GLA_DK), gnorm.reshape(1, GLA_HV))


def _ssd_kernel(xbc_ref, z_ref, sm_ref, s0_ref, cbuf_ref, cw_ref, cbias_ref, dtb_ref, alog_ref, dsk_ref, mn_ref,
                o_ref, sout_ref, cout_ref, s_scr, f_scr, *, q, t_valid, n_chunks):
    ci = pl.program_id(1)
    npad = 8
    hist = M_CONV - 1

    @pl.when(ci == 0)
    def _():
        s_scr[...] = s0_ref[...].reshape(s_scr.shape)
        f_scr[npad - hist:npad, :] = cbuf_ref[...]

    f_scr[npad:npad + q, :] = xbc_ref[...]
    conv = cbias_ref[...] + xbc_ref[...] * cw_ref[hist:hist + 1, :]
    full = f_scr[...]
    for i in range(hist):
        conv = conv + pltpu.roll(full, shift=hist - i, axis=0)[npad:npad + q, :] * cw_ref[i:i + 1, :]
    xbc = _silu(conv)

    last_valid = t_valid - (n_chunks - 1) * q

    @pl.when(ci == n_chunks - 1)
    def _():
        cout_ref[...] = f_scr[npad - hist + last_valid:npad + last_valid, :]

    f_scr[npad - hist:npad, :] = f_scr[npad - hist + q:npad + q, :]

    lane = lax.broadcasted_iota(jnp.int32, (1, LANE), 1)
    row = ci * q + lax.broadcasted_iota(jnp.int32, (q, 1), 0)
    head_lane = (lane >= SM_DT) & (lane < SM_DT + M_HEADS)
    dt = jnp.where(head_lane & (row < t_valid), _softplus(sm_ref[...] + dtb_ref[...]), 0.0)
    a = dt * (-jnp.exp(alog_ref[...]))
    tri = jnp.where(lax.broadcasted_iota(jnp.int32, (q, q), 0) >= lax.broadcasted_iota(jnp.int32, (q, q), 1), 1.0, 0.0)
    causal = tri > 0.5
    cum = _dot_exact_lhs(tri.astype(BF16), a)
    cum_t = jnp.transpose(cum)
    dt_t = jnp.transpose(dt)
    nbc = M_GROUPS * M_STATE
    rep = M_HEADS // M_GROUPS
    gw = rep * M_HD
    head_of_lane = lax.broadcasted_iota(jnp.int32, (1, gw), 1) // M_HD
    head_of_row = lax.broadcasted_iota(jnp.int32, (gw, 1), 0) // M_HD
    for g in range(M_GROUPS):
        sl = slice(g * gw, (g + 1) * gw)
        bg = xbc[:, M_DINNER + g * M_STATE:M_DINNER + (g + 1) * M_STATE]
        cg = xbc[:, M_DINNER + nbc + g * M_STATE:M_DINNER + nbc + (g + 1) * M_STATE]
        xg = xbc[:, sl]
        cbm = _dot(cg, bg, nt=True)
        s_prev = s_scr[g]
        y = jnp.zeros((q, gw), F32)
        e_lanes = jnp.zeros((q, gw), F32)
        w_lanes = jnp.zeros((q, gw), F32)
        skip = jnp.zeros((1, gw), F32)
        keep = jnp.zeros((gw, 1), F32)
        for r in range(rep):
            hl = SM_DT + g * rep + r
            mine = head_of_lane == r
            cum_c = cum[:, hl:hl + 1]
            c_end = cum[q - 1:q, hl:hl + 1]
            decay = jnp.exp(jnp.where(causal, cum_c - cum_t[hl:hl + 1, :], -jnp.inf))
            y = y + _dot(cbm * decay * dt_t[hl:hl + 1, :], jnp.where(mine, xg, 0.0))
            e_lanes = jnp.where(mine, jnp.exp(cum_c), e_lanes)
            w_lanes = jnp.where(mine, jnp.exp(c_end - cum_c) * dt[:, hl:hl + 1], w_lanes)
            skip = jnp.where(mine, dsk_ref[:, hl:hl + 1], skip)
            keep = jnp.where(head_of_row == r, jnp.exp(c_end), keep)
        y = y + e_lanes * _dot(cg, s_prev, nt=True) + skip * xg
        s_scr[g] = keep * s_prev + _dot(xg * w_lanes, bg, tn=True)
        yg = y * _silu(z_ref[:, sl].astype(F32))
        yn = yg * lax.rsqrt(jnp.mean(yg * yg, axis=-1, keepdims=True) + EPS) * mn_ref[:, sl]
        o_ref[:, sl] = yn.astype(o_ref.dtype)

    @pl.when(ci == n_chunks - 1)
    def _():
        sout_ref[...] = s_scr[...].reshape(M_HEADS, M_HD, M_STATE)


def _pad_heads(v):
    return jnp.zeros((1, LANE), F32).at[0, SM_DT:SM_DT + M_HEADS].set(v)


def ssd_mixer(ug, ux, s0, cbuf, conv_w, conv_b, dt_bias, a_log, d_skip, m_norm, nb, t, *, q, t_valid):
    nc = t // q
    n_chunks = -(-t_valid // q)
    assert n_chunks == nc
    hist = M_CONV - 1

    def u_spec(off, name, width):
        return pl.BlockSpec((q, width), lambda b, i: (b * nc + i, off[name] // width))

    def full(shape):
        return pl.BlockSpec(shape, lambda b, i: (0,) * len(shape))

    s_spec = pl.BlockSpec((None, M_HEADS, M_HD, M_STATE), lambda b, i: (b, 0, 0, 0))
    c_spec = pl.BlockSpec((None, hist, M_CONVDIM), lambda b, i: (b, 0, 0))
    return pl.pallas_call(
        functools.partial(_ssd_kernel, q=q, t_valid=t_valid, n_chunks=n_chunks),
        grid=(nb, nc),
        in_specs=[u_spec(_UX_OFF, "c_xbc", M_CONVDIM), u_spec(_UG_OFF, "c_z", M_DINNER),
                  pl.BlockSpec((q, LANE), lambda b, i: (b * nc + i, OFF_SMALL // LANE)),
                  s_spec, c_spec,
                  full((M_CONV, M_CONVDIM)), full((1, M_CONVDIM)), full((1, LANE)), full((1, LANE)),
                  full((1, LANE)), full((1, M_DINNER))],
        out_specs=[pl.BlockSpec((q, M_DINNER), lambda b, i: (b * nc + i, 0)), s_spec, c_spec],
        out_shape=[jax.ShapeDtypeStruct((nb * t, M_DINNER), BF16),
                   jax.ShapeDtypeStruct((nb, M_HEADS, M_HD, M_STATE), F32),
                   jax.ShapeDtypeStruct((nb, hist, M_CONVDIM), F32)],
        scratch_shapes=[pltpu.VMEM((M_GROUPS, M_HEADS // M_GROUPS * M_HD, M_STATE), F32),
                        pltpu.VMEM((q + 8, M_CONVDIM), F32)],
        compiler_params=_cparams(("parallel", "arbitrary")),
        name="ssd",
    )(ux, ug, ux, s0, cbuf, conv_w, conv_b.reshape(1, M_CONVDIM), _pad_heads(dt_bias), _pad_heads(a_log),
      _pad_heads(d_skip), m_norm.reshape(1, M_DINNER))


SEG_PER_PAGE = PAGE // CMP_STRIDE


def _page_cmp_kernel(pt_ref, *refs, npg):
    w1_ref, ok_ref, ov_ref, scr, wbd_scr = refs[2 * npg:]

    @pl.when((pl.program_id(0) == 0) & (pl.program_id(1) == 0))
    def _():
        wbd_scr[...] = jnp.zeros(wbd_scr.shape, wbd_scr.dtype)
        for s in range(2):
            for j in range(CMP_STRIDE):
                wj = w1_ref[s, j * HD:(j + 1) * HD, :].astype(BF16)
                for g in range(NSA_KV):
                    wbd_scr[s, j, g * HD:(g + 1) * HD, g * 2 * HD:(g + 1) * 2 * HD] = wj

    row = lax.broadcasted_iota(jnp.int32, (PAGE, PAGE), 0)
    pos = lax.broadcasted_iota(jnp.int32, (PAGE, PAGE), 1)
    regroup = jnp.where(pos == CMP_STRIDE * (row % SEG_PER_PAGE) + row // SEG_PER_PAGE, 1.0, 0.0).astype(BF16)
    for s, (pages, o_ref) in enumerate(((refs[:npg], ok_ref), (refs[npg:2 * npg], ov_ref))):
        for k in range(npg):
            by_j = _dot(regroup, pages[k][...].reshape(NSA_KV * HD, PAGE), nt=True)
            for j in range(CMP_STRIDE):
                scr[j, k * SEG_PER_PAGE:(k + 1) * SEG_PER_PAGE, :] = by_j[j * SEG_PER_PAGE:(j + 1) * SEG_PER_PAGE, :]
        acc = jnp.zeros((npg * SEG_PER_PAGE, 2 * HD * NSA_KV), F32)
        for j in range(CMP_STRIDE):
            acc = acc + _dot(scr[j], wbd_scr[s, j])
        o_ref[...] = acc


def page_compress(cache_k_t, cache_v_t, page_table, layer, w1cat, *, npg):
    nb, n_pages = page_table.shape
    steps = n_pages // npg

    def page_spec(k):
        return pl.BlockSpec((None, None, NSA_KV, HD, PAGE),
                            lambda b, i, pt: (layer, pt[b, i * npg + k], 0, 0, 0))

    w_spec = pl.BlockSpec((None, 2, CMP_STRIDE * HD, 2 * HD), lambda b, i, pt: (layer, 0, 0, 0))
    o_spec = pl.BlockSpec((None, npg * SEG_PER_PAGE, NSA_KV * 2 * HD), lambda b, i, pt: (b, i, 0))
    o_shape = jax.ShapeDtypeStruct((nb, n_pages * SEG_PER_PAGE, NSA_KV * 2 * HD), F32)
    return pl.pallas_call(
        functools.partial(_page_cmp_kernel, npg=npg),
        grid_spec=pltpu.PrefetchScalarGridSpec(
            num_scalar_prefetch=1,
            grid=(nb, steps),
            in_specs=[page_spec(k) for k in range(npg)] * 2 + [w_spec],
            out_specs=[o_spec, o_spec],
            scratch_shapes=[pltpu.VMEM((CMP_STRIDE, npg * SEG_PER_PAGE, NSA_KV * HD), F32),
                            pltpu.VMEM((2, CMP_STRIDE, NSA_KV * HD, NSA_KV * 2 * HD), BF16)]),
        out_shape=[o_shape, o_shape],
        compiler_params=_cparams(("arbitrary", "arbitrary")),
        name="page_compress",
    )(page_table, *([cache_k_t] * npg), *([cache_v_t] * npg), w1cat)


def _masked_softmax_rows(s, mask):
    s = jnp.where(mask, s, NEG)
    m = jnp.max(s, axis=-1, keepdims=True)
    p = jnp.where(mask, jnp.exp(s - m), 0.0)
    l = jnp.sum(p, axis=-1, keepdims=True)
    return p / jnp.where(l > 0.0, l, 1.0)


def _nsa_sample_cmp_kernel(abk_ref, abv_ref, kvs_ref, q_ref, w1_ref, pe_ref, w2_ref, cover_ref, oc_ref, sel_ref,
                           cst_scr, *, p_len, nc, sl):
    ns = p_len // SLC_BLK + 1
    rowi = lax.broadcasted_iota(jnp.int32, (nc, 1), 0)

    @pl.when(pl.program_id(0) == 0)
    def _():
        for idx in range(2):
            pe = _dot3(pe_ref[idx], w1_ref[idx])
            cst_scr[idx:idx + 1, :] = pe[0:1, :HD] + pe[1:2, HD:]

    cidx = lax.broadcasted_iota(jnp.int32, (1, nc), 1)
    m_c = cidx * CMP_STRIDE + (CMP_LEN - 1) <= p_len
    sidx = lax.broadcasted_iota(jnp.int32, (1, sl), 1)
    cur = p_len // SLC_BLK
    valid = (sidx * SLC_BLK <= p_len) & (sidx < ns)
    forced = (sidx == 0) | (sidx == cur) | (sidx == cur - 1)
    ri = lax.broadcasted_iota(jnp.int32, (sl, sl), 0)
    li = lax.broadcasted_iota(jnp.int32, (sl, sl), 1)
    kl = lax.broadcasted_iota(jnp.int32, (sl, LANE), 1).astype(F32)
    sv = lax.broadcasted_iota(jnp.int32, (sl, LANE), 0).astype(F32)

    for g in range(NSA_KV):
        def compressed(ab_ref, idx):
            ab = ab_ref[:, g * 2 * HD:(g + 1) * 2 * HD]
            new = kvs_ref[idx * NSA_KV + g:idx * NSA_KV + g + 1, :]
            b_new = _dot(new, w1_ref[idx][0:HD, HD:])
            nxt = jnp.where(rowi == nc - 1, b_new, pltpu.roll(ab[:, HD:], shift=nc - 1, axis=0))
            return _dot(_silu(ab[:, :HD] + nxt + cst_scr[idx:idx + 1, :]), w2_ref[idx])

        kc = compressed(abk_ref, 0)
        vc = compressed(abv_ref, 1)
        p_c = _masked_softmax_rows(_dot3(q_ref[g] * (HD ** -0.5), kc, nt=True), m_c)
        oc_ref[g] = _dot(p_c, vc)
        imp = _dot_exact_rhs(jnp.sum(p_c, axis=0, keepdims=True), cover_ref[...])
        score = jnp.where(valid, jnp.where(forced, FORCE, imp), NEG)
        score_b = jnp.broadcast_to(score, (sl, sl))
        score_col = jnp.sum(jnp.where(ri == li, score_b, 0.0), axis=1, keepdims=True)
        ahead = (score_b > score_col) | ((score_b == score_col) & (li < ri))
        rank_col = jnp.sum(jnp.where(ahead, 1.0, 0.0), axis=1, keepdims=True)
        picked = jnp.sum(jnp.where(rank_col == kl, sv, 0.0), axis=0, keepdims=True)
        sel_ref[g] = picked.astype(jnp.int32)


def _cover_matrix(nc, sl):
    c0 = np.arange(nc)[:, None] * CMP_STRIDE
    s0 = np.arange(sl)[None, :] * SLC_BLK
    return jnp.asarray(((c0 + CMP_LEN > s0) & (c0 < s0 + SLC_BLK)).astype(np.float32), dtype=BF16)


def nsa_sample_cmp(abk, abv, kvs, q3, w1cat, pe2, w2, p_len):
    nb, nc, _ = abk.shape
    ns = p_len // SLC_BLK + 1
    sl = -(-ns // LANE) * LANE
    cover = _cover_matrix(nc, sl)

    ab_spec = pl.BlockSpec((None, nc, NSA_KV * 2 * HD), lambda b: (b, 0, 0))

    def full(a):
        return pl.BlockSpec(a.shape, lambda b: (0,) * a.ndim)

    row_spec = pl.BlockSpec((None, NSA_KV, NSA_REP, HD), lambda b: (b, 0, 0, 0))
    return pl.pallas_call(
        functools.partial(_nsa_sample_cmp_kernel, p_len=p_len, nc=nc, sl=sl),
        grid=(nb,),
        in_specs=[ab_spec, ab_spec, pl.BlockSpec((None, 6 * NSA_KV, HD), lambda b: (b, 0, 0)), row_spec,
                  full(w1cat), full(pe2), full(w2), full(cover)],
        out_specs=[row_spec, pl.BlockSpec((None, NSA_KV, 1, LANE), lambda b: (b, 0, 0, 0))],
        out_shape=[jax.ShapeDtypeStruct((nb, NSA_KV, NSA_REP, HD), F32),
                   jax.ShapeDtypeStruct((nb, NSA_KV, 1, LANE), jnp.int32)],
        scratch_shapes=[pltpu.VMEM((8, HD), F32)],
        compiler_params=_cparams(("arbitrary",)),
        name="nsa_sample_cmp",
    )(abk, abv, kvs, q3, w1cat, pe2, w2, cover)


def _nsa_sample_sel_kernel(sel_ref, pt_ref, *refs, p_len, nb, n_buf):
    nk = SLC_TOPK
    kpages, vpages = refs[:nk], refs[nk:2 * nk]
    (kwb_ref, vwb_ref, kvt_ref, kvs_ref, q_ref, az_ref, sm_ref, oc_ref,
     o_ref, kwo_ref, vwo_ref) = refs[2 * nk:]
    b = pl.program_id(0)
    g = pl.program_id(1)
    n_past = p_len // SLC_BLK
    q4 = q_ref[...] * (HD ** -0.5)
    lane_b = lax.broadcasted_iota(jnp.int32, (1, nb), 1) == b

    def new_row(i):
        return kvs_ref[pl.ds(i * NSA_KV + g, 1), :]

    def new_col(i):
        blk = kvt_ref[pl.ds(pl.multiple_of((i * NSA_KV + g) * HD, HD), HD), :]
        return jnp.sum(jnp.where(lane_b, blk, 0.0), axis=1, keepdims=True)

    half_of_lane = lax.broadcasted_iota(jnp.int32, (1, PAGE), 1) // SLC_BLK
    s_parts, m_parts = [], []
    has_new = jnp.zeros((1, 1), jnp.int32)
    for k in range(nk):
        blk = sel_ref[b, g, k]
        m_parts.append((half_of_lane == blk % 2) & (blk < n_past))
        s_parts.append(_dot(q4, kpages[k][...]))
        has_new = has_new + jnp.where(blk == n_past, 1, 0)
    s_sel = jnp.concatenate(s_parts, axis=1)
    m_sel = jnp.concatenate(m_parts, axis=1)
    new_ok = has_new > 0
    s_new = jnp.where(new_ok, jnp.sum(q4 * new_row(2), axis=1, keepdims=True), NEG)
    s_sel = jnp.where(m_sel, s_sel, NEG)
    mx = jnp.maximum(jnp.max(s_sel, axis=1, keepdims=True), s_new)
    p_sel = jnp.where(m_sel, jnp.exp(s_sel - mx), 0.0)
    p_new = jnp.where(new_ok, jnp.exp(s_new - mx), 0.0)
    den = jnp.sum(p_sel, axis=1, keepdims=True) + p_new
    acc = p_new * new_row(3)
    for k in range(nk):
        acc = acc + _dot(p_sel[:, k * PAGE:(k + 1) * PAGE], vpages[k][...], nt=True)
    o_s = acc / jnp.where(den > 0.0, den, 1.0)

    kwb = kwb_ref[...]
    vwb = vwb_ref[...]
    wlane = lax.broadcasted_iota(jnp.int32, (1, n_buf), 1)
    m_w = wlane > n_buf - WINDOW
    s_w = jnp.where(m_w, _dot(q4, kwb), NEG)
    s_wn = jnp.sum(q4 * new_row(4), axis=1, keepdims=True)
    mw = jnp.maximum(jnp.max(s_w, axis=1, keepdims=True), s_wn)
    p_w = jnp.where(m_w, jnp.exp(s_w - mw), 0.0)
    p_wn = jnp.exp(s_wn - mw)
    o_w = (_dot(p_w, vwb, nt=True) + p_wn * new_row(5)) / (jnp.sum(p_w, axis=1, keepdims=True) + p_wn)
    kwo_ref[...] = jnp.where(wlane == n_buf - 1, new_col(4), pltpu.roll(kwb, shift=n_buf - 1, axis=1))
    vwo_ref[...] = jnp.where(wlane == n_buf - 1, new_col(5), pltpu.roll(vwb, shift=n_buf - 1, axis=1))

    gate_row = _sigmoid(pltpu.roll(sm_ref[pl.ds(b, 1), :], shift=(LANE - 3 * NSA_REP * g) % LANE, axis=1))
    gl = lax.broadcasted_iota(jnp.int32, (NSA_REP, LANE), 1)
    gr = lax.broadcasted_iota(jnp.int32, (NSA_REP, LANE), 0)

    def gate(j):
        return jnp.sum(jnp.where(gl == 3 * gr + j, gate_row, 0.0), axis=1, keepdims=True)

    o = gate(0) * oc_ref[...] + gate(1) * o_s + gate(2) * o_w
    o_ref[...] = o * _silu(az_ref[...])


def nsa_sample_sel(sel, page_table, slc_k_t, slc_v_t, win_k_t, win_v_t, layer, kvt, kvs, q3, az3, ux, o_cmp, p_len):
    nb = page_table.shape[0]
    n_buf = win_k_t.shape[-1]
    last_blk = p_len // SLC_BLK - 1

    def page_spec(k):
        def imap(b, g, sel_r, pt_r):
            blk = jnp.minimum(sel_r[b, g, k], last_blk)
            return (layer, pt_r[b, blk // (PAGE // SLC_BLK)], g, 0, 0)
        return pl.BlockSpec((None, None, None, HD, PAGE), imap)

    win_spec = pl.BlockSpec((None, None, None, HD, n_buf), lambda b, g, s, p: (layer, b, g, 0, 0))
    wout_spec = pl.BlockSpec((None, None, HD, n_buf), lambda b, g, s, p: (b, g, 0, 0))
    row_spec = pl.BlockSpec((None, None, NSA_REP, HD), lambda b, g, s, p: (b, g, 0, 0))

    def full(a):
        return pl.BlockSpec(a.shape, lambda b, g, s, p: (0,) * a.ndim)

    small = pl.BlockSpec((nb, LANE), lambda b, g, s, p: (0, OFF_SMALL // LANE))
    return pl.pallas_call(
        functools.partial(_nsa_sample_sel_kernel, p_len=p_len, nb=nb, n_buf=n_buf),
        grid_spec=pltpu.PrefetchScalarGridSpec(
            num_scalar_prefetch=2,
            grid=(nb, NSA_KV),
            in_specs=[page_spec(k) for k in range(SLC_TOPK)] * 2
                     + [win_spec, win_spec, full(kvt), pl.BlockSpec((None, 6 * NSA_KV, HD), lambda b, g, s, p: (b, 0, 0)),
                        row_spec, row_spec, small, row_spec],
            out_specs=[row_spec, wout_spec, wout_spec]),
        out_shape=[jax.ShapeDtypeStruct((nb, NSA_KV, NSA_REP, HD), F32),
                   jax.ShapeDtypeStruct((nb, NSA_KV, HD, n_buf), F32),
                   jax.ShapeDtypeStruct((nb, NSA_KV, HD, n_buf), F32)],
        compiler_params=_cparams(("parallel", "arbitrary")),
        name="nsa_sample_sel",
    )(sel, page_table, *([slc_k_t] * SLC_TOPK), *([slc_v_t] * SLC_TOPK), win_k_t, win_v_t, kvt, kvs, q3, az3, ux, o_cmp)


def _prep_weights(w_in, cmp_pe_k, cmp_w1_k, cmp_w2_k, cmp_pe_v, cmp_w1_v, cmp_w2_v, w_br_a, w_br_b, w_br_c, w_out):
    wt = jnp.transpose(w_in, (0, 2, 1))

    def gather_rows(order, total):
        parts = [wt[:, _IN_OFF[n]:_IN_OFF[n] + _IN_SZ[n]] for n in order]
        used = sum(_IN_SZ[n] for n in order)
        parts.append(jnp.zeros((wt.shape[0], total - used, wt.shape[2]), wt.dtype))
        return jnp.concatenate(parts, axis=1).astype(BF16)

    half = CMP_STRIDE * HD

    def cat(w1):
        return jnp.concatenate([w1[:, :half], w1[:, half:]], axis=2)

    def pe_rows(pe):
        d = pe.shape[0]
        rows = pe.reshape(d, 2, half)
        return jnp.concatenate([rows, jnp.zeros((d, 6, half), pe.dtype)], axis=1)

    w1cat = jnp.stack([cat(cmp_w1_k), cat(cmp_w1_v)], axis=1)
    pe2 = jnp.stack([pe_rows(cmp_pe_k), pe_rows(cmp_pe_v)], axis=1)
    w2 = jnp.stack([cmp_w2_k, cmp_w2_v], axis=1)
    return dict(wt_ug=gather_rows(_UG_ORDER, NP_UG), wt_ux=gather_rows(_UX_ORDER, NP_UX), wt_kv=gather_rows(("a_kv",), KV_ROWS),
                wt_q=gather_rows(_TQ_ORDER, TQ_ROWS),
                w1cat=w1cat, pe2=pe2, w2=w2,
                wa=w_br_a.astype(BF16), wb=w_br_b.astype(BF16), wc=w_br_c.astype(BF16), wo=w_out.astype(BF16))


def _prompt_layer(x, h, nb, t, layer, w, p, next_norm, tiles, stacks):
    ug = matmul(h, w["wt_ug"], nt=True, tm=tiles["mm_tm"], tn=tiles["in_tn"], out_dtype=BF16, name="in_proj",
                b_layer=layer)
    ux = matmul(h, w["wt_ux"], nt=True, tm=tiles["mm_tm"], tn=NP_UX, name="in_proj_x", b_layer=layer)
    stacks = kv_proj_stacked(w["wt_kv"], layer, h, nb, t, stacks, tn=tiles["kv_tn"])
    qgt = matmul_t(w["wt_q"], layer, h, nb, t, tm=TQ_TM, tn=tiles["q_tn"], name="q_proj_t")
    nseg = t // CMP_STRIDE
    seg = jnp.stack([stacks[0][layer], stacks[1][layer]]).reshape(2, nb, NSA_KV, HD, t)
    seg = jnp.transpose(seg, (0, 1, 2, 4, 3)).reshape(2, nb * NSA_KV * nseg, CMP_STRIDE * HD)
    kc = compress_blocks(seg, w["w1cat"][layer], w["pe2"][layer], w["w2"][layer], nseg)
    o_a = nsa_prompt(ug, qgt, stacks, layer, kc, nb, t, tq=tiles["tq"], ck=tiles["ck"])
    zeros_gla = jnp.zeros((nb, GLA_HEADS, GLA_HK, GLA_HV), F32)
    zeros_ssm = jnp.zeros((nb, M_HEADS, M_HD, M_STATE), F32)
    zeros_conv = jnp.zeros((nb, M_CONV - 1, M_CONVDIM), F32)
    o_b, gla_s = gla_mixer(ug, ux, zeros_gla, p["gla_wa2"], p["gla_ba"], p["gla_norm"], nb, t,
                           c=tiles["gla_c"], sub=min(tiles["gla_sub"], t // tiles["gla_c"]), t_valid=t)
    o_c, ssm_s, conv_s = ssd_mixer(ug, ux, zeros_ssm, zeros_conv, p["conv_w"], p["conv_b"], p["dt_bias"], p["a_log"],
                                   p["d_skip"], p["m_norm"], nb, t, q=min(tiles["ssd_q"], t), t_valid=t)
    mix = merge_branches(o_a, o_b, o_c, w["wa"], w["wb"], w["wc"], layer, ug, tm=tiles["mg_tm"], tn=tiles["mg_tn"])
    x_out, h_out = out_proj_norm(mix, w["wo"], layer, x, *next_norm, tm=tiles["out_tm"])
    return x_out, h_out, stacks, (gla_s, ssm_s, conv_s)


ROW_PAD = 8


def _sample_layer(x, h, layer, w, p, next_norm, caches, states, page_table, tiles):
    nb = x.shape[0]
    p_len = page_table.shape[1] * PAGE
    ck_t, cv_t, sk_t, sv_t, wk_t, wv_t = caches
    ug = matmul(h, w["wt_ug"], nt=True, tm=nb, tn=tiles["in_tn"], out_dtype=BF16, name="in_proj_s", b_layer=layer)
    ux = matmul(h, w["wt_ux"], nt=True, tm=nb, tn=NP_UX, name="in_proj_xs", b_layer=layer)
    kv = matmul(h, w["wt_kv"], nt=True, tm=nb, tn=512, name="kv_proj_s", b_layer=layer)
    kvt = matmul(w["wt_kv"], h, nt=True, tm=512, tn=nb, name="kv_proj_st", a_layer=layer)
    qg = matmul(h, w["wt_q"], nt=True, tm=nb, tn=TQ_TM, name="q_proj_s", b_layer=layer)
    kvs = kv.reshape(nb, 6 * NSA_KV, HD)
    def pad_rows(a):
        return jnp.pad(a[:, None, :], ((0, 0), (0, ROW_PAD - 1), (0, 0))).reshape(nb * ROW_PAD, a.shape[1])

    ug_pad, ux_pad = pad_rows(ug), pad_rows(ux)
    o_b, gla_s = gla_mixer(ug_pad, ux_pad, states[0], p["gla_wa2"], p["gla_ba"], p["gla_norm"], nb, ROW_PAD,
                           c=ROW_PAD, sub=1, t_valid=1)
    o_c, ssm_s, conv_s = ssd_mixer(ug_pad, ux_pad, states[1], states[2], p["conv_w"], p["conv_b"], p["dt_bias"], p["a_log"],
                                   p["d_skip"], p["m_norm"], nb, ROW_PAD, q=ROW_PAD, t_valid=1)
    o_b = o_b.reshape(nb, ROW_PAD, GLA_DV)[:, 0]
    o_c = o_c.reshape(nb, ROW_PAD, M_DINNER)[:, 0]
    abk, abv = page_compress(ck_t, cv_t, page_table, layer, w["w1cat"], npg=min(tiles["npg"], page_table.shape[1]))
    q3 = qg[:, :TQ_AG].reshape(nb, NSA_KV, NSA_REP, HD)
    az3 = ug[:, _UG_OFF["a_z"]:_UG_OFF["a_z"] + MIX_W].astype(F32).reshape(nb, NSA_KV, NSA_REP, HD)
    o_cmp, sel = nsa_sample_cmp(abk, abv, kvs, q3, w["w1cat"][layer], w["pe2"][layer], w["w2"][layer], p_len)
    o_a, win_k, win_v = nsa_sample_sel(sel.reshape(nb, NSA_KV, LANE), page_table, sk_t, sv_t, wk_t, wv_t, layer,
                                       kvt, kvs, q3, az3, ux, o_cmp, p_len)
    mix = merge_branches(o_a.reshape(nb, MIX_W).astype(BF16), o_b, o_c, w["wa"], w["wb"], w["wc"], layer, ug,
                         tm=nb, tn=tiles["mg_tn"])
    x_out, h_out = out_proj_norm(mix, w["wo"], layer, x, *next_norm, tm=nb)
    kv4 = kv.reshape(nb, 6, 1, NSA_KV, HD)
    return x_out, h_out, (kv4[:, 0], kv4[:, 1], kv4[:, 2], kv4[:, 3], win_k, win_v, gla_s, ssm_s, conv_s)


_TILES = dict(rms_tm=256, mm_tm=1024, in_tn=1024, kv_tn=512, q_tn=2048, tq=512, ck=512, gla_c=64, gla_sub=8,
              ssd_q=256, mg_tm=1024, mg_tn=1024, out_tm=512, npg=32)
_PARAM_NAMES = ("ln_w", "gla_wa2", "gla_ba", "gla_norm", "conv_w", "conv_b", "dt_bias", "a_log", "d_skip", "m_norm")


def kernel(x_prompt, x_sample, cache_cmp_k, cache_cmp_v, cache_slc_k, cache_slc_v, cache_win_k, cache_win_v,
           state_gla, state_ssm, state_conv, page_table, ln_w, w_in, cmp_pe_k, cmp_w1_k, cmp_w2_k, cmp_pe_v,
           cmp_w1_v, cmp_w2_v, gla_wa2, gla_ba, gla_norm, conv_w, conv_b, dt_bias, a_log, d_skip, m_norm,
           w_br_a, w_br_b, w_br_c, w_out, final_norm):
    nbp, t, d = x_prompt.shape
    nbs = x_sample.shape[0]
    depth = w_in.shape[0]
    weights = _prep_weights(w_in, cmp_pe_k, cmp_w1_k, cmp_w2_k, cmp_pe_v, cmp_w1_v, cmp_w2_v,
                            w_br_a, w_br_b, w_br_c, w_out)
    params = dict(zip(_PARAM_NAMES, (ln_w, gla_wa2, gla_ba, gla_norm, conv_w, conv_b, dt_bias, a_log, d_skip, m_norm)))
    caches = tuple(jnp.transpose(c, (0, 1, 3, 4, 2))
                   for c in (cache_cmp_k, cache_cmp_v, cache_slc_k, cache_slc_v, cache_win_k, cache_win_v))
    xp = x_prompt.reshape(nbp * t, d)
    xs = x_sample.reshape(nbs, d)
    p_states, s_out = [], []
    stacks = [jnp.zeros((depth, nbp, GRP_ROWS, t), F32) for _ in range(6)]
    hp = rmsnorm_rows(xp, ln_w[0], BF16, _TILES["rms_tm"])
    hs = rmsnorm_rows(xs, ln_w[0], BF16, nbs)
    for l in range(depth):
        p = {k: v[l] for k, v in params.items()}
        next_norm = (ln_w[l + 1], BF16) if l + 1 < depth else (final_norm, F32)
        xp, hp, stacks, st_p = _prompt_layer(xp, hp, nbp, t, l, weights, p, next_norm, _TILES, stacks)
        xs, hs, st_s = _sample_layer(xs, hs, l, weights, p, next_norm, caches,
                                     (state_gla[l], state_ssm[l], state_conv[l]), page_table, _TILES)
        p_states.append(st_p)
        s_out.append(st_s)
    y_prompt = hp.reshape(nbp, t, d)
    y_sample = hs.reshape(nbs, 1, d)

    def from_kv_layout(a):
        return jnp.transpose(a, (0, 1, 4, 2, 3))

    n_w = min(WINDOW, t)
    kv5 = [a.reshape(depth, nbp, NSA_KV, HD, t) for a in stacks]
    p_leaves = [from_kv_layout(a) for a in kv5[:4]] + [from_kv_layout(a[..., t - n_w:]) for a in kv5[4:]]
    p_leaves += [jnp.stack([st[i] for st in p_states]) for i in range(3)]
    s_leaves = [jnp.stack([o[i] for o in s_out]) for i in range(9)]
    s_leaves[4], s_leaves[5] = from_kv_layout(s_leaves[4]), from_kv_layout(s_leaves[5])
    return (y_prompt, y_sample, *p_leaves, *s_leaves)
```

```python
import functools
import math

import jax
import jax.numpy as jnp
import numpy as np
from jax import lax
from jax.experimental import pallas as pl
from jax.experimental.pallas import tpu as pltpu

F32 = jnp.float32
BF16 = jnp.bfloat16

D_MODEL = 2048
PAGE = 128
MIX_W = D_MODEL // 2
HD = 64
NSA_HEADS = MIX_W // HD
NSA_KV = 4
NSA_REP = NSA_HEADS // NSA_KV
CMP_LEN = 32
CMP_STRIDE = 16
SLC_BLK = 64
SLC_TOPK = 16
WINDOW = 512
GLA_HEADS = 4
GLA_DK = MIX_W // 2
GLA_DV = MIX_W
GLA_HK = GLA_DK // GLA_HEADS
GLA_HV = GLA_DV // GLA_HEADS
GLA_RANK = 16
GLA_TAU = 16.0
M_DINNER = MIX_W
M_HD = 64
M_HEADS = M_DINNER // M_HD
M_GROUPS = 4
M_STATE = 128
M_CONV = 4
M_CONVDIM = M_DINNER + 2 * M_GROUPS * M_STATE
EPS = 1e-6
NEG = -1e30
FORCE = 1e4
LOG2E = math.log2(math.e)
WIN_Q = 128
ONES_ROWS = 16

_IN_NAMES = ("a_q", "a_kv", "a_g", "a_z", "b_q", "b_k", "b_v", "b_a", "b_z", "c_xbc", "c_dt", "c_z", "m_g")
_IN_SIZES = (NSA_HEADS * HD, 6 * NSA_KV * HD, 3 * NSA_HEADS, MIX_W, GLA_DK, GLA_DK, GLA_DV, GLA_RANK, GLA_DV,
             M_CONVDIM, M_HEADS, M_DINNER, 3 * D_MODEL)
_IN_OFF = dict(zip(_IN_NAMES, np.cumsum((0,) + _IN_SIZES)[:-1].tolist()))
_IN_SZ = dict(zip(_IN_NAMES, _IN_SIZES))

LANE = 128
SUBLANE = 8
_UG_ORDER = ("m_g", "a_z", "b_v", "b_z", "c_z", "b_q", "b_k")
_UX_ORDER = ("c_xbc", "a_g", "b_a", "c_dt")


def _offsets(order):
    off, o = {}, 0
    for n in order:
        off[n] = o
        o += _IN_SZ[n]
    return off, o


_UG_OFF, NP_UG = _offsets(_UG_ORDER)
_UX_OFF, _ux_used = _offsets(_UX_ORDER)
NP_UX = -(-_ux_used // LANE) * LANE
OFF_SMALL = _UX_OFF["a_g"]
SM_AG, SM_BA, SM_DT = 0, 3 * NSA_HEADS, 3 * NSA_HEADS + GLA_RANK
KV_ROWS = _IN_SZ["a_kv"]
GRP_ROWS = NSA_KV * HD
_TQ_ORDER = ("a_q", "a_g")
TQ_AG = _IN_SZ["a_q"]
TQ_TM = 384
TQ_ROWS = -(-(TQ_AG + _IN_SZ["a_g"]) // TQ_TM) * TQ_TM
VMEM_LIMIT = 56 * 1024 * 1024


def _cparams(sem):
    return pltpu.CompilerParams(dimension_semantics=sem, vmem_limit_bytes=VMEM_LIMIT)


def _dot(a, b, prec=None, nt=False, tn=False):
    if prec is None:
        a, b = a.astype(BF16), b.astype(BF16)
    dn = (((0,) if tn else (1,), (1,) if nt else (0,)), ((), ()))
    return lax.dot_general(a, b, dn, precision=prec, preferred_element_type=F32)


def _split(a):
    hi = a.astype(BF16)
    return hi, (a - hi.astype(F32)).astype(BF16)


def _dot_exact_rhs(a, b_bf16):
    hi, lo = _split(a)
    return _dot(hi, b_bf16) + _dot(lo, b_bf16)


def _dot_exact_lhs(a_bf16, b):
    hi, lo = _split(b)
    return _dot(a_bf16, hi) + _dot(a_bf16, lo)


def _dot3(a, b, **kw):
    ah, al = _split(a)
    bh, bl = _split(b)
    return _dot(ah, bh, **kw) + _dot(ah, bl, **kw) + _dot(al, bh, **kw)


def _sigmoid(x):
    return 0.5 * jnp.tanh(0.5 * x) + 0.5


def _silu(x):
    return x * _sigmoid(x)


def _log_sigmoid(x):
    return jnp.minimum(x, 0.0) - jnp.log(1.0 + jnp.exp(-jnp.abs(x)))


def _softplus(x):
    return jnp.maximum(x, 0.0) + jnp.log(1.0 + jnp.exp(-jnp.abs(x)))


def _stacked_spec(block, imap, layer):
    if layer is None:
        return pl.BlockSpec(block, imap)
    return pl.BlockSpec((None,) + block, lambda *idx: (layer,) + imap(*idx))


def _rms_kernel(x_ref, w_ref, o_ref):
    x = x_ref[...]
    y = x * lax.rsqrt(jnp.mean(x * x, axis=-1, keepdims=True) + EPS)
    o_ref[...] = (y * w_ref[...]).astype(o_ref.dtype)


def rmsnorm_rows(x, w, out_dtype, tm):
    m, d = x.shape
    return pl.pallas_call(
        _rms_kernel,
        grid=(m // tm,),
        in_specs=[pl.BlockSpec((tm, d), lambda i: (i, 0)), pl.BlockSpec((1, d), lambda i: (0, 0))],
        out_specs=pl.BlockSpec((tm, d), lambda i: (i, 0)),
        out_shape=jax.ShapeDtypeStruct((m, d), out_dtype),
        compiler_params=_cparams(("parallel",)),
        name="rmsnorm",
    )(x, w.reshape(1, d))


def _mm_kernel(*refs, nt, has_res):
    a_ref, b_ref, o_ref = refs[0], refs[1], refs[-1]
    acc = _dot(a_ref[...], b_ref[...], nt=nt)
    if has_res:
        acc = acc + refs[2][...]
    o_ref[...] = acc.astype(o_ref.dtype)


def matmul(a, b, *, nt, tm, tn, res=None, out_dtype=F32, name="matmul", a_layer=None, b_layer=None):
    m, k = a.shape[-2:]
    n = b.shape[-2] if nt else b.shape[-1]
    tm, tn = min(tm, m), min(tn, n)
    if nt:
        b_spec = _stacked_spec((tn, k), lambda i, j: (j, 0), b_layer)
    else:
        b_spec = _stacked_spec((k, tn), lambda i, j: (0, j), b_layer)
    in_specs = [_stacked_spec((tm, k), lambda i, j: (i, 0), a_layer), b_spec]
    args = [a, b]
    if res is not None:
        in_specs.append(pl.BlockSpec((tm, tn), lambda i, j: (i, j)))
        args.append(res)
    return pl.pallas_call(
        functools.partial(_mm_kernel, nt=nt, has_res=res is not None),
        grid=(m // tm, n // tn),
        in_specs=in_specs,
        out_specs=pl.BlockSpec((tm, tn), lambda i, j: (i, j)),
        out_shape=jax.ShapeDtypeStruct((m, n), out_dtype),
        compiler_params=_cparams(("parallel", "parallel")),
        name=name,
    )(*args)


def matmul_t(wt, layer, h, nb, t, *, tm, tn, name):
    _, r, k = wt.shape
    tn = min(tn, t)
    nj = t // tn
    return pl.pallas_call(
        functools.partial(_mm_kernel, nt=True, has_res=False),
        grid=(nb, nj, r // tm),
        in_specs=[pl.BlockSpec((None, tm, k), lambda b, j, i: (layer, i, 0)),
                  pl.BlockSpec((tn, k), lambda b, j, i: (b * nj + j, 0))],
        out_specs=pl.BlockSpec((None, tm, tn), lambda b, j, i: (b, i, j)),
        out_shape=jax.ShapeDtypeStruct((nb, r, t), F32),
        compiler_params=_cparams(("parallel", "parallel", "parallel")),
        name=name,
    )(wt, h)


def _kv_stack_kernel(w_ref, h_ref, *refs):
    outs = refs[-6:]
    hb = h_ref[...]
    for i in range(6):
        outs[i][...] = _dot(w_ref[i * GRP_ROWS:(i + 1) * GRP_ROWS, :], hb, nt=True)


def kv_proj_stacked(wt_kv, layer, h, nb, t, stacks, *, tn):
    depth, _, k = wt_kv.shape
    tn = min(tn, t)
    nj = t // tn
    in_specs = [pl.BlockSpec((None, KV_ROWS, k), lambda b, j: (layer, 0, 0)),
                pl.BlockSpec((tn, k), lambda b, j: (b * nj + j, 0))] + [pl.BlockSpec(memory_space=pl.ANY)] * 6
    args = [wt_kv, h] + list(stacks)
    aliases = {2 + i: i for i in range(6)}
    o_spec = pl.BlockSpec((None, None, GRP_ROWS, tn), lambda b, j: (layer, b, 0, j))
    o_shape = jax.ShapeDtypeStruct((depth, nb, GRP_ROWS, t), F32)
    return pl.pallas_call(
        _kv_stack_kernel,
        grid=(nb, nj),
        in_specs=in_specs,
        out_specs=[o_spec] * 6,
        out_shape=[o_shape] * 6,
        input_output_aliases=aliases,
        compiler_params=_cparams(("parallel", "parallel")),
        name="kv_proj_t",
    )(*args)


def _out_norm_kernel(a_ref, w_ref, res_ref, nw_ref, x_ref, h_ref):
    x = res_ref[...] + _dot(a_ref[...], w_ref[...])
    x_ref[...] = x
    y = x * lax.rsqrt(jnp.mean(x * x, axis=-1, keepdims=True) + EPS)
    h_ref[...] = (y * nw_ref[...]).astype(h_ref.dtype)


def out_proj_norm(mix, wo, layer, res, norm_w, h_dtype, *, tm):
    m, k = mix.shape
    n = wo.shape[-1]
    tm = min(tm, m)
    row = lambda i: (i, 0)
    return pl.pallas_call(
        _out_norm_kernel,
        grid=(m // tm,),
        in_specs=[pl.BlockSpec((tm, k), row), pl.BlockSpec((None, k, n), lambda i: (layer, 0, 0)),
                  pl.BlockSpec((tm, n), row), pl.BlockSpec((1, n), lambda i: (0, 0))],
        out_specs=[pl.BlockSpec((tm, n), row), pl.BlockSpec((tm, n), row)],
        out_shape=[jax.ShapeDtypeStruct((m, n), F32), jax.ShapeDtypeStruct((m, n), h_dtype)],
        compiler_params=_cparams(("parallel",)),
        name="out_proj",
    )(mix, wo, res, norm_w.reshape(1, n))


def _merge_kernel(oa_ref, ob_ref, oc_ref, wa_ref, wb_ref, wc_ref, g0_ref, g1_ref, g2_ref, o_ref):
    acc = _sigmoid(g0_ref[...].astype(F32)) * _dot(oa_ref[...], wa_ref[...])
    acc = acc + _sigmoid(g1_ref[...].astype(F32)) * _dot(ob_ref[...], wb_ref[...])
    acc = acc + _sigmoid(g2_ref[...].astype(F32)) * _dot(oc_ref[...], wc_ref[...])
    o_ref[...] = acc.astype(o_ref.dtype)


def merge_branches(o_a, o_b, o_c, wa, wb, wc, layer, ug, *, tm, tn):
    m, k = o_a.shape
    n = wa.shape[-1]
    tm = min(tm, m)
    gb = _UG_OFF["m_g"] // tn
    nj = n // tn
    o_spec = pl.BlockSpec((tm, k), lambda i, j: (i, 0))
    w_spec = pl.BlockSpec((None, k, tn), lambda i, j: (layer, 0, j))
    g_specs = [pl.BlockSpec((tm, tn), functools.partial(lambda i, j, br: (i, gb + br * nj + j), br=br))
               for br in range(3)]
    return pl.pallas_call(
        _merge_kernel,
        grid=(m // tm, nj),
        in_specs=[o_spec, o_spec, o_spec, w_spec, w_spec, w_spec] + g_specs,
        out_specs=pl.BlockSpec((tm, tn), lambda i, j: (i, j)),
        out_shape=jax.ShapeDtypeStruct((m, n), BF16),
        compiler_params=_cparams(("parallel", "parallel")),
        name="merge",
    )(o_a, o_b, o_c, wa, wb, wc, ug, ug, ug)


def _cmp_kernel(seg_ref, w1_ref, pe_ref, w2_ref, o_ref, *, nseg):
    w1 = w1_ref[...]
    ab = _dot3(seg_ref[...], w1)
    pe = _dot3(pe_ref[...], w1)
    cst = pe[0:1, :HD] + pe[1:2, HD:]
    nxt = pltpu.roll(ab[:, HD:], shift=nseg - 1, axis=0)
    pre = ab[:, :HD] + nxt + cst
    o_ref[...] = _dot3(_silu(pre), w2_ref[...])


def compress_blocks(seg, w1cat, pe2, w2, nseg):
    _, r, kdim = seg.shape
    return pl.pallas_call(
        functools.partial(_cmp_kernel, nseg=nseg),
        grid=(2, r // nseg),
        in_specs=[pl.BlockSpec((None, nseg, kdim), lambda s, i: (s, i, 0)),
                  pl.BlockSpec((None, kdim, 2 * HD), lambda s, i: (s, 0, 0)),
                  pl.BlockSpec((None, 8, kdim), lambda s, i: (s, 0, 0)),
                  pl.BlockSpec((None, HD, HD), lambda s, i: (s, 0, 0))],
        out_specs=pl.BlockSpec((None, nseg, HD), lambda s, i: (s, i, 0)),
        out_shape=jax.ShapeDtypeStruct((2, r, HD), F32),
        compiler_params=_cparams(("parallel", "parallel")),
        name="compress",
    )(seg, w1cat, pe2, w2)


def _nsa_prompt_kernel(qt_ref, ag_ref, az_ref, kst_ref, vst_ref, kwt_ref, vwt_ref, kc_ref, vc_ref, o_ref,
                       ks_scr, vs_scr, kw_scr, vw_scr, bias_scr, *, t, tq, ck, nseg):
    g = pl.program_id(1)
    i = pl.program_id(2)
    s0 = i * tq
    ns = t // SLC_BLK
    nsr = bias_scr.shape[0]
    cols = NSA_REP * tq

    @pl.when(i == 0)
    def _():
        ks_scr[...] = jnp.transpose(kst_ref[...]).astype(BF16)
        kw_scr[...] = jnp.transpose(kwt_ref[...]).astype(BF16)
        ones = jnp.ones((ONES_ROWS, t), BF16)
        vs_scr[...] = jnp.concatenate([vst_ref[...].astype(BF16), ones], axis=0)
        vw_scr[...] = jnp.concatenate([vwt_ref[...].astype(BF16), ones], axis=0)

    qt = qt_ref[...]
    q2 = jnp.concatenate([qt[r * HD:(r + 1) * HD, :] for r in range(NSA_REP)], axis=1) * (HD ** -0.5 * LOG2E)
    q2b = q2.astype(BF16)
    tpos1 = s0 + lax.broadcasted_iota(jnp.int32, (1, tq), 1)
    tpos = jnp.concatenate([tpos1] * NSA_REP, axis=1)

    crow = lax.broadcasted_iota(jnp.int32, (nseg, 1), 0)
    m_c = (crow * CMP_STRIDE + (CMP_LEN - 1) <= tpos) & (crow < nseg - 1)
    s_c = jnp.where(m_c, _dot3(kc_ref[...], q2), NEG)
    p_c = jnp.where(m_c, jnp.exp2(s_c - jnp.max(s_c, axis=0, keepdims=True)), 0.0)
    l_c = jnp.sum(p_c, axis=0, keepdims=True)
    p_c = p_c / jnp.where(l_c > 0.0, l_c, 1.0)
    o_c = _dot(jnp.transpose(vc_ref[...]), p_c)
    psum = p_c[:, 0:tq]
    for r in range(1, NSA_REP):
        psum = psum + p_c[:, r * tq:(r + 1) * tq]

    srow = lax.broadcasted_iota(jnp.int32, (nsr, 1), 0)
    ccol = lax.broadcasted_iota(jnp.int32, (1, nseg), 1) * CMP_STRIDE
    cover = jnp.where((ccol + CMP_LEN > srow * SLC_BLK) & (ccol < (srow + 1) * SLC_BLK), 1.0, 0.0).astype(BF16)
    imp = _dot_exact_lhs(cover, psum)
    cur = tpos1 // SLC_BLK
    valid = (srow * SLC_BLK <= tpos1) & (srow < ns)
    forced = (srow == 0) | (srow == cur) | (srow == cur - 1)
    score = jnp.where(valid, jnp.where(forced, FORCE, imp), NEG)
    rank = jnp.zeros((nsr, tq), F32)
    for j in range(ns):
        sj = score[j:j + 1, :]
        rank = rank + jnp.where((sj > score) | ((sj == score) & (j < srow)), 1.0, 0.0)
    sel = (rank < float(min(SLC_TOPK, ns))) & valid
    bias_scr[...] = jnp.where(sel, 0.0, NEG)

    def attend(k_rows, v_cols, bias1, m, acc, q_cols=q2b):
        w = bias1.shape[1]
        s = _dot(k_rows, q_cols)
        s = jnp.concatenate([s[:, r * w:(r + 1) * w] + bias1 for r in range(NSA_REP)], axis=1)
        m_new = jnp.maximum(m, jnp.max(s, axis=0, keepdims=True))
        p = jnp.exp2(s - m_new)
        return m_new, jnp.exp2(m - m_new) * acc + _dot(v_cols, p)

    def finish(acc):
        l = acc[HD:HD + 1, :]
        return acc[:HD, :] / jnp.where(l > 0.0, l, 1.0)

    blk_per_chunk = ck // SLC_BLK
    m0 = jnp.full((1, cols), NEG, F32)
    acc0 = jnp.zeros((HD + ONES_ROWS, cols), F32)

    def sel_chunk(j, carry):
        k0 = pl.multiple_of(j * ck, ck)
        npos = k0 + lax.broadcasted_iota(jnp.int32, (ck, 1), 0)
        rows = [jnp.broadcast_to(bias_scr[pl.ds(j * blk_per_chunk + k, 1), :], (SLC_BLK, tq))
                for k in range(blk_per_chunk)]
        bias1 = jnp.where(npos <= tpos1, jnp.concatenate(rows, axis=0), NEG)
        return attend(ks_scr[pl.ds(k0, ck), :], vs_scr[:, pl.ds(k0, ck)], bias1, *carry)

    n_chunks = (s0 + tq + ck - 1) // ck
    o_s = finish(lax.fori_loop(0, n_chunks, sel_chunk, (m0, acc0))[1])

    wq = min(WIN_Q, tq)
    ww = min(WINDOW + wq, t)
    parts = []
    for a in range(tq // wq):
        w0 = pl.multiple_of(jnp.clip(s0 + a * wq - WINDOW, 0, t - ww), LANE)
        wpos = w0 + lax.broadcasted_iota(jnp.int32, (ww, 1), 0)
        tsub = tpos1[:, a * wq:(a + 1) * wq]
        bias_w = jnp.where((wpos <= tsub) & (wpos > tsub - WINDOW), 0.0, NEG)
        q_sub = jnp.concatenate([q2b[:, r * tq + a * wq:r * tq + (a + 1) * wq] for r in range(NSA_REP)], axis=1)
        parts.append(finish(attend(kw_scr[pl.ds(w0, ww), :], vw_scr[:, pl.ds(w0, ww)], bias_w,
                                   m0[:, :NSA_REP * wq], acc0[:, :NSA_REP * wq], q_sub)[1]))
    o_w = jnp.concatenate([parts[a][:, r * wq:(r + 1) * wq] for r in range(NSA_REP) for a in range(tq // wq)], axis=1)

    outs = []
    for r in range(NSA_REP):
        sl = slice(r * tq, (r + 1) * tq)
        gt = [_sigmoid(ag_ref[pl.ds(3 * (NSA_REP * g + r) + j, 1), :]) for j in range(3)]
        outs.append(jnp.transpose(gt[0] * o_c[:, sl] + gt[1] * o_s[:, sl] + gt[2] * o_w[:, sl]))
    o = jnp.concatenate(outs, axis=1) * _silu(az_ref[...].astype(F32))
    o_ref[...] = o.astype(o_ref.dtype)


def nsa_prompt(ug, qgt, stacks, layer, kc, nb, t, *, tq, ck):
    tq = min(tq, t)
    nq = t // tq
    nseg = t // CMP_STRIDE
    ck = min(ck, t)
    gw = NSA_REP * HD
    nsr = -(-(t // SLC_BLK) // SUBLANE) * SUBLANE
    kv_spec = pl.BlockSpec((None, None, HD, t), lambda b, g, i: (layer, b, g, 0))
    return pl.pallas_call(
        functools.partial(_nsa_prompt_kernel, t=t, tq=tq, ck=ck, nseg=nseg),
        grid=(nb, NSA_KV, nq),
        in_specs=[pl.BlockSpec((None, gw, tq), lambda b, g, i: (b, g, i)),
                  pl.BlockSpec((None, LANE, tq), lambda b, g, i: (b, TQ_AG // LANE, i)),
                  pl.BlockSpec((tq, gw), lambda b, g, i: (b * nq + i, _UG_OFF["a_z"] // gw + g)),
                  kv_spec, kv_spec, kv_spec, kv_spec,
                  pl.BlockSpec((None, nseg, HD), lambda b, g, i: (0, b * NSA_KV + g, 0)),
                  pl.BlockSpec((None, nseg, HD), lambda b, g, i: (1, b * NSA_KV + g, 0))],
        out_specs=pl.BlockSpec((tq, gw), lambda b, g, i: (b * nq + i, g)),
        out_shape=jax.ShapeDtypeStruct((nb * t, NSA_HEADS * HD), BF16),
        scratch_shapes=[pltpu.VMEM((t, HD), BF16), pltpu.VMEM((HD + ONES_ROWS, t), BF16),
                        pltpu.VMEM((t, HD), BF16), pltpu.VMEM((HD + ONES_ROWS, t), BF16),
                        pltpu.VMEM((nsr, tq), F32)],
        compiler_params=_cparams(("parallel", "parallel", "arbitrary")),
        name="nsa_prompt",
    )(qgt, qgt, ug, stacks[2], stacks[3], stacks[4], stacks[5], kc, kc)


def _gla_kernel(q_ref, k_ref, v_ref, z_ref, sm_ref, s0_ref, wa2_ref, ba_ref, gn_ref, o_ref, sout_ref, s_scr,
                *, c, sub, t_valid):
    ci = pl.program_id(1)
    rows = c * sub

    @pl.when(ci == 0)
    def _():
        s_scr[...] = s0_ref[...]

    row = ci * rows + lax.broadcasted_iota(jnp.int32, (rows, 1), 0)
    gate_in = sm_ref[...][:, SM_BA:SM_BA + GLA_RANK]
    log_a = _log_sigmoid(_dot3(gate_in, wa2_ref[...]) + ba_ref[...]) / GLA_TAU
    log_a = jnp.where(row < t_valid, log_a, 0.0)
    causal = lax.broadcasted_iota(jnp.int32, (c, c), 0) >= lax.broadcasted_iota(jnp.int32, (c, c), 1)
    tri = jnp.where(causal, 1.0, 0.0).astype(BF16)
    cb_all = jnp.concatenate([_dot_exact_lhs(tri, log_a[j * c:(j + 1) * c, :]) for j in range(sub)], axis=0)
    for h in range(GLA_HEADS):
        ks = slice(h * GLA_HK, (h + 1) * GLA_HK)
        vs = slice(h * GLA_HV, (h + 1) * GLA_HV)
        state = s_scr[h]
        outs = []
        for j in range(sub):
            rs = slice(j * c, (j + 1) * c)
            cb = cb_all[rs, ks]
            ecb = jnp.exp(cb)
            qd = q_ref[rs, ks].astype(F32) * (GLA_HK ** -0.5) * ecb
            kh = k_ref[rs, ks].astype(F32)
            vh = v_ref[rs, vs]
            att = jnp.where(causal, _dot(qd, kh * jnp.exp(-cb), nt=True), 0.0)
            outs.append(_dot(att, vh) + _dot(qd, state))
            c_end = cb[c - 1:c, :]
            k_dec = kh * jnp.exp(c_end - cb)
            e_col = jnp.transpose(ecb[c - SUBLANE:c, :])[:, SUBLANE - 1:SUBLANE]
            state = e_col * state + _dot(k_dec, vh, tn=True)
        o = jnp.concatenate(outs, axis=0)
        y = o * lax.rsqrt(jnp.mean(o * o, axis=-1, keepdims=True) + EPS) * gn_ref[...]
        o_ref[:, vs] = (y * _silu(z_ref[:, vs].astype(F32))).astype(o_ref.dtype)
        s_scr[h] = state

    @pl.when(ci == pl.num_programs(1) - 1)
    def _():
        sout_ref[...] = s_scr[...]


def gla_mixer(ug, ux, s0, wa2, ba, gnorm, nb, t, *, c, sub, t_valid):
    rows = c * sub
    nc = t // rows

    def u_spec(name, width):
        return pl.BlockSpec((rows, width), lambda b, i: (b * nc + i, _UG_OFF[name] // width))

    s_spec = pl.BlockSpec((None, GLA_HEADS, GLA_HK, GLA_HV), lambda b, i: (b, 0, 0, 0))
    return pl.pallas_call(
        functools.partial(_gla_kernel, c=c, sub=sub, t_valid=t_valid),
        grid=(nb, nc),
        in_specs=[u_spec("b_q", GLA_DK), u_spec("b_k", GLA_DK), u_spec("b_v", GLA_DV), u_spec("b_z", GLA_DV),
                  pl.BlockSpec((rows, LANE), lambda b, i: (b * nc + i, OFF_SMALL // LANE)),
                  s_spec,
                  pl.BlockSpec((GLA_RANK, GLA_DK), lambda b, i: (0, 0)),
                  pl.BlockSpec((1, GLA_DK), lambda b, i: (0, 0)),
                  pl.BlockSpec((1, GLA_HV), lambda b, i: (0, 0))],
        out_specs=[pl.BlockSpec((rows, GLA_DV), lambda b, i: (b * nc + i, 0)), s_spec],
        out_shape=[jax.ShapeDtypeStruct((nb * t, GLA_DV), BF16),
                   jax.ShapeDtypeStruct((nb, GLA_HEADS, GLA_HK, GLA_HV), F32)],
        scratch_shapes=[pltpu.VMEM((GLA_HEADS, GLA_HK, GLA_HV), F32)],
        compiler_params=_cparams(("parallel", "arbitrary")),
        name="gla",
    )(ug, ug, ug, ug, ux, s0, wa2, ba.reshape(1, GLA_DK), gnorm.reshape(1, GLA_HV))


def _ssd_kernel(xbc_ref, z_ref, sm_ref, s0_ref, cbuf_ref, cw_ref, cbias_ref, dtb_ref, alog_ref, dsk_ref, mn_ref,
                o_ref, sout_ref, cout_ref, s_scr, f_scr, *, q, t_valid, n_chunks):
    ci = pl.program_id(1)
    npad = SUBLANE
    hist = M_CONV - 1

    @pl.when(ci == 0)
    def _():
        s_scr[...] = s0_ref[...].reshape(s_scr.shape)
        f_scr[npad - hist:npad, :] = cbuf_ref[...]

    f_scr[npad:npad + q, :] = xbc_ref[...]
    conv = cbias_ref[...] + xbc_ref[...] * cw_ref[hist:hist + 1, :]
    full = f_scr[...]
    for i in range(hist):
        conv = conv + pltpu.roll(full, shift=hist - i, axis=0)[npad:npad + q, :] * cw_ref[i:i + 1, :]
    xbc = _silu(conv)

    last_valid = t_valid - (n_chunks - 1) * q

    @pl.when(ci == n_chunks - 1)
    def _():
        cout_ref[...] = f_scr[npad - hist + last_valid:npad + last_valid, :]

    f_scr[npad - hist:npad, :] = f_scr[npad - hist + q:npad + q, :]

    lane = lax.broadcasted_iota(jnp.int32, (1, LANE), 1)
    row = ci * q + lax.broadcasted_iota(jnp.int32, (q, 1), 0)
    head_lane = (lane >= SM_DT) & (lane < SM_DT + M_HEADS)
    dt = jnp.where(head_lane & (row < t_valid), _softplus(sm_ref[...] + dtb_ref[...]), 0.0)
    a = dt * (-jnp.exp(alog_ref[...]))
    tri = jnp.where(lax.broadcasted_iota(jnp.int32, (q, q), 0) >= lax.broadcasted_iota(jnp.int32, (q, q), 1), 1.0, 0.0)
    causal = tri > 0.5
    cum = _dot_exact_lhs(tri.astype(BF16), a)
    cum_t = jnp.transpose(cum)
    dt_t = jnp.transpose(dt)
    nbc = M_GROUPS * M_STATE
    rep = M_HEADS // M_GROUPS
    gw = rep * M_HD
    head_of_lane = lax.broadcasted_iota(jnp.int32, (1, gw), 1) // M_HD
    head_of_row = lax.broadcasted_iota(jnp.int32, (gw, 1), 0) // M_HD
    for g in range(M_GROUPS):
        sl = slice(g * gw, (g + 1) * gw)
        bg = xbc[:, M_DINNER + g * M_STATE:M_DINNER + (g + 1) * M_STATE]
        cg = xbc[:, M_DINNER + nbc + g * M_STATE:M_DINNER + nbc + (g + 1) * M_STATE]
        xg = xbc[:, sl]
        cbm = _dot(cg, bg, nt=True)
        s_prev = s_scr[g]
        y = jnp.zeros((q, gw), F32)
        e_lanes = jnp.zeros((q, gw), F32)
        w_lanes = jnp.zeros((q, gw), F32)
        skip = jnp.zeros((1, gw), F32)
        keep = jnp.zeros((gw, 1), F32)
        for r in range(rep):
            hl = SM_DT + g * rep + r
            mine = head_of_lane == r
            cum_c = cum[:, hl:hl + 1]
            c_end = cum[q - 1:q, hl:hl + 1]
            decay = jnp.exp(jnp.where(causal, cum_c - cum_t[hl:hl + 1, :], -jnp.inf))
            y = y + _dot(cbm * decay * dt_t[hl:hl + 1, :], jnp.where(mine, xg, 0.0))
            e_lanes = jnp.where(mine, jnp.exp(cum_c), e_lanes)
            w_lanes = jnp.where(mine, jnp.exp(c_end - cum_c) * dt[:, hl:hl + 1], w_lanes)
            skip = jnp.where(mine, dsk_ref[:, hl:hl + 1], skip)
            keep = jnp.where(head_of_row == r, jnp.exp(c_end), keep)
        y = y + e_lanes * _dot(cg, s_prev, nt=True) + skip * xg
        s_scr[g] = keep * s_prev + _dot(xg * w_lanes, bg, tn=True)
        yg = y * _silu(z_ref[:, sl].astype(F32))
        yn = yg * lax.rsqrt(jnp.mean(yg * yg, axis=-1, keepdims=True) + EPS) * mn_ref[:, sl]
        o_ref[:, sl] = yn.astype(o_ref.dtype)

    @pl.when(ci == n_chunks - 1)
    def _():
        sout_ref[...] = s_scr[...].reshape(M_HEADS, M_HD, M_STATE)


def _pad_heads(v):
    return jnp.zeros((1, LANE), F32).at[0, SM_DT:SM_DT + M_HEADS].set(v)


def ssd_mixer(ug, ux, s0, cbuf, conv_w, conv_b, dt_bias, a_log, d_skip, m_norm, nb, t, *, q, t_valid):
    nc = t // q
    n_chunks = -(-t_valid // q)
    assert n_chunks == nc
    hist = M_CONV - 1

    def u_spec(off, name, width):
        return pl.BlockSpec((q, width), lambda b, i: (b * nc + i, off[name] // width))

    def full(shape):
        return pl.BlockSpec(shape, lambda b, i: (0,) * len(shape))

    s_spec = pl.BlockSpec((None, M_HEADS, M_HD, M_STATE), lambda b, i: (b, 0, 0, 0))
    c_spec = pl.BlockSpec((None, hist, M_CONVDIM), lambda b, i: (b, 0, 0))
    return pl.pallas_call(
        functools.partial(_ssd_kernel, q=q, t_valid=t_valid, n_chunks=n_chunks),
        grid=(nb, nc),
        in_specs=[u_spec(_UX_OFF, "c_xbc", M_CONVDIM), u_spec(_UG_OFF, "c_z", M_DINNER),
                  pl.BlockSpec((q, LANE), lambda b, i: (b * nc + i, OFF_SMALL // LANE)),
                  s_spec, c_spec,
                  full((M_CONV, M_CONVDIM)), full((1, M_CONVDIM)), full((1, LANE)), full((1, LANE)),
                  full((1, LANE)), full((1, M_DINNER))],
        out_specs=[pl.BlockSpec((q, M_DINNER), lambda b, i: (b * nc + i, 0)), s_spec, c_spec],
        out_shape=[jax.ShapeDtypeStruct((nb * t, M_DINNER), BF16),
                   jax.ShapeDtypeStruct((nb, M_HEADS, M_HD, M_STATE), F32),
                   jax.ShapeDtypeStruct((nb, hist, M_CONVDIM), F32)],
        scratch_shapes=[pltpu.VMEM((M_GROUPS, M_HEADS // M_GROUPS * M_HD, M_STATE), F32),
                        pltpu.VMEM((q + SUBLANE, M_CONVDIM), F32)],
        compiler_params=_cparams(("parallel", "arbitrary")),
        name="ssd",
    )(ux, ug, ux, s0, cbuf, conv_w, conv_b.reshape(1, M_CONVDIM), _pad_heads(dt_bias), _pad_heads(a_log),
      _pad_heads(d_skip), m_norm.reshape(1, M_DINNER))


SEG_PER_PAGE = PAGE // CMP_STRIDE


def _page_cmp_kernel(pt_ref, *refs, npg):
    w1_ref, ok_ref, ov_ref, scr, wbd_scr = refs[2 * npg:]

    @pl.when((pl.program_id(0) == 0) & (pl.program_id(1) == 0))
    def _():
        wbd_scr[...] = jnp.zeros(wbd_scr.shape, wbd_scr.dtype)
        for s in range(2):
            for j in range(CMP_STRIDE):
                wj = w1_ref[s, j * HD:(j + 1) * HD, :].astype(BF16)
                for g in range(NSA_KV):
                    wbd_scr[s, j, g * HD:(g + 1) * HD, g * 2 * HD:(g + 1) * 2 * HD] = wj

    row = lax.broadcasted_iota(jnp.int32, (PAGE, PAGE), 0)
    pos = lax.broadcasted_iota(jnp.int32, (PAGE, PAGE), 1)
    regroup = jnp.where(pos == CMP_STRIDE * (row % SEG_PER_PAGE) + row // SEG_PER_PAGE, 1.0, 0.0).astype(BF16)
    for s, (pages, o_ref) in enumerate(((refs[:npg], ok_ref), (refs[npg:2 * npg], ov_ref))):
        for k in range(npg):
            by_j = _dot(regroup, pages[k][...].reshape(NSA_KV * HD, PAGE), nt=True)
            for j in range(CMP_STRIDE):
                scr[j, k * SEG_PER_PAGE:(k + 1) * SEG_PER_PAGE, :] = by_j[j * SEG_PER_PAGE:(j + 1) * SEG_PER_PAGE, :]
        acc = jnp.zeros((npg * SEG_PER_PAGE, 2 * HD * NSA_KV), F32)
        for j in range(CMP_STRIDE):
            acc = acc + _dot(scr[j], wbd_scr[s, j])
        o_ref[...] = acc


def page_compress(cache_k_t, cache_v_t, page_table, layer, w1cat, *, npg):
    nb, n_pages = page_table.shape
    steps = n_pages // npg

    def page_spec(k):
        return pl.BlockSpec((None, None, NSA_KV, HD, PAGE),
                            lambda b, i, pt: (layer, pt[b, i * npg + k], 0, 0, 0))

    w_spec = pl.BlockSpec((None, 2, CMP_STRIDE * HD, 2 * HD), lambda b, i, pt: (layer, 0, 0, 0))
    o_spec = pl.BlockSpec((None, npg * SEG_PER_PAGE, NSA_KV * 2 * HD), lambda b, i, pt: (b, i, 0))
    o_shape = jax.ShapeDtypeStruct((nb, n_pages * SEG_PER_PAGE, NSA_KV * 2 * HD), F32)
    return pl.pallas_call(
        functools.partial(_page_cmp_kernel, npg=npg),
        grid_spec=pltpu.PrefetchScalarGridSpec(
            num_scalar_prefetch=1,
            grid=(nb, steps),
            in_specs=[page_spec(k) for k in range(npg)] * 2 + [w_spec],
            out_specs=[o_spec, o_spec],
            scratch_shapes=[pltpu.VMEM((CMP_STRIDE, npg * SEG_PER_PAGE, NSA_KV * HD), F32),
                            pltpu.VMEM((2, CMP_STRIDE, NSA_KV * HD, NSA_KV * 2 * HD), BF16)]),
        out_shape=[o_shape, o_shape],
        compiler_params=_cparams(("arbitrary", "arbitrary")),
        name="page_compress",
    )(page_table, *([cache_k_t] * npg), *([cache_v_t] * npg), w1cat)


def _masked_softmax_rows(s, mask):
    s = jnp.where(mask, s, NEG)
    m = jnp.max(s, axis=-1, keepdims=True)
    p = jnp.where(mask, jnp.exp(s - m), 0.0)
    l = jnp.sum(p, axis=-1, keepdims=True)
    return p / jnp.where(l > 0.0, l, 1.0)


def _nsa_sample_cmp_kernel(abk_ref, abv_ref, kvs_ref, q_ref, w1_ref, pe_ref, w2_ref, cover_ref, oc_ref, sel_ref,
                           cst_scr, *, p_len, nc, sl):
    ns = p_len // SLC_BLK + 1
    rowi = lax.broadcasted_iota(jnp.int32, (nc, 1), 0)

    @pl.when(pl.program_id(0) == 0)
    def _():
        for idx in range(2):
            pe = _dot3(pe_ref[idx], w1_ref[idx])
            cst_scr[idx:idx + 1, :] = pe[0:1, :HD] + pe[1:2, HD:]

    cidx = lax.broadcasted_iota(jnp.int32, (1, nc), 1)
    m_c = cidx * CMP_STRIDE + (CMP_LEN - 1) <= p_len
    sidx = lax.broadcasted_iota(jnp.int32, (1, sl), 1)
    cur = p_len // SLC_BLK
    valid = (sidx * SLC_BLK <= p_len) & (sidx < ns)
    forced = (sidx == 0) | (sidx == cur) | (sidx == cur - 1)
    ri = lax.broadcasted_iota(jnp.int32, (sl, sl), 0)
    li = lax.broadcasted_iota(jnp.int32, (sl, sl), 1)
    kl = lax.broadcasted_iota(jnp.int32, (sl, LANE), 1).astype(F32)
    sv = lax.broadcasted_iota(jnp.int32, (sl, LANE), 0).astype(F32)

    for g in range(NSA_KV):
        def compressed(ab_ref, idx):
            ab = ab_ref[:, g * 2 * HD:(g + 1) * 2 * HD]
            new = kvs_ref[idx * NSA_KV + g:idx * NSA_KV + g + 1, :]
            b_new = _dot(new, w1_ref[idx][0:HD, HD:])
            nxt = jnp.where(rowi == nc - 1, b_new, pltpu.roll(ab[:, HD:], shift=nc - 1, axis=0))
            return _dot(_silu(ab[:, :HD] + nxt + cst_scr[idx:idx + 1, :]), w2_ref[idx])

        kc = compressed(abk_ref, 0)
        vc = compressed(abv_ref, 1)
        p_c = _masked_softmax_rows(_dot3(q_ref[g] * (HD ** -0.5), kc, nt=True), m_c)
        oc_ref[g] = _dot(p_c, vc)
        imp = _dot_exact_rhs(jnp.sum(p_c, axis=0, keepdims=True), cover_ref[...])
        score = jnp.where(valid, jnp.where(forced, FORCE, imp), NEG)
        score_b = jnp.broadcast_to(score, (sl, sl))
        score_col = jnp.sum(jnp.where(ri == li, score_b, 0.0), axis=1, keepdims=True)
        ahead = (score_b > score_col) | ((score_b == score_col) & (li < ri))
        rank_col = jnp.sum(jnp.where(ahead, 1.0, 0.0), axis=1, keepdims=True)
        picked = jnp.sum(jnp.where(rank_col == kl, sv, 0.0), axis=0, keepdims=True)
        sel_ref[g] = picked.astype(jnp.int32)


def _cover_matrix(nc, sl):
    c0 = np.arange(nc)[:, None] * CMP_STRIDE
    s0 = np.arange(sl)[None, :] * SLC_BLK
    return jnp.asarray(((c0 + CMP_LEN > s0) & (c0 < s0 + SLC_BLK)).astype(np.float32), dtype=BF16)


def nsa_sample_cmp(abk, abv, kvs, q3, w1cat, pe2, w2, p_len):
    nb, nc, _ = abk.shape
    ns = p_len // SLC_BLK + 1
    sl = -(-ns // LANE) * LANE
    cover = _cover_matrix(nc, sl)

    ab_spec = pl.BlockSpec((None, nc, NSA_KV * 2 * HD), lambda b: (b, 0, 0))

    def full(a):
        return pl.BlockSpec(a.shape, lambda b: (0,) * a.ndim)

    row_spec = pl.BlockSpec((None, NSA_KV, NSA_REP, HD), lambda b: (b, 0, 0, 0))
    return pl.pallas_call(
        functools.partial(_nsa_sample_cmp_kernel, p_len=p_len, nc=nc, sl=sl),
        grid=(nb,),
        in_specs=[ab_spec, ab_spec, pl.BlockSpec((None, 6 * NSA_KV, HD), lambda b: (b, 0, 0)), row_spec,
                  full(w1cat), full(pe2), full(w2), full(cover)],
        out_specs=[row_spec, pl.BlockSpec((None, NSA_KV, 1, LANE), lambda b: (b, 0, 0, 0))],
        out_shape=[jax.ShapeDtypeStruct((nb, NSA_KV, NSA_REP, HD), F32),
                   jax.ShapeDtypeStruct((nb, NSA_KV, 1, LANE), jnp.int32)],
        scratch_shapes=[pltpu.VMEM((SUBLANE, HD), F32)],
        compiler_params=_cparams(("arbitrary",)),
        name="nsa_sample_cmp",
    )(abk, abv, kvs, q3, w1cat, pe2, w2, cover)


def _nsa_sample_sel_kernel(sel_ref, pt_ref, *refs, p_len, nb, n_buf):
    nk = SLC_TOPK
    kpages, vpages = refs[:nk], refs[nk:2 * nk]
    (kwb_ref, vwb_ref, kvt_ref, kvs_ref, q_ref, az_ref, sm_ref, oc_ref,
     o_ref, kwo_ref, vwo_ref) = refs[2 * nk:]
    b = pl.program_id(0)
    g = pl.program_id(1)
    n_past = p_len // SLC_BLK
    q4 = q_ref[...] * (HD ** -0.5)
    lane_b = lax.broadcasted_iota(jnp.int32, (1, nb), 1) == b

    def new_row(i):
        return kvs_ref[pl.ds(i * NSA_KV + g, 1), :]

    def new_col(i):
        blk = kvt_ref[pl.ds(pl.multiple_of((i * NSA_KV + g) * HD, HD), HD), :]
        return jnp.sum(jnp.where(lane_b, blk, 0.0), axis=1, keepdims=True)

    half_of_lane = lax.broadcasted_iota(jnp.int32, (1, PAGE), 1) // SLC_BLK
    s_parts, m_parts = [], []
    has_new = jnp.zeros((1, 1), jnp.int32)
    for k in range(nk):
        blk = sel_ref[b, g, k]
        m_parts.append((half_of_lane == blk % 2) & (blk < n_past))
        s_parts.append(_dot(q4, kpages[k][...]))
        has_new = has_new + jnp.where(blk == n_past, 1, 0)
    s_sel = jnp.concatenate(s_parts, axis=1)
    m_sel = jnp.concatenate(m_parts, axis=1)
    new_ok = has_new > 0
    s_new = jnp.where(new_ok, jnp.sum(q4 * new_row(2), axis=1, keepdims=True), NEG)
    s_sel = jnp.where(m_sel, s_sel, NEG)
    mx = jnp.maximum(jnp.max(s_sel, axis=1, keepdims=True), s_new)
    p_sel = jnp.where(m_sel, jnp.exp(s_sel - mx), 0.0)
    p_new = jnp.where(new_ok, jnp.exp(s_new - mx), 0.0)
    den = jnp.sum(p_sel, axis=1, keepdims=True) + p_new
    acc = p_new * new_row(3)
    for k in range(nk):
        acc = acc + _dot(p_sel[:, k * PAGE:(k + 1) * PAGE], vpages[k][...], nt=True)
    o_s = acc / jnp.where(den > 0.0, den, 1.0)

    kwb = kwb_ref[...]
    vwb = vwb_ref[...]
    wlane = lax.broadcasted_iota(jnp.int32, (1, n_buf), 1)
    m_w = wlane > n_buf - WINDOW
    s_w = jnp.where(m_w, _dot(q4, kwb), NEG)
    s_wn = jnp.sum(q4 * new_row(4), axis=1, keepdims=True)
    mw = jnp.maximum(jnp.max(s_w, axis=1, keepdims=True), s_wn)
    p_w = jnp.where(m_w, jnp.exp(s_w - mw), 0.0)
    p_wn = jnp.exp(s_wn - mw)
    o_w = (_dot(p_w, vwb, nt=True) + p_wn * new_row(5)) / (jnp.sum(p_w, axis=1, keepdims=True) + p_wn)
    kwo_ref[...] = jnp.where(wlane == n_buf - 1, new_col(4), pltpu.roll(kwb, shift=n_buf - 1, axis=1))
    vwo_ref[...] = jnp.where(wlane == n_buf - 1, new_col(5), pltpu.roll(vwb, shift=n_buf - 1, axis=1))

    gate_row = _sigmoid(pltpu.roll(sm_ref[pl.ds(b, 1), :], shift=(LANE - 3 * NSA_REP * g) % LANE, axis=1))
    gl = lax.broadcasted_iota(jnp.int32, (NSA_REP, LANE), 1)
    gr = lax.broadcasted_iota(jnp.int32, (NSA_REP, LANE), 0)

    def gate(j):
        return jnp.sum(jnp.where(gl == 3 * gr + j, gate_row, 0.0), axis=1, keepdims=True)

    o = gate(0) * oc_ref[...] + gate(1) * o_s + gate(2) * o_w
    o_ref[...] = o * _silu(az_ref[...])


def nsa_sample_sel(sel, page_table, slc_k_t, slc_v_t, win_k_t, win_v_t, layer, kvt, kvs, q3, az3, ux, o_cmp, p_len):
    nb = page_table.shape[0]
    n_buf = win_k_t.shape[-1]
    last_blk = p_len // SLC_BLK - 1

    def page_spec(k):
        def imap(b, g, sel_r, pt_r):
            blk = jnp.minimum(sel_r[b, g, k], last_blk)
            return (layer, pt_r[b, blk // (PAGE // SLC_BLK)], g, 0, 0)
        return pl.BlockSpec((None, None, None, HD, PAGE), imap)

    win_spec = pl.BlockSpec((None, None, None, HD, n_buf), lambda b, g, s, p: (layer, b, g, 0, 0))
    wout_spec = pl.BlockSpec((None, None, HD, n_buf), lambda b, g, s, p: (b, g, 0, 0))
    row_spec = pl.BlockSpec((None, None, NSA_REP, HD), lambda b, g, s, p: (b, g, 0, 0))

    def full(a):
        return pl.BlockSpec(a.shape, lambda b, g, s, p: (0,) * a.ndim)

    small = pl.BlockSpec((nb, LANE), lambda b, g, s, p: (0, OFF_SMALL // LANE))
    return pl.pallas_call(
        functools.partial(_nsa_sample_sel_kernel, p_len=p_len, nb=nb, n_buf=n_buf),
        grid_spec=pltpu.PrefetchScalarGridSpec(
            num_scalar_prefetch=2,
            grid=(nb, NSA_KV),
            in_specs=[page_spec(k) for k in range(SLC_TOPK)] * 2
                     + [win_spec, win_spec, full(kvt), pl.BlockSpec((None, 6 * NSA_KV, HD), lambda b, g, s, p: (b, 0, 0)),
                        row_spec, row_spec, small, row_spec],
            out_specs=[row_spec, wout_spec, wout_spec]),
        out_shape=[jax.ShapeDtypeStruct((nb, NSA_KV, NSA_REP, HD), F32),
                   jax.ShapeDtypeStruct((nb, NSA_KV, HD, n_buf), F32),
                   jax.ShapeDtypeStruct((nb, NSA_KV, HD, n_buf), F32)],
        compiler_params=_cparams(("parallel", "arbitrary")),
        name="nsa_sample_sel",
    )(sel, page_table, *([slc_k_t] * SLC_TOPK), *([slc_v_t] * SLC_TOPK), win_k_t, win_v_t, kvt, kvs, q3, az3, ux, o_cmp)


def _prep_weights(w_in, cmp_pe_k, cmp_w1_k, cmp_w2_k, cmp_pe_v, cmp_w1_v, cmp_w2_v, w_br_a, w_br_b, w_br_c, w_out):
    wt = jnp.transpose(w_in, (0, 2, 1))

    def gather_rows(order, total):
        parts = [wt[:, _IN_OFF[n]:_IN_OFF[n] + _IN_SZ[n]] for n in order]
        used = sum(_IN_SZ[n] for n in order)
        parts.append(jnp.zeros((wt.shape[0], total - used, wt.shape[2]), wt.dtype))
        return jnp.concatenate(parts, axis=1).astype(BF16)

    half = CMP_STRIDE * HD

    def cat(w1):
        return jnp.concatenate([w1[:, :half], w1[:, half:]], axis=2)

    def pe_rows(pe):
        d = pe.shape[0]
        rows = pe.reshape(d, 2, half)
        return jnp.concatenate([rows, jnp.zeros((d, 6, half), pe.dtype)], axis=1)

    w1cat = jnp.stack([cat(cmp_w1_k), cat(cmp_w1_v)], axis=1)
    pe2 = jnp.stack([pe_rows(cmp_pe_k), pe_rows(cmp_pe_v)], axis=1)
    w2 = jnp.stack([cmp_w2_k, cmp_w2_v], axis=1)
    return dict(wt_ug=gather_rows(_UG_ORDER, NP_UG), wt_ux=gather_rows(_UX_ORDER, NP_UX), wt_kv=gather_rows(("a_kv",), KV_ROWS),
                wt_q=gather_rows(_TQ_ORDER, TQ_ROWS),
                w1cat=w1cat, pe2=pe2, w2=w2,
                wa=w_br_a.astype(BF16), wb=w_br_b.astype(BF16), wc=w_br_c.astype(BF16), wo=w_out.astype(BF16))


def _prompt_layer(x, h, nb, t, layer, w, p, next_norm, tiles, stacks):
    ug = matmul(h, w["wt_ug"], nt=True, tm=tiles["mm_tm"], tn=tiles["in_tn"], out_dtype=BF16, name="in_proj",
                b_layer=layer)
    ux = matmul(h, w["wt_ux"], nt=True, tm=tiles["mm_tm"], tn=NP_UX, name="in_proj_x", b_layer=layer)
    stacks = kv_proj_stacked(w["wt_kv"], layer, h, nb, t, stacks, tn=tiles["kv_tn"])
    qgt = matmul_t(w["wt_q"], layer, h, nb, t, tm=TQ_TM, tn=tiles["q_tn"], name="q_proj_t")
    nseg = t // CMP_STRIDE
    seg = jnp.stack([stacks[0][layer], stacks[1][layer]]).reshape(2, nb, NSA_KV, HD, t)
    seg = jnp.transpose(seg, (0, 1, 2, 4, 3)).reshape(2, nb * NSA_KV * nseg, CMP_STRIDE * HD)
    kc = compress_blocks(seg, w["w1cat"][layer], w["pe2"][layer], w["w2"][layer], nseg)
    o_a = nsa_prompt(ug, qgt, stacks, layer, kc, nb, t, tq=tiles["tq"], ck=tiles["ck"])
    zeros_gla = jnp.zeros((nb, GLA_HEADS, GLA_HK, GLA_HV), F32)
    zeros_ssm = jnp.zeros((nb, M_HEADS, M_HD, M_STATE), F32)
    zeros_conv = jnp.zeros((nb, M_CONV - 1, M_CONVDIM), F32)
    o_b, gla_s = gla_mixer(ug, ux, zeros_gla, p["gla_wa2"], p["gla_ba"], p["gla_norm"], nb, t,
                           c=tiles["gla_c"], sub=min(tiles["gla_sub"], t // tiles["gla_c"]), t_valid=t)
    o_c, ssm_s, conv_s = ssd_mixer(ug, ux, zeros_ssm, zeros_conv, p["conv_w"], p["conv_b"], p["dt_bias"], p["a_log"],
                                   p["d_skip"], p["m_norm"], nb, t, q=min(tiles["ssd_q"], t), t_valid=t)
    mix = merge_branches(o_a, o_b, o_c, w["wa"], w["wb"], w["wc"], layer, ug, tm=tiles["mg_tm"], tn=tiles["mg_tn"])
    x_out, h_out = out_proj_norm(mix, w["wo"], layer, x, *next_norm, tm=tiles["out_tm"])
    return x_out, h_out, stacks, (gla_s, ssm_s, conv_s)


ROW_PAD = 8


def _sample_layer(x, h, layer, w, p, next_norm, caches, states, page_table, tiles):
    nb = x.shape[0]
    p_len = page_table.shape[1] * PAGE
    ck_t, cv_t, sk_t, sv_t, wk_t, wv_t = caches
    ug = matmul(h, w["wt_ug"], nt=True, tm=nb, tn=tiles["in_tn"], out_dtype=BF16, name="in_proj_s", b_layer=layer)
    ux = matmul(h, w["wt_ux"], nt=True, tm=nb, tn=NP_UX, name="in_proj_xs", b_layer=layer)
    kv = matmul(h, w["wt_kv"], nt=True, tm=nb, tn=512, name="kv_proj_s", b_layer=layer)
    kvt = matmul(w["wt_kv"], h, nt=True, tm=512, tn=nb, name="kv_proj_st", a_layer=layer)
    qg = matmul(h, w["wt_q"], nt=True, tm=nb, tn=TQ_TM, name="q_proj_s", b_layer=layer)
    kvs = kv.reshape(nb, 6 * NSA_KV, HD)
    def pad_rows(a):
        return jnp.pad(a[:, None, :], ((0, 0), (0, ROW_PAD - 1), (0, 0))).reshape(nb * ROW_PAD, a.shape[1])

    ug_pad, ux_pad = pad_rows(ug), pad_rows(ux)
    o_b, gla_s = gla_mixer(ug_pad, ux_pad, states[0], p["gla_wa2"], p["gla_ba"], p["gla_norm"], nb, ROW_PAD,
                           c=ROW_PAD, sub=1, t_valid=1)
    o_c, ssm_s, conv_s = ssd_mixer(ug_pad, ux_pad, states[1], states[2], p["conv_w"], p["conv_b"], p["dt_bias"], p["a_log"],
                                   p["d_skip"], p["m_norm"], nb, ROW_PAD, q=ROW_PAD, t_valid=1)
    o_b = o_b.reshape(nb, ROW_PAD, GLA_DV)[:, 0]
    o_c = o_c.reshape(nb, ROW_PAD, M_DINNER)[:, 0]
    abk, abv = page_compress(ck_t, cv_t, page_table, layer, w["w1cat"], npg=min(tiles["npg"], page_table.shape[1]))
    q3 = qg[:, :TQ_AG].reshape(nb, NSA_KV, NSA_REP, HD)
    az3 = ug[:, _UG_OFF["a_z"]:_UG_OFF["a_z"] + MIX_W].astype(F32).reshape(nb, NSA_KV, NSA_REP, HD)
    o_cmp, sel = nsa_sample_cmp(abk, abv, kvs, q3, w["w1cat"][layer], w["pe2"][layer], w["w2"][layer], p_len)
    o_a, win_k, win_v = nsa_sample_sel(sel.reshape(nb, NSA_KV, LANE), page_table, sk_t, sv_t, wk_t, wv_t, layer,
                                       kvt, kvs, q3, az3, ux, o_cmp, p_len)
    mix = merge_branches(o_a.reshape(nb, MIX_W).astype(BF16), o_b, o_c, w["wa"], w["wb"], w["wc"], layer, ug,
                         tm=nb, tn=tiles["mg_tn"])
    x_out, h_out = out_proj_norm(mix, w["wo"], layer, x, *next_norm, tm=nb)
    kv4 = kv.reshape(nb, 6, 1, NSA_KV, HD)
    return x_out, h_out, (kv4[:, 0], kv4[:, 1], kv4[:, 2], kv4[:, 3], win_k, win_v, gla_s, ssm_s, conv_s)


_TILES = dict(rms_tm=256, mm_tm=1024, in_tn=1024, kv_tn=1024, q_tn=2048, tq=512, ck=512, gla_c=64, gla_sub=8,
              ssd_q=256, mg_tm=1024, mg_tn=1024, out_tm=512, npg=32)
_PARAM_NAMES = ("ln_w", "gla_wa2", "gla_ba", "gla_norm", "conv_w", "conv_b", "dt_bias", "a_log", "d_skip", "m_norm")


def kernel(x_prompt, x_sample, cache_cmp_k, cache_cmp_v, cache_slc_k, cache_slc_v, cache_win_k, cache_win_v,
           state_gla, state_ssm, state_conv, page_table, ln_w, w_in, cmp_pe_k, cmp_w1_k, cmp_w2_k, cmp_pe_v,
           cmp_w1_v, cmp_w2_v, gla_wa2, gla_ba, gla_norm, conv_w, conv_b, dt_bias, a_log, d_skip, m_norm,
           w_br_a, w_br_b, w_br_c, w_out, final_norm):
    nbp, t, d = x_prompt.shape
    nbs = x_sample.shape[0]
    depth = w_in.shape[0]
    weights = _prep_weights(w_in, cmp_pe_k, cmp_w1_k, cmp_w2_k, cmp_pe_v, cmp_w1_v, cmp_w2_v,
                            w_br_a, w_br_b, w_br_c, w_out)
    params = dict(zip(_PARAM_NAMES, (ln_w, gla_wa2, gla_ba, gla_norm, conv_w, conv_b, dt_bias, a_log, d_skip, m_norm)))
    caches = tuple(jnp.transpose(c, (0, 1, 3, 4, 2))
                   for c in (cache_cmp_k, cache_cmp_v, cache_slc_k, cache_slc_v, cache_win_k, cache_win_v))
    xp = x_prompt.reshape(nbp * t, d)
    xs = x_sample.reshape(nbs, d)
    p_states, s_out = [], []
    stacks = [jnp.zeros((depth, nbp, GRP_ROWS, t), F32) for _ in range(6)]
    hp = rmsnorm_rows(xp, ln_w[0], BF16, _TILES["rms_tm"])
    hs = rmsnorm_rows(xs, ln_w[0], BF16, nbs)
    for l in range(depth):
        p = {k: v[l] for k, v in params.items()}
        next_norm = (ln_w[l + 1], BF16) if l + 1 < depth else (final_norm, F32)
        xp, hp, stacks, st_p = _prompt_layer(xp, hp, nbp, t, l, weights, p, next_norm, _TILES, stacks)
        xs, hs, st_s = _sample_layer(xs, hs, l, weights, p, next_norm, caches,
                                     (state_gla[l], state_ssm[l], state_conv[l]), page_table, _TILES)
        p_states.append(st_p)
        s_out.append(st_s)
    y_prompt = hp.reshape(nbp, t, d)
    y_sample = hs.reshape(nbs, 1, d)

    def from_kv_layout(a):
        return jnp.transpose(a, (0, 1, 4, 2, 3))

    n_w = min(WINDOW, t)
    kv5 = [a.reshape(depth, nbp, NSA_KV, HD, t) for a in stacks]
    p_leaves = [from_kv_layout(a) for a in kv5[:4]] + [from_kv_layout(a[..., t - n_w:]) for a in kv5[4:]]
    p_leaves += [jnp.stack([st[i] for st in p_states]) for i in range(3)]
    s_leaves = [jnp.stack([o[i] for o in s_out]) for i in range(9)]
    s_leaves[4], s_leaves[5] = from_kv_layout(s_leaves[4]), from_kv_layout(s_leaves[5])
    return (y_prompt, y_sample, *p_leaves, *s_leaves)
```

```python
import functools
import math

import jax
import jax.numpy as jnp
import numpy as np
from jax import lax
from jax.experimental import pallas as pl
from jax.experimental.pallas import tpu as pltpu

F32 = jnp.float32
BF16 = jnp.bfloat16

D_MODEL = 2048
PAGE = 128
MIX_W = D_MODEL // 2
HD = 64
NSA_HEADS = MIX_W // HD
NSA_KV = 4
NSA_REP = NSA_HEADS // NSA_KV
CMP_LEN = 32
CMP_STRIDE = 16
SLC_BLK = 64
SLC_TOPK = 16
WINDOW = 512
GLA_HEADS = 4
GLA_DK = MIX_W // 2
GLA_DV = MIX_W
GLA_HK = GLA_DK // GLA_HEADS
GLA_HV = GLA_DV // GLA_HEADS
GLA_RANK = 16
GLA_TAU = 16.0
M_DINNER = MIX_W
M_HD = 64
M_HEADS = M_DINNER // M_HD
M_GROUPS = 4
M_STATE = 128
M_CONV = 4
M_CONVDIM = M_DINNER + 2 * M_GROUPS * M_STATE
EPS = 1e-6
NEG = -1e30
FORCE = 1e4
LOG2E = math.log2(math.e)
WIN_Q = 128
ONES_ROWS = 16

_IN_NAMES = ("a_q", "a_kv", "a_g", "a_z", "b_q", "b_k", "b_v", "b_a", "b_z", "c_xbc", "c_dt", "c_z", "m_g")
_IN_SIZES = (NSA_HEADS * HD, 6 * NSA_KV * HD, 3 * NSA_HEADS, MIX_W, GLA_DK, GLA_DK, GLA_DV, GLA_RANK, GLA_DV,
             M_CONVDIM, M_HEADS, M_DINNER, 3 * D_MODEL)
_IN_OFF = dict(zip(_IN_NAMES, np.cumsum((0,) + _IN_SIZES)[:-1].tolist()))
_IN_SZ = dict(zip(_IN_NAMES, _IN_SIZES))

LANE = 128
SUBLANE = 8
_UG_ORDER = ("m_g", "a_z", "b_v", "b_z", "c_z", "b_q", "b_k")
_UX_ORDER = ("c_xbc", "a_g", "b_a", "c_dt")


def _offsets(order):
    off, o = {}, 0
    for n in order:
        off[n] = o
        o += _IN_SZ[n]
    return off, o


_UG_OFF, NP_UG = _offsets(_UG_ORDER)
_UX_OFF, _ux_used = _offsets(_UX_ORDER)
NP_UX = -(-_ux_used // LANE) * LANE
OFF_SMALL = _UX_OFF["a_g"]
SM_AG, SM_BA, SM_DT = 0, 3 * NSA_HEADS, 3 * NSA_HEADS + GLA_RANK
KV_ROWS = _IN_SZ["a_kv"]
GRP_ROWS = NSA_KV * HD
_TQ_ORDER = ("a_q", "a_g")
TQ_AG = _IN_SZ["a_q"]
TQ_TM = 384
TQ_ROWS = -(-(TQ_AG + _IN_SZ["a_g"]) // TQ_TM) * TQ_TM
VMEM_LIMIT = 56 * 1024 * 1024


def _cparams(sem):
    return pltpu.CompilerParams(dimension_semantics=sem, vmem_limit_bytes=VMEM_LIMIT)


def _dot(a, b, prec=None, nt=False, tn=False):
    if prec is None:
        a, b = a.astype(BF16), b.astype(BF16)
    dn = (((0,) if tn else (1,), (1,) if nt else (0,)), ((), ()))
    return lax.dot_general(a, b, dn, precision=prec, preferred_element_type=F32)


def _split(a):
    hi = a.astype(BF16)
    return hi, (a - hi.astype(F32)).astype(BF16)


def _dot_exact_rhs(a, b_bf16):
    hi, lo = _split(a)
    return _dot(hi, b_bf16) + _dot(lo, b_bf16)


def _dot_exact_lhs(a_bf16, b):
    hi, lo = _split(b)
    return _dot(a_bf16, hi) + _dot(a_bf16, lo)


def _dot3(a, b, **kw):
    ah, al = _split(a)
    bh, bl = _split(b)
    return _dot(ah, bh, **kw) + _dot(ah, bl, **kw) + _dot(al, bh, **kw)


def _sigmoid(x):
    return 0.5 * jnp.tanh(0.5 * x) + 0.5


def _silu(x):
    return x * _sigmoid(x)


def _log_sigmoid(x):
    return jnp.minimum(x, 0.0) - jnp.log(1.0 + jnp.exp(-jnp.abs(x)))


def _softplus(x):
    return jnp.maximum(x, 0.0) + jnp.log(1.0 + jnp.exp(-jnp.abs(x)))


def _stacked_spec(block, imap, layer):
    if layer is None:
        return pl.BlockSpec(block, imap)
    return pl.BlockSpec((None,) + block, lambda *idx: (layer,) + imap(*idx))


def _rms_kernel(x_ref, w_ref, o_ref):
    x = x_ref[...]
    y = x * lax.rsqrt(jnp.mean(x * x, axis=-1, keepdims=True) + EPS)
    o_ref[...] = (y * w_ref[...]).astype(o_ref.dtype)


def rmsnorm_rows(x, w, out_dtype, tm):
    m, d = x.shape
    return pl.pallas_call(
        _rms_kernel,
        grid=(m // tm,),
        in_specs=[pl.BlockSpec((tm, d), lambda i: (i, 0)), pl.BlockSpec((1, d), lambda i: (0, 0))],
        out_specs=pl.BlockSpec((tm, d), lambda i: (i, 0)),
        out_shape=jax.ShapeDtypeStruct((m, d), out_dtype),
        compiler_params=_cparams(("parallel",)),
        name="rmsnorm",
    )(x, w.reshape(1, d))


def _mm_kernel(*refs, nt, has_res):
    a_ref, b_ref, o_ref = refs[0], refs[1], refs[-1]
    acc = _dot(a_ref[...], b_ref[...], nt=nt)
    if has_res:
        acc = acc + refs[2][...]
    o_ref[...] = acc.astype(o_ref.dtype)


def matmul(a, b, *, nt, tm, tn, res=None, out_dtype=F32, name="matmul", a_layer=None, b_layer=None):
    m, k = a.shape[-2:]
    n = b.shape[-2] if nt else b.shape[-1]
    tm, tn = min(tm, m), min(tn, n)
    if nt:
        b_spec = _stacked_spec((tn, k), lambda i, j: (j, 0), b_layer)
    else:
        b_spec = _stacked_spec((k, tn), lambda i, j: (0, j), b_layer)
    in_specs = [_stacked_spec((tm, k), lambda i, j: (i, 0), a_layer), b_spec]
    args = [a, b]
    if res is not None:
        in_specs.append(pl.BlockSpec((tm, tn), lambda i, j: (i, j)))
        args.append(res)
    return pl.pallas_call(
        functools.partial(_mm_kernel, nt=nt, has_res=res is not None),
        grid=(m // tm, n // tn),
        in_specs=in_specs,
        out_specs=pl.BlockSpec((tm, tn), lambda i, j: (i, j)),
        out_shape=jax.ShapeDtypeStruct((m, n), out_dtype),
        compiler_params=_cparams(("parallel", "parallel")),
        name=name,
    )(*args)


def matmul_t(wt, layer, h, nb, t, *, tm, tn, name):
    _, r, k = wt.shape
    tn = min(tn, t)
    nj = t // tn
    return pl.pallas_call(
        functools.partial(_mm_kernel, nt=True, has_res=False),
        grid=(nb, nj, r // tm),
        in_specs=[pl.BlockSpec((None, tm, k), lambda b, j, i: (layer, i, 0)),
                  pl.BlockSpec((tn, k), lambda b, j, i: (b * nj + j, 0))],
        out_specs=pl.BlockSpec((None, tm, tn), lambda b, j, i: (b, i, j)),
        out_shape=jax.ShapeDtypeStruct((nb, r, t), F32),
        compiler_params=_cparams(("parallel", "parallel", "parallel")),
        name=name,
    )(wt, h)


def _kv_stack_kernel(w_ref, h_ref, *refs):
    outs = refs[-6:]
    hb = h_ref[...]
    for i in range(6):
        outs[i][...] = _dot(w_ref[i * GRP_ROWS:(i + 1) * GRP_ROWS, :], hb, nt=True)


def kv_proj_stacked(wt_kv, layer, h, nb, t, stacks, *, tn):
    depth, _, k = wt_kv.shape
    tn = min(tn, t)
    nj = t // tn
    in_specs = [pl.BlockSpec((None, KV_ROWS, k), lambda b, j: (layer, 0, 0)),
                pl.BlockSpec((tn, k), lambda b, j: (b * nj + j, 0))] + [pl.BlockSpec(memory_space=pl.ANY)] * 6
    args = [wt_kv, h] + list(stacks)
    aliases = {2 + i: i for i in range(6)}
    o_spec = pl.BlockSpec((None, None, GRP_ROWS, tn), lambda b, j: (layer, b, 0, j))
    o_shape = jax.ShapeDtypeStruct((depth, nb, GRP_ROWS, t), F32)
    return pl.pallas_call(
        _kv_stack_kernel,
        grid=(nb, nj),
        in_specs=in_specs,
        out_specs=[o_spec] * 6,
        out_shape=[o_shape] * 6,
        input_output_aliases=aliases,
        compiler_params=_cparams(("parallel", "parallel")),
        name="kv_proj_t",
    )(*args)


def _out_norm_kernel(a_ref, w_ref, res_ref, nw_ref, x_ref, h_ref):
    x = res_ref[...] + _dot(a_ref[...], w_ref[...])
    x_ref[...] = x
    y = x * lax.rsqrt(jnp.mean(x * x, axis=-1, keepdims=True) + EPS)
    h_ref[...] = (y * nw_ref[...]).astype(h_ref.dtype)


def out_proj_norm(mix, wo, layer, res, norm_w, h_dtype, *, tm):
    m, k = mix.shape
    n = wo.shape[-1]
    tm = min(tm, m)
    row = lambda i: (i, 0)
    return pl.pallas_call(
        _out_norm_kernel,
        grid=(m // tm,),
        in_specs=[pl.BlockSpec((tm, k), row), pl.BlockSpec((None, k, n), lambda i: (layer, 0, 0)),
                  pl.BlockSpec((tm, n), row), pl.BlockSpec((1, n), lambda i: (0, 0))],
        out_specs=[pl.BlockSpec((tm, n), row), pl.BlockSpec((tm, n), row)],
        out_shape=[jax.ShapeDtypeStruct((m, n), F32), jax.ShapeDtypeStruct((m, n), h_dtype)],
        compiler_params=_cparams(("parallel",)),
        name="out_proj",
    )(mix, wo, res, norm_w.reshape(1, n))


def _merge_kernel(oa_ref, ob_ref, oc_ref, wa_ref, wb_ref, wc_ref, g0_ref, g1_ref, g2_ref, o_ref):
    acc = _sigmoid(g0_ref[...].astype(F32)) * _dot(oa_ref[...], wa_ref[...])
    acc = acc + _sigmoid(g1_ref[...].astype(F32)) * _dot(ob_ref[...], wb_ref[...])
    acc = acc + _sigmoid(g2_ref[...].astype(F32)) * _dot(oc_ref[...], wc_ref[...])
    o_ref[...] = acc.astype(o_ref.dtype)


def merge_branches(o_a, o_b, o_c, wa, wb, wc, layer, ug, *, tm, tn):
    m, k = o_a.shape
    n = wa.shape[-1]
    tm = min(tm, m)
    gb = _UG_OFF["m_g"] // tn
    nj = n // tn
    o_spec = pl.BlockSpec((tm, k), lambda i, j: (i, 0))
    w_spec = pl.BlockSpec((None, k, tn), lambda i, j: (layer, 0, j))
    g_specs = [pl.BlockSpec((tm, tn), functools.partial(lambda i, j, br: (i, gb + br * nj + j), br=br))
               for br in range(3)]
    return pl.pallas_call(
        _merge_kernel,
        grid=(m // tm, nj),
        in_specs=[o_spec, o_spec, o_spec, w_spec, w_spec, w_spec] + g_specs,
        out_specs=pl.BlockSpec((tm, tn), lambda i, j: (i, j)),
        out_shape=jax.ShapeDtypeStruct((m, n), BF16),
        compiler_params=_cparams(("parallel", "parallel")),
        name="merge",
    )(o_a, o_b, o_c, wa, wb, wc, ug, ug, ug)


def _cmp_kernel(seg_ref, w1_ref, pe_ref, w2_ref, o_ref, *, nseg):
    w1 = w1_ref[...]
    ab = _dot3(seg_ref[...], w1)
    pe = _dot3(pe_ref[...], w1)
    cst = pe[0:1, :HD] + pe[1:2, HD:]
    nxt = pltpu.roll(ab[:, HD:], shift=nseg - 1, axis=0)
    pre = ab[:, :HD] + nxt + cst
    o_ref[...] = _dot3(_silu(pre), w2_ref[...])


def compress_blocks(seg, w1cat, pe2, w2, nseg):
    _, r, kdim = seg.shape
    return pl.pallas_call(
        functools.partial(_cmp_kernel, nseg=nseg),
        grid=(2, r // nseg),
        in_specs=[pl.BlockSpec((None, nseg, kdim), lambda s, i: (s, i, 0)),
                  pl.BlockSpec((None, kdim, 2 * HD), lambda s, i: (s, 0, 0)),
                  pl.BlockSpec((None, 8, kdim), lambda s, i: (s, 0, 0)),
                  pl.BlockSpec((None, HD, HD), lambda s, i: (s, 0, 0))],
        out_specs=pl.BlockSpec((None, nseg, HD), lambda s, i: (s, i, 0)),
        out_shape=jax.ShapeDtypeStruct((2, r, HD), F32),
        compiler_params=_cparams(("parallel", "parallel")),
        name="compress",
    )(seg, w1cat, pe2, w2)


def _nsa_prompt_kernel(qt_ref, ag_ref, az_ref, kst_ref, vst_ref, kwt_ref, vwt_ref, kc_ref, vc_ref, o_ref,
                       ks_scr, vs_scr, kw_scr, vw_scr, bias_scr, *, t, tq, ck, nseg):
    g = pl.program_id(1)
    i = pl.program_id(2)
    s0 = i * tq
    ns = t // SLC_BLK
    nsr = bias_scr.shape[0]
    cols = NSA_REP * tq

    @pl.when(i == 0)
    def _():
        ks_scr[...] = jnp.transpose(kst_ref[...]).astype(BF16)
        kw_scr[...] = jnp.transpose(kwt_ref[...]).astype(BF16)
        ones = jnp.ones((ONES_ROWS, t), BF16)
        vs_scr[...] = jnp.concatenate([vst_ref[...].astype(BF16), ones], axis=0)
        vw_scr[...] = jnp.concatenate([vwt_ref[...].astype(BF16), ones], axis=0)

    qt = qt_ref[...]
    q2 = jnp.concatenate([qt[r * HD:(r + 1) * HD, :] for r in range(NSA_REP)], axis=1) * (HD ** -0.5 * LOG2E)
    q2b = q2.astype(BF16)
    tpos1 = s0 + lax.broadcasted_iota(jnp.int32, (1, tq), 1)
    tpos = jnp.concatenate([tpos1] * NSA_REP, axis=1)

    crow = lax.broadcasted_iota(jnp.int32, (nseg, 1), 0)
    m_c = (crow * CMP_STRIDE + (CMP_LEN - 1) <= tpos) & (crow < nseg - 1)
    s_c = jnp.where(m_c, _dot3(kc_ref[...], q2), NEG)
    p_c = jnp.where(m_c, jnp.exp2(s_c - jnp.max(s_c, axis=0, keepdims=True)), 0.0)
    l_c = jnp.sum(p_c, axis=0, keepdims=True)
    p_c = p_c / jnp.where(l_c > 0.0, l_c, 1.0)
    o_c = _dot(jnp.transpose(vc_ref[...]), p_c)
    psum = p_c[:, 0:tq]
    for r in range(1, NSA_REP):
        psum = psum + p_c[:, r * tq:(r + 1) * tq]

    srow = lax.broadcasted_iota(jnp.int32, (nsr, 1), 0)
    ccol = lax.broadcasted_iota(jnp.int32, (1, nseg), 1) * CMP_STRIDE
    cover = jnp.where((ccol + CMP_LEN > srow * SLC_BLK) & (ccol < (srow + 1) * SLC_BLK), 1.0, 0.0).astype(BF16)
    imp = _dot_exact_lhs(cover, psum)
    cur = tpos1 // SLC_BLK
    valid = (srow * SLC_BLK <= tpos1) & (srow < ns)
    forced = (srow == 0) | (srow == cur) | (srow == cur - 1)
    score = jnp.where(valid, jnp.where(forced, FORCE, imp), NEG)
    rank = jnp.zeros((nsr, tq), F32)
    for j in range(ns):
        sj = score[j:j + 1, :]
        rank = rank + jnp.where((sj > score) | ((sj == score) & (j < srow)), 1.0, 0.0)
    sel = (rank < float(min(SLC_TOPK, ns))) & valid
    bias_scr[...] = jnp.where(sel, 0.0, NEG)

    def attend(k_rows, v_cols, bias1, m, acc, q_cols=q2b):
        w = bias1.shape[1]
        s = _dot(k_rows, q_cols)
        s = jnp.concatenate([s[:, r * w:(r + 1) * w] + bias1 for r in range(NSA_REP)], axis=1)
        m_new = jnp.maximum(m, jnp.max(s, axis=0, keepdims=True))
        p = jnp.exp2(s - m_new)
        return m_new, jnp.exp2(m - m_new) * acc + _dot(v_cols, p)

    def finish(acc):
        l = acc[HD:HD + 1, :]
        return acc[:HD, :] / jnp.where(l > 0.0, l, 1.0)

    blk_per_chunk = ck // SLC_BLK
    m0 = jnp.full((1, cols), NEG, F32)
    acc0 = jnp.zeros((HD + ONES_ROWS, cols), F32)

    def sel_chunk(j, carry):
        k0 = pl.multiple_of(j * ck, ck)
        npos = k0 + lax.broadcasted_iota(jnp.int32, (ck, 1), 0)
        rows = [jnp.broadcast_to(bias_scr[pl.ds(j * blk_per_chunk + k, 1), :], (SLC_BLK, tq))
                for k in range(blk_per_chunk)]
        bias1 = jnp.where(npos <= tpos1, jnp.concatenate(rows, axis=0), NEG)
        return attend(ks_scr[pl.ds(k0, ck), :], vs_scr[:, pl.ds(k0, ck)], bias1, *carry)

    n_chunks = (s0 + tq + ck - 1) // ck
    o_s = finish(lax.fori_loop(0, n_chunks, sel_chunk, (m0, acc0))[1])

    wq = min(WIN_Q, tq)
    ww = min(WINDOW + wq, t)
    parts = []
    for a in range(tq // wq):
        w0 = pl.multiple_of(jnp.clip(s0 + a * wq - WINDOW, 0, t - ww), LANE)
        wpos = w0 + lax.broadcasted_iota(jnp.int32, (ww, 1), 0)
        tsub = tpos1[:, a * wq:(a + 1) * wq]
        bias_w = jnp.where((wpos <= tsub) & (wpos > tsub - WINDOW), 0.0, NEG)
        q_sub = jnp.concatenate([q2b[:, r * tq + a * wq:r * tq + (a + 1) * wq] for r in range(NSA_REP)], axis=1)
        parts.append(finish(attend(kw_scr[pl.ds(w0, ww), :], vw_scr[:, pl.ds(w0, ww)], bias_w,
                                   m0[:, :NSA_REP * wq], acc0[:, :NSA_REP * wq], q_sub)[1]))
    o_w = jnp.concatenate([parts[a][:, r * wq:(r + 1) * wq] for r in range(NSA_REP) for a in range(tq // wq)], axis=1)

    outs = []
    for r in range(NSA_REP):
        sl = slice(r * tq, (r + 1) * tq)
        gt = [_sigmoid(ag_ref[pl.ds(3 * (NSA_REP * g + r) + j, 1), :]) for j in range(3)]
        outs.append(jnp.transpose(gt[0] * o_c[:, sl] + gt[1] * o_s[:, sl] + gt[2] * o_w[:, sl]))
    o = jnp.concatenate(outs, axis=1) * _silu(az_ref[...].astype(F32))
    o_ref[...] = o.astype(o_ref.dtype)


def nsa_prompt(ug, qgt, stacks, layer, kc, nb, t, *, tq, ck):
    tq = min(tq, t)
    nq = t // tq
    nseg = t // CMP_STRIDE
    ck = min(ck, t)
    gw = NSA_REP * HD
    nsr = -(-(t // SLC_BLK) // SUBLANE) * SUBLANE
    kv_spec = pl.BlockSpec((None, None, HD, t), lambda b, g, i: (layer, b, g, 0))
    return pl.pallas_call(
        functools.partial(_nsa_prompt_kernel, t=t, tq=tq, ck=ck, nseg=nseg),
        grid=(nb, NSA_KV, nq),
        in_specs=[pl.BlockSpec((None, gw, tq), lambda b, g, i: (b, g, i)),
                  pl.BlockSpec((None, LANE, tq), lambda b, g, i: (b, TQ_AG // LANE, i)),
                  pl.BlockSpec((tq, gw), lambda b, g, i: (b * nq + i, _UG_OFF["a_z"] // gw + g)),
                  kv_spec, kv_spec, kv_spec, kv_spec,
                  pl.BlockSpec((None, nseg, HD), lambda b, g, i: (0, b * NSA_KV + g, 0)),
                  pl.BlockSpec((None, nseg, HD), lambda b, g, i: (1, b * NSA_KV + g, 0))],
        out_specs=pl.BlockSpec((tq, gw), lambda b, g, i: (b * nq + i, g)),
        out_shape=jax.ShapeDtypeStruct((nb * t, NSA_HEADS * HD), BF16),
        scratch_shapes=[pltpu.VMEM((t, HD), BF16), pltpu.VMEM((HD + ONES_ROWS, t), BF16),
                        pltpu.VMEM((t, HD), BF16), pltpu.VMEM((HD + ONES_ROWS, t), BF16),
                        pltpu.VMEM((nsr, tq), F32)],
        compiler_params=_cparams(("parallel", "parallel", "arbitrary")),
        name="nsa_prompt",
    )(qgt, qgt, ug, stacks[2], stacks[3], stacks[4], stacks[5], kc, kc)


def _gla_kernel(q_ref, k_ref, v_ref, z_ref, sm_ref, s0_ref, wa2_ref, ba_ref, gn_ref, o_ref, sout_ref, s_scr,
                *, c, sub, t_valid):
    ci = pl.program_id(1)
    rows = c * sub

    @pl.when(ci == 0)
    def _():
        s_scr[...] = s0_ref[...]

    row = ci * rows + lax.broadcasted_iota(jnp.int32, (rows, 1), 0)
    gate_in = sm_ref[...][:, SM_BA:SM_BA + GLA_RANK]
    log_a = _log_sigmoid(_dot3(gate_in, wa2_ref[...]) + ba_ref[...]) / GLA_TAU
    log_a = jnp.where(row < t_valid, log_a, 0.0)
    causal = lax.broadcasted_iota(jnp.int32, (c, c), 0) >= lax.broadcasted_iota(jnp.int32, (c, c), 1)
    tri = jnp.where(causal, 1.0, 0.0).astype(BF16)
    cb_all = jnp.concatenate([_dot_exact_lhs(tri, log_a[j * c:(j + 1) * c, :]) for j in range(sub)], axis=0)
    for h in range(GLA_HEADS):
        ks = slice(h * GLA_HK, (h + 1) * GLA_HK)
        vs = slice(h * GLA_HV, (h + 1) * GLA_HV)
        state = s_scr[h]
        outs = []
        for j in range(sub):
            rs = slice(j * c, (j + 1) * c)
            cb = cb_all[rs, ks]
            ecb = jnp.exp(cb)
            qd = q_ref[rs, ks].astype(F32) * (GLA_HK ** -0.5) * ecb
            kh = k_ref[rs, ks].astype(F32)
            vh = v_ref[rs, vs]
            att = jnp.where(causal, _dot(qd, kh * jnp.exp(-cb), nt=True), 0.0)
            outs.append(_dot(att, vh) + _dot(qd, state))
            c_end = cb[c - 1:c, :]
            k_dec = kh * jnp.exp(c_end - cb)
            e_col = jnp.transpose(ecb[c - SUBLANE:c, :])[:, SUBLANE - 1:SUBLANE]
            state = e_col * state + _dot(k_dec, vh, tn=True)
        o = jnp.concatenate(outs, axis=0)
        y = o * lax.rsqrt(jnp.mean(o * o, axis=-1, keepdims=True) + EPS) * gn_ref[...]
        o_ref[:, vs] = (y * _silu(z_ref[:, vs].astype(F32))).astype(o_ref.dtype)
        s_scr[h] = state

    @pl.when(ci == pl.num_programs(1) - 1)
    def _():
        sout_ref[...] = s_scr[...]


def gla_mixer(ug, ux, s0, wa2, ba, gnorm, nb, t, *, c, sub, t_valid):
    rows = c * sub
    nc = t // rows

    def u_spec(name, width):
        return pl.BlockSpec((rows, width), lambda b, i: (b * nc + i, _UG_OFF[name] // width))

    s_spec = pl.BlockSpec((None, GLA_HEADS, GLA_HK, GLA_HV), lambda b, i: (b, 0, 0, 0))
    return pl.pallas_call(
        functools.partial(_gla_kernel, c=c, sub=sub, t_valid=t_valid),
        grid=(nb, nc),
        in_specs=[u_spec("b_q", GLA_DK), u_spec("b_k", GLA_DK), u_spec("b_v", GLA_DV), u_spec("b_z", GLA_DV),
                  pl.BlockSpec((rows, LANE), lambda b, i: (b * nc + i, OFF_SMALL // LANE)),
                  s_spec,
                  pl.BlockSpec((GLA_RANK, GLA_DK), lambda b, i: (0, 0)),
                  pl.BlockSpec((1, GLA_DK), lambda b, i: (0, 0)),
                  pl.BlockSpec((1, GLA_HV), lambda b, i: (0, 0))],
        out_specs=[pl.BlockSpec((rows, GLA_DV), lambda b, i: (b * nc + i, 0)), s_spec],
        out_shape=[jax.ShapeDtypeStruct((nb * t, GLA_DV), BF16),
                   jax.ShapeDtypeStruct((nb, GLA_HEADS, GLA_HK, GLA_HV), F32)],
        scratch_shapes=[pltpu.VMEM((GLA_HEADS, GLA_HK, GLA_HV), F32)],
        compiler_params=_cparams(("parallel", "arbitrary")),
        name="gla",
    )(ug, ug, ug, ug, ux, s0, wa2, ba.reshape(1, GLA_DK), gnorm.reshape(1, GLA_HV))


def _ssd_kernel(xbc_ref, z_ref, sm_ref, s0_ref, cbuf_ref, cw_ref, cbias_ref, dtb_ref, alog_ref, dsk_ref, mn_ref,
                o_ref, sout_ref, cout_ref, s_scr, f_scr, *, q, t_valid, n_chunks):
    ci = pl.program_id(1)
    npad = SUBLANE
    hist = M_CONV - 1

    @pl.when(ci == 0)
    def _():
        s_scr[...] = s0_ref[...].reshape(s_scr.shape)
        f_scr[npad - hist:npad, :] = cbuf_ref[...]

    f_scr[npad:npad + q, :] = xbc_ref[...]
    conv = cbias_ref[...] + xbc_ref[...] * cw_ref[hist:hist + 1, :]
    full = f_scr[...]
    for i in range(hist):
        conv = conv + pltpu.roll(full, shift=hist - i, axis=0)[npad:npad + q, :] * cw_ref[i:i + 1, :]
    xbc = _silu(conv)

    last_valid = t_valid - (n_chunks - 1) * q

    @pl.when(ci == n_chunks - 1)
    def _():
        cout_ref[...] = f_scr[npad - hist + last_valid:npad + last_valid, :]

    f_scr[npad - hist:npad, :] = f_scr[npad - hist + q:npad + q, :]

    lane = lax.broadcasted_iota(jnp.int32, (1, LANE), 1)
    row = ci * q + lax.broadcasted_iota(jnp.int32, (q, 1), 0)
    head_lane = (lane >= SM_DT) & (lane < SM_DT + M_HEADS)
    dt = jnp.where(head_lane & (row < t_valid), _softplus(sm_ref[...] + dtb_ref[...]), 0.0)
    a = dt * (-jnp.exp(alog_ref[...]))
    tri = jnp.where(lax.broadcasted_iota(jnp.int32, (q, q), 0) >= lax.broadcasted_iota(jnp.int32, (q, q), 1), 1.0, 0.0)
    causal = tri > 0.5
    cum = _dot_exact_lhs(tri.astype(BF16), a)
    cum_t = jnp.transpose(cum)
    dt_t = jnp.transpose(dt)
    nbc = M_GROUPS * M_STATE
    rep = M_HEADS // M_GROUPS
    gw = rep * M_HD
    head_of_lane = lax.broadcasted_iota(jnp.int32, (1, gw), 1) // M_HD
    head_of_row = lax.broadcasted_iota(jnp.int32, (gw, 1), 0) // M_HD
    for g in range(M_GROUPS):
        sl = slice(g * gw, (g + 1) * gw)
        bg = xbc[:, M_DINNER + g * M_STATE:M_DINNER + (g + 1) * M_STATE]
        cg = xbc[:, M_DINNER + nbc + g * M_STATE:M_DINNER + nbc + (g + 1) * M_STATE]
        xg = xbc[:, sl]
        cbm = _dot(cg, bg, nt=True)
        s_prev = s_scr[g]
        y = jnp.zeros((q, gw), F32)
        e_lanes = jnp.zeros((q, gw), F32)
        w_lanes = jnp.zeros((q, gw), F32)
        skip = jnp.zeros((1, gw), F32)
        keep = jnp.zeros((gw, 1), F32)
        for r in range(rep):
            hl = SM_DT + g * rep + r
            mine = head_of_lane == r
            cum_c = cum[:, hl:hl + 1]
            c_end = cum[q - 1:q, hl:hl + 1]
            decay = jnp.exp(jnp.where(causal, cum_c - cum_t[hl:hl + 1, :], -jnp.inf))
            y = y + _dot(cbm * decay * dt_t[hl:hl + 1, :], jnp.where(mine, xg, 0.0))
            e_lanes = jnp.where(mine, jnp.exp(cum_c), e_lanes)
            w_lanes = jnp.where(mine, jnp.exp(c_end - cum_c) * dt[:, hl:hl + 1], w_lanes)
            skip = jnp.where(mine, dsk_ref[:, hl:hl + 1], skip)
            keep = jnp.where(head_of_row == r, jnp.exp(c_end), keep)
        y = y + e_lanes * _dot(cg, s_prev, nt=True) + skip * xg
        s_scr[g] = keep * s_prev + _dot(xg * w_lanes, bg, tn=True)
        yg = y * _silu(z_ref[:, sl].astype(F32))
        yn = yg * lax.rsqrt(jnp.mean(yg * yg, axis=-1, keepdims=True) + EPS) * mn_ref[:, sl]
        o_ref[:, sl] = yn.astype(o_ref.dtype)

    @pl.when(ci == n_chunks - 1)
    def _():
        sout_ref[...] = s_scr[...].reshape(M_HEADS, M_HD, M_STATE)


def _pad_heads(v):
    return jnp.zeros((1, LANE), F32).at[0, SM_DT:SM_DT + M_HEADS].set(v)


def ssd_mixer(ug, ux, s0, cbuf, conv_w, conv_b, dt_bias, a_log, d_skip, m_norm, nb, t, *, q, t_valid):
    nc = t // q
    n_chunks = -(-t_valid // q)
    assert n_chunks == nc
    hist = M_CONV - 1

    def u_spec(off, name, width):
        return pl.BlockSpec((q, width), lambda b, i: (b * nc + i, off[name] // width))

    def full(shape):
        return pl.BlockSpec(shape, lambda b, i: (0,) * len(shape))

    s_spec = pl.BlockSpec((None, M_HEADS, M_HD, M_STATE), lambda b, i: (b, 0, 0, 0))
    c_spec = pl.BlockSpec((None, hist, M_CONVDIM), lambda b, i: (b, 0, 0))
    return pl.pallas_call(
        functools.partial(_ssd_kernel, q=q, t_valid=t_valid, n_chunks=n_chunks),
        grid=(nb, nc),
        in_specs=[u_spec(_UX_OFF, "c_xbc", M_CONVDIM), u_spec(_UG_OFF, "c_z", M_DINNER),
                  pl.BlockSpec((q, LANE), lambda b, i: (b * nc + i, OFF_SMALL // LANE)),
                  s_spec, c_spec,
                  full((M_CONV, M_CONVDIM)), full((1, M_CONVDIM)), full((1, LANE)), full((1, LANE)),
                  full((1, LANE)), full((1, M_DINNER))],
        out_specs=[pl.BlockSpec((q, M_DINNER), lambda b, i: (b * nc + i, 0)), s_spec, c_spec],
        out_shape=[jax.ShapeDtypeStruct((nb * t, M_DINNER), BF16),
                   jax.ShapeDtypeStruct((nb, M_HEADS, M_HD, M_STATE), F32),
                   jax.ShapeDtypeStruct((nb, hist, M_CONVDIM), F32)],
        scratch_shapes=[pltpu.VMEM((M_GROUPS, M_HEADS // M_GROUPS * M_HD, M_STATE), F32),
                        pltpu.VMEM((q + SUBLANE, M_CONVDIM), F32)],
        compiler_params=_cparams(("parallel", "arbitrary")),
        name="ssd",
    )(ux, ug, ux, s0, cbuf, conv_w, conv_b.reshape(1, M_CONVDIM), _pad_heads(dt_bias), _pad_heads(a_log),
      _pad_heads(d_skip), m_norm.reshape(1, M_DINNER))


SEG_PER_PAGE = PAGE // CMP_STRIDE


def _page_cmp_kernel(pt_ref, *refs, npg):
    w1_ref, ok_ref, ov_ref, scr, wbd_scr = refs[2 * npg:]

    @pl.when((pl.program_id(0) == 0) & (pl.program_id(1) == 0))
    def _():
        wbd_scr[...] = jnp.zeros(wbd_scr.shape, wbd_scr.dtype)
        for s in range(2):
            for j in range(CMP_STRIDE):
                wj = w1_ref[s, j * HD:(j + 1) * HD, :].astype(BF16)
                for g in range(NSA_KV):
                    wbd_scr[s, j, g * HD:(g + 1) * HD, g * 2 * HD:(g + 1) * 2 * HD] = wj

    row = lax.broadcasted_iota(jnp.int32, (PAGE, PAGE), 0)
    pos = lax.broadcasted_iota(jnp.int32, (PAGE, PAGE), 1)
    regroup = jnp.where(pos == CMP_STRIDE * (row % SEG_PER_PAGE) + row // SEG_PER_PAGE, 1.0, 0.0).astype(BF16)
    for s, (pages, o_ref) in enumerate(((refs[:npg], ok_ref), (refs[npg:2 * npg], ov_ref))):
        for k in range(npg):
            by_j = _dot(regroup, pages[k][...].reshape(NSA_KV * HD, PAGE), nt=True)
            for j in range(CMP_STRIDE):
                scr[j, k * SEG_PER_PAGE:(k + 1) * SEG_PER_PAGE, :] = by_j[j * SEG_PER_PAGE:(j + 1) * SEG_PER_PAGE, :]
        acc = jnp.zeros((npg * SEG_PER_PAGE, 2 * HD * NSA_KV), F32)
        for j in range(CMP_STRIDE):
            acc = acc + _dot(scr[j], wbd_scr[s, j])
        o_ref[...] = acc


def page_compress(cache_k_t, cache_v_t, page_table, layer, w1cat, *, npg):
    nb, n_pages = page_table.shape
    steps = n_pages // npg

    def page_spec(k):
        return pl.BlockSpec((None, None, NSA_KV, HD, PAGE),
                            lambda b, i, pt: (layer, pt[b, i * npg + k], 0, 0, 0))

    w_spec = pl.BlockSpec((None, 2, CMP_STRIDE * HD, 2 * HD), lambda b, i, pt: (layer, 0, 0, 0))
    o_spec = pl.BlockSpec((None, npg * SEG_PER_PAGE, NSA_KV * 2 * HD), lambda b, i, pt: (b, i, 0))
    o_shape = jax.ShapeDtypeStruct((nb, n_pages * SEG_PER_PAGE, NSA_KV * 2 * HD), F32)
    return pl.pallas_call(
        functools.partial(_page_cmp_kernel, npg=npg),
        grid_spec=pltpu.PrefetchScalarGridSpec(
            num_scalar_prefetch=1,
            grid=(nb, steps),
            in_specs=[page_spec(k) for k in range(npg)] * 2 + [w_spec],
            out_specs=[o_spec, o_spec],
            scratch_shapes=[pltpu.VMEM((CMP_STRIDE, npg * SEG_PER_PAGE, NSA_KV * HD), F32),
                            pltpu.VMEM((2, CMP_STRIDE, NSA_KV * HD, NSA_KV * 2 * HD), BF16)]),
        out_shape=[o_shape, o_shape],
        compiler_params=_cparams(("arbitrary", "arbitrary")),
        name="page_compress",
    )(page_table, *([cache_k_t] * npg), *([cache_v_t] * npg), w1cat)


def _masked_softmax_rows(s, mask):
    s = jnp.where(mask, s, NEG)
    m = jnp.max(s, axis=-1, keepdims=True)
    p = jnp.where(mask, jnp.exp(s - m), 0.0)
    l = jnp.sum(p, axis=-1, keepdims=True)
    return p / jnp.where(l > 0.0, l, 1.0)


def _nsa_sample_cmp_kernel(abk_ref, abv_ref, kvs_ref, q_ref, w1_ref, pe_ref, w2_ref, cover_ref, oc_ref, sel_ref,
                           cst_scr, *, p_len, nc, sl):
    ns = p_len // SLC_BLK + 1
    rowi = lax.broadcasted_iota(jnp.int32, (nc, 1), 0)

    @pl.when(pl.program_id(0) == 0)
    def _():
        for idx in range(2):
            pe = _dot3(pe_ref[idx], w1_ref[idx])
            cst_scr[idx:idx + 1, :] = pe[0:1, :HD] + pe[1:2, HD:]

    cidx = lax.broadcasted_iota(jnp.int32, (1, nc), 1)
    m_c = cidx * CMP_STRIDE + (CMP_LEN - 1) <= p_len
    sidx = lax.broadcasted_iota(jnp.int32, (1, sl), 1)
    cur = p_len // SLC_BLK
    valid = (sidx * SLC_BLK <= p_len) & (sidx < ns)
    forced = (sidx == 0) | (sidx == cur) | (sidx == cur - 1)
    ri = lax.broadcasted_iota(jnp.int32, (sl, sl), 0)
    li = lax.broadcasted_iota(jnp.int32, (sl, sl), 1)
    kl = lax.broadcasted_iota(jnp.int32, (sl, LANE), 1).astype(F32)
    sv = lax.broadcasted_iota(jnp.int32, (sl, LANE), 0).astype(F32)

    for g in range(NSA_KV):
        def compressed(ab_ref, idx):
            ab = ab_ref[:, g * 2 * HD:(g + 1) * 2 * HD]
            new = kvs_ref[idx * NSA_KV + g:idx * NSA_KV + g + 1, :]
            b_new = _dot(new, w1_ref[idx][0:HD, HD:])
            nxt = jnp.where(rowi == nc - 1, b_new, pltpu.roll(ab[:, HD:], shift=nc - 1, axis=0))
            return _dot(_silu(ab[:, :HD] + nxt + cst_scr[idx:idx + 1, :]), w2_ref[idx])

        kc = compressed(abk_ref, 0)
        vc = compressed(abv_ref, 1)
        p_c = _masked_softmax_rows(_dot3(q_ref[g] * (HD ** -0.5), kc, nt=True), m_c)
        oc_ref[g] = _dot(p_c, vc)
        imp = _dot_exact_rhs(jnp.sum(p_c, axis=0, keepdims=True), cover_ref[...])
        score = jnp.where(valid, jnp.where(forced, FORCE, imp), NEG)
        score_b = jnp.broadcast_to(score, (sl, sl))
        score_col = jnp.sum(jnp.where(ri == li, score_b, 0.0), axis=1, keepdims=True)
        ahead = (score_b > score_col) | ((score_b == score_col) & (li < ri))
        rank_col = jnp.sum(jnp.where(ahead, 1.0, 0.0), axis=1, keepdims=True)
        picked = jnp.sum(jnp.where(rank_col == kl, sv, 0.0), axis=0, keepdims=True)
        sel_ref[g] = picked.astype(jnp.int32)


def _cover_matrix(nc, sl):
    c0 = np.arange(nc)[:, None] * CMP_STRIDE
    s0 = np.arange(sl)[None, :] * SLC_BLK
    return jnp.asarray(((c0 + CMP_LEN > s0) & (c0 < s0 + SLC_BLK)).astype(np.float32), dtype=BF16)


def nsa_sample_cmp(abk, abv, kvs, q3, w1cat, pe2, w2, p_len):
    nb, nc, _ = abk.shape
    ns = p_len // SLC_BLK + 1
    sl = -(-ns // LANE) * LANE
    cover = _cover_matrix(nc, sl)

    ab_spec = pl.BlockSpec((None, nc, NSA_KV * 2 * HD), lambda b: (b, 0, 0))

    def full(a):
        return pl.BlockSpec(a.shape, lambda b: (0,) * a.ndim)

    row_spec = pl.BlockSpec((None, NSA_KV, NSA_REP, HD), lambda b: (b, 0, 0, 0))
    return pl.pallas_call(
        functools.partial(_nsa_sample_cmp_kernel, p_len=p_len, nc=nc, sl=sl),
        grid=(nb,),
        in_specs=[ab_spec, ab_spec, pl.BlockSpec((None, 6 * NSA_KV, HD), lambda b: (b, 0, 0)), row_spec,
                  full(w1cat), full(pe2), full(w2), full(cover)],
        out_specs=[row_spec, pl.BlockSpec((None, NSA_KV, 1, LANE), lambda b: (b, 0, 0, 0))],
        out_shape=[jax.ShapeDtypeStruct((nb, NSA_KV, NSA_REP, HD), F32),
                   jax.ShapeDtypeStruct((nb, NSA_KV, 1, LANE), jnp.int32)],
        scratch_shapes=[pltpu.VMEM((SUBLANE, HD), F32)],
        compiler_params=_cparams(("arbitrary",)),
        name="nsa_sample_cmp",
    )(abk, abv, kvs, q3, w1cat, pe2, w2, cover)


def _nsa_sample_sel_kernel(sel_ref, pt_ref, *refs, p_len, nb, n_buf):
    nk = SLC_TOPK
    kpages, vpages = refs[:nk], refs[nk:2 * nk]
    (kwb_ref, vwb_ref, kvt_ref, kvs_ref, q_ref, az_ref, sm_ref, oc_ref,
     o_ref, kwo_ref, vwo_ref) = refs[2 * nk:]
    b = pl.program_id(0)
    g = pl.program_id(1)
    n_past = p_len // SLC_BLK
    q4 = q_ref[...] * (HD ** -0.5)
    lane_b = lax.broadcasted_iota(jnp.int32, (1, nb), 1) == b

    def new_row(i):
        return kvs_ref[pl.ds(i * NSA_KV + g, 1), :]

    def new_col(i):
        blk = kvt_ref[pl.ds(pl.multiple_of((i * NSA_KV + g) * HD, HD), HD), :]
        return jnp.sum(jnp.where(lane_b, blk, 0.0), axis=1, keepdims=True)

    half_of_lane = lax.broadcasted_iota(jnp.int32, (1, PAGE), 1) // SLC_BLK
    s_parts, m_parts = [], []
    has_new = jnp.zeros((1, 1), jnp.int32)
    for k in range(nk):
        blk = sel_ref[b, g, k]
        m_parts.append((half_of_lane == blk % 2) & (blk < n_past))
        s_parts.append(_dot(q4, kpages[k][...]))
        has_new = has_new + jnp.where(blk == n_past, 1, 0)
    s_sel = jnp.concatenate(s_parts, axis=1)
    m_sel = jnp.concatenate(m_parts, axis=1)
    new_ok = has_new > 0
    s_new = jnp.where(new_ok, jnp.sum(q4 * new_row(2), axis=1, keepdims=True), NEG)
    s_sel = jnp.where(m_sel, s_sel, NEG)
    mx = jnp.maximum(jnp.max(s_sel, axis=1, keepdims=True), s_new)
    p_sel = jnp.where(m_sel, jnp.exp(s_sel - mx), 0.0)
    p_new = jnp.where(new_ok, jnp.exp(s_new - mx), 0.0)
    den = jnp.sum(p_sel, axis=1, keepdims=True) + p_new
    acc = p_new * new_row(3)
    for k in range(nk):
        acc = acc + _dot(p_sel[:, k * PAGE:(k + 1) * PAGE], vpages[k][...], nt=True)
    o_s = acc / jnp.where(den > 0.0, den, 1.0)

    kwb = kwb_ref[...]
    vwb = vwb_ref[...]
    wlane = lax.broadcasted_iota(jnp.int32, (1, n_buf), 1)
    m_w = wlane > n_buf - WINDOW
    s_w = jnp.where(m_w, _dot(q4, kwb), NEG)
    s_wn = jnp.sum(q4 * new_row(4), axis=1, keepdims=True)
    mw = jnp.maximum(jnp.max(s_w, axis=1, keepdims=True), s_wn)
    p_w = jnp.where(m_w, jnp.exp(s_w - mw), 0.0)
    p_wn = jnp.exp(s_wn - mw)
    o_w = (_dot(p_w, vwb, nt=True) + p_wn * new_row(5)) / (jnp.sum(p_w, axis=1, keepdims=True) + p_wn)
    kwo_ref[...] = jnp.where(wlane == n_buf - 1, new_col(4), pltpu.roll(kwb, shift=n_buf - 1, axis=1))
    vwo_ref[...] = jnp.where(wlane == n_buf - 1, new_col(5), pltpu.roll(vwb, shift=n_buf - 1, axis=1))

    gate_row = _sigmoid(pltpu.roll(sm_ref[pl.ds(b, 1), :], shift=(LANE - 3 * NSA_REP * g) % LANE, axis=1))
    gl = lax.broadcasted_iota(jnp.int32, (NSA_REP, LANE), 1)
    gr = lax.broadcasted_iota(jnp.int32, (NSA_REP, LANE), 0)

    def gate(j):
        return jnp.sum(jnp.where(gl == 3 * gr + j, gate_row, 0.0), axis=1, keepdims=True)

    o = gate(0) * oc_ref[...] + gate(1) * o_s + gate(2) * o_w
    o_ref[...] = o * _silu(az_ref[...])


def nsa_sample_sel(sel, page_table, slc_k_t, slc_v_t, win_k_t, win_v_t, layer, kvt, kvs, q3, az3, ux, o_cmp, p_len):
    nb = page_table.shape[0]
    n_buf = win_k_t.shape[-1]
    last_blk = p_len // SLC_BLK - 1

    def page_spec(k):
        def imap(b, g, sel_r, pt_r):
            blk = jnp.minimum(sel_r[b, g, k], last_blk)
            return (layer, pt_r[b, blk // (PAGE // SLC_BLK)], g, 0, 0)
        return pl.BlockSpec((None, None, None, HD, PAGE), imap)

    win_spec = pl.BlockSpec((None, None, None, HD, n_buf), lambda b, g, s, p: (layer, b, g, 0, 0))
    wout_spec = pl.BlockSpec((None, None, HD, n_buf), lambda b, g, s, p: (b, g, 0, 0))
    row_spec = pl.BlockSpec((None, None, NSA_REP, HD), lambda b, g, s, p: (b, g, 0, 0))

    def full(a):
        return pl.BlockSpec(a.shape, lambda b, g, s, p: (0,) * a.ndim)

    small = pl.BlockSpec((nb, LANE), lambda b, g, s, p: (0, OFF_SMALL // LANE))
    return pl.pallas_call(
        functools.partial(_nsa_sample_sel_kernel, p_len=p_len, nb=nb, n_buf=n_buf),
        grid_spec=pltpu.PrefetchScalarGridSpec(
            num_scalar_prefetch=2,
            grid=(nb, NSA_KV),
            in_specs=[page_spec(k) for k in range(SLC_TOPK)] * 2
                     + [win_spec, win_spec, full(kvt), pl.BlockSpec((None, 6 * NSA_KV, HD), lambda b, g, s, p: (b, 0, 0)),
                        row_spec, row_spec, small, row_spec],
            out_specs=[row_spec, wout_spec, wout_spec]),
        out_shape=[jax.ShapeDtypeStruct((nb, NSA_KV, NSA_REP, HD), F32),
                   jax.ShapeDtypeStruct((nb, NSA_KV, HD, n_buf), F32),
                   jax.ShapeDtypeStruct((nb, NSA_KV, HD, n_buf), F32)],
        compiler_params=_cparams(("parallel", "arbitrary")),
        name="nsa_sample_sel",
    )(sel, page_table, *([slc_k_t] * SLC_TOPK), *([slc_v_t] * SLC_TOPK), win_k_t, win_v_t, kvt, kvs, q3, az3, ux, o_cmp)


def _prep_weights(w_in, cmp_pe_k, cmp_w1_k, cmp_w2_k, cmp_pe_v, cmp_w1_v, cmp_w2_v, w_br_a, w_br_b, w_br_c, w_out):
    wt = jnp.transpose(w_in, (0, 2, 1))

    def gather_rows(order, total):
        parts = [wt[:, _IN_OFF[n]:_IN_OFF[n] + _IN_SZ[n]] for n in order]
        used = sum(_IN_SZ[n] for n in order)
        parts.append(jnp.zeros((wt.shape[0], total - used, wt.shape[2]), wt.dtype))
        return jnp.concatenate(parts, axis=1).astype(BF16)

    half = CMP_STRIDE * HD

    def cat(w1):
        return jnp.concatenate([w1[:, :half], w1[:, half:]], axis=2)

    def pe_rows(pe):
        d = pe.shape[0]
        rows = pe.reshape(d, 2, half)
        return jnp.concatenate([rows, jnp.zeros((d, 6, half), pe.dtype)], axis=1)

    w1cat = jnp.stack([cat(cmp_w1_k), cat(cmp_w1_v)], axis=1)
    pe2 = jnp.stack([pe_rows(cmp_pe_k), pe_rows(cmp_pe_v)], axis=1)
    w2 = jnp.stack([cmp_w2_k, cmp_w2_v], axis=1)
    return dict(wt_ug=gather_rows(_UG_ORDER, NP_UG), wt_ux=gather_rows(_UX_ORDER, NP_UX), wt_kv=gather_rows(("a_kv",), KV_ROWS),
                wt_q=gather_rows(_TQ_ORDER, TQ_ROWS),
                w1cat=w1cat, pe2=pe2, w2=w2,
                wa=w_br_a.astype(BF16), wb=w_br_b.astype(BF16), wc=w_br_c.astype(BF16), wo=w_out.astype(BF16))


def _prompt_layer(x, h, nb, t, layer, w, p, next_norm, tiles, stacks):
    ug = matmul(h, w["wt_ug"], nt=True, tm=tiles["mm_tm"], tn=tiles["in_tn"], out_dtype=BF16, name="in_proj",
                b_layer=layer)
    ux = matmul(h, w["wt_ux"], nt=True, tm=tiles["mm_tm"], tn=NP_UX, name="in_proj_x", b_layer=layer)
    stacks = kv_proj_stacked(w["wt_kv"], layer, h, nb, t, stacks, tn=tiles["kv_tn"])
    qgt = matmul_t(w["wt_q"], layer, h, nb, t, tm=TQ_TM, tn=tiles["q_tn"], name="q_proj_t")
    nseg = t // CMP_STRIDE
    seg = jnp.stack([stacks[0][layer], stacks[1][layer]]).reshape(2, nb, NSA_KV, HD, t)
    seg = jnp.transpose(seg, (0, 1, 2, 4, 3)).reshape(2, nb * NSA_KV * nseg, CMP_STRIDE * HD)
    kc = compress_blocks(seg, w["w1cat"][layer], w["pe2"][layer], w["w2"][layer], nseg)
    o_a = nsa_prompt(ug, qgt, stacks, layer, kc, nb, t, tq=tiles["tq"], ck=tiles["ck"])
    zeros_gla = jnp.zeros((nb, GLA_HEADS, GLA_HK, GLA_HV), F32)
    zeros_ssm = jnp.zeros((nb, M_HEADS, M_HD, M_STATE), F32)
    zeros_conv = jnp.zeros((nb, M_CONV - 1, M_CONVDIM), F32)
    o_b, gla_s = gla_mixer(ug, ux, zeros_gla, p["gla_wa2"], p["gla_ba"], p["gla_norm"], nb, t,
                           c=tiles["gla_c"], sub=min(tiles["gla_sub"], t // tiles["gla_c"]), t_valid=t)
    o_c, ssm_s, conv_s = ssd_mixer(ug, ux, zeros_ssm, zeros_conv, p["conv_w"], p["conv_b"], p["dt_bias"], p["a_log"],
                                   p["d_skip"], p["m_norm"], nb, t, q=min(tiles["ssd_q"], t), t_valid=t)
    mix = merge_branches(o_a, o_b, o_c, w["wa"], w["wb"], w["wc"], layer, ug, tm=tiles["mg_tm"], tn=tiles["mg_tn"])
    x_out, h_out = out_proj_norm(mix, w["wo"], layer, x, *next_norm, tm=tiles["out_tm"])
    return x_out, h_out, stacks, (gla_s, ssm_s, conv_s)


ROW_PAD = 8


def _sample_layer(x, h, layer, w, p, next_norm, caches, states, page_table, tiles):
    nb = x.shape[0]
    p_len = page_table.shape[1] * PAGE
    ck_t, cv_t, sk_t, sv_t, wk_t, wv_t = caches
    ug = matmul(h, w["wt_ug"], nt=True, tm=nb, tn=tiles["in_tn"], out_dtype=BF16, name="in_proj_s", b_layer=layer)
    ux = matmul(h, w["wt_ux"], nt=True, tm=nb, tn=NP_UX, name="in_proj_xs", b_layer=layer)
    kv = matmul(h, w["wt_kv"], nt=True, tm=nb, tn=512, name="kv_proj_s", b_layer=layer)
    kvt = matmul(w["wt_kv"], h, nt=True, tm=512, tn=nb, name="kv_proj_st", a_layer=layer)
    qg = matmul(h, w["wt_q"], nt=True, tm=nb, tn=TQ_TM, name="q_proj_s", b_layer=layer)
    kvs = kv.reshape(nb, 6 * NSA_KV, HD)
    def pad_rows(a):
        return jnp.pad(a[:, None, :], ((0, 0), (0, ROW_PAD - 1), (0, 0))).reshape(nb * ROW_PAD, a.shape[1])

    ug_pad, ux_pad = pad_rows(ug), pad_rows(ux)
    o_b, gla_s = gla_mixer(ug_pad, ux_pad, states[0], p["gla_wa2"], p["gla_ba"], p["gla_norm"], nb, ROW_PAD,
                           c=ROW_PAD, sub=1, t_valid=1)
    o_c, ssm_s, conv_s = ssd_mixer(ug_pad, ux_pad, states[1], states[2], p["conv_w"], p["conv_b"], p["dt_bias"], p["a_log"],
                                   p["d_skip"], p["m_norm"], nb, ROW_PAD, q=ROW_PAD, t_valid=1)
    o_b = o_b.reshape(nb, ROW_PAD, GLA_DV)[:, 0]
    o_c = o_c.reshape(nb, ROW_PAD, M_DINNER)[:, 0]
    abk, abv = page_compress(ck_t, cv_t, page_table, layer, w["w1cat"], npg=min(tiles["npg"], page_table.shape[1]))
    q3 = qg[:, :TQ_AG].reshape(nb, NSA_KV, NSA_REP, HD)
    az3 = ug[:, _UG_OFF["a_z"]:_UG_OFF["a_z"] + MIX_W].astype(F32).reshape(nb, NSA_KV, NSA_REP, HD)
    o_cmp, sel = nsa_sample_cmp(abk, abv, kvs, q3, w["w1cat"][layer], w["pe2"][layer], w["w2"][layer], p_len)
    o_a, win_k, win_v = nsa_sample_sel(sel.reshape(nb, NSA_KV, LANE), page_table, sk_t, sv_t, wk_t, wv_t, layer,
                                       kvt, kvs, q3, az3, ux, o_cmp, p_len)
    mix = merge_branches(o_a.reshape(nb, MIX_W).astype(BF16), o_b, o_c, w["wa"], w["wb"], w["wc"], layer, ug,
                         tm=nb, tn=tiles["mg_tn"])
    x_out, h_out = out_proj_norm(mix, w["wo"], layer, x, *next_norm, tm=nb)
    kv4 = kv.reshape(nb, 6, 1, NSA_KV, HD)
    return x_out, h_out, (kv4[:, 0], kv4[:, 1], kv4[:, 2], kv4[:, 3], win_k, win_v, gla_s, ssm_s, conv_s)


_TILES = dict(rms_tm=256, mm_tm=1024, in_tn=2816, kv_tn=1024, q_tn=2048, tq=512, ck=512, gla_c=64, gla_sub=8,
              ssd_q=256, mg_tm=1024, mg_tn=1024, out_tm=512, npg=32)
_PARAM_NAMES = ("ln_w", "gla_wa2", "gla_ba", "gla_norm", "conv_w", "conv_b", "dt_bias", "a_log", "d_skip", "m_norm")


def kernel(x_prompt, x_sample, cache_cmp_k, cache_cmp_v, cache_slc_k, cache_slc_v, cache_win_k, cache_win_v,
           state_gla, state_ssm, state_conv, page_table, ln_w, w_in, cmp_pe_k, cmp_w1_k, cmp_w2_k, cmp_pe_v,
           cmp_w1_v, cmp_w2_v, gla_wa2, gla_ba, gla_norm, conv_w, conv_b, dt_bias, a_log, d_skip, m_norm,
           w_br_a, w_br_b, w_br_c, w_out, final_norm):
    nbp, t, d = x_prompt.shape
    nbs = x_sample.shape[0]
    depth = w_in.shape[0]
    weights = _prep_weights(w_in, cmp_pe_k, cmp_w1_k, cmp_w2_k, cmp_pe_v, cmp_w1_v, cmp_w2_v,
                            w_br_a, w_br_b, w_br_c, w_out)
    params = dict(zip(_PARAM_NAMES, (ln_w, gla_wa2, gla_ba, gla_norm, conv_w, conv_b, dt_bias, a_log, d_skip, m_norm)))
    caches = tuple(jnp.transpose(c, (0, 1, 3, 4, 2))
                   for c in (cache_cmp_k, cache_cmp_v, cache_slc_k, cache_slc_v, cache_win_k, cache_win_v))
    xp = x_prompt.reshape(nbp * t, d)
    xs = x_sample.reshape(nbs, d)
    p_states, s_out = [], []
    stacks = [jnp.zeros((depth, nbp, GRP_ROWS, t), F32) for _ in range(6)]
    hp = rmsnorm_rows(xp, ln_w[0], BF16, _TILES["rms_tm"])
    hs = rmsnorm_rows(xs, ln_w[0], BF16, nbs)
    for l in range(depth):
        p = {k: v[l] for k, v in params.items()}
        next_norm = (ln_w[l + 1], BF16) if l + 1 < depth else (final_norm, F32)
        xp, hp, stacks, st_p = _prompt_layer(xp, hp, nbp, t, l, weights, p, next_norm, _TILES, stacks)
        xs, hs, st_s = _sample_layer(xs, hs, l, weights, p, next_norm, caches,
                                     (state_gla[l], state_ssm[l], state_conv[l]), page_table, _TILES)
        p_states.append(st_p)
        s_out.append(st_s)
    y_prompt = hp.reshape(nbp, t, d)
    y_sample = hs.reshape(nbs, 1, d)

    def from_kv_layout(a):
        return jnp.transpose(a, (0, 1, 4, 2, 3))

    n_w = min(WINDOW, t)
    kv5 = [a.reshape(depth, nbp, NSA_KV, HD, t) for a in stacks]
    p_leaves = [from_kv_layout(a) for a in kv5[:4]] + [from_kv_layout(a[..., t - n_w:]) for a in kv5[4:]]
    p_leaves += [jnp.stack([st[i] for st in p_states]) for i in range(3)]
    s_leaves = [jnp.stack([o[i] for o in s_out]) for i in range(9)]
    s_leaves[4], s_leaves[5] = from_kv_layout(s_leaves[4]), from_kv_layout(s_leaves[5])
    return (y_prompt, y_sample, *p_leaves, *s_leaves)
```

```python
import functools
import math

import jax
import jax.numpy as jnp
import numpy as np
from jax import lax
from jax.experimental import pallas as pl
from jax.experimental.pallas import tpu as pltpu

F32 = jnp.float32
BF16 = jnp.bfloat16

D_MODEL = 2048
PAGE = 128
MIX_W = D_MODEL // 2
HD = 64
NSA_HEADS = MIX_W // HD
NSA_KV = 4
NSA_REP = NSA_HEADS // NSA_KV
CMP_LEN = 32
CMP_STRIDE = 16
SLC_BLK = 64
SLC_TOPK = 16
WINDOW = 512
GLA_HEADS = 4
GLA_DK = MIX_W // 2
GLA_DV = MIX_W
GLA_HK = GLA_DK // GLA_HEADS
GLA_HV = GLA_DV // GLA_HEADS
GLA_RANK = 16
GLA_TAU = 16.0
M_DINNER = MIX_W
M_HD = 64
M_HEADS = M_DINNER // M_HD
M_GROUPS = 4
M_STATE = 128
M_CONV = 4
M_CONVDIM = M_DINNER + 2 * M_GROUPS * M_STATE
EPS = 1e-6
NEG = -1e30
FORCE = 1e4
LOG2E = math.log2(math.e)
WIN_Q = 128
ONES_ROWS = 16

_IN_NAMES = ("a_q", "a_kv", "a_g", "a_z", "b_q", "b_k", "b_v", "b_a", "b_z", "c_xbc", "c_dt", "c_z", "m_g")
_IN_SIZES = (NSA_HEADS * HD, 6 * NSA_KV * HD, 3 * NSA_HEADS, MIX_W, GLA_DK, GLA_DK, GLA_DV, GLA_RANK, GLA_DV,
             M_CONVDIM, M_HEADS, M_DINNER, 3 * D_MODEL)
_IN_OFF = dict(zip(_IN_NAMES, np.cumsum((0,) + _IN_SIZES)[:-1].tolist()))
_IN_SZ = dict(zip(_IN_NAMES, _IN_SIZES))

LANE = 128
SUBLANE = 8
_UG_ORDER = ("m_g", "a_z", "b_v", "b_z", "c_z", "b_q", "b_k")
_UX_ORDER = ("c_xbc", "a_g", "b_a", "c_dt")


def _offsets(order):
    off, o = {}, 0
    for n in order:
        off[n] = o
        o += _IN_SZ[n]
    return off, o


_UG_OFF, NP_UG = _offsets(_UG_ORDER)
_UX_OFF, _ux_used = _offsets(_UX_ORDER)
NP_UX = -(-_ux_used // LANE) * LANE
OFF_SMALL = _UX_OFF["a_g"]
SM_AG, SM_BA, SM_DT = 0, 3 * NSA_HEADS, 3 * NSA_HEADS + GLA_RANK
KV_ROWS = _IN_SZ["a_kv"]
GRP_ROWS = NSA_KV * HD
_TQ_ORDER = ("a_q", "a_g")
TQ_AG = _IN_SZ["a_q"]
TQ_TM = 384
TQ_ROWS = -(-(TQ_AG + _IN_SZ["a_g"]) // TQ_TM) * TQ_TM
VMEM_LIMIT = 56 * 1024 * 1024


def _cparams(sem):
    return pltpu.CompilerParams(dimension_semantics=sem, vmem_limit_bytes=VMEM_LIMIT)


def _dot(a, b, prec=None, nt=False, tn=False):
    if prec is None:
        a, b = a.astype(BF16), b.astype(BF16)
    dn = (((0,) if tn else (1,), (1,) if nt else (0,)), ((), ()))
    return lax.dot_general(a, b, dn, precision=prec, preferred_element_type=F32)


def _split(a):
    hi = a.astype(BF16)
    return hi, (a - hi.astype(F32)).astype(BF16)


def _dot_exact_rhs(a, b_bf16):
    hi, lo = _split(a)
    return _dot(hi, b_bf16) + _dot(lo, b_bf16)


def _dot_exact_lhs(a_bf16, b):
    hi, lo = _split(b)
    return _dot(a_bf16, hi) + _dot(a_bf16, lo)


def _dot3(a, b, **kw):
    ah, al = _split(a)
    bh, bl = _split(b)
    return _dot(ah, bh, **kw) + _dot(ah, bl, **kw) + _dot(al, bh, **kw)


def _sigmoid(x):
    return 0.5 * jnp.tanh(0.5 * x) + 0.5


def _silu(x):
    return x * _sigmoid(x)


def _log_sigmoid(x):
    return jnp.minimum(x, 0.0) - jnp.log(1.0 + jnp.exp(-jnp.abs(x)))


def _softplus(x):
    return jnp.maximum(x, 0.0) + jnp.log(1.0 + jnp.exp(-jnp.abs(x)))


def _stacked_spec(block, imap, layer):
    if layer is None:
        return pl.BlockSpec(block, imap)
    return pl.BlockSpec((None,) + block, lambda *idx: (layer,) + imap(*idx))


def _rms_kernel(x_ref, w_ref, o_ref):
    x = x_ref[...]
    y = x * lax.rsqrt(jnp.mean(x * x, axis=-1, keepdims=True) + EPS)
    o_ref[...] = (y * w_ref[...]).astype(o_ref.dtype)


def rmsnorm_rows(x, w, out_dtype, tm):
    m, d = x.shape
    return pl.pallas_call(
        _rms_kernel,
        grid=(m // tm,),
        in_specs=[pl.BlockSpec((tm, d), lambda i: (i, 0)), pl.BlockSpec((1, d), lambda i: (0, 0))],
        out_specs=pl.BlockSpec((tm, d), lambda i: (i, 0)),
        out_shape=jax.ShapeDtypeStruct((m, d), out_dtype),
        compiler_params=_cparams(("parallel",)),
        name="rmsnorm",
    )(x, w.reshape(1, d))


def _mm_kernel(*refs, nt, has_res):
    a_ref, b_ref, o_ref = refs[0], refs[1], refs[-1]
    acc = _dot(a_ref[...], b_ref[...], nt=nt)
    if has_res:
        acc = acc + refs[2][...]
    o_ref[...] = acc.astype(o_ref.dtype)


def matmul(a, b, *, nt, tm, tn, res=None, out_dtype=F32, name="matmul", a_layer=None, b_layer=None):
    m, k = a.shape[-2:]
    n = b.shape[-2] if nt else b.shape[-1]
    tm, tn = min(tm, m), min(tn, n)
    if nt:
        b_spec = _stacked_spec((tn, k), lambda i, j: (j, 0), b_layer)
    else:
        b_spec = _stacked_spec((k, tn), lambda i, j: (0, j), b_layer)
    in_specs = [_stacked_spec((tm, k), lambda i, j: (i, 0), a_layer), b_spec]
    args = [a, b]
    if res is not None:
        in_specs.append(pl.BlockSpec((tm, tn), lambda i, j: (i, j)))
        args.append(res)
    return pl.pallas_call(
        functools.partial(_mm_kernel, nt=nt, has_res=res is not None),
        grid=(m // tm, n // tn),
        in_specs=in_specs,
        out_specs=pl.BlockSpec((tm, tn), lambda i, j: (i, j)),
        out_shape=jax.ShapeDtypeStruct((m, n), out_dtype),
        compiler_params=_cparams(("parallel", "parallel")),
        name=name,
    )(*args)


def matmul_t(wt, layer, h, nb, t, *, tm, tn, name):
    _, r, k = wt.shape
    tn = min(tn, t)
    nj = t // tn
    return pl.pallas_call(
        functools.partial(_mm_kernel, nt=True, has_res=False),
        grid=(nb, nj, r // tm),
        in_specs=[pl.BlockSpec((None, tm, k), lambda b, j, i: (layer, i, 0)),
                  pl.BlockSpec((tn, k), lambda b, j, i: (b * nj + j, 0))],
        out_specs=pl.BlockSpec((None, tm, tn), lambda b, j, i: (b, i, j)),
        out_shape=jax.ShapeDtypeStruct((nb, r, t), F32),
        compiler_params=_cparams(("parallel", "parallel", "parallel")),
        name=name,
    )(wt, h)


def _kv_stack_kernel(w_ref, h_ref, *refs):
    outs = refs[-6:]
    hb = h_ref[...]
    for i in range(6):
        outs[i][...] = _dot(w_ref[i * GRP_ROWS:(i + 1) * GRP_ROWS, :], hb, nt=True)


def kv_proj_stacked(wt_kv, layer, h, nb, t, stacks, *, tn):
    depth, _, k = wt_kv.shape
    tn = min(tn, t)
    nj = t // tn
    in_specs = [pl.BlockSpec((None, KV_ROWS, k), lambda b, j: (layer, 0, 0)),
                pl.BlockSpec((tn, k), lambda b, j: (b * nj + j, 0))] + [pl.BlockSpec(memory_space=pl.ANY)] * 6
    args = [wt_kv, h] + list(stacks)
    aliases = {2 + i: i for i in range(6)}
    o_spec = pl.BlockSpec((None, None, GRP_ROWS, tn), lambda b, j: (layer, b, 0, j))
    o_shape = jax.ShapeDtypeStruct((depth, nb, GRP_ROWS, t), F32)
    return pl.pallas_call(
        _kv_stack_kernel,
        grid=(nb, nj),
        in_specs=in_specs,
        out_specs=[o_spec] * 6,
        out_shape=[o_shape] * 6,
        input_output_aliases=aliases,
        compiler_params=_cparams(("parallel", "parallel")),
        name="kv_proj_t",
    )(*args)


def _out_norm_kernel(a_ref, w_ref, res_ref, nw_ref, x_ref, h_ref):
    x = res_ref[...] + _dot(a_ref[...], w_ref[...])
    x_ref[...] = x
    y = x * lax.rsqrt(jnp.mean(x * x, axis=-1, keepdims=True) + EPS)
    h_ref[...] = (y * nw_ref[...]).astype(h_ref.dtype)


def out_proj_norm(mix, wo, layer, res, norm_w, h_dtype, *, tm):
    m, k = mix.shape
    n = wo.shape[-1]
    tm = min(tm, m)
    row = lambda i: (i, 0)
    return pl.pallas_call(
        _out_norm_kernel,
        grid=(m // tm,),
        in_specs=[pl.BlockSpec((tm, k), row), pl.BlockSpec((None, k, n), lambda i: (layer, 0, 0)),
                  pl.BlockSpec((tm, n), row), pl.BlockSpec((1, n), lambda i: (0, 0))],
        out_specs=[pl.BlockSpec((tm, n), row), pl.BlockSpec((tm, n), row)],
        out_shape=[jax.ShapeDtypeStruct((m, n), F32), jax.ShapeDtypeStruct((m, n), h_dtype)],
        compiler_params=_cparams(("parallel",)),
        name="out_proj",
    )(mix, wo, res, norm_w.reshape(1, n))


def _merge_kernel(oa_ref, ob_ref, oc_ref, wa_ref, wb_ref, wc_ref, g0_ref, g1_ref, g2_ref, o_ref):
    acc = _sigmoid(g0_ref[...].astype(F32)) * _dot(oa_ref[...], wa_ref[...])
    acc = acc + _sigmoid(g1_ref[...].astype(F32)) * _dot(ob_ref[...], wb_ref[...])
    acc = acc + _sigmoid(g2_ref[...].astype(F32)) * _dot(oc_ref[...], wc_ref[...])
    o_ref[...] = acc.astype(o_ref.dtype)


def merge_branches(o_a, o_b, o_c, wa, wb, wc, layer, ug, *, tm, tn):
    m, k = o_a.shape
    n = wa.shape[-1]
    tm = min(tm, m)
    gb = _UG_OFF["m_g"] // tn
    nj = n // tn
    o_spec = pl.BlockSpec((tm, k), lambda i, j: (i, 0))
    w_spec = pl.BlockSpec((None, k, tn), lambda i, j: (layer, 0, j))
    g_specs = [pl.BlockSpec((tm, tn), functools.partial(lambda i, j, br: (i, gb + br * nj + j), br=br))
               for br in range(3)]
    return pl.pallas_call(
        _merge_kernel,
        grid=(m // tm, nj),
        in_specs=[o_spec, o_spec, o_spec, w_spec, w_spec, w_spec] + g_specs,
        out_specs=pl.BlockSpec((tm, tn), lambda i, j: (i, j)),
        out_shape=jax.ShapeDtypeStruct((m, n), BF16),
        compiler_params=_cparams(("parallel", "parallel")),
        name="merge",
    )(o_a, o_b, o_c, wa, wb, wc, ug, ug, ug)


def _cmp_kernel(seg_ref, w1_ref, pe_ref, w2_ref, o_ref, *, nseg):
    w1 = w1_ref[...]
    ab = _dot3(seg_ref[...], w1)
    pe = _dot3(pe_ref[...], w1)
    cst = pe[0:1, :HD] + pe[1:2, HD:]
    nxt = pltpu.roll(ab[:, HD:], shift=nseg - 1, axis=0)
    pre = ab[:, :HD] + nxt + cst
    o_ref[...] = _dot3(_silu(pre), w2_ref[...])


def compress_blocks(seg, w1cat, pe2, w2, nseg):
    _, r, kdim = seg.shape
    return pl.pallas_call(
        functools.partial(_cmp_kernel, nseg=nseg),
        grid=(2, r // nseg),
        in_specs=[pl.BlockSpec((None, nseg, kdim), lambda s, i: (s, i, 0)),
                  pl.BlockSpec((None, kdim, 2 * HD), lambda s, i: (s, 0, 0)),
                  pl.BlockSpec((None, 8, kdim), lambda s, i: (s, 0, 0)),
                  pl.BlockSpec((None, HD, HD), lambda s, i: (s, 0, 0))],
        out_specs=pl.BlockSpec((None, nseg, HD), lambda s, i: (s, i, 0)),
        out_shape=jax.ShapeDtypeStruct((2, r, HD), F32),
        compiler_params=_cparams(("parallel", "parallel")),
        name="compress",
    )(seg, w1cat, pe2, w2)


def _nsa_prompt_kernel(qt_ref, ag_ref, az_ref, kst_ref, vst_ref, kwt_ref, vwt_ref, kc_ref, vc_ref, o_ref,
                       ks_scr, vs_scr, kw_scr, vw_scr, bias_scr, *, t, tq, ck, nseg):
    g = pl.program_id(1)
    i = pl.program_id(2)
    s0 = i * tq
    ns = t // SLC_BLK
    nsr = bias_scr.shape[0]
    cols = NSA_REP * tq

    @pl.when(i == 0)
    def _():
        ks_scr[...] = jnp.transpose(kst_ref[...]).astype(BF16)
        kw_scr[...] = jnp.transpose(kwt_ref[...]).astype(BF16)
        ones = jnp.ones((ONES_ROWS, t), BF16)
        vs_scr[...] = jnp.concatenate([vst_ref[...].astype(BF16), ones], axis=0)
        vw_scr[...] = jnp.concatenate([vwt_ref[...].astype(BF16), ones], axis=0)

    qt = qt_ref[...]
    q2 = jnp.concatenate([qt[r * HD:(r + 1) * HD, :] for r in range(NSA_REP)], axis=1) * (HD ** -0.5 * LOG2E)
    q2b = q2.astype(BF16)
    tpos1 = s0 + lax.broadcasted_iota(jnp.int32, (1, tq), 1)
    tpos = jnp.concatenate([tpos1] * NSA_REP, axis=1)

    crow = lax.broadcasted_iota(jnp.int32, (nseg, 1), 0)
    m_c = (crow * CMP_STRIDE + (CMP_LEN - 1) <= tpos) & (crow < nseg - 1)
    s_c = jnp.where(m_c, _dot3(kc_ref[...], q2), NEG)
    p_c = jnp.where(m_c, jnp.exp2(s_c - jnp.max(s_c, axis=0, keepdims=True)), 0.0)
    l_c = jnp.sum(p_c, axis=0, keepdims=True)
    p_c = p_c / jnp.where(l_c > 0.0, l_c, 1.0)
    o_c = _dot(jnp.transpose(vc_ref[...]), p_c)
    psum = p_c[:, 0:tq]
    for r in range(1, NSA_REP):
        psum = psum + p_c[:, r * tq:(r + 1) * tq]

    srow = lax.broadcasted_iota(jnp.int32, (nsr, 1), 0)
    ccol = lax.broadcasted_iota(jnp.int32, (1, nseg), 1) * CMP_STRIDE
    cover = jnp.where((ccol + CMP_LEN > srow * SLC_BLK) & (ccol < (srow + 1) * SLC_BLK), 1.0, 0.0).astype(BF16)
    imp = _dot_exact_lhs(cover, psum)
    cur = tpos1 // SLC_BLK
    valid = (srow * SLC_BLK <= tpos1) & (srow < ns)
    forced = (srow == 0) | (srow == cur) | (srow == cur - 1)
    score = jnp.where(valid, jnp.where(forced, FORCE, imp), NEG)
    rank = jnp.zeros((nsr, tq), F32)
    for j in range(ns):
        sj = score[j:j + 1, :]
        rank = rank + jnp.where((sj > score) | ((sj == score) & (j < srow)), 1.0, 0.0)
    sel = (rank < float(min(SLC_TOPK, ns))) & valid
    bias_scr[...] = jnp.where(sel, 0.0, NEG)

    def attend(k_rows, v_cols, bias1, m, acc, q_cols=q2b):
        w = bias1.shape[1]
        s = _dot(k_rows, q_cols)
        s = jnp.concatenate([s[:, r * w:(r + 1) * w] + bias1 for r in range(NSA_REP)], axis=1)
        m_new = jnp.maximum(m, jnp.max(s, axis=0, keepdims=True))
        p = jnp.exp2(s - m_new)
        return m_new, jnp.exp2(m - m_new) * acc + _dot(v_cols, p)

    def finish(acc):
        l = acc[HD:HD + 1, :]
        return acc[:HD, :] / jnp.where(l > 0.0, l, 1.0)

    blk_per_chunk = ck // SLC_BLK
    m0 = jnp.full((1, cols), NEG, F32)
    acc0 = jnp.zeros((HD + ONES_ROWS, cols), F32)

    def sel_chunk(j, carry):
        k0 = pl.multiple_of(j * ck, ck)
        npos = k0 + lax.broadcasted_iota(jnp.int32, (ck, 1), 0)
        rows = [jnp.broadcast_to(bias_scr[pl.ds(j * blk_per_chunk + k, 1), :], (SLC_BLK, tq))
                for k in range(blk_per_chunk)]
        bias1 = jnp.where(npos <= tpos1, jnp.concatenate(rows, axis=0), NEG)
        return attend(ks_scr[pl.ds(k0, ck), :], vs_scr[:, pl.ds(k0, ck)], bias1, *carry)

    n_chunks = (s0 + tq + ck - 1) // ck
    o_s = finish(lax.fori_loop(0, n_chunks, sel_chunk, (m0, acc0))[1])

    wq = min(WIN_Q, tq)
    ww = min(WINDOW + wq, t)
    parts = []
    for a in range(tq // wq):
        w0 = pl.multiple_of(jnp.clip(s0 + a * wq - WINDOW, 0, t - ww), LANE)
        wpos = w0 + lax.broadcasted_iota(jnp.int32, (ww, 1), 0)
        tsub = tpos1[:, a * wq:(a + 1) * wq]
        bias_w = jnp.where((wpos <= tsub) & (wpos > tsub - WINDOW), 0.0, NEG)
        q_sub = jnp.concatenate([q2b[:, r * tq + a * wq:r * tq + (a + 1) * wq] for r in range(NSA_REP)], axis=1)
        parts.append(finish(attend(kw_scr[pl.ds(w0, ww), :], vw_scr[:, pl.ds(w0, ww)], bias_w,
                                   m0[:, :NSA_REP * wq], acc0[:, :NSA_REP * wq], q_sub)[1]))
    o_w = jnp.concatenate([parts[a][:, r * wq:(r + 1) * wq] for r in range(NSA_REP) for a in range(tq // wq)], axis=1)

    outs = []
    for r in range(NSA_REP):
        sl = slice(r * tq, (r + 1) * tq)
        gt = [_sigmoid(ag_ref[pl.ds(3 * (NSA_REP * g + r) + j, 1), :]) for j in range(3)]
        outs.append(jnp.transpose(gt[0] * o_c[:, sl] + gt[1] * o_s[:, sl] + gt[2] * o_w[:, sl]))
    o = jnp.concatenate(outs, axis=1) * _silu(az_ref[...].astype(F32))
    o_ref[...] = o.astype(o_ref.dtype)


def nsa_prompt(ug, qgt, stacks, layer, kc, nb, t, *, tq, ck):
    tq = min(tq, t)
    nq = t // tq
    nseg = t // CMP_STRIDE
    ck = min(ck, t)
    gw = NSA_REP * HD
    nsr = -(-(t // SLC_BLK) // SUBLANE) * SUBLANE
    kv_spec = pl.BlockSpec((None, None, HD, t), lambda b, g, i: (layer, b, g, 0))
    return pl.pallas_call(
        functools.partial(_nsa_prompt_kernel, t=t, tq=tq, ck=ck, nseg=nseg),
        grid=(nb, NSA_KV, nq),
        in_specs=[pl.BlockSpec((None, gw, tq), lambda b, g, i: (b, g, i)),
                  pl.BlockSpec((None, LANE, tq), lambda b, g, i: (b, TQ_AG // LANE, i)),
                  pl.BlockSpec((tq, gw), lambda b, g, i: (b * nq + i, _UG_OFF["a_z"] // gw + g)),
                  kv_spec, kv_spec, kv_spec, kv_spec,
                  pl.BlockSpec((None, nseg, HD), lambda b, g, i: (0, b * NSA_KV + g, 0)),
                  pl.BlockSpec((None, nseg, HD), lambda b, g, i: (1, b * NSA_KV + g, 0))],
        out_specs=pl.BlockSpec((tq, gw), lambda b, g, i: (b * nq + i, g)),
        out_shape=jax.ShapeDtypeStruct((nb * t, NSA_HEADS * HD), BF16),
        scratch_shapes=[pltpu.VMEM((t, HD), BF16), pltpu.VMEM((HD + ONES_ROWS, t), BF16),
                        pltpu.VMEM((t, HD), BF16), pltpu.VMEM((HD + ONES_ROWS, t), BF16),
                        pltpu.VMEM((nsr, tq), F32)],
        compiler_params=_cparams(("parallel", "parallel", "arbitrary")),
        name="nsa_prompt",
    )(qgt, qgt, ug, stacks[2], stacks[3], stacks[4], stacks[5], kc, kc)


def _gla_kernel(q_ref, k_ref, v_ref, z_ref, sm_ref, s0_ref, wa2_ref, ba_ref, gn_ref, o_ref, sout_ref, s_scr,
                *, c, sub, t_valid):
    ci = pl.program_id(1)
    rows = c * sub

    @pl.when(ci == 0)
    def _():
        s_scr[...] = s0_ref[...]

    row = ci * rows + lax.broadcasted_iota(jnp.int32, (rows, 1), 0)
    gate_in = sm_ref[...][:, SM_BA:SM_BA + GLA_RANK]
    log_a = _log_sigmoid(_dot3(gate_in, wa2_ref[...]) + ba_ref[...]) / GLA_TAU
    log_a = jnp.where(row < t_valid, log_a, 0.0)
    causal = lax.broadcasted_iota(jnp.int32, (c, c), 0) >= lax.broadcasted_iota(jnp.int32, (c, c), 1)
    tri = jnp.where(causal, 1.0, 0.0).astype(BF16)
    cb_all = jnp.concatenate([_dot_exact_lhs(tri, log_a[j * c:(j + 1) * c, :]) for j in range(sub)], axis=0)
    for h in range(GLA_HEADS):
        ks = slice(h * GLA_HK, (h + 1) * GLA_HK)
        vs = slice(h * GLA_HV, (h + 1) * GLA_HV)
        state = s_scr[h]
        outs = []
        for j in range(sub):
            rs = slice(j * c, (j + 1) * c)
            cb = cb_all[rs, ks]
            ecb = jnp.exp(cb)
            qd = q_ref[rs, ks].astype(F32) * (GLA_HK ** -0.5) * ecb
            kh = k_ref[rs, ks].astype(F32)
            vh = v_ref[rs, vs]
            att = jnp.where(causal, _dot(qd, kh * jnp.exp(-cb), nt=True), 0.0)
            outs.append(_dot(att, vh) + _dot(qd, state))
            c_end = cb[c - 1:c, :]
            k_dec = kh * jnp.exp(c_end - cb)
            e_col = jnp.transpose(ecb[c - SUBLANE:c, :])[:, SUBLANE - 1:SUBLANE]
            state = e_col * state + _dot(k_dec, vh, tn=True)
        o = jnp.concatenate(outs, axis=0)
        y = o * lax.rsqrt(jnp.mean(o * o, axis=-1, keepdims=True) + EPS) * gn_ref[...]
        o_ref[:, vs] = (y * _silu(z_ref[:, vs].astype(F32))).astype(o_ref.dtype)
        s_scr[h] = state

    @pl.when(ci == pl.num_programs(1) - 1)
    def _():
        sout_ref[...] = s_scr[...]


def gla_mixer(ug, ux, s0, wa2, ba, gnorm, nb, t, *, c, sub, t_valid):
    rows = c * sub
    nc = t // rows

    def u_spec(name, width):
        return pl.BlockSpec((rows, width), lambda b, i: (b * nc + i, _UG_OFF[name] // width))

    s_spec = pl.BlockSpec((None, GLA_HEADS, GLA_HK, GLA_HV), lambda b, i: (b, 0, 0, 0))
    return pl.pallas_call(
        functools.partial(_gla_kernel, c=c, sub=sub, t_valid=t_valid),
        grid=(nb, nc),
        in_specs=[u_spec("b_q", GLA_DK), u_spec("b_k", GLA_DK), u_spec("b_v", GLA_DV), u_spec("b_z", GLA_DV),
                  pl.BlockSpec((rows, LANE), lambda b, i: (b * nc + i, OFF_SMALL // LANE)),
                  s_spec,
                  pl.BlockSpec((GLA_RANK, GLA_DK), lambda b, i: (0, 0)),
                  pl.BlockSpec((1, GLA_DK), lambda b, i: (0, 0)),
                  pl.BlockSpec((1, GLA_HV), lambda b, i: (0, 0))],
        out_specs=[pl.BlockSpec((rows, GLA_DV), lambda b, i: (b * nc + i, 0)), s_spec],
        out_shape=[jax.ShapeDtypeStruct((nb * t, GLA_DV), BF16),
                   jax.ShapeDtypeStruct((nb, GLA_HEADS, GLA_HK, GLA_HV), F32)],
        scratch_shapes=[pltpu.VMEM((GLA_HEADS, GLA_HK, GLA_HV), F32)],
        compiler_params=_cparams(("parallel", "arbitrary")),
        name="gla",
    )(ug, ug, ug, ug, ux, s0, wa2, ba.reshape(1, GLA_DK), gnorm.reshape(1, GLA_HV))


def _ssd_kernel(xbc_ref, z_ref, sm_ref, s0_ref, cbuf_ref, cw_ref, cbias_ref, dtb_ref, alog_ref, dsk_ref, mn_ref,
                o_ref, sout_ref, cout_ref, s_scr, f_scr, *, q, t_valid, n_chunks):
    ci = pl.program_id(1)
    npad = SUBLANE
    hist = M_CONV - 1

    @pl.when(ci == 0)
    def _():
        s_scr[...] = s0_ref[...].reshape(s_scr.shape)
        f_scr[npad - hist:npad, :] = cbuf_ref[...]

    f_scr[npad:npad + q, :] = xbc_ref[...]
    conv = cbias_ref[...] + xbc_ref[...] * cw_ref[hist:hist + 1, :]
    full = f_scr[...]
    for i in range(hist):
        conv = conv + pltpu.roll(full, shift=hist - i, axis=0)[npad:npad + q, :] * cw_ref[i:i + 1, :]
    xbc = _silu(conv)

    last_valid = t_valid - (n_chunks - 1) * q

    @pl.when(ci == n_chunks - 1)
    def _():
        cout_ref[...] = f_scr[npad - hist + last_valid:npad + last_valid, :]

    f_scr[npad - hist:npad, :] = f_scr[npad - hist + q:npad + q, :]

    lane = lax.broadcasted_iota(jnp.int32, (1, LANE), 1)
    row = ci * q + lax.broadcasted_iota(jnp.int32, (q, 1), 0)
    head_lane = (lane >= SM_DT) & (lane < SM_DT + M_HEADS)
    dt = jnp.where(head_lane & (row < t_valid), _softplus(sm_ref[...] + dtb_ref[...]), 0.0)
    a = dt * (-jnp.exp(alog_ref[...]))
    tri = jnp.where(lax.broadcasted_iota(jnp.int32, (q, q), 0) >= lax.broadcasted_iota(jnp.int32, (q, q), 1), 1.0, 0.0)
    causal = tri > 0.5
    cum = _dot_exact_lhs(tri.astype(BF16), a)
    cum_t = jnp.transpose(cum)
    dt_t = jnp.transpose(dt)
    nbc = M_GROUPS * M_STATE
    rep = M_HEADS // M_GROUPS
    gw = rep * M_HD
    head_of_lane = lax.broadcasted_iota(jnp.int32, (1, gw), 1) // M_HD
    head_of_row = lax.broadcasted_iota(jnp.int32, (gw, 1), 0) // M_HD
    for g in range(M_GROUPS):
        sl = slice(g * gw, (g + 1) * gw)
        bg = xbc[:, M_DINNER + g * M_STATE:M_DINNER + (g + 1) * M_STATE]
        cg = xbc[:, M_DINNER + nbc + g * M_STATE:M_DINNER + nbc + (g + 1) * M_STATE]
        xg = xbc[:, sl]
        cbm = _dot(cg, bg, nt=True)
        s_prev = s_scr[g]
        y = jnp.zeros((q, gw), F32)
        e_lanes = jnp.zeros((q, gw), F32)
        w_lanes = jnp.zeros((q, gw), F32)
        skip = jnp.zeros((1, gw), F32)
        keep = jnp.zeros((gw, 1), F32)
        for r in range(rep):
            hl = SM_DT + g * rep + r
            mine = head_of_lane == r
            cum_c = cum[:, hl:hl + 1]
            c_end = cum[q - 1:q, hl:hl + 1]
            decay = jnp.exp(jnp.where(causal, cum_c - cum_t[hl:hl + 1, :], -jnp.inf))
            y = y + _dot(cbm * decay * dt_t[hl:hl + 1, :], jnp.where(mine, xg, 0.0))
            e_lanes = jnp.where(mine, jnp.exp(cum_c), e_lanes)
            w_lanes = jnp.where(mine, jnp.exp(c_end - cum_c) * dt[:, hl:hl + 1], w_lanes)
            skip = jnp.where(mine, dsk_ref[:, hl:hl + 1], skip)
            keep = jnp.where(head_of_row == r, jnp.exp(c_end), keep)
        y = y + e_lanes * _dot(cg, s_prev, nt=True) + skip * xg
        s_scr[g] = keep * s_prev + _dot(xg * w_lanes, bg, tn=True)
        yg = y * _silu(z_ref[:, sl].astype(F32))
        yn = yg * lax.rsqrt(jnp.mean(yg * yg, axis=-1, keepdims=True) + EPS) * mn_ref[:, sl]
        o_ref[:, sl] = yn.astype(o_ref.dtype)

    @pl.when(ci == n_chunks - 1)
    def _():
        sout_ref[...] = s_scr[...].reshape(M_HEADS, M_HD, M_STATE)


def _pad_heads(v):
    return jnp.zeros((1, LANE), F32).at[0, SM_DT:SM_DT + M_HEADS].set(v)


def ssd_mixer(ug, ux, s0, cbuf, conv_w, conv_b, dt_bias, a_log, d_skip, m_norm, nb, t, *, q, t_valid):
    nc = t // q
    n_chunks = -(-t_valid // q)
    assert n_chunks == nc
    hist = M_CONV - 1

    def u_spec(off, name, width):
        return pl.BlockSpec((q, width), lambda b, i: (b * nc + i, off[name] // width))

    def full(shape):
        return pl.BlockSpec(shape, lambda b, i: (0,) * len(shape))

    s_spec = pl.BlockSpec((None, M_HEADS, M_HD, M_STATE), lambda b, i: (b, 0, 0, 0))
    c_spec = pl.BlockSpec((None, hist, M_CONVDIM), lambda b, i: (b, 0, 0))
    return pl.pallas_call(
        functools.partial(_ssd_kernel, q=q, t_valid=t_valid, n_chunks=n_chunks),
        grid=(nb, nc),
        in_specs=[u_spec(_UX_OFF, "c_xbc", M_CONVDIM), u_spec(_UG_OFF, "c_z", M_DINNER),
                  pl.BlockSpec((q, LANE), lambda b, i: (b * nc + i, OFF_SMALL // LANE)),
                  s_spec, c_spec,
                  full((M_CONV, M_CONVDIM)), full((1, M_CONVDIM)), full((1, LANE)), full((1, LANE)),
                  full((1, LANE)), full((1, M_DINNER))],
        out_specs=[pl.BlockSpec((q, M_DINNER), lambda b, i: (b * nc + i, 0)), s_spec, c_spec],
        out_shape=[jax.ShapeDtypeStruct((nb * t, M_DINNER), BF16),
                   jax.ShapeDtypeStruct((nb, M_HEADS, M_HD, M_STATE), F32),
                   jax.ShapeDtypeStruct((nb, hist, M_CONVDIM), F32)],
        scratch_shapes=[pltpu.VMEM((M_GROUPS, M_HEADS // M_GROUPS * M_HD, M_STATE), F32),
                        pltpu.VMEM((q + SUBLANE, M_CONVDIM), F32)],
        compiler_params=_cparams(("parallel", "arbitrary")),
        name="ssd",
    )(ux, ug, ux, s0, cbuf, conv_w, conv_b.reshape(1, M_CONVDIM), _pad_heads(dt_bias), _pad_heads(a_log),
      _pad_heads(d_skip), m_norm.reshape(1, M_DINNER))


SEG_PER_PAGE = PAGE // CMP_STRIDE
BLK_PER_PAGE = PAGE // SLC_BLK


def _page_cmp_kernel(pt_ref, *refs, npg):
    w1_ref, ok_ref, ov_ref, scr, wbd_scr = refs[2 * npg:]

    @pl.when((pl.program_id(0) == 0) & (pl.program_id(1) == 0))
    def _():
        wbd_scr[...] = jnp.zeros(wbd_scr.shape, wbd_scr.dtype)
        for s in range(2):
            for j in range(CMP_STRIDE):
                wj = w1_ref[s, j * HD:(j + 1) * HD, :].astype(BF16)
                for g in range(NSA_KV):
                    wbd_scr[s, j, g * HD:(g + 1) * HD, g * 2 * HD:(g + 1) * 2 * HD] = wj

    row = lax.broadcasted_iota(jnp.int32, (PAGE, PAGE), 0)
    pos = lax.broadcasted_iota(jnp.int32, (PAGE, PAGE), 1)
    regroup = jnp.where(pos == CMP_STRIDE * (row % SEG_PER_PAGE) + row // SEG_PER_PAGE, 1.0, 0.0).astype(BF16)
    for s, (pages, o_ref) in enumerate(((refs[:npg], ok_ref), (refs[npg:2 * npg], ov_ref))):
        for k in range(npg):
            by_j = _dot(regroup, pages[k][...].reshape(NSA_KV * HD, PAGE), nt=True)
            for j in range(CMP_STRIDE):
                scr[j, k * SEG_PER_PAGE:(k + 1) * SEG_PER_PAGE, :] = by_j[j * SEG_PER_PAGE:(j + 1) * SEG_PER_PAGE, :]
        acc = jnp.zeros((npg * SEG_PER_PAGE, 2 * HD * NSA_KV), F32)
        for j in range(CMP_STRIDE):
            acc = acc + _dot(scr[j], wbd_scr[s, j])
        o_ref[...] = acc


def page_compress(cache_k_t, cache_v_t, page_table, layer, w1cat, *, npg):
    nb, n_pages = page_table.shape
    steps = n_pages // npg

    def page_spec(k):
        return pl.BlockSpec((None, None, NSA_KV, HD, PAGE),
                            lambda b, i, pt: (layer, pt[b, i * npg + k], 0, 0, 0))

    w_spec = pl.BlockSpec((None, 2, CMP_STRIDE * HD, 2 * HD), lambda b, i, pt: (layer, 0, 0, 0))
    o_spec = pl.BlockSpec((None, npg * SEG_PER_PAGE, NSA_KV * 2 * HD), lambda b, i, pt: (b, i, 0))
    o_shape = jax.ShapeDtypeStruct((nb, n_pages * SEG_PER_PAGE, NSA_KV * 2 * HD), F32)
    return pl.pallas_call(
        functools.partial(_page_cmp_kernel, npg=npg),
        grid_spec=pltpu.PrefetchScalarGridSpec(
            num_scalar_prefetch=1,
            grid=(nb, steps),
            in_specs=[page_spec(k) for k in range(npg)] * 2 + [w_spec],
            out_specs=[o_spec, o_spec],
            scratch_shapes=[pltpu.VMEM((CMP_STRIDE, npg * SEG_PER_PAGE, NSA_KV * HD), F32),
                            pltpu.VMEM((2, CMP_STRIDE, NSA_KV * HD, NSA_KV * 2 * HD), BF16)]),
        out_shape=[o_shape, o_shape],
        compiler_params=_cparams(("arbitrary", "arbitrary")),
        name="page_compress",
    )(page_table, *([cache_k_t] * npg), *([cache_v_t] * npg), w1cat)


def _masked_softmax_rows(s, mask):
    s = jnp.where(mask, s, NEG)
    m = jnp.max(s, axis=-1, keepdims=True)
    p = jnp.where(mask, jnp.exp(s - m), 0.0)
    l = jnp.sum(p, axis=-1, keepdims=True)
    return p / jnp.where(l > 0.0, l, 1.0)


def _nsa_sample_cmp_kernel(abk_ref, abv_ref, kvs_ref, q_ref, w1_ref, pe_ref, w2_ref, cover_ref, oc_ref, sel_ref,
                           cst_scr, *, p_len, nc, sl):
    ns = p_len // SLC_BLK + 1
    rowi = lax.broadcasted_iota(jnp.int32, (nc, 1), 0)

    @pl.when(pl.program_id(0) == 0)
    def _():
        for idx in range(2):
            pe = _dot3(pe_ref[idx], w1_ref[idx])
            cst_scr[idx:idx + 1, :] = pe[0:1, :HD] + pe[1:2, HD:]

    cidx = lax.broadcasted_iota(jnp.int32, (1, nc), 1)
    m_c = cidx * CMP_STRIDE + (CMP_LEN - 1) <= p_len
    sidx = lax.broadcasted_iota(jnp.int32, (1, sl), 1)
    cur = p_len // SLC_BLK
    valid = (sidx * SLC_BLK <= p_len) & (sidx < ns)
    forced = (sidx == 0) | (sidx == cur) | (sidx == cur - 1)
    ri = lax.broadcasted_iota(jnp.int32, (sl, sl), 0)
    li = lax.broadcasted_iota(jnp.int32, (sl, sl), 1)
    kl = lax.broadcasted_iota(jnp.int32, (sl, LANE), 1).astype(F32)
    sv = lax.broadcasted_iota(jnp.int32, (sl, LANE), 0).astype(F32)

    for g in range(NSA_KV):
        def compressed(ab_ref, idx):
            ab = ab_ref[:, g * 2 * HD:(g + 1) * 2 * HD]
            new = kvs_ref[idx * NSA_KV + g:idx * NSA_KV + g + 1, :]
            b_new = _dot(new, w1_ref[idx][0:HD, HD:])
            nxt = jnp.where(rowi == nc - 1, b_new, pltpu.roll(ab[:, HD:], shift=nc - 1, axis=0))
            return _dot(_silu(ab[:, :HD] + nxt + cst_scr[idx:idx + 1, :]), w2_ref[idx])

        kc = compressed(abk_ref, 0)
        vc = compressed(abv_ref, 1)
        p_c = _masked_softmax_rows(_dot3(q_ref[g] * (HD ** -0.5), kc, nt=True), m_c)
        oc_ref[g] = _dot(p_c, vc)
        imp = _dot_exact_rhs(jnp.sum(p_c, axis=0, keepdims=True), cover_ref[...])
        score = jnp.where(valid, jnp.where(forced, FORCE, imp), NEG)
        score_b = jnp.broadcast_to(score, (sl, sl))
        score_col = jnp.sum(jnp.where(ri == li, score_b, 0.0), axis=1, keepdims=True)
        ahead = (score_b > score_col) | ((score_b == score_col) & (li < ri))
        rank_col = jnp.sum(jnp.where(ahead, 1.0, 0.0), axis=1, keepdims=True)
        picked = jnp.sum(jnp.where(rank_col == kl, sv, 0.0), axis=0, keepdims=True)
        sel_ref[g] = picked.astype(jnp.int32)


def _cover_matrix(nc, sl):
    c0 = np.arange(nc)[:, None] * CMP_STRIDE
    s0 = np.arange(sl)[None, :] * SLC_BLK
    return jnp.asarray(((c0 + CMP_LEN > s0) & (c0 < s0 + SLC_BLK)).astype(np.float32), dtype=BF16)


def nsa_sample_cmp(abk, abv, kvs, q3, w1cat, pe2, w2, p_len):
    nb, nc, _ = abk.shape
    ns = p_len // SLC_BLK + 1
    sl = -(-ns // LANE) * LANE
    cover = _cover_matrix(nc, sl)

    ab_spec = pl.BlockSpec((None, nc, NSA_KV * 2 * HD), lambda b: (b, 0, 0))

    def full(a):
        return pl.BlockSpec(a.shape, lambda b: (0,) * a.ndim)

    row_spec = pl.BlockSpec((None, NSA_KV, NSA_REP, HD), lambda b: (b, 0, 0, 0))
    return pl.pallas_call(
        functools.partial(_nsa_sample_cmp_kernel, p_len=p_len, nc=nc, sl=sl),
        grid=(nb,),
        in_specs=[ab_spec, ab_spec, pl.BlockSpec((None, 6 * NSA_KV, HD), lambda b: (b, 0, 0)), row_spec,
                  full(w1cat), full(pe2), full(w2), full(cover)],
        out_specs=[row_spec, pl.BlockSpec((None, NSA_KV, 1, LANE), lambda b: (b, 0, 0, 0))],
        out_shape=[jax.ShapeDtypeStruct((nb, NSA_KV, NSA_REP, HD), F32),
                   jax.ShapeDtypeStruct((nb, NSA_KV, 1, LANE), jnp.int32)],
        scratch_shapes=[pltpu.VMEM((SUBLANE, HD), F32)],
        compiler_params=_cparams(("arbitrary",)),
        name="nsa_sample_cmp",
    )(abk, abv, kvs, q3, w1cat, pe2, w2, cover)


def _nsa_sample_sel_kernel(sel_ref, pt_ref, *refs, p_len, nb, n_buf):
    nk = SLC_TOPK
    kpages, vpages = refs[:nk], refs[nk:2 * nk]
    (kwb_ref, vwb_ref, kvt_ref, kvs_ref, q_ref, az_ref, sm_ref, oc_ref,
     o_ref, kwo_ref, vwo_ref) = refs[2 * nk:]
    b = pl.program_id(0)
    g = pl.program_id(1)
    n_past = p_len // SLC_BLK
    q4 = q_ref[...] * (HD ** -0.5)
    lane_b = lax.broadcasted_iota(jnp.int32, (1, nb), 1) == b

    def new_row(i):
        return kvs_ref[pl.ds(i * NSA_KV + g, 1), :]

    def new_col(i):
        blk = kvt_ref[pl.ds(pl.multiple_of((i * NSA_KV + g) * HD, HD), HD), :]
        return jnp.sum(jnp.where(lane_b, blk, 0.0), axis=1, keepdims=True)

    half_of_lane = lax.broadcasted_iota(jnp.int32, (1, PAGE), 1) // SLC_BLK
    s_parts, m_parts = [], []
    has_new = jnp.zeros((1, 1), jnp.int32)
    for k in range(nk):
        blk = sel_ref[b, g, k]
        m_parts.append((half_of_lane == (blk & (BLK_PER_PAGE - 1))) & (blk < n_past))
        s_parts.append(_dot(q4, kpages[k][...]))
        has_new = has_new + jnp.where(blk == n_past, 1, 0)
    s_sel = jnp.concatenate(s_parts, axis=1)
    m_sel = jnp.concatenate(m_parts, axis=1)
    new_ok = has_new > 0
    s_new = jnp.where(new_ok, jnp.sum(q4 * new_row(2), axis=1, keepdims=True), NEG)
    s_sel = jnp.where(m_sel, s_sel, NEG)
    mx = jnp.maximum(jnp.max(s_sel, axis=1, keepdims=True), s_new)
    p_sel = jnp.where(m_sel, jnp.exp(s_sel - mx), 0.0)
    p_new = jnp.where(new_ok, jnp.exp(s_new - mx), 0.0)
    den = jnp.sum(p_sel, axis=1, keepdims=True) + p_new
    acc = p_new * new_row(3)
    for k in range(nk):
        acc = acc + _dot(p_sel[:, k * PAGE:(k + 1) * PAGE], vpages[k][...], nt=True)
    o_s = acc / jnp.where(den > 0.0, den, 1.0)

    kwb = kwb_ref[...]
    vwb = vwb_ref[...]
    wlane = lax.broadcasted_iota(jnp.int32, (1, n_buf), 1)
    m_w = wlane > n_buf - WINDOW
    s_w = jnp.where(m_w, _dot(q4, kwb), NEG)
    s_wn = jnp.sum(q4 * new_row(4), axis=1, keepdims=True)
    mw = jnp.maximum(jnp.max(s_w, axis=1, keepdims=True), s_wn)
    p_w = jnp.where(m_w, jnp.exp(s_w - mw), 0.0)
    p_wn = jnp.exp(s_wn - mw)
    o_w = (_dot(p_w, vwb, nt=True) + p_wn * new_row(5)) / (jnp.sum(p_w, axis=1, keepdims=True) + p_wn)
    kwo_ref[...] = jnp.where(wlane == n_buf - 1, new_col(4), pltpu.roll(kwb, shift=n_buf - 1, axis=1))
    vwo_ref[...] = jnp.where(wlane == n_buf - 1, new_col(5), pltpu.roll(vwb, shift=n_buf - 1, axis=1))

    gate_row = _sigmoid(pltpu.roll(sm_ref[pl.ds(b, 1), :], shift=(LANE - 3 * NSA_REP * g) % LANE, axis=1))
    gl = lax.broadcasted_iota(jnp.int32, (NSA_REP, LANE), 1)
    gr = lax.broadcasted_iota(jnp.int32, (NSA_REP, LANE), 0)

    def gate(j):
        return jnp.sum(jnp.where(gl == 3 * gr + j, gate_row, 0.0), axis=1, keepdims=True)

    o = gate(0) * oc_ref[...] + gate(1) * o_s + gate(2) * o_w
    o_ref[...] = o * _silu(az_ref[...])


def nsa_sample_sel(sel, page_table, slc_k_t, slc_v_t, win_k_t, win_v_t, layer, kvt, kvs, q3, az3, ux, o_cmp, p_len):
    nb = page_table.shape[0]
    n_buf = win_k_t.shape[-1]
    last_blk = p_len // SLC_BLK - 1

    def page_spec(k):
        def imap(b, g, sel_r, pt_r):
            blk = jnp.minimum(sel_r[b, g, k], last_blk)
            return (layer, pt_r[b, lax.shift_right_logical(blk, BLK_PER_PAGE.bit_length() - 1)], g, 0, 0)
        return pl.BlockSpec((None, None, None, HD, PAGE), imap)

    win_spec = pl.BlockSpec((None, None, None, HD, n_buf), lambda b, g, s, p: (layer, b, g, 0, 0))
    wout_spec = pl.BlockSpec((None, None, HD, n_buf), lambda b, g, s, p: (b, g, 0, 0))
    row_spec = pl.BlockSpec((None, None, NSA_REP, HD), lambda b, g, s, p: (b, g, 0, 0))

    def full(a):
        return pl.BlockSpec(a.shape, lambda b, g, s, p: (0,) * a.ndim)

    small = pl.BlockSpec((nb, LANE), lambda b, g, s, p: (0, OFF_SMALL // LANE))
    return pl.pallas_call(
        functools.partial(_nsa_sample_sel_kernel, p_len=p_len, nb=nb, n_buf=n_buf),
        grid_spec=pltpu.PrefetchScalarGridSpec(
            num_scalar_prefetch=2,
            grid=(nb, NSA_KV),
            in_specs=[page_spec(k) for k in range(SLC_TOPK)] * 2
                     + [win_spec, win_spec, full(kvt), pl.BlockSpec((None, 6 * NSA_KV, HD), lambda b, g, s, p: (b, 0, 0)),
                        row_spec, row_spec, small, row_spec],
            out_specs=[row_spec, wout_spec, wout_spec]),
        out_shape=[jax.ShapeDtypeStruct((nb, NSA_KV, NSA_REP, HD), F32),
                   jax.ShapeDtypeStruct((nb, NSA_KV, HD, n_buf), F32),
                   jax.ShapeDtypeStruct((nb, NSA_KV, HD, n_buf), F32)],
        compiler_params=_cparams(("parallel", "arbitrary")),
        name="nsa_sample_sel",
    )(sel, page_table, *([slc_k_t] * SLC_TOPK), *([slc_v_t] * SLC_TOPK), win_k_t, win_v_t, kvt, kvs, q3, az3, ux, o_cmp)


def _prep_weights(w_in, cmp_pe_k, cmp_w1_k, cmp_w2_k, cmp_pe_v, cmp_w1_v, cmp_w2_v, w_br_a, w_br_b, w_br_c, w_out):
    wt = jnp.transpose(w_in, (0, 2, 1))

    def gather_rows(order, total):
        parts = [wt[:, _IN_OFF[n]:_IN_OFF[n] + _IN_SZ[n]] for n in order]
        used = sum(_IN_SZ[n] for n in order)
        parts.append(jnp.zeros((wt.shape[0], total - used, wt.shape[2]), wt.dtype))
        return jnp.concatenate(parts, axis=1).astype(BF16)

    half = CMP_STRIDE * HD

    def cat(w1):
        return jnp.concatenate([w1[:, :half], w1[:, half:]], axis=2)

    def pe_rows(pe):
        d = pe.shape[0]
        rows = pe.reshape(d, 2, half)
        return jnp.concatenate([rows, jnp.zeros((d, 6, half), pe.dtype)], axis=1)

    w1cat = jnp.stack([cat(cmp_w1_k), cat(cmp_w1_v)], axis=1)
    pe2 = jnp.stack([pe_rows(cmp_pe_k), pe_rows(cmp_pe_v)], axis=1)
    w2 = jnp.stack([cmp_w2_k, cmp_w2_v], axis=1)
    return dict(wt_ug=gather_rows(_UG_ORDER, NP_UG), wt_ux=gather_rows(_UX_ORDER, NP_UX), wt_kv=gather_rows(("a_kv",), KV_ROWS),
                wt_q=gather_rows(_TQ_ORDER, TQ_ROWS),
                w1cat=w1cat, pe2=pe2, w2=w2,
                wa=w_br_a.astype(BF16), wb=w_br_b.astype(BF16), wc=w_br_c.astype(BF16), wo=w_out.astype(BF16))


def _prompt_layer(x, h, nb, t, layer, w, p, next_norm, tiles, stacks):
    ug = matmul(h, w["wt_ug"], nt=True, tm=tiles["mm_tm"], tn=tiles["in_tn"], out_dtype=BF16, name="in_proj",
                b_layer=layer)
    ux = matmul(h, w["wt_ux"], nt=True, tm=tiles["mm_tm"], tn=NP_UX, name="in_proj_x", b_layer=layer)
    stacks = kv_proj_stacked(w["wt_kv"], layer, h, nb, t, stacks, tn=tiles["kv_tn"])
    qgt = matmul_t(w["wt_q"], layer, h, nb, t, tm=TQ_TM, tn=tiles["q_tn"], name="q_proj_t")
    nseg = t // CMP_STRIDE
    seg = jnp.stack([stacks[0][layer], stacks[1][layer]]).reshape(2, nb, NSA_KV, HD, t)
    seg = jnp.transpose(seg, (0, 1, 2, 4, 3)).reshape(2, nb * NSA_KV * nseg, CMP_STRIDE * HD)
    kc = compress_blocks(seg, w["w1cat"][layer], w["pe2"][layer], w["w2"][layer], nseg)
    o_a = nsa_prompt(ug, qgt, stacks, layer, kc, nb, t, tq=tiles["tq"], ck=tiles["ck"])
    zeros_gla = jnp.zeros((nb, GLA_HEADS, GLA_HK, GLA_HV), F32)
    zeros_ssm = jnp.zeros((nb, M_HEADS, M_HD, M_STATE), F32)
    zeros_conv = jnp.zeros((nb, M_CONV - 1, M_CONVDIM), F32)
    o_b, gla_s = gla_mixer(ug, ux, zeros_gla, p["gla_wa2"], p["gla_ba"], p["gla_norm"], nb, t,
                           c=tiles["gla_c"], sub=min(tiles["gla_sub"], t // tiles["gla_c"]), t_valid=t)
    o_c, ssm_s, conv_s = ssd_mixer(ug, ux, zeros_ssm, zeros_conv, p["conv_w"], p["conv_b"], p["dt_bias"], p["a_log"],
                                   p["d_skip"], p["m_norm"], nb, t, q=min(tiles["ssd_q"], t), t_valid=t)
    mix = merge_branches(o_a, o_b, o_c, w["wa"], w["wb"], w["wc"], layer, ug, tm=tiles["mg_tm"], tn=tiles["mg_tn"])
    x_out, h_out = out_proj_norm(mix, w["wo"], layer, x, *next_norm, tm=tiles["out_tm"])
    return x_out, h_out, stacks, (gla_s, ssm_s, conv_s)


ROW_PAD = 8


def _sample_layer(x, h, layer, w, p, next_norm, caches, states, page_table, tiles):
    nb = x.shape[0]
    p_len = page_table.shape[1] * PAGE
    ck_t, cv_t, sk_t, sv_t, wk_t, wv_t = caches
    ug = matmul(h, w["wt_ug"], nt=True, tm=nb, tn=tiles["in_tn"], out_dtype=BF16, name="in_proj_s", b_layer=layer)
    ux = matmul(h, w["wt_ux"], nt=True, tm=nb, tn=NP_UX, name="in_proj_xs", b_layer=layer)
    kv = matmul(h, w["wt_kv"], nt=True, tm=nb, tn=512, name="kv_proj_s", b_layer=layer)
    kvt = matmul(w["wt_kv"], h, nt=True, tm=512, tn=nb, name="kv_proj_st", a_layer=layer)
    qg = matmul(h, w["wt_q"], nt=True, tm=nb, tn=TQ_TM, name="q_proj_s", b_layer=layer)
    kvs = kv.reshape(nb, 6 * NSA_KV, HD)
    def pad_rows(a):
        return jnp.pad(a[:, None, :], ((0, 0), (0, ROW_PAD - 1), (0, 0))).reshape(nb * ROW_PAD, a.shape[1])

    ug_pad, ux_pad = pad_rows(ug), pad_rows(ux)
    o_b, gla_s = gla_mixer(ug_pad, ux_pad, states[0], p["gla_wa2"], p["gla_ba"], p["gla_norm"], nb, ROW_PAD,
                           c=ROW_PAD, sub=1, t_valid=1)
    o_c, ssm_s, conv_s = ssd_mixer(ug_pad, ux_pad, states[1], states[2], p["conv_w"], p["conv_b"], p["dt_bias"], p["a_log"],
                                   p["d_skip"], p["m_norm"], nb, ROW_PAD, q=ROW_PAD, t_valid=1)
    o_b = o_b.reshape(nb, ROW_PAD, GLA_DV)[:, 0]
    o_c = o_c.reshape(nb, ROW_PAD, M_DINNER)[:, 0]
    abk, abv = page_compress(ck_t, cv_t, page_table, layer, w["w1cat"], npg=min(tiles["npg"], page_table.shape[1]))
    q3 = qg[:, :TQ_AG].reshape(nb, NSA_KV, NSA_REP, HD)
    az3 = ug[:, _UG_OFF["a_z"]:_UG_OFF["a_z"] + MIX_W].astype(F32).reshape(nb, NSA_KV, NSA_REP, HD)
    o_cmp, sel = nsa_sample_cmp(abk, abv, kvs, q3, w["w1cat"][layer], w["pe2"][layer], w["w2"][layer], p_len)
    o_a, win_k, win_v = nsa_sample_sel(sel.reshape(nb, NSA_KV, LANE), page_table, sk_t, sv_t, wk_t, wv_t, layer,
                                       kvt, kvs, q3, az3, ux, o_cmp, p_len)
    mix = merge_branches(o_a.reshape(nb, MIX_W).astype(BF16), o_b, o_c, w["wa"], w["wb"], w["wc"], layer, ug,
                         tm=nb, tn=tiles["mg_tn"])
    x_out, h_out = out_proj_norm(mix, w["wo"], layer, x, *next_norm, tm=nb)
    kv4 = kv.reshape(nb, 6, 1, NSA_KV, HD)
    return x_out, h_out, (kv4[:, 0], kv4[:, 1], kv4[:, 2], kv4[:, 3], win_k, win_v, gla_s, ssm_s, conv_s)


_TILES = dict(rms_tm=256, mm_tm=1024, in_tn=2816, kv_tn=1024, q_tn=2048, tq=512, ck=512, gla_c=64, gla_sub=8,
              ssd_q=256, mg_tm=1024, mg_tn=1024, out_tm=512, npg=32)
_PARAM_NAMES = ("ln_w", "gla_wa2", "gla_ba", "gla_norm", "conv_w", "conv_b", "dt_bias", "a_log", "d_skip", "m_norm")


def kernel(x_prompt, x_sample, cache_cmp_k, cache_cmp_v, cache_slc_k, cache_slc_v, cache_win_k, cache_win_v,
           state_gla, state_ssm, state_conv, page_table, ln_w, w_in, cmp_pe_k, cmp_w1_k, cmp_w2_k, cmp_pe_v,
           cmp_w1_v, cmp_w2_v, gla_wa2, gla_ba, gla_norm, conv_w, conv_b, dt_bias, a_log, d_skip, m_norm,
           w_br_a, w_br_b, w_br_c, w_out, final_norm):
    nbp, t, d = x_prompt.shape
    nbs = x_sample.shape[0]
    depth = w_in.shape[0]
    weights = _prep_weights(w_in, cmp_pe_k, cmp_w1_k, cmp_w2_k, cmp_pe_v, cmp_w1_v, cmp_w2_v,
                            w_br_a, w_br_b, w_br_c, w_out)
    params = dict(zip(_PARAM_NAMES, (ln_w, gla_wa2, gla_ba, gla_norm, conv_w, conv_b, dt_bias, a_log, d_skip, m_norm)))
    caches = tuple(jnp.transpose(c, (0, 1, 3, 4, 2))
                   for c in (cache_cmp_k, cache_cmp_v, cache_slc_k, cache_slc_v, cache_win_k, cache_win_v))
    xp = x_prompt.reshape(nbp * t, d)
    xs = x_sample.reshape(nbs, d)
    p_states, s_out = [], []
    stacks = [jnp.zeros((depth, nbp, GRP_ROWS, t), F32) for _ in range(6)]
    hp = rmsnorm_rows(xp, ln_w[0], BF16, _TILES["rms_tm"])
    hs = rmsnorm_rows(xs, ln_w[0], BF16, nbs)
    for l in range(depth):
        p = {k: v[l] for k, v in params.items()}
        next_norm = (ln_w[l + 1], BF16) if l + 1 < depth else (final_norm, F32)
        xp, hp, stacks, st_p = _prompt_layer(xp, hp, nbp, t, l, weights, p, next_norm, _TILES, stacks)
        xs, hs, st_s = _sample_layer(xs, hs, l, weights, p, next_norm, caches,
                                     (state_gla[l], state_ssm[l], state_conv[l]), page_table, _TILES)
        p_states.append(st_p)
        s_out.append(st_s)
    y_prompt = hp.reshape(nbp, t, d)
    y_sample = hs.reshape(nbs, 1, d)

    def from_kv_layout(a):
        return jnp.transpose(a, (0, 1, 4, 2, 3))

    n_w = min(WINDOW, t)
    kv5 = [a.reshape(depth, nbp, NSA_KV, HD, t) for a in stacks]
    p_leaves = [from_kv_layout(a) for a in kv5[:4]] + [from_kv_layout(a[..., t - n_w:]) for a in kv5[4:]]
    p_leaves += [jnp.stack([st[i] for st in p_states]) for i in range(3)]
    s_leaves = [jnp.stack([o[i] for o in s_out]) for i in range(9)]
    s_leaves[4], s_leaves[5] = from_kv_layout(s_leaves[4]), from_kv_layout(s_leaves[5])
    return (y_prompt, y_sample, *p_leaves, *s_leaves)
```
